```python
import jax, jax.numpy as jnp
from jax import lax
import numpy as np

D_MODEL = 1024
BATCH = 16
SEQ = 4096
DEPTH = 4

CHUNK = 64
Q_BLOCK = 128
MEM_LEN = 256
NORM_EPS = 1e-6

MIX_WIDTH = D_MODEL
MLA_WIDTH = MIX_WIDTH // 2
RWKV_WIDTH = MIX_WIDTH - MLA_WIDTH

MLA_HEADS = 8
MLA_V_DIM = MLA_WIDTH // MLA_HEADS
MLA_NOPE_DIM = MLA_V_DIM
MLA_ROPE_DIM = MLA_NOPE_DIM // 2
MLA_Q_RANK = 3 * D_MODEL // 16
MLA_KV_RANK = D_MODEL // 8
ROPE_THETA = 10000.0

RWKV_HEAD = 64
RWKV_HEADS = RWKV_WIDTH // RWKV_HEAD
DECAY_RANK = 32
ICL_RANK = 32
GATE_RANK = 96
VRES_RANK = 32
GN_EPS = 64e-5

CA_HEADS = 4
CA_HEAD_DIM = D_MODEL // CA_HEADS
D_FF = 4 * D_MODEL

MLA_COLS = MLA_Q_RANK + MLA_KV_RANK + MLA_ROPE_DIM
RWKV_COLS0 = 3 * RWKV_WIDTH + DECAY_RANK + ICL_RANK + GATE_RANK
RWKV_COLS = RWKV_COLS0 + VRES_RANK
IN_COLS0 = MLA_COLS + RWKV_COLS0
IN_COLS = MLA_COLS + RWKV_COLS

kernel_name = "hybrid_mla_rwkv7_stream_encoder"


def rms_norm(x, g, eps=NORM_EPS):
    xf = x.astype(jnp.float32)
    y = xf * lax.rsqrt(jnp.mean(xf * xf, axis=-1, keepdims=True) + eps)
    return (y * g.astype(jnp.float32)).astype(x.dtype)


def rope_tables(positions):
    inv = ROPE_THETA ** (-jnp.arange(0, MLA_ROPE_DIM, 2, dtype=jnp.float32) / MLA_ROPE_DIM)
    ang = positions.astype(jnp.float32)[..., None] * inv
    return jnp.cos(ang), jnp.sin(ang)


def apply_rope(x, cos, sin):
    x1, x2 = jnp.split(x.astype(jnp.float32), 2, axis=-1)
    return jnp.concatenate([x1 * cos - x2 * sin, x1 * sin + x2 * cos], axis=-1).astype(x.dtype)


def token_shift(z, mu):
    prev = jnp.pad(z, ((0, 0), (1, 0), (0, 0)))[:, :-1]
    return z + (prev - z) * mu


def chunk_causal_mla_attention(q_nope, q_rope, k_nope, k_rope, v):
    B_, S_, H, _ = q_nope.shape
    nb = S_ // Q_BLOCK
    scale = (MLA_NOPE_DIM + MLA_ROPE_DIM) ** -0.5
    key_chunk = jnp.arange(S_) // CHUNK

    def to_blocks(t):
        return jnp.moveaxis(t.reshape(B_, nb, Q_BLOCK, *t.shape[2:]), 1, 0)

    def block(args):
        qn, qr, start = args
        s = (jnp.einsum('bqhd,bkhd->bhqk', qn, k_nope, preferred_element_type=jnp.float32)
             + jnp.einsum('bqhr,bkr->bhqk', qr, k_rope, preferred_element_type=jnp.float32)) * scale
        q_chunk = (start + jnp.arange(Q_BLOCK)) // CHUNK
        s = jnp.where(key_chunk[None, :] <= q_chunk[:, None], s, -jnp.inf)
        p = jax.nn.softmax(s, axis=-1).astype(v.dtype)
        return jnp.einsum('bhqk,bkhd->bqhd', p, v)

    out = lax.map(block, (to_blocks(q_nope), to_blocks(q_rope), jnp.arange(nb) * Q_BLOCK))
    return jnp.moveaxis(out, 0, 1).reshape(B_, S_, H, v.shape[-1])


def mla_group(cols, cos, sin, q_norm, w_uq, kv_norm, w_ukv, out_norm):
    B_, S_, _ = cols.shape
    c_q = cols[..., :MLA_Q_RANK]
    c_kv = cols[..., MLA_Q_RANK:MLA_Q_RANK + MLA_KV_RANK]
    k_rope = cols[..., MLA_Q_RANK + MLA_KV_RANK:]
    q = (rms_norm(c_q, q_norm) @ w_uq).reshape(B_, S_, MLA_HEADS, MLA_NOPE_DIM + MLA_ROPE_DIM)
    q_nope = q[..., :MLA_NOPE_DIM]
    q_rope = apply_rope(q[..., MLA_NOPE_DIM:], cos[:, :, None], sin[:, :, None])
    k_rope = apply_rope(k_rope, cos, sin)
    kv = (rms_norm(c_kv, kv_norm) @ w_ukv).reshape(B_, S_, MLA_HEADS, MLA_NOPE_DIM + MLA_V_DIM)
    k_nope, v = kv[..., :MLA_NOPE_DIM], kv[..., MLA_NOPE_DIM:]
    o = chunk_causal_mla_attention(q_nope, q_rope, k_nope, k_rope, v)
    o = rms_norm(o, out_norm.reshape(MLA_HEADS, MLA_V_DIM))
    return o.reshape(B_, S_, MLA_WIDTH)


def wkv7(r, decay, k, v, a, b):
    B_, S_, H, N = r.shape
    xs = tuple(jnp.swapaxes(t.astype(jnp.float32), 0, 1) for t in (r, decay, k, v, a, b))

    def step(state, inp):
        rt, wt, kt, vt, at, bt = inp
        sa = jnp.einsum('bhvk,bhk->bhv', state, at)
        state = state * wt[:, :, None, :] + sa[..., None] * bt[:, :, None, :] + vt[..., None] * kt[:, :, None, :]
        return state, jnp.einsum('bhvk,bhk->bhv', state, rt)

    s0 = jnp.zeros((B_, H, N, N), jnp.float32)
    _, ys = lax.scan(step, s0, xs)
    return jnp.swapaxes(ys, 0, 1)


def rwkv7_group(cols, w0, w2, a0, a2, g2, k_k, k_a, r_k, ln_w, ln_b, v_first, v0, v2):
    B_, S_, _ = cols.shape
    RW, H, N = RWKV_WIDTH, RWKV_HEADS, RWKV_HEAD
    r = cols[..., :RW]
    k = cols[..., RW:2 * RW]
    v = cols[..., 2 * RW:3 * RW]
    o = 3 * RW
    w_l = cols[..., o:o + DECAY_RANK]
    o += DECAY_RANK
    a_l = cols[..., o:o + ICL_RANK]
    o += ICL_RANK
    g_l = cols[..., o:o + GATE_RANK]
    o += GATE_RANK

    w_log = -jax.nn.softplus(-(w0 + jnp.tanh(w_l) @ w2)) - 0.5
    decay = jnp.exp(-jnp.exp(w_log.astype(jnp.float32)))
    a = jax.nn.sigmoid(a0 + a_l @ a2)
    g = jax.nn.sigmoid(g_l) @ g2
    if v0 is None:
        v_first = v
    else:
        v_l = cols[..., o:o + VRES_RANK]
        v = v + (v_first - v) * jax.nn.sigmoid(v0 + v_l @ v2)

    heads = lambda t: t.reshape(B_, S_, H, N)
    kk = heads(k * k_k).astype(jnp.float32)
    kk = kk / jnp.maximum(jnp.sqrt(jnp.sum(kk * kk, axis=-1, keepdims=True)), 1e-12)
    k = k * (1.0 + (a - 1.0) * k_a)
    rh, kh, vh, ah = heads(r), heads(k), heads(v), heads(a)
    y = wkv7(rh, heads(decay), kh, vh, -kk, kk * ah)

    mu = jnp.mean(y, axis=-1, keepdims=True)
    var = jnp.mean(jnp.square(y - mu), axis=-1, keepdims=True)
    y = (y - mu) * lax.rsqrt(var + GN_EPS) * ln_w.reshape(H, N) + ln_b.reshape(H, N)
    y = y + jnp.sum(rh * kh * r_k, axis=-1, keepdims=True) * vh
    out = (y.reshape(B_, S_, RW) * g).astype(cols.dtype)
    return out, v_first


def mem_cross_attention(hn, mem_n, wq, wkv, wo):
    B_, S_, _ = hn.shape
    M = mem_n.shape[1]
    q = (hn @ wq).reshape(B_, S_, CA_HEADS, CA_HEAD_DIM)
    kv = (mem_n @ wkv).reshape(B_, M, 2, CA_HEADS, CA_HEAD_DIM)
    k, v = kv[:, :, 0], kv[:, :, 1]
    s = jnp.einsum('bshd,bmhd->bhsm', q, k, preferred_element_type=jnp.float32) * (CA_HEAD_DIM ** -0.5)
    p = jax.nn.softmax(s, axis=-1).astype(v.dtype)
    o = jnp.einsum('bhsm,bmhd->bshd', p, v).reshape(B_, S_, D_MODEL)
    return o @ wo


def squared_relu_mlp(hn, w_up, w_down):
    return jnp.square(jax.nn.relu(hn @ w_up)) @ w_down


def setup_inputs(seed: int = 0) -> dict:
    key = jax.random.key(seed)
    keys = iter(jax.random.split(key, 64))
    f32 = jnp.float32

    def nrm(shape, scale):
        return scale * jax.random.normal(next(keys), shape, f32)

    def gain(shape):
        return 1.0 + nrm(shape, 0.02)

    L, L1 = DEPTH, DEPTH - 1
    RW = RWKV_WIDTH
    x = jax.random.normal(next(keys), (BATCH, SEQ, D_MODEL), f32)
    mem = jax.random.normal(next(keys), (BATCH, MEM_LEN, D_MODEL), f32)
    start = jax.random.randint(next(keys), (BATCH, 1), 0, 64) * CHUNK
    positions = (start + jnp.arange(SEQ)[None, :]).astype(jnp.int32)

    n = jnp.arange(RW, dtype=f32) / (RW - 1)
    decay_speed = -7.0 + 5.0 * n ** 0.85 + 0.5
    w0 = decay_speed[None, :] + nrm((L, RW), 0.1)

    return {
        "x": x,
        "mem": mem,
        "positions": positions,
        "mix_norm": gain((L, D_MODEL)),
        "w_in_first": nrm((D_MODEL, IN_COLS0), D_MODEL ** -0.5),
        "w_in_rest": nrm((L1, D_MODEL, IN_COLS), D_MODEL ** -0.5),
        "shift_mu_first": jax.random.uniform(next(keys), (RWKV_COLS0,), f32, 0.2, 0.8),
        "shift_mu_rest": jax.random.uniform(next(keys), (L1, RWKV_COLS), f32, 0.2, 0.8),
        "mla_q_norm": gain((L, MLA_Q_RANK)),
        "mla_w_uq": nrm((L, MLA_Q_RANK, MLA_HEADS * (MLA_NOPE_DIM + MLA_ROPE_DIM)), MLA_Q_RANK ** -0.5),
        "mla_kv_norm": gain((L, MLA_KV_RANK)),
        "mla_w_ukv": nrm((L, MLA_KV_RANK, MLA_HEADS * (MLA_NOPE_DIM + MLA_V_DIM)), MLA_KV_RANK ** -0.5),
        "mla_out_norm": gain((L, MLA_WIDTH)),
        "rwkv_w0": w0,
        "rwkv_w2": nrm((L, DECAY_RANK, RW), 0.1 * DECAY_RANK ** -0.5),
        "rwkv_a0": nrm((L, RW), 0.1),
        "rwkv_a2": nrm((L, ICL_RANK, RW), 0.5 * ICL_RANK ** -0.5),
        "rwkv_g2": nrm((L, GATE_RANK, RW), GATE_RANK ** -0.5),
        "rwkv_v0": 1.0 + nrm((L1, RW), 0.1),
        "rwkv_v2": nrm((L1, VRES_RANK, RW), 0.5 * VRES_RANK ** -0.5),
        "rwkv_k_k": 0.85 + nrm((L, RW), 0.02),
        "rwkv_k_a": 1.0 + nrm((L, RW), 0.02),
        "rwkv_r_k": nrm((L, RWKV_HEADS, RWKV_HEAD), 0.1),
        "rwkv_ln_w": gain((L, RW)),
        "rwkv_ln_b": nrm((L, RW), 0.02),
        "w_out": nrm((L, MIX_WIDTH, D_MODEL), MIX_WIDTH ** -0.5),
        "ca_norm": gain((L, D_MODEL)),
        "mem_norm": gain((D_MODEL,)),
        "ca_wq": nrm((L, D_MODEL, D_MODEL), D_MODEL ** -0.5),
        "ca_wkv": nrm((L, D_MODEL, 2 * D_MODEL), D_MODEL ** -0.5),
        "ca_wo": nrm((L, D_MODEL, D_MODEL), D_MODEL ** -0.5),
        "mlp_norm": gain((L, D_MODEL)),
        "mlp_w_up": nrm((L, D_MODEL, D_FF), D_MODEL ** -0.5),
        "mlp_w_down": nrm((L, D_FF, D_MODEL), D_FF ** -0.5),
        "final_norm": gain((D_MODEL,)),
    }


def reference(x, mem, positions, mix_norm, w_in_first, w_in_rest, shift_mu_first, shift_mu_rest,
              mla_q_norm, mla_w_uq, mla_kv_norm, mla_w_ukv, mla_out_norm,
              rwkv_w0, rwkv_w2, rwkv_a0, rwkv_a2, rwkv_g2, rwkv_v0, rwkv_v2,
              rwkv_k_k, rwkv_k_a, rwkv_r_k, rwkv_ln_w, rwkv_ln_b, w_out,
              ca_norm, mem_norm, ca_wq, ca_wkv, ca_wo,
              mlp_norm, mlp_w_up, mlp_w_down, final_norm):
    cos, sin = rope_tables(positions)
    mem_n = rms_norm(mem, mem_norm)
    h = x
    v_first = None
    for l in range(DEPTH):
        if l == 0:
            w_in, mu, v0, v2 = w_in_first, shift_mu_first, None, None
        else:
            w_in, mu, v0, v2 = w_in_rest[l - 1], shift_mu_rest[l - 1], rwkv_v0[l - 1], rwkv_v2[l - 1]
        proj = rms_norm(h, mix_norm[l]) @ w_in
        att = mla_group(proj[..., :MLA_COLS], cos, sin, mla_q_norm[l], mla_w_uq[l],
                        mla_kv_norm[l], mla_w_ukv[l], mla_out_norm[l])
        rw, v_first = rwkv7_group(token_shift(proj[..., MLA_COLS:], mu),
                                  rwkv_w0[l], rwkv_w2[l], rwkv_a0[l], rwkv_a2[l], rwkv_g2[l],
                                  rwkv_k_k[l], rwkv_k_a[l], rwkv_r_k[l], rwkv_ln_w[l], rwkv_ln_b[l],
                                  v_first, v0, v2)
        h = h + jnp.concatenate([att, rw], axis=-1) @ w_out[l]
        h = h + mem_cross_attention(rms_norm(h, ca_norm[l]), mem_n, ca_wq[l], ca_wkv[l], ca_wo[l])
        h = h + squared_relu_mlp(rms_norm(h, mlp_norm[l]), mlp_w_up[l], mlp_w_down[l])
    return rms_norm(h, final_norm)
```

```python
import functools
import math

import jax
import jax.numpy as jnp
from jax import lax
from jax.experimental import pallas as pl
from jax.experimental.pallas import tpu as pltpu

F32 = jnp.float32
BF16 = jnp.bfloat16

CHUNK = 64
NORM_EPS = 1e-6
MLA_HEADS = 8
MLA_V_DIM = 64
MLA_NOPE_DIM = 64
MLA_ROPE_DIM = 32
MLA_Q_RANK = 192
MLA_KV_RANK = 128
ROPE_THETA = 10000.0
RWKV_HEAD = 64
RWKV_HEADS = 8
RWKV_WIDTH = 512
DECAY_RANK = 32
ICL_RANK = 32
GATE_RANK = 96
VRES_RANK = 32
GN_EPS = 64e-5
CA_HEADS = 4
MLA_COLS = MLA_Q_RANK + MLA_KV_RANK + MLA_ROPE_DIM

LANE = 128
HEAD_PAD = 128
MLA_C_PAD = 512
RW_C_PAD = 1792
LORA_PAD = 256
WKV_CHUNK = 64
LOG2E = 1.4426950408889634
NEG_BIG = -1e30
VMEM_LIMIT = 56 * 1024 * 1024

HI = lax.Precision.HIGHEST


def _dot(a, b, precision=None):
    return jnp.dot(a, b, preferred_element_type=F32, precision=precision)


def _dot_nt(a, b, precision=None):
    return lax.dot_general(a, b, (((1,), (1,)), ((), ())),
                           preferred_element_type=F32, precision=precision)


def _dot_tn(a, b, precision=None):
    return lax.dot_general(a, b, (((0,), (0,)), ((), ())),
                           preferred_element_type=F32, precision=precision)


def _rms(x, g, n, eps=NORM_EPS):
    ms = jnp.sum(x * x, axis=-1, keepdims=True) * (1.0 / n)
    return x * lax.rsqrt(ms + eps) * g


def _sigmoid(x):
    return 1.0 / (1.0 + jnp.exp(-x))


def _params(*sem):
    return pltpu.CompilerParams(dimension_semantics=sem, vmem_limit_bytes=VMEM_LIMIT)


def _full(shape):
    nd = len(shape)
    return pl.BlockSpec(shape, lambda *_: (0,) * nd)


def _rope_kernel(pos_ref, inv_ref, ct_ref, st_ref):
    ang = pos_ref[...] * inv_ref[...]
    c = jnp.cos(ang)
    s = jnp.sin(ang)
    lane = lax.broadcasted_iota(jnp.int32, ang.shape, 1)
    half = MLA_NOPE_DIM + MLA_ROPE_DIM // 2
    end = MLA_NOPE_DIM + MLA_ROPE_DIM
    ct_ref[...] = jnp.where(lane < MLA_NOPE_DIM, 1.0, jnp.where(lane < end, c, 0.0))
    st_ref[...] = jnp.where(lane < MLA_NOPE_DIM, 0.0,
                            jnp.where(lane < half, -s, jnp.where(lane < end, s, 0.0)))


def _rope_tables(pos_f, inv_row, tm):
    t = pos_f.shape[0]
    return pl.pallas_call(
        _rope_kernel,
        grid=(t // tm,),
        in_specs=[pl.BlockSpec((tm, 1), lambda i: (i, 0)), _full((1, LANE))],
        out_specs=[pl.BlockSpec((tm, LANE), lambda i: (i, 0))] * 2,
        out_shape=[jax.ShapeDtypeStruct((t, LANE), F32)] * 2,
        compiler_params=_params("parallel"),
        name="rope_tables",
    )(pos_f, inv_row)


def _in_proj_kernel(h_ref, g_ref, w_ref, mu_ref, mla_ref, rw_ref, carry_ref, *, tiles_per_seq):
    i = pl.program_id(0)
    d = h_ref.shape[-1]
    hn = _rms(h_ref[...], g_ref[...], d).astype(BF16)
    proj = _dot(hn, w_ref[...])
    mla_ref[...] = proj[:, :MLA_C_PAD].astype(BF16)
    z = proj[:, MLA_C_PAD:]
    tm = z.shape[0]

    @pl.when(i % tiles_per_seq == 0)
    def _():
        carry_ref[...] = jnp.zeros_like(carry_ref)

    prev = pltpu.roll(z, 1, axis=0)
    row = lax.broadcasted_iota(jnp.int32, z.shape, 0)
    prev = jnp.where(row == 0, carry_ref[7:8, :], prev)
    rw_ref[...] = (z + (prev - z) * mu_ref[...]).astype(BF16)
    carry_ref[...] = z[tm - 8:, :]


def _in_proj(h, g, w, mu, seq, tm):
    t, d = h.shape
    n = w.shape[1]
    kern = functools.partial(_in_proj_kernel, tiles_per_seq=seq // tm)
    return pl.pallas_call(
        kern,
        grid=(t // tm,),
        in_specs=[pl.BlockSpec((tm, d), lambda i: (i, 0)), _full((1, d)),
                  _full((d, n)), _full((1, RW_C_PAD))],
        out_specs=[pl.BlockSpec((tm, MLA_C_PAD), lambda i: (i, 0)),
                   pl.BlockSpec((tm, RW_C_PAD), lambda i: (i, 0))],
        out_shape=[jax.ShapeDtypeStruct((t, MLA_C_PAD), BF16),
                   jax.ShapeDtypeStruct((t, RW_C_PAD), BF16)],
        scratch_shapes=[pltpu.VMEM((8, RW_C_PAD), F32)],
        compiler_params=_params("arbitrary"),
        name="in_proj",
    )(h, g, w, mu)


def _mla_prep_kernel(c_ref, ct_ref, st_ref, gq_ref, gkv_ref, wqa_ref, wqb_ref,
                     wka_ref, pb_ref, wv_ref, q_ref, k_ref, v_ref, *, q_scale):
    c = c_ref[...].astype(F32)
    nq = _rms(c[:, :256], gq_ref[...], MLA_Q_RANK).astype(BF16)
    nkv = _rms(c[:, 256:384], gkv_ref[...], MLA_KV_RANK).astype(BF16)
    kr = c_ref[:, 384:512]
    ct = jnp.tile(ct_ref[...], (1, MLA_HEADS))
    st = jnp.tile(st_ref[...], (1, MLA_HEADS))
    q = _dot(nq, wqa_ref[...]) * ct + _dot(nq, wqb_ref[...]) * st
    q_ref[...] = (q * q_scale).astype(BF16)
    x = jnp.concatenate([nkv, kr], axis=1)
    k = _dot(x, wka_ref[...]) * ct + _dot(kr, pb_ref[...]) * st
    k_ref[...] = k.astype(BF16)
    v_ref[...] = _dot(nkv, wv_ref[...]).astype(BF16)


def _mla_prep(c, ct, st, gq, gkv, wqa, wqb, wka, pb, wv, tm):
    t = c.shape[0]
    hw = MLA_HEADS * HEAD_PAD
    q_scale = (MLA_NOPE_DIM + MLA_ROPE_DIM) ** -0.5 * LOG2E
    row = lambda w: pl.BlockSpec((tm, w), lambda i: (i, 0))
    return pl.pallas_call(
        functools.partial(_mla_prep_kernel, q_scale=q_scale),
        grid=(t // tm,),
        in_specs=[row(MLA_C_PAD), row(LANE), row(LANE), _full(gq.shape), _full(gkv.shape),
                  _full(wqa.shape), _full(wqb.shape), _full(wka.shape), _full(pb.shape),
                  _full(wv.shape)],
        out_specs=[row(hw)] * 3,
        out_shape=[jax.ShapeDtypeStruct((t, hw), BF16)] * 3,
        compiler_params=_params("parallel"),
        name="mla_prep",
    )(c, ct, st, gq, gkv, wqa, wqb, wka, pb, wv)


def _attn_kernel(q_ref, k_ref, v_ref, g_ref, o_ref, *, tq):
    qi = pl.program_id(2)
    q = q_ref[0]

    def step(j, carry, masked):
        m, l, acc = carry
        off = pl.multiple_of(j * tq, tq)
        kb = k_ref[0, pl.ds(off, tq), :]
        vb = v_ref[0, pl.ds(off, tq), :]
        s = _dot_nt(q, kb)
        if masked:
            rq = lax.broadcasted_iota(jnp.int32, s.shape, 0) // CHUNK
            ck = lax.broadcasted_iota(jnp.int32, s.shape, 1) // CHUNK
            s = jnp.where(ck <= rq, s, NEG_BIG)
        m_new = jnp.maximum(m, jnp.max(s, axis=-1, keepdims=True))
        p = jnp.exp2(s - m_new)
        alpha = jnp.exp2(m - m_new)
        l = alpha * l + jnp.sum(p, axis=-1, keepdims=True)
        acc = alpha * acc + _dot(p.astype(BF16), vb)
        return m_new, l, acc

    init = (jnp.full((tq, 1), NEG_BIG, F32), jnp.zeros((tq, 1), F32),
            jnp.zeros((tq, HEAD_PAD), F32))
    carry = lax.fori_loop(0, qi, lambda j, c: step(j, c, False), init)
    m, l, acc = step(qi, carry, True)
    o = acc / l
    o_ref[0] = _rms(o, g_ref[...], MLA_V_DIM).astype(BF16)


def _mla_attention(q, k, v, g, tq):
    b, s, hw = q.shape
    nh = hw // HEAD_PAD
    return pl.pallas_call(
        functools.partial(_attn_kernel, tq=tq),
        grid=(b, nh, s // tq),
        in_specs=[pl.BlockSpec((1, tq, HEAD_PAD), lambda bi, h, i: (bi, i, h)),
                  pl.BlockSpec((1, s, HEAD_PAD), lambda bi, h, i: (bi, 0, h)),
                  pl.BlockSpec((1, s, HEAD_PAD), lambda bi, h, i: (bi, 0, h)),
                  pl.BlockSpec((1, HEAD_PAD), lambda bi, h, i: (0, h))],
        out_specs=pl.BlockSpec((1, tq, HEAD_PAD), lambda bi, h, i: (bi, i, h)),
        out_shape=jax.ShapeDtypeStruct((b, s, hw), BF16),
        compiler_params=_params("parallel", "parallel", "arbitrary"),
        name="mla_attention",
    )(q, k, v, g)


def _rwkv_prep_kernel(*refs, first):
    if first:
        (x_ref, w2_ref, bd_ref, w0_ref, a0_ref, kk_ref_p, ka_ref,
         r_o, k_o, v_o, kk_o, kka_o, lw_o, g_o) = refs
        vf_ref = v0_ref = None
    else:
        (x_ref, vf_ref, w2_ref, bd_ref, w0_ref, a0_ref, kk_ref_p, ka_ref, v0_ref,
         r_o, k_o, v_o, kk_o, kka_o, lw_o, g_o) = refs
    rw = RWKV_WIDTH
    r_o[...] = x_ref[:, 0:rw]
    k = x_ref[:, rw:2 * rw].astype(F32)
    v = x_ref[:, 2 * rw:3 * rw].astype(F32)
    tail = x_ref[:, 3 * rw:3 * rw + LORA_PAD].astype(F32)
    lane = lax.broadcasted_iota(jnp.int32, tail.shape, 1)
    g_lo = DECAY_RANK + ICL_RANK
    g_hi = g_lo + GATE_RANK
    act = jnp.where(lane < DECAY_RANK, jnp.tanh(tail),
                    jnp.where((lane >= g_lo) & (lane < g_hi), _sigmoid(tail), tail))
    lo = _dot(act.astype(BF16), w2_ref[...])
    z = -(w0_ref[...] + lo[:, 0:rw])
    softplus = jnp.maximum(z, 0.0) + jnp.log(1.0 + jnp.exp(-jnp.abs(z)))
    lw_o[...] = -jnp.exp(-softplus - 0.5)
    a = _sigmoid(a0_ref[...] + lo[:, rw:2 * rw])
    g_o[...] = lo[:, 2 * rw:3 * rw].astype(BF16)
    if not first:
        mix = _sigmoid(v0_ref[...] + lo[:, 3 * rw:4 * rw])
        v = v + (vf_ref[...].astype(F32) - v) * mix
    v_o[...] = v.astype(BF16)
    kk = k * kk_ref_p[...]
    sq = kk * kk
    sq_hi = sq.astype(BF16)
    sq_lo = (sq - sq_hi.astype(F32)).astype(BF16)
    ss = _dot(sq_hi, bd_ref[...]) + _dot(sq_lo, bd_ref[...])
    kk = kk / jnp.maximum(jnp.sqrt(ss), 1e-12)
    kk_o[...] = kk.astype(BF16)
    kka_o[...] = (kk * a).astype(BF16)
    k_o[...] = (k * (1.0 + (a - 1.0) * ka_ref[...])).astype(BF16)


def _rwkv_prep(x, vf, w2, bd, w0, a0, k_k, k_a, v0, tm):
    t = x.shape[0]
    rw = RWKV_WIDTH
    first = vf is None
    row = lambda w: pl.BlockSpec((tm, w), lambda i: (i, 0))
    vec = _full((1, rw))
    if first:
        args = (x, w2, bd, w0, a0, k_k, k_a)
        in_specs = [row(RW_C_PAD), _full(w2.shape), _full(bd.shape), vec, vec, vec, vec]
    else:
        args = (x, vf, w2, bd, w0, a0, k_k, k_a, v0)
        in_specs = [row(RW_C_PAD), row(rw), _full(w2.shape), _full(bd.shape),
                    vec, vec, vec, vec, vec]
    out_dt = [BF16, BF16, BF16, BF16, BF16, F32, BF16]
    return pl.pallas_call(
        functools.partial(_rwkv_prep_kernel, first=first),
        grid=(t // tm,),
        in_specs=in_specs,
        out_specs=[row(rw)] * 7,
        out_shape=[jax.ShapeDtypeStruct((t, rw), dt) for dt in out_dt],
        compiler_params=_params("parallel"),
        name="rwkv_prep",
    )(*args)


def _unit_lower_inverse(a, row, col, eye, prec):
    d16 = (row // 16) == (col // 16)
    d32 = (row // 32) == (col // 32)
    d = jnp.where(d16, a, 0.0)
    x = eye + d
    d2 = _dot(d, d, prec)
    t = _dot(jnp.concatenate([x, d2], axis=0), d2, prec)
    x = x + t[:128]
    d4 = t[128:]
    t = _dot(jnp.concatenate([x, d4], axis=0), d4, prec)
    x = x + t[:128]
    d8 = t[128:]
    x = x + _dot(x, d8, prec)
    e = jnp.where(d32, a, 0.0) - d
    x = x + _dot(x, _dot(e, x, prec), prec)
    f = jnp.where(d32, 0.0, a)
    x = x + _dot(x, _dot(f, x, prec), prec)
    return x


def _wkv_kernel(r_ref, k_ref, v_ref, kk_ref, kka_ref, lw_ref, g_ref,
                rk_ref, lnw_ref, lnb_ref, bd_ref, o_ref, s_ref, *, nch, prec_inv, prec_state):
    c_len = WKV_CHUNK

    @pl.when(pl.program_id(2) == 0)
    def _():
        s_ref[...] = jnp.zeros_like(s_ref)

    row = lax.broadcasted_iota(jnp.int32, (128, 128), 0)
    col = lax.broadcasted_iota(jnp.int32, (128, 128), 1)
    same = (row // c_len) == (col // c_len)
    mask_s = same & (col < row)
    mask_i = same & (col <= row)
    eye = jnp.where(row == col, 1.0, 0.0)
    lane0 = lax.broadcasted_iota(jnp.int32, (c_len, 128), 1) < RWKV_HEAD
    tr = lax.broadcasted_iota(jnp.int32, (c_len, c_len), 0)
    tc = lax.broadcasted_iota(jnp.int32, (c_len, c_len), 1)
    ltri = jnp.where(tc <= tr, 1.0, 0.0)

    def stack(x):
        return jnp.concatenate([jnp.where(lane0, x, 0.0), jnp.where(lane0, 0.0, x)], axis=0)

    bd = bd_ref[...]
    for c in range(nch):
        sl = pl.ds(c * c_len, c_len)
        r = r_ref[0, sl, :].astype(F32)
        k = k_ref[0, sl, :].astype(F32)
        v = v_ref[0, sl, :].astype(F32)
        kk = kk_ref[0, sl, :].astype(F32)
        kka = kka_ref[0, sl, :].astype(F32)
        lw = lw_ref[0, sl, :]

        cum = _dot(ltri, lw, HI)
        tot = cum[c_len - 1:c_len, :]
        p_in = jnp.exp(cum)
        p_ex = jnp.exp(cum - lw)
        p_inv = jnp.exp(-cum)
        p_end = jnp.exp(tot - cum)
        at = stack(-kk * p_ex)
        rt = stack(r * p_in)
        bt = kka * p_inv
        kt = k * p_inv
        bh = stack(kka * p_end)
        kh = stack(k * p_end)
        vs = stack(v)

        lhs = jnp.concatenate([at, rt], axis=0)
        rhs = jnp.concatenate([bt, bt, kt, kt], axis=0)
        quad = _dot_nt(lhs, rhs, prec_inv)
        a_ab = jnp.where(mask_s, quad[:128, :128], 0.0)
        a_ak = jnp.where(mask_s, quad[:128, 128:], 0.0)
        a_rb = jnp.where(mask_i, quad[128:, :128], 0.0)
        a_rk = jnp.where(mask_i, quad[128:, 128:], 0.0)

        tinv = _unit_lower_inverse(a_ab, row, col, eye, prec_inv)
        akv = _dot(a_ak, vs, prec_inv)
        wu = _dot(tinv, jnp.concatenate([at, akv], axis=1), prec_inv)
        wt = wu[:, :128]
        ut = wu[:, 128:]

        s0 = s_ref[...]
        m1 = _dot_nt(jnp.concatenate([wt, rt], axis=0), s0, prec_state)
        us = m1[:128] + ut
        uv = jnp.concatenate([us, vs], axis=0)
        ys = m1[128:] + _dot(jnp.concatenate([a_rb, a_rk], axis=1), uv, prec_state)
        s_new = s0 * jnp.exp(tot) + _dot_tn(uv, jnp.concatenate([bh, kh], axis=0), prec_state)
        s_ref[...] = s_new

        y = ys[:c_len] + ys[c_len:]
        mean = _dot(y, bd, HI)
        yc = y - mean
        var = _dot(yc * yc, bd, HI)
        yn = yc * lax.rsqrt(var + GN_EPS) * lnw_ref[...] + lnb_ref[...]
        bonus = _dot(r * k * rk_ref[...], bd, HI) * float(RWKV_HEAD)
        yn = yn + bonus * v
        o_ref[0, sl, :] = (yn * g_ref[0, sl, :].astype(F32)).astype(BF16)


def _wkv(r, k, v, kk, kka, lw, g, r_k, ln_w, ln_b, bd, rows, prec_inv, prec_state):
    b, s, w = r.shape
    npair = w // LANE
    blk = pl.BlockSpec((1, rows, LANE), lambda bi, p, i: (bi, i, p))
    vec = pl.BlockSpec((1, LANE), lambda bi, p, i: (0, p))
    kern = functools.partial(_wkv_kernel, nch=rows // WKV_CHUNK,
                             prec_inv=prec_inv, prec_state=prec_state)
    return pl.pallas_call(
        kern,
        grid=(b, npair, s // rows),
        in_specs=[blk] * 7 + [vec] * 3 + [pl.BlockSpec((LANE, LANE), lambda bi, p, i: (0, 0))],
        out_specs=blk,
        out_shape=jax.ShapeDtypeStruct((b, s, w), BF16),
        scratch_shapes=[pltpu.VMEM((LANE, LANE), F32)],
        compiler_params=_params("parallel", "parallel", "arbitrary"),
        name="wkv7",
    )(r, k, v, kk, kka, lw, g, r_k, ln_w, ln_b, bd)


def _mem_kv_kernel(m_ref, g_ref, w_ref, o_ref):
    d = m_ref.shape[-1]
    mn = _rms(m_ref[...], g_ref[...], d).astype(BF16)
    o_ref[...] = _dot(mn, w_ref[0]).astype(BF16)


def _mem_kv(mem2d, g, wkv, tm):
    t, d = mem2d.shape
    nl, _, n = wkv.shape
    return pl.pallas_call(
        _mem_kv_kernel,
        grid=(nl, t // tm),
        in_specs=[pl.BlockSpec((tm, d), lambda l, i: (i, 0)), _full((1, d)),
                  pl.BlockSpec((1, d, n), lambda l, i: (l, 0, 0))],
        out_specs=pl.BlockSpec((tm, n), lambda l, i: (i, l)),
        out_shape=jax.ShapeDtypeStruct((t, nl * n), BF16),
        compiler_params=_params("parallel", "parallel"),
        name="mem_kv",
    )(mem2d, g, wkv)


def _mix_out_kernel(h_ref, att_ref, rw_ref, woa_ref, wob_ref, g_ref, wq_ref,
                    mk_ref, mv_ref, wo_ref, o_ref, *, q_scale):
    d = h_ref.shape[-1]
    h1 = h_ref[...] + _dot(att_ref[...], woa_ref[0]) + _dot(rw_ref[...], wob_ref[0])
    hn = _rms(h1, g_ref[0], d).astype(BF16)
    q = (_dot(hn, wq_ref[0]) * q_scale).astype(BF16)
    dh = d // CA_HEADS
    outs = []
    for hd in range(CA_HEADS):
        sl = slice(hd * dh, (hd + 1) * dh)
        s = _dot_nt(q[:, sl], mk_ref[:, sl])
        m = jnp.max(s, axis=-1, keepdims=True)
        p = jnp.exp2(s - m)
        l = jnp.sum(p, axis=-1, keepdims=True)
        outs.append((_dot(p.astype(BF16), mv_ref[:, sl]) / l).astype(BF16))
    o = jnp.concatenate(outs, axis=1)
    o_ref[...] = h1 + _dot(o, wo_ref[0])


def _mix_out(h, att, rw, woa, wob, g, wq, memkv, wo, layer, seq, mem_len, tm):
    t, d = h.shape
    tiles = seq // tm
    q_scale = (d // CA_HEADS) ** -0.5 * LOG2E
    row = lambda w: pl.BlockSpec((tm, w), lambda i: (i, 0))
    lay = lambda a: pl.BlockSpec((1,) + a.shape[1:], lambda i: (layer,) + (0,) * (a.ndim - 1))
    return pl.pallas_call(
        functools.partial(_mix_out_kernel, q_scale=q_scale),
        grid=(t // tm,),
        in_specs=[row(d), row(att.shape[1]), row(rw.shape[1]), lay(woa), lay(wob), lay(g),
                  lay(wq),
                  pl.BlockSpec((mem_len, d), lambda i: (i // tiles, 2 * layer)),
                  pl.BlockSpec((mem_len, d), lambda i: (i // tiles, 2 * layer + 1)),
                  lay(wo)],
        out_specs=row(d),
        out_shape=jax.ShapeDtypeStruct((t, d), F32),
        compiler_params=_params("parallel"),
        name="mix_out_cross_attention",
    )(h, att, rw, woa, wob, g, wq, memkv, memkv, wo)


def _mlp_kernel(h_ref, g_ref, wu_ref, wd_ref, gf_ref, o_ref, *, last, ff_chunk):
    d = h_ref.shape[-1]
    h = h_ref[...]
    hn = _rms(h, g_ref[0], d).astype(BF16)
    acc = h
    for c in range(wu_ref.shape[-1] // ff_chunk):
        sl = slice(c * ff_chunk, (c + 1) * ff_chunk)
        u = jnp.maximum(_dot(hn, wu_ref[0, :, sl]), 0.0)
        acc = acc + _dot((u * u).astype(BF16), wd_ref[0, sl, :])
    if last:
        acc = _rms(acc, gf_ref[...], d)
    o_ref[...] = acc


def _mlp(h, g, wu, wd, gf, layer, last, tm):
    t, d = h.shape
    row = pl.BlockSpec((tm, d), lambda i: (i, 0))
    lay = lambda a: pl.BlockSpec((1,) + a.shape[1:], lambda i: (layer,) + (0,) * (a.ndim - 1))
    return pl.pallas_call(
        functools.partial(_mlp_kernel, last=last, ff_chunk=1024),
        grid=(t // tm,),
        in_specs=[row, lay(g), lay(wu), lay(wd), _full((1, d))],
        out_specs=row,
        out_shape=jax.ShapeDtypeStruct((t, d), F32),
        compiler_params=_params("parallel"),
        name="mlp",
    )(h, g, wu, wd, gf)


def _pad_cols(a, n):
    return jnp.pad(a, ((0, 0), (0, n - a.shape[1])))


def _swap_halves(a):
    half = a.shape[-1] // 2
    return jnp.concatenate([a[..., half:], a[..., :half]], axis=-1)


def _in_weights(w_in, mu):
    cq = w_in[:, :MLA_Q_RANK]
    ckv = w_in[:, MLA_Q_RANK:MLA_Q_RANK + MLA_KV_RANK]
    kr = w_in[:, MLA_Q_RANK + MLA_KV_RANK:MLA_COLS]
    mla = jnp.concatenate([_pad_cols(cq, 256), ckv, kr, _swap_halves(kr)], axis=1)
    w = jnp.concatenate([_pad_cols(mla, MLA_C_PAD), _pad_cols(w_in[:, MLA_COLS:], RW_C_PAD)], axis=1)
    return w.astype(BF16), _pad_cols(mu[None, :], RW_C_PAD)


def _head_pad(a, n_heads, per_head, lo, hi, at):
    rows = a.shape[0]
    a = a.reshape(rows, n_heads, per_head)[:, :, lo:hi]
    a = jnp.pad(a, ((0, 0), (0, 0), (at, HEAD_PAD - at - (hi - lo))))
    return a.reshape(rows, n_heads * HEAD_PAD)


def _mla_weights(w_uq, w_ukv):
    dq = MLA_NOPE_DIM + MLA_ROPE_DIM
    h = MLA_HEADS
    wq3 = w_uq.reshape(MLA_Q_RANK, h, dq)
    rope_sw = _swap_halves(wq3[:, :, MLA_NOPE_DIM:]).reshape(MLA_Q_RANK, h * MLA_ROPE_DIM)
    wqa = _head_pad(w_uq, h, dq, 0, dq, 0)
    wqb = _head_pad(rope_sw, h, MLA_ROPE_DIM, 0, MLA_ROPE_DIM, MLA_NOPE_DIM)
    pad_q = ((0, 256 - MLA_Q_RANK), (0, 0))
    wqa = jnp.pad(wqa, pad_q)
    wqb = jnp.pad(wqb, pad_q)
    dkv = MLA_NOPE_DIM + MLA_V_DIM
    wk = _head_pad(w_ukv, h, dkv, 0, MLA_NOPE_DIM, 0)
    wv = _head_pad(w_ukv, h, dkv, MLA_NOPE_DIM, dkv, 0)
    eye = jnp.eye(MLA_ROPE_DIM, dtype=F32)
    place = _head_pad(jnp.tile(eye, (1, h)), h, MLA_ROPE_DIM, 0, MLA_ROPE_DIM, MLA_NOPE_DIM)
    zeros = jnp.zeros_like(place)
    pad_rows = jnp.zeros((LANE - 2 * MLA_ROPE_DIM, h * HEAD_PAD), F32)
    pa = jnp.concatenate([place, zeros, pad_rows], axis=0)
    pb = jnp.concatenate([zeros, place, pad_rows], axis=0)
    wka = jnp.concatenate([wk, pa], axis=0)
    return tuple(a.astype(BF16) for a in (wqa, wqb, wka, pb, wv))


def _lora_weights(w2, a2, g2, v2):
    rw = RWKV_WIDTH
    z = lambda r: jnp.zeros((r, rw), F32)
    rows = [
        jnp.concatenate([w2, z(DECAY_RANK), z(DECAY_RANK), z(DECAY_RANK)], axis=1),
        jnp.concatenate([z(ICL_RANK), a2, z(ICL_RANK), z(ICL_RANK)], axis=1),
        jnp.concatenate([z(GATE_RANK), z(GATE_RANK), g2, z(GATE_RANK)], axis=1),
        jnp.concatenate([z(VRES_RANK)] * 3 + [v2 if v2 is not None else z(VRES_RANK)], axis=1),
    ]
    w = jnp.concatenate(rows, axis=0)
    return jnp.pad(w, ((0, LORA_PAD - w.shape[0]), (0, 0))).astype(BF16)


def _block_diag(n, blk, value):
    i = jnp.arange(n)
    return jnp.where((i[:, None] // blk) == (i[None, :] // blk), value, 0.0)


def kernel(x, mem, positions, mix_norm, w_in_first, w_in_rest, shift_mu_first, shift_mu_rest, mla_q_norm, mla_w_uq, mla_kv_norm, mla_w_ukv, mla_out_norm, rwkv_w0, rwkv_w2, rwkv_a0, rwkv_a2, rwkv_g2, rwkv_v0, rwkv_v2, rwkv_k_k, rwkv_k_a, rwkv_r_k, rwkv_ln_w, rwkv_ln_b, w_out, ca_norm, mem_norm, ca_wq, ca_wkv, ca_wo, mlp_norm, mlp_w_up, mlp_w_down, final_norm):
    b, s, d = x.shape
    depth = mix_norm.shape[0]
    mem_len = mem.shape[1]
    t = b * s
    tm = min(512, s)
    tq = min(256, s)
    wkv_rows = min(256, s)
    rw = RWKV_WIDTH

    inv = ROPE_THETA ** (-jnp.arange(0, MLA_ROPE_DIM, 2, dtype=F32) / MLA_ROPE_DIM)
    inv_row = jnp.concatenate([jnp.zeros((MLA_NOPE_DIM,), F32), inv, inv,
                               jnp.zeros((HEAD_PAD - MLA_NOPE_DIM - MLA_ROPE_DIM,), F32)])[None, :]
    ct, st = _rope_tables(positions.astype(F32).reshape(t, 1), inv_row, tm)

    memkv = _mem_kv(mem.reshape(b * mem_len, d), mem_norm[None, :], ca_wkv.astype(BF16),
                    min(512, b * mem_len))

    woa = jnp.pad(w_out[:, :MLA_HEADS * MLA_V_DIM].reshape(depth, MLA_HEADS, MLA_V_DIM, d),
                  ((0, 0), (0, 0), (0, HEAD_PAD - MLA_V_DIM), (0, 0)))
    woa = woa.reshape(depth, MLA_HEADS * HEAD_PAD, d).astype(BF16)
    wob = w_out[:, MLA_HEADS * MLA_V_DIM:].astype(BF16)
    wq_b = ca_wq.astype(BF16)
    wo_b = ca_wo.astype(BF16)
    wu_b = mlp_w_up.astype(BF16)
    wd_b = mlp_w_down.astype(BF16)
    ca_g = ca_norm[:, None, :]
    mlp_g = mlp_norm[:, None, :]
    bd512 = _block_diag(rw, RWKV_HEAD, 1.0).astype(BF16)
    bd128 = _block_diag(LANE, RWKV_HEAD, 1.0 / RWKV_HEAD).astype(F32)

    h = x.reshape(t, d)
    v_first = None
    for l in range(depth):
        if l == 0:
            w_in, mu, v0, v2 = w_in_first, shift_mu_first, None, None
        else:
            w_in, mu, v0, v2 = w_in_rest[l - 1], shift_mu_rest[l - 1], rwkv_v0[l - 1], rwkv_v2[l - 1]
        w_all, mu_p = _in_weights(w_in, mu)
        mla_c, rw_c = _in_proj(h, mix_norm[l][None, :], w_all, mu_p, s, tm)

        wqa, wqb, wka, pb, wv = _mla_weights(mla_w_uq[l], mla_w_ukv[l])
        gq = _pad_cols(mla_q_norm[l][None, :], 256)
        q, k, v = _mla_prep(mla_c, ct, st, gq, mla_kv_norm[l][None, :], wqa, wqb, wka, pb, wv, tm)
        hw = MLA_HEADS * HEAD_PAD
        g_out = _head_pad(mla_out_norm[l][None, :], MLA_HEADS, MLA_V_DIM, 0, MLA_V_DIM, 0)
        att = _mla_attention(q.reshape(b, s, hw), k.reshape(b, s, hw), v.reshape(b, s, hw),
                             g_out, tq)

        w2c = _lora_weights(rwkv_w2[l], rwkv_a2[l], rwkv_g2[l], v2)
        vec = lambda a: a.reshape(1, rw)
        r_, k_, v_, kk_, kka_, lw_, g_ = _rwkv_prep(
            rw_c, v_first, w2c, bd512, vec(rwkv_w0[l]), vec(rwkv_a0[l]), vec(rwkv_k_k[l]),
            vec(rwkv_k_a[l]), None if v0 is None else vec(v0), tm)
        if l == 0:
            v_first = v_
        sh = lambda a: a.reshape(b, s, rw)
        y = _wkv(sh(r_), sh(k_), sh(v_), sh(kk_), sh(kka_), sh(lw_), sh(g_),
                 vec(rwkv_r_k[l]), vec(rwkv_ln_w[l]), vec(rwkv_ln_b[l]), bd128,
                 wkv_rows, HI, HI)

        h = _mix_out(h, att.reshape(t, hw), y.reshape(t, rw), woa, wob, ca_g, wq_b, memkv, wo_b,
                     l, s, mem_len, tm)
        h = _mlp(h, mlp_g, wu_b, wd_b, final_norm[None, :], l, l == depth - 1, tm)
    return h.reshape(b, s, d)
```

```python
import functools
import math

import jax
import jax.numpy as jnp
from jax import lax
from jax.experimental import pallas as pl
from jax.experimental.pallas import tpu as pltpu

F32 = jnp.float32
BF16 = jnp.bfloat16

CHUNK = 64
NORM_EPS = 1e-6
MLA_HEADS = 8
MLA_V_DIM = 64
MLA_NOPE_DIM = 64
MLA_ROPE_DIM = 32
MLA_Q_RANK = 192
MLA_KV_RANK = 128
ROPE_THETA = 10000.0
RWKV_HEAD = 64
RWKV_HEADS = 8
RWKV_WIDTH = 512
DECAY_RANK = 32
ICL_RANK = 32
GATE_RANK = 96
VRES_RANK = 32
GN_EPS = 64e-5
CA_HEADS = 4
MLA_COLS = MLA_Q_RANK + MLA_KV_RANK + MLA_ROPE_DIM

LANE = 128
HEAD_PAD = 128
MLA_C_PAD = 512
RW_C_PAD = 1792
LORA_PAD = 256
WKV_CHUNK = 64
ATTN_HEADS_PER_STEP = 4
LOG2E = 1.4426950408889634
NEG_BIG = -1e30
VMEM_LIMIT = 56 * 1024 * 1024

HI = lax.Precision.HIGHEST
WKV_MODES = ("bf16", "bf16", "bf16", "bf16")


def _dot(a, b, precision=None):
    return jnp.dot(a, b, preferred_element_type=F32, precision=precision)


def _dot_nt(a, b, precision=None):
    return lax.dot_general(a, b, (((1,), (1,)), ((), ())),
                           preferred_element_type=F32, precision=precision)


def _dot_tn(a, b, precision=None):
    return lax.dot_general(a, b, (((0,), (0,)), ((), ())),
                           preferred_element_type=F32, precision=precision)


def _rms(x, g, n, eps=NORM_EPS):
    ms = jnp.sum(x * x, axis=-1, keepdims=True) * (1.0 / n)
    return x * lax.rsqrt(ms + eps) * g


def _sigmoid(x):
    return 1.0 / (1.0 + jnp.exp(-x))


def _params(*sem):
    return pltpu.CompilerParams(dimension_semantics=sem, vmem_limit_bytes=VMEM_LIMIT)


def _full(shape):
    nd = len(shape)
    return pl.BlockSpec(shape, lambda *_: (0,) * nd)


def _rope_kernel(pos_ref, inv_ref, ct_ref, st_ref):
    ang = pos_ref[...] * inv_ref[...]
    c = jnp.cos(ang)
    s = jnp.sin(ang)
    lane = lax.broadcasted_iota(jnp.int32, ang.shape, 1)
    half = MLA_NOPE_DIM + MLA_ROPE_DIM // 2
    end = MLA_NOPE_DIM + MLA_ROPE_DIM
    ct_ref[...] = jnp.where(lane < MLA_NOPE_DIM, 1.0, jnp.where(lane < end, c, 0.0))
    st_ref[...] = jnp.where(lane < MLA_NOPE_DIM, 0.0,
                            jnp.where(lane < half, -s, jnp.where(lane < end, s, 0.0)))


def _rope_tables(pos_f, inv_row, tm):
    t = pos_f.shape[0]
    return pl.pallas_call(
        _rope_kernel,
        grid=(t // tm,),
        in_specs=[pl.BlockSpec((tm, 1), lambda i: (i, 0)), _full((1, LANE))],
        out_specs=[pl.BlockSpec((tm, LANE), lambda i: (i, 0))] * 2,
        out_shape=[jax.ShapeDtypeStruct((t, LANE), F32)] * 2,
        compiler_params=_params("parallel"),
        name="rope_tables",
    )(pos_f, inv_row)


def _in_proj_kernel(h_ref, g_ref, w_ref, mu_ref, mla_ref, rw_ref, carry_ref, *, tiles_per_seq):
    i = pl.program_id(0)
    d = h_ref.shape[-1]
    hn = _rms(h_ref[...], g_ref[...], d).astype(BF16)
    proj = _dot(hn, w_ref[...])
    mla_ref[...] = proj[:, :MLA_C_PAD].astype(BF16)
    z = proj[:, MLA_C_PAD:]
    tm = z.shape[0]

    @pl.when(i % tiles_per_seq == 0)
    def _():
        carry_ref[...] = jnp.zeros_like(carry_ref)

    prev = pltpu.roll(z, 1, axis=0)
    row = lax.broadcasted_iota(jnp.int32, z.shape, 0)
    prev = jnp.where(row == 0, carry_ref[7:8, :], prev)
    rw_ref[...] = (z + (prev - z) * mu_ref[...]).astype(BF16)
    carry_ref[...] = z[tm - 8:, :]


def _in_proj(h, g, w, mu, seq, tm):
    t, d = h.shape
    n = w.shape[1]
    kern = functools.partial(_in_proj_kernel, tiles_per_seq=seq // tm)
    return pl.pallas_call(
        kern,
        grid=(t // tm,),
        in_specs=[pl.BlockSpec((tm, d), lambda i: (i, 0)), _full((1, d)),
                  _full((d, n)), _full((1, RW_C_PAD))],
        out_specs=[pl.BlockSpec((tm, MLA_C_PAD), lambda i: (i, 0)),
                   pl.BlockSpec((tm, RW_C_PAD), lambda i: (i, 0))],
        out_shape=[jax.ShapeDtypeStruct((t, MLA_C_PAD), BF16),
                   jax.ShapeDtypeStruct((t, RW_C_PAD), BF16)],
        scratch_shapes=[pltpu.VMEM((8, RW_C_PAD), F32)],
        compiler_params=_params("arbitrary"),
        name="in_proj",
    )(h, g, w, mu)


def _mla_prep_kernel(c_ref, ct_ref, st_ref, gq_ref, gkv_ref, wqa_ref, wqb_ref,
                     wka_ref, pb_ref, wv_ref, q_ref, k_ref, v_ref, *, q_scale):
    c = c_ref[...].astype(F32)
    nq = _rms(c[:, :256], gq_ref[...], MLA_Q_RANK).astype(BF16)
    nkv = _rms(c[:, 256:384], gkv_ref[...], MLA_KV_RANK).astype(BF16)
    kr = c_ref[:, 384:512]
    ct = jnp.tile(ct_ref[...], (1, MLA_HEADS))
    st = jnp.tile(st_ref[...], (1, MLA_HEADS))
    q = _dot(nq, wqa_ref[...]) * ct + _dot(nq, wqb_ref[...]) * st
    q_ref[...] = (q * q_scale).astype(BF16)
    x = jnp.concatenate([nkv, kr], axis=1)
    k = _dot(x, wka_ref[...]) * ct + _dot(kr, pb_ref[...]) * st
    k_ref[...] = k.astype(BF16)
    v_ref[...] = _dot_nt(wv_ref[...], nkv).astype(BF16)


def _mla_prep(c, ct, st, gq, gkv, wqa, wqb, wka, pb, wvt, tm):
    t = c.shape[0]
    hw = MLA_HEADS * HEAD_PAD
    vw = wvt.shape[0]
    q_scale = (MLA_NOPE_DIM + MLA_ROPE_DIM) ** -0.5 * LOG2E
    row = lambda w: pl.BlockSpec((tm, w), lambda i: (i, 0))
    return pl.pallas_call(
        functools.partial(_mla_prep_kernel, q_scale=q_scale),
        grid=(t // tm,),
        in_specs=[row(MLA_C_PAD), row(LANE), row(LANE), _full(gq.shape), _full(gkv.shape),
                  _full(wqa.shape), _full(wqb.shape), _full(wka.shape), _full(pb.shape),
                  _full(wvt.shape)],
        out_specs=[row(hw), row(hw), pl.BlockSpec((vw, tm), lambda i: (0, i))],
        out_shape=[jax.ShapeDtypeStruct((t, hw), BF16), jax.ShapeDtypeStruct((t, hw), BF16),
                   jax.ShapeDtypeStruct((vw, t), BF16)],
        compiler_params=_params("parallel"),
        name="mla_prep",
    )(c, ct, st, gq, gkv, wqa, wqb, wka, pb, wvt)


def _attn_kernel(q_ref, k_ref, vt_ref, g_ref, o_ref, *, tq, nh):
    qi = pl.program_id(2)
    dv = MLA_V_DIM
    heads = range(nh)
    q = [q_ref[0, :, h * HEAD_PAD:(h + 1) * HEAD_PAD] for h in heads]

    def step(j, carry, masked):
        m, l, acc = carry
        off = pl.multiple_of(j * tq, tq)
        s = [_dot_nt(k_ref[0, pl.ds(off, tq), h * HEAD_PAD:(h + 1) * HEAD_PAD], q[h])
             for h in heads]
        if masked:
            ck = lax.broadcasted_iota(jnp.int32, (tq, tq), 0) // CHUNK
            cq = lax.broadcasted_iota(jnp.int32, (tq, tq), 1) // CHUNK
            s = [jnp.where(ck <= cq, s[h], NEG_BIG) for h in heads]
        m_new = [jnp.maximum(m[h], jnp.max(s[h], axis=0, keepdims=True)) for h in heads]
        p = [jnp.exp2(s[h] - m_new[h]) for h in heads]
        alpha = [jnp.exp2(m[h] - m_new[h]) for h in heads]
        l = [alpha[h] * l[h] + jnp.sum(p[h], axis=0, keepdims=True) for h in heads]
        pv = [_dot(vt_ref[h * dv:(h + 1) * dv, pl.ds(off, tq)], p[h].astype(BF16))
              for h in heads]
        acc = [alpha[h] * acc[h] + pv[h] for h in heads]
        return m_new, l, acc

    init = ([jnp.full((1, tq), NEG_BIG, F32)] * nh, [jnp.zeros((1, tq), F32)] * nh,
            [jnp.zeros((dv, tq), F32)] * nh)
    carry = lax.fori_loop(0, qi, lambda j, c: step(j, c, False), init)
    m, l, acc = step(qi, carry, True)
    outs = []
    for h in heads:
        o = acc[h] / l[h]
        ms = jnp.sum(o * o, axis=0, keepdims=True) * (1.0 / dv)
        outs.append(o * lax.rsqrt(ms + NORM_EPS))
    o_t = jnp.concatenate(outs, axis=0)
    o_ref[0] = (o_t.T * g_ref[...]).astype(BF16)


def _mla_attention(q, k, vt, g, seq, tq, nh):
    b = q.shape[0] // seq
    hw = q.shape[1]
    groups = hw // (nh * HEAD_PAD)
    ow = nh * MLA_V_DIM
    q3 = q.reshape(b, seq, hw)
    k3 = k.reshape(b, seq, hw)
    return pl.pallas_call(
        functools.partial(_attn_kernel, tq=tq, nh=nh),
        grid=(b, groups, seq // tq),
        in_specs=[pl.BlockSpec((1, tq, nh * HEAD_PAD), lambda bi, h, i: (bi, i, h)),
                  pl.BlockSpec((1, seq, nh * HEAD_PAD), lambda bi, h, i: (bi, 0, h)),
                  pl.BlockSpec((ow, seq), lambda bi, h, i: (h, bi)),
                  pl.BlockSpec((1, ow), lambda bi, h, i: (0, h))],
        out_specs=pl.BlockSpec((1, tq, ow), lambda bi, h, i: (bi, i, h)),
        out_shape=jax.ShapeDtypeStruct((b, seq, groups * ow), BF16),
        compiler_params=_params("parallel", "parallel", "arbitrary"),
        name="mla_attention",
    )(q3, k3, vt, g)


def _rwkv_prep_kernel(*refs, first):
    if first:
        (x_ref, w2_ref, bd_ref, w0_ref, a0_ref, kk_ref_p, ka_ref,
         r_o, k_o, v_o, kk_o, kka_o, lw_o, g_o) = refs
        vf_ref = v0_ref = None
    else:
        (x_ref, vf_ref, w2_ref, bd_ref, w0_ref, a0_ref, kk_ref_p, ka_ref, v0_ref,
         r_o, k_o, v_o, kk_o, kka_o, lw_o, g_o) = refs
    rw = RWKV_WIDTH
    r_o[...] = x_ref[:, 0:rw]
    k = x_ref[:, rw:2 * rw].astype(F32)
    v = x_ref[:, 2 * rw:3 * rw].astype(F32)
    tail = x_ref[:, 3 * rw:3 * rw + LORA_PAD].astype(F32)
    lane = lax.broadcasted_iota(jnp.int32, tail.shape, 1)
    g_lo = DECAY_RANK + ICL_RANK
    g_hi = g_lo + GATE_RANK
    act = jnp.where(lane < DECAY_RANK, jnp.tanh(tail),
                    jnp.where((lane >= g_lo) & (lane < g_hi), _sigmoid(tail), tail))
    lo = _dot(act.astype(BF16), w2_ref[...])
    z = -(w0_ref[...] + lo[:, 0:rw])
    softplus = jnp.maximum(z, 0.0) + jnp.log(1.0 + jnp.exp(-jnp.abs(z)))
    lw_o[...] = -jnp.exp(-softplus - 0.5)
    a = _sigmoid(a0_ref[...] + lo[:, rw:2 * rw])
    g_o[...] = lo[:, 2 * rw:3 * rw].astype(BF16)
    if not first:
        mix = _sigmoid(v0_ref[...] + lo[:, 3 * rw:4 * rw])
        v = v + (vf_ref[...].astype(F32) - v) * mix
    v_o[...] = v.astype(BF16)
    kk = k * kk_ref_p[...]
    sq = kk * kk
    sq_hi = sq.astype(BF16)
    sq_lo = (sq - sq_hi.astype(F32)).astype(BF16)
    ss = _dot(sq_hi, bd_ref[...]) + _dot(sq_lo, bd_ref[...])
    kk = kk / jnp.maximum(jnp.sqrt(ss), 1e-12)
    kk_o[...] = kk.astype(BF16)
    kka_o[...] = (kk * a).astype(BF16)
    k_o[...] = (k * (1.0 + (a - 1.0) * ka_ref[...])).astype(BF16)


def _rwkv_prep(x, vf, w2, bd, w0, a0, k_k, k_a, v0, tm):
    t = x.shape[0]
    rw = RWKV_WIDTH
    first = vf is None
    row = lambda w: pl.BlockSpec((tm, w), lambda i: (i, 0))
    vec = _full((1, rw))
    if first:
        args = (x, w2, bd, w0, a0, k_k, k_a)
        in_specs = [row(RW_C_PAD), _full(w2.shape), _full(bd.shape), vec, vec, vec, vec]
    else:
        args = (x, vf, w2, bd, w0, a0, k_k, k_a, v0)
        in_specs = [row(RW_C_PAD), row(rw), _full(w2.shape), _full(bd.shape),
                    vec, vec, vec, vec, vec]
    out_dt = [BF16, BF16, BF16, BF16, BF16, F32, BF16]
    return pl.pallas_call(
        functools.partial(_rwkv_prep_kernel, first=first),
        grid=(t // tm,),
        in_specs=in_specs,
        out_specs=[row(rw)] * 7,
        out_shape=[jax.ShapeDtypeStruct((t, rw), dt) for dt in out_dt],
        compiler_params=_params("parallel"),
        name="rwkv_prep",
    )(*args)


_DIMS = {"nn": (((1,), (0,)), ((), ())), "nt": (((1,), (1,)), ((), ())),
         "tn": (((0,), (0,)), ((), ()))}


def _split(x):
    hi = x.astype(BF16)
    return hi, (x - hi.astype(F32)).astype(BF16)


def _mm(a, b, mode, dims="nn"):
    dg = lambda x, y: lax.dot_general(x, y, _DIMS[dims], preferred_element_type=F32)
    if mode == "bf16":
        return dg(a.astype(BF16), b.astype(BF16))
    if mode == "x3":
        a_hi, a_lo = _split(a)
        b_hi, b_lo = _split(b)
        return dg(a_hi, b_hi) + (dg(a_lo, b_hi) + dg(a_hi, b_lo))
    if mode == "ax":
        a_hi = a.astype(BF16)
        b_hi, b_r = _split(b)
        b_mid, b_lo = _split(b - b_hi.astype(F32))
        del b_r
        return dg(a_hi, b_hi) + (dg(a_hi, b_mid) + dg(a_hi, b_lo))
    return lax.dot_general(a, b, _DIMS[dims], preferred_element_type=F32, precision=HI)


def _unit_lower_solve(a, rhs, row, col, eye, mode):
    return _unit_lower_solve_many([a], [rhs], row, col, eye, mode)[0]


def _unit_lower_solve_many(a_list, rhs_list, row, col, eye, mode):
    d16 = (row // 16) == (col // 16)
    d32 = (row // 32) == (col // 32)
    n = range(len(a_list))
    d = [jnp.where(d16, a, 0.0) for a in a_list]
    x = [eye + d[i] for i in n]
    d2 = [_mm(d[i], d[i], mode) for i in n]
    t = [_mm(d2[i], jnp.concatenate([x[i], d2[i]], axis=1), mode) for i in n]
    x = [x[i] + t[i][:, :128] for i in n]
    d4 = [t[i][:, 128:] for i in n]
    t = [_mm(d4[i], jnp.concatenate([x[i], d4[i]], axis=1), mode) for i in n]
    x = [x[i] + t[i][:, :128] for i in n]
    d8 = [t[i][:, 128:] for i in n]
    t = [_mm(d8[i], x[i], mode) for i in n]
    x = [x[i] + t[i] for i in n]
    e = [jnp.where(d32, a_list[i], 0.0) - d[i] for i in n]
    t = [_mm(e[i], x[i], mode) for i in n]
    t = [_mm(x[i], t[i], mode) for i in n]
    x = [x[i] + t[i] for i in n]
    f = [jnp.where(d32, 0.0, a_list[i]) for i in n]
    z = [_mm(x[i], rhs_list[i], mode) for i in n]
    t = [_mm(f[i], z[i], mode) for i in n]
    t = [_mm(x[i], t[i], mode) for i in n]
    return [z[i] + t[i] for i in n]


def _wkv_kernel(r_ref, k_ref, v_ref, kk_ref, kka_ref, lw_ref, g_ref,
                rk_ref, lnw_ref, lnb_ref, bd_ref, o_ref, s_ref, *, nch, modes):
    m_quad, m_inv, m_state, m_post = modes
    c_len = WKV_CHUNK

    npair = s_ref.shape[0]

    @pl.when(pl.program_id(1) == 0)
    def _():
        s_ref[...] = jnp.zeros_like(s_ref)

    row = lax.broadcasted_iota(jnp.int32, (128, 128), 0)
    col = lax.broadcasted_iota(jnp.int32, (128, 128), 1)
    same = (row // c_len) == (col // c_len)
    mask_s = same & (col < row)
    mask_i = same & (col <= row)
    eye = jnp.where(row == col, 1.0, 0.0)
    lane0 = lax.broadcasted_iota(jnp.int32, (c_len, 128), 1) < RWKV_HEAD
    tr = lax.broadcasted_iota(jnp.int32, (c_len, c_len), 0)
    tc = lax.broadcasted_iota(jnp.int32, (c_len, c_len), 1)
    ltri = jnp.where(tc <= tr, 1.0, 0.0)

    def stack(x):
        return jnp.concatenate([jnp.where(lane0, x, 0.0), jnp.where(lane0, 0.0, x)], axis=0)

    bd = bd_ref[...]
    par = []
    for c, hp in [(c, hp) for c in range(nch) for hp in range(npair)]:
        sl = pl.ds(c * c_len, c_len)
        ln = slice(hp * LANE, (hp + 1) * LANE)
        r = r_ref[0, sl, ln].astype(F32)
        k = k_ref[0, sl, ln].astype(F32)
        v = v_ref[0, sl, ln].astype(F32)
        kk = kk_ref[0, sl, ln].astype(F32)
        kka = kka_ref[0, sl, ln].astype(F32)
        lw = lw_ref[0, sl, ln]
        cum = _mm(ltri, lw, "ax")
        tot = cum[c_len - 1:c_len, :]
        p_in = jnp.exp(cum)
        p_ex = jnp.exp(cum - lw)
        p_inv = jnp.exp(-cum)
        p_end = jnp.exp(tot - cum)
        at = stack(-kk * p_ex)
        rt = stack(r * p_in)
        bt = kka * p_inv
        kt = k * p_inv
        par.append(dict(
            r=r, k=k, v=v, tot=tot, at=at, rt=rt, vs=stack(v),
            bkh=jnp.concatenate([stack(kka * p_end), stack(k * p_end)], axis=0),
            lhs=jnp.concatenate([at, rt], axis=0),
            rhs=jnp.concatenate([bt, bt, kt, kt], axis=0)))
    for p in par:
        quad = _mm(p["lhs"], p["rhs"], m_quad, "nt")
        p["a_ab"] = jnp.where(mask_s, quad[:128, :128], 0.0)
        p["a_ak"] = jnp.where(mask_s, quad[:128, 128:], 0.0)
        p["a_r"] = jnp.concatenate([jnp.where(mask_i, quad[128:, :128], 0.0),
                                    jnp.where(mask_i, quad[128:, 128:], 0.0)], axis=1)
    for p in par:
        p["akv"] = _mm(p["a_ak"], p["vs"], m_quad)
    wus = _unit_lower_solve_many(
        [p["a_ab"] for p in par],
        [jnp.concatenate([p["at"], p["akv"]], axis=1) for p in par], row, col, eye, m_inv)

    pairs = range(npair)
    state = [s_ref[hp] for hp in pairs]
    for c in range(nch):
        sl = pl.ds(c * c_len, c_len)
        pc = [par[c * npair + hp] for hp in pairs]
        wu = [wus[c * npair + hp] for hp in pairs]
        m1 = [_mm(jnp.concatenate([wu[hp][:, :128], pc[hp]["rt"]], axis=0), state[hp],
                  m_state, "nt") for hp in pairs]
        uv = [jnp.concatenate([m1[hp][:128] + wu[hp][:, 128:], pc[hp]["vs"]], axis=0)
              for hp in pairs]
        upd = [_mm(uv[hp], pc[hp]["bkh"], m_state, "tn") for hp in pairs]
        state = [state[hp] * jnp.exp(pc[hp]["tot"]) + upd[hp] for hp in pairs]
        ys = [m1[hp][128:] + _mm(pc[hp]["a_r"], uv[hp], m_state) for hp in pairs]
        y = [ys[hp][:c_len] + ys[hp][c_len:] for hp in pairs]
        mean = [_mm(y[hp], bd, m_post) for hp in pairs]
        yc = [y[hp] - mean[hp] for hp in pairs]
        var = [_mm(yc[hp] * yc[hp], bd, m_post) for hp in pairs]
        for hp in pairs:
            ln = slice(hp * LANE, (hp + 1) * LANE)
            r, k, v = pc[hp]["r"], pc[hp]["k"], pc[hp]["v"]
            yn = yc[hp] * lax.rsqrt(var[hp] + GN_EPS) * lnw_ref[:, ln] + lnb_ref[:, ln]
            bonus = _mm(r * k * rk_ref[:, ln], bd, m_post) * float(RWKV_HEAD)
            yn = yn + bonus * v
            o_ref[0, sl, ln] = (yn * g_ref[0, sl, ln].astype(F32)).astype(BF16)
    for hp in pairs:
        s_ref[hp] = state[hp]


def _wkv(r, k, v, kk, kka, lw, g, r_k, ln_w, ln_b, bd, rows, modes):
    b, s, w = r.shape
    blk = pl.BlockSpec((1, rows, w), lambda bi, i: (bi, i, 0))
    kern = functools.partial(_wkv_kernel, nch=rows // WKV_CHUNK, modes=modes)
    return pl.pallas_call(
        kern,
        grid=(b, s // rows),
        in_specs=[blk] * 7 + [_full((1, w))] * 3 + [_full((LANE, LANE))],
        out_specs=blk,
        out_shape=jax.ShapeDtypeStruct((b, s, w), BF16),
        scratch_shapes=[pltpu.VMEM((w // LANE, LANE, LANE), F32)],
        compiler_params=_params("parallel", "arbitrary"),
        name="wkv7",
    )(r, k, v, kk, kka, lw, g, r_k, ln_w, ln_b, bd)


def _mem_kv_kernel(m_ref, g_ref, w_ref, o_ref):
    d = m_ref.shape[-1]
    mn = _rms(m_ref[...], g_ref[...], d).astype(BF16)
    o_ref[...] = _dot(mn, w_ref[0]).astype(BF16)


def _mem_kv(mem2d, g, wkv, tm):
    t, d = mem2d.shape
    nl, _, n = wkv.shape
    return pl.pallas_call(
        _mem_kv_kernel,
        grid=(nl, t // tm),
        in_specs=[pl.BlockSpec((tm, d), lambda l, i: (i, 0)), _full((1, d)),
                  pl.BlockSpec((1, d, n), lambda l, i: (l, 0, 0))],
        out_specs=pl.BlockSpec((tm, n), lambda l, i: (i, l)),
        out_shape=jax.ShapeDtypeStruct((t, nl * n), BF16),
        compiler_params=_params("parallel", "parallel"),
        name="mem_kv",
    )(mem2d, g, wkv)


def _mix_out_kernel(h_ref, att_ref, rw_ref, woa_ref, wob_ref, g_ref, wq_ref,
                    mk_ref, mv_ref, wo_ref, o_ref, *, q_scale):
    d = h_ref.shape[-1]
    h1 = h_ref[...] + _dot(att_ref[...], woa_ref[0]) + _dot(rw_ref[...], wob_ref[0])
    hn = _rms(h1, g_ref[0], d).astype(BF16)
    q = (_dot(hn, wq_ref[0]) * q_scale).astype(BF16)
    dh = d // CA_HEADS
    outs = []
    for hd in range(CA_HEADS):
        sl = slice(hd * dh, (hd + 1) * dh)
        s = _dot_nt(q[:, sl], mk_ref[:, sl])
        m = jnp.max(s, axis=-1, keepdims=True)
        p = jnp.exp2(s - m)
        l = jnp.sum(p, axis=-1, keepdims=True)
        outs.append((_dot(p.astype(BF16), mv_ref[:, sl]) / l).astype(BF16))
    o = jnp.concatenate(outs, axis=1)
    o_ref[...] = h1 + _dot(o, wo_ref[0])


def _mix_out(h, att, rw, woa, wob, g, wq, memkv, wo, layer, seq, mem_len, tm):
    t, d = h.shape
    tiles = seq // tm
    q_scale = (d // CA_HEADS) ** -0.5 * LOG2E
    row = lambda w: pl.BlockSpec((tm, w), lambda i: (i, 0))
    lay = lambda a: pl.BlockSpec((1,) + a.shape[1:], lambda i: (layer,) + (0,) * (a.ndim - 1))
    return pl.pallas_call(
        functools.partial(_mix_out_kernel, q_scale=q_scale),
        grid=(t // tm,),
        in_specs=[row(d), row(att.shape[1]), row(rw.shape[1]), lay(woa), lay(wob), lay(g),
                  lay(wq),
                  pl.BlockSpec((mem_len, d), lambda i: (i // tiles, 2 * layer)),
                  pl.BlockSpec((mem_len, d), lambda i: (i // tiles, 2 * layer + 1)),
                  lay(wo)],
        out_specs=row(d),
        out_shape=jax.ShapeDtypeStruct((t, d), F32),
        compiler_params=_params("parallel"),
        name="mix_out_cross_attention",
    )(h, att, rw, woa, wob, g, wq, memkv, memkv, wo)


def _mlp_kernel(h_ref, g_ref, wu_ref, wd_ref, gf_ref, o_ref, *, last, ff_chunk):
    d = h_ref.shape[-1]
    h = h_ref[...]
    hn = _rms(h, g_ref[0], d).astype(BF16)
    acc = h
    for c in range(wu_ref.shape[-1] // ff_chunk):
        sl = slice(c * ff_chunk, (c + 1) * ff_chunk)
        u = jnp.maximum(_dot(hn, wu_ref[0, :, sl]), 0.0)
        acc = acc + _dot((u * u).astype(BF16), wd_ref[0, sl, :])
    if last:
        acc = _rms(acc, gf_ref[...], d)
    o_ref[...] = acc


def _mlp(h, g, wu, wd, gf, layer, last, tm):
    t, d = h.shape
    row = pl.BlockSpec((tm, d), lambda i: (i, 0))
    lay = lambda a: pl.BlockSpec((1,) + a.shape[1:], lambda i: (layer,) + (0,) * (a.ndim - 1))
    return pl.pallas_call(
        functools.partial(_mlp_kernel, last=last, ff_chunk=1024),
        grid=(t // tm,),
        in_specs=[row, lay(g), lay(wu), lay(wd), _full((1, d))],
        out_specs=row,
        out_shape=jax.ShapeDtypeStruct((t, d), F32),
        compiler_params=_params("parallel"),
        name="mlp",
    )(h, g, wu, wd, gf)


def _pad_cols(a, n):
    return jnp.pad(a, ((0, 0), (0, n - a.shape[1])))


def _swap_halves(a):
    half = a.shape[-1] // 2
    return jnp.concatenate([a[..., half:], a[..., :half]], axis=-1)


def _in_weights(w_in, mu):
    cq = w_in[:, :MLA_Q_RANK]
    ckv = w_in[:, MLA_Q_RANK:MLA_Q_RANK + MLA_KV_RANK]
    kr = w_in[:, MLA_Q_RANK + MLA_KV_RANK:MLA_COLS]
    mla = jnp.concatenate([_pad_cols(cq, 256), ckv, kr, _swap_halves(kr)], axis=1)
    w = jnp.concatenate([_pad_cols(mla, MLA_C_PAD), _pad_cols(w_in[:, MLA_COLS:], RW_C_PAD)], axis=1)
    return w.astype(BF16), _pad_cols(mu[None, :], RW_C_PAD)


def _head_pad(a, n_heads, per_head, lo, hi, at):
    rows = a.shape[0]
    a = a.reshape(rows, n_heads, per_head)[:, :, lo:hi]
    a = jnp.pad(a, ((0, 0), (0, 0), (at, HEAD_PAD - at - (hi - lo))))
    return a.reshape(rows, n_heads * HEAD_PAD)


def _mla_weights(w_uq, w_ukv):
    dq = MLA_NOPE_DIM + MLA_ROPE_DIM
    h = MLA_HEADS
    wq3 = w_uq.reshape(MLA_Q_RANK, h, dq)
    rope_sw = _swap_halves(wq3[:, :, MLA_NOPE_DIM:]).reshape(MLA_Q_RANK, h * MLA_ROPE_DIM)
    wqa = _head_pad(w_uq, h, dq, 0, dq, 0)
    wqb = _head_pad(rope_sw, h, MLA_ROPE_DIM, 0, MLA_ROPE_DIM, MLA_NOPE_DIM)
    pad_q = ((0, 256 - MLA_Q_RANK), (0, 0))
    wqa = jnp.pad(wqa, pad_q)
    wqb = jnp.pad(wqb, pad_q)
    dkv = MLA_NOPE_DIM + MLA_V_DIM
    wk = _head_pad(w_ukv, h, dkv, 0, MLA_NOPE_DIM, 0)
    wvt = w_ukv.reshape(MLA_KV_RANK, h, dkv)[:, :, MLA_NOPE_DIM:].reshape(MLA_KV_RANK, -1).T
    eye = jnp.eye(MLA_ROPE_DIM, dtype=F32)
    place = _head_pad(jnp.tile(eye, (1, h)), h, MLA_ROPE_DIM, 0, MLA_ROPE_DIM, MLA_NOPE_DIM)
    zeros = jnp.zeros_like(place)
    pad_rows = jnp.zeros((LANE - 2 * MLA_ROPE_DIM, h * HEAD_PAD), F32)
    pa = jnp.concatenate([place, zeros, pad_rows], axis=0)
    pb = jnp.concatenate([zeros, place, pad_rows], axis=0)
    wka = jnp.concatenate([wk, pa], axis=0)
    return tuple(a.astype(BF16) for a in (wqa, wqb, wka, pb, wvt))


def _lora_weights(w2, a2, g2, v2):
    rw = RWKV_WIDTH
    z = lambda r: jnp.zeros((r, rw), F32)
    rows = [
        jnp.concatenate([w2, z(DECAY_RANK), z(DECAY_RANK), z(DECAY_RANK)], axis=1),
        jnp.concatenate([z(ICL_RANK), a2, z(ICL_RANK), z(ICL_RANK)], axis=1),
        jnp.concatenate([z(GATE_RANK), z(GATE_RANK), g2, z(GATE_RANK)], axis=1),
        jnp.concatenate([z(VRES_RANK)] * 3 + [v2 if v2 is not None else z(VRES_RANK)], axis=1),
    ]
    w = jnp.concatenate(rows, axis=0)
    return jnp.pad(w, ((0, LORA_PAD - w.shape[0]), (0, 0))).astype(BF16)


def _block_diag(n, blk, value):
    i = jnp.arange(n)
    return jnp.where((i[:, None] // blk) == (i[None, :] // blk), value, 0.0)


def kernel(x, mem, positions, mix_norm, w_in_first, w_in_rest, shift_mu_first, shift_mu_rest, mla_q_norm, mla_w_uq, mla_kv_norm, mla_w_ukv, mla_out_norm, rwkv_w0, rwkv_w2, rwkv_a0, rwkv_a2, rwkv_g2, rwkv_v0, rwkv_v2, rwkv_k_k, rwkv_k_a, rwkv_r_k, rwkv_ln_w, rwkv_ln_b, w_out, ca_norm, mem_norm, ca_wq, ca_wkv, ca_wo, mlp_norm, mlp_w_up, mlp_w_down, final_norm):
    b, s, d = x.shape
    depth = mix_norm.shape[0]
    mem_len = mem.shape[1]
    t = b * s
    tm = min(512, s)
    tq = min(256, s)
    wkv_rows = min(128, s)
    rw = RWKV_WIDTH

    inv = ROPE_THETA ** (-jnp.arange(0, MLA_ROPE_DIM, 2, dtype=F32) / MLA_ROPE_DIM)
    inv_row = jnp.concatenate([jnp.zeros((MLA_NOPE_DIM,), F32), inv, inv,
                               jnp.zeros((HEAD_PAD - MLA_NOPE_DIM - MLA_ROPE_DIM,), F32)])[None, :]
    ct, st = _rope_tables(positions.astype(F32).reshape(t, 1), inv_row, tm)

    memkv = _mem_kv(mem.reshape(b * mem_len, d), mem_norm[None, :], ca_wkv.astype(BF16),
                    min(512, b * mem_len))

    woa = w_out[:, :MLA_HEADS * MLA_V_DIM].astype(BF16)
    wob = w_out[:, MLA_HEADS * MLA_V_DIM:].astype(BF16)
    wq_b = ca_wq.astype(BF16)
    wo_b = ca_wo.astype(BF16)
    wu_b = mlp_w_up.astype(BF16)
    wd_b = mlp_w_down.astype(BF16)
    ca_g = ca_norm[:, None, :]
    mlp_g = mlp_norm[:, None, :]
    bd512 = _block_diag(rw, RWKV_HEAD, 1.0).astype(BF16)
    bd128 = _block_diag(LANE, RWKV_HEAD, 1.0 / RWKV_HEAD).astype(F32)

    h = x.reshape(t, d)
    v_first = None
    for l in range(depth):
        if l == 0:
            w_in, mu, v0, v2 = w_in_first, shift_mu_first, None, None
        else:
            w_in, mu, v0, v2 = w_in_rest[l - 1], shift_mu_rest[l - 1], rwkv_v0[l - 1], rwkv_v2[l - 1]
        w_all, mu_p = _in_weights(w_in, mu)
        mla_c, rw_c = _in_proj(h, mix_norm[l][None, :], w_all, mu_p, s, tm)

        wqa, wqb, wka, pb, wvt = _mla_weights(mla_w_uq[l], mla_w_ukv[l])
        gq = _pad_cols(mla_q_norm[l][None, :], 256)
        q, k, vt = _mla_prep(mla_c, ct, st, gq, mla_kv_norm[l][None, :], wqa, wqb, wka, pb, wvt, tm)
        att = _mla_attention(q, k, vt, mla_out_norm[l][None, :], s, tq, ATTN_HEADS_PER_STEP)

        w2c = _lora_weights(rwkv_w2[l], rwkv_a2[l], rwkv_g2[l], v2)
        vec = lambda a: a.reshape(1, rw)
        r_, k_, v_, kk_, kka_, lw_, g_ = _rwkv_prep(
            rw_c, v_first, w2c, bd512, vec(rwkv_w0[l]), vec(rwkv_a0[l]), vec(rwkv_k_k[l]),
            vec(rwkv_k_a[l]), None if v0 is None else vec(v0), tm)
        if l == 0:
            v_first = v_
        sh = lambda a: a.reshape(b, s, rw)
        y = _wkv(sh(r_), sh(k_), sh(v_), sh(kk_), sh(kka_), sh(lw_), sh(g_),
                 vec(rwkv_r_k[l]), vec(rwkv_ln_w[l]), vec(rwkv_ln_b[l]), bd128,
                 wkv_rows, WKV_MODES)

        h = _mix_out(h, att.reshape(t, -1), y.reshape(t, rw), woa, wob, ca_g, wq_b, memkv, wo_b,
                     l, s, mem_len, tm)
        h = _mlp(h, mlp_g, wu_b, wd_b, final_norm[None, :], l, l == depth - 1, tm)
    return h.reshape(b, s, d)
```

```python
import functools
import math

import jax
import jax.numpy as jnp
from jax import lax
from jax.experimental import pallas as pl
from jax.experimental.pallas import tpu as pltpu

F32 = jnp.float32
BF16 = jnp.bfloat16

CHUNK = 64
NORM_EPS = 1e-6
MLA_HEADS = 8
MLA_V_DIM = 64
MLA_NOPE_DIM = 64
MLA_ROPE_DIM = 32
MLA_Q_RANK = 192
MLA_KV_RANK = 128
ROPE_THETA = 10000.0
RWKV_HEAD = 64
RWKV_HEADS = 8
RWKV_WIDTH = 512
DECAY_RANK = 32
ICL_RANK = 32
GATE_RANK = 96
VRES_RANK = 32
GN_EPS = 64e-5
CA_HEADS = 4
MLA_COLS = MLA_Q_RANK + MLA_KV_RANK + MLA_ROPE_DIM

LANE = 128
HEAD_PAD = 128
MLA_C_PAD = 512
RW_C_PAD = 1792
LORA_PAD = 256
WKV_CHUNK = 64
ATTN_HEADS_PER_STEP = 4
ATTN_ONES_ROWS = 16
LOG2E = 1.4426950408889634
NEG_BIG = -1e30
VMEM_LIMIT = 56 * 1024 * 1024

HI = lax.Precision.HIGHEST
WKV_MODES = ("bf16", "bf16", "bf16", "bf16")


def _dot(a, b, precision=None):
    return jnp.dot(a, b, preferred_element_type=F32, precision=precision)


def _dot_nt(a, b, precision=None):
    return lax.dot_general(a, b, (((1,), (1,)), ((), ())),
                           preferred_element_type=F32, precision=precision)


def _dot_tn(a, b, precision=None):
    return lax.dot_general(a, b, (((0,), (0,)), ((), ())),
                           preferred_element_type=F32, precision=precision)


def _rms(x, g, n, eps=NORM_EPS):
    ms = jnp.sum(x * x, axis=-1, keepdims=True) * (1.0 / n)
    return x * lax.rsqrt(ms + eps) * g


def _sigmoid(x):
    return 1.0 / (1.0 + jnp.exp(-x))


def _params(*sem):
    return pltpu.CompilerParams(dimension_semantics=sem, vmem_limit_bytes=VMEM_LIMIT)


def _full(shape):
    nd = len(shape)
    return pl.BlockSpec(shape, lambda *_: (0,) * nd)


def _rope_kernel(pos_ref, inv_ref, ct_ref, st_ref):
    ang = pos_ref[...] * inv_ref[...]
    c = jnp.cos(ang)
    s = jnp.sin(ang)
    lane = lax.broadcasted_iota(jnp.int32, ang.shape, 1)
    half = MLA_NOPE_DIM + MLA_ROPE_DIM // 2
    end = MLA_NOPE_DIM + MLA_ROPE_DIM
    ct_ref[...] = jnp.where(lane < MLA_NOPE_DIM, 1.0, jnp.where(lane < end, c, 0.0))
    st_ref[...] = jnp.where(lane < MLA_NOPE_DIM, 0.0,
                            jnp.where(lane < half, -s, jnp.where(lane < end, s, 0.0)))


def _rope_tables(pos_f, inv_row, tm):
    t = pos_f.shape[0]
    return pl.pallas_call(
        _rope_kernel,
        grid=(t // tm,),
        in_specs=[pl.BlockSpec((tm, 1), lambda i: (i, 0)), _full((1, LANE))],
        out_specs=[pl.BlockSpec((tm, LANE), lambda i: (i, 0))] * 2,
        out_shape=[jax.ShapeDtypeStruct((t, LANE), F32)] * 2,
        compiler_params=_params("parallel"),
        name="rope_tables",
    )(pos_f, inv_row)


def _in_proj_kernel(h_ref, g_ref, w_ref, mu_ref, mla_ref, rw_ref, carry_ref, *, tiles_per_seq):
    i = pl.program_id(0)
    d = h_ref.shape[-1]
    hn = _rms(h_ref[...], g_ref[...], d).astype(BF16)
    proj = _dot(hn, w_ref[...])
    mla_ref[...] = proj[:, :MLA_C_PAD].astype(BF16)
    z = proj[:, MLA_C_PAD:]
    tm = z.shape[0]

    @pl.when(i % tiles_per_seq == 0)
    def _():
        carry_ref[...] = jnp.zeros_like(carry_ref)

    prev = pltpu.roll(z, 1, axis=0)
    row = lax.broadcasted_iota(jnp.int32, z.shape, 0)
    prev = jnp.where(row == 0, carry_ref[7:8, :], prev)
    rw_ref[...] = (z + (prev - z) * mu_ref[...]).astype(BF16)
    carry_ref[...] = z[tm - 8:, :]


def _in_proj(h, g, w, mu, seq, tm):
    t, d = h.shape
    n = w.shape[1]
    kern = functools.partial(_in_proj_kernel, tiles_per_seq=seq // tm)
    return pl.pallas_call(
        kern,
        grid=(t // tm,),
        in_specs=[pl.BlockSpec((tm, d), lambda i: (i, 0)), _full((1, d)),
                  _full((d, n)), _full((1, RW_C_PAD))],
        out_specs=[pl.BlockSpec((tm, MLA_C_PAD), lambda i: (i, 0)),
                   pl.BlockSpec((tm, RW_C_PAD), lambda i: (i, 0))],
        out_shape=[jax.ShapeDtypeStruct((t, MLA_C_PAD), BF16),
                   jax.ShapeDtypeStruct((t, RW_C_PAD), BF16)],
        scratch_shapes=[pltpu.VMEM((8, RW_C_PAD), F32)],
        compiler_params=_params("arbitrary"),
        name="in_proj",
    )(h, g, w, mu)


def _mla_prep_kernel(c_ref, ct_ref, st_ref, gq_ref, gkv_ref, wqa_ref, wqb_ref,
                     wka_ref, pb_ref, wv_ref, q_ref, k_ref, v_ref, *, q_scale):
    c = c_ref[...].astype(F32)
    nq = _rms(c[:, :256], gq_ref[...], MLA_Q_RANK).astype(BF16)
    nkv = _rms(c[:, 256:384], gkv_ref[...], MLA_KV_RANK).astype(BF16)
    kr = c_ref[:, 384:512]
    ct = jnp.tile(ct_ref[...], (1, MLA_HEADS))
    st = jnp.tile(st_ref[...], (1, MLA_HEADS))
    q = _dot(nq, wqa_ref[...]) * ct + _dot(nq, wqb_ref[...]) * st
    q_ref[...] = (q * q_scale).astype(BF16)
    x = jnp.concatenate([nkv, kr], axis=1)
    k = _dot(x, wka_ref[...]) * ct + _dot(kr, pb_ref[...]) * st
    k_ref[...] = k.astype(BF16)
    v_ref[...] = _dot_nt(wv_ref[...], nkv).astype(BF16)


def _mla_prep(c, ct, st, gq, gkv, wqa, wqb, wka, pb, wvt, tm):
    t = c.shape[0]
    hw = MLA_HEADS * HEAD_PAD
    vw = wvt.shape[0]
    q_scale = (MLA_NOPE_DIM + MLA_ROPE_DIM) ** -0.5 * LOG2E
    row = lambda w: pl.BlockSpec((tm, w), lambda i: (i, 0))
    return pl.pallas_call(
        functools.partial(_mla_prep_kernel, q_scale=q_scale),
        grid=(t // tm,),
        in_specs=[row(MLA_C_PAD), row(LANE), row(LANE), _full(gq.shape), _full(gkv.shape),
                  _full(wqa.shape), _full(wqb.shape), _full(wka.shape), _full(pb.shape),
                  _full(wvt.shape)],
        out_specs=[row(hw), row(hw), pl.BlockSpec((vw, tm), lambda i: (0, i))],
        out_shape=[jax.ShapeDtypeStruct((t, hw), BF16), jax.ShapeDtypeStruct((t, hw), BF16),
                   jax.ShapeDtypeStruct((vw, t), BF16)],
        compiler_params=_params("parallel"),
        name="mla_prep",
    )(c, ct, st, gq, gkv, wqa, wqb, wka, pb, wvt)


def _attn_kernel(q_ref, k_ref, vt_ref, g_ref, o_ref, m_ref, a_ref, p_ref, acc_ref, s_ref, s2_ref,
                 *, tq, nh):
    qi = pl.program_id(2)
    dv = MLA_V_DIM
    heads = range(nh)

    ones_rows = jnp.where(lax.broadcasted_iota(jnp.int32, (ATTN_ONES_ROWS, tq), 0) == 0,
                          1.0, 0.0).astype(BF16)

    m_ref[...] = jnp.full_like(m_ref, NEG_BIG)
    a_ref[...] = jnp.ones_like(a_ref)
    p_ref[...] = jnp.zeros_like(p_ref)
    acc_ref[...] = jnp.zeros_like(acc_ref)

    def flush_values(j):
        off = pl.multiple_of(j * tq, tq)
        pv = [_dot(jnp.concatenate([vt_ref[h * dv:(h + 1) * dv, pl.ds(off, tq)], ones_rows],
                                   axis=0), p_ref[h]) for h in heads]
        for h in heads:
            acc_ref[h] = a_ref[h] * acc_ref[h] + pv[h]

    def scores(j, h, dst):
        off = pl.multiple_of(j * tq, tq)
        dst[h] = _dot_nt(k_ref[0, pl.ds(off, tq), h * HEAD_PAD:(h + 1) * HEAD_PAD],
                         q_ref[0, :, h * HEAD_PAD:(h + 1) * HEAD_PAD])

    def softmax(h, src, masked):
        s = src[h]
        if masked:
            ck = lax.broadcasted_iota(jnp.int32, (tq, tq), 0) // CHUNK
            cq = lax.broadcasted_iota(jnp.int32, (tq, tq), 1) // CHUNK
            s = jnp.where(ck <= cq, s, NEG_BIG)
        m_old = m_ref[h]
        m_new = jnp.maximum(m_old, jnp.max(s, axis=0, keepdims=True))
        p_ref[h] = jnp.exp2((s - m_new).astype(BF16))
        a_ref[h] = jnp.exp2(m_old - m_new)
        m_ref[h] = m_new

    def stage(j, src, dst, masked):
        flush_values(jnp.maximum(j - 1, 0))
        for h in heads:
            if dst is not None:
                scores(j + 1, h, dst)
            softmax(h, src, masked)

    for h in heads:
        scores(0, h, s_ref)

    @pl.loop(0, qi // 2)
    def _(t):
        stage(2 * t, s_ref, s2_ref, False)
        stage(2 * t + 1, s2_ref, s_ref, False)

    @pl.when(qi % 2 == 0)
    def _():
        stage(qi, s_ref, None, True)

    @pl.when(qi % 2 == 1)
    def _():
        stage(qi - 1, s_ref, s2_ref, False)
        stage(qi, s2_ref, None, True)

    flush_values(qi)
    outs = []
    for h in heads:
        acc = acc_ref[h]
        o = acc[:dv] / acc[dv:dv + 1]
        ms = jnp.sum(o * o, axis=0, keepdims=True) * (1.0 / dv)
        outs.append(o * lax.rsqrt(ms + NORM_EPS))
    o_t = jnp.concatenate(outs, axis=0)
    o_ref[0] = (o_t.T * g_ref[...]).astype(BF16)


def _mla_attention(q, k, vt, g, seq, tq, nh):
    b = q.shape[0] // seq
    hw = q.shape[1]
    groups = hw // (nh * HEAD_PAD)
    ow = nh * MLA_V_DIM
    q3 = q.reshape(b, seq, hw)
    k3 = k.reshape(b, seq, hw)
    return pl.pallas_call(
        functools.partial(_attn_kernel, tq=tq, nh=nh),
        grid=(b, groups, seq // tq),
        in_specs=[pl.BlockSpec((1, tq, nh * HEAD_PAD), lambda bi, h, i: (bi, i, h)),
                  pl.BlockSpec((1, seq, nh * HEAD_PAD), lambda bi, h, i: (bi, 0, h)),
                  pl.BlockSpec((ow, seq), lambda bi, h, i: (h, bi)),
                  pl.BlockSpec((1, ow), lambda bi, h, i: (0, h))],
        out_specs=pl.BlockSpec((1, tq, ow), lambda bi, h, i: (bi, i, h)),
        out_shape=jax.ShapeDtypeStruct((b, seq, groups * ow), BF16),
        scratch_shapes=[pltpu.VMEM((nh, 1, tq), F32), pltpu.VMEM((nh, 1, tq), F32),
                        pltpu.VMEM((nh, tq, tq), BF16),
                        pltpu.VMEM((nh, MLA_V_DIM + ATTN_ONES_ROWS, tq), F32),
                        pltpu.VMEM((nh, tq, tq), F32), pltpu.VMEM((nh, tq, tq), F32)],
        compiler_params=_params("parallel", "parallel", "arbitrary"),
        name="mla_attention",
    )(q3, k3, vt, g)


def _rwkv_prep_kernel(*refs, first):
    if first:
        (x_ref, w2_ref, bd_ref, w0_ref, a0_ref, kk_ref_p, ka_ref,
         r_o, k_o, v_o, kk_o, kka_o, lw_o, g_o) = refs
        vf_ref = v0_ref = None
    else:
        (x_ref, vf_ref, w2_ref, bd_ref, w0_ref, a0_ref, kk_ref_p, ka_ref, v0_ref,
         r_o, k_o, v_o, kk_o, kka_o, lw_o, g_o) = refs
    rw = RWKV_WIDTH
    r_o[...] = x_ref[:, 0:rw]
    k = x_ref[:, rw:2 * rw].astype(F32)
    v = x_ref[:, 2 * rw:3 * rw].astype(F32)
    tail = x_ref[:, 3 * rw:3 * rw + LORA_PAD].astype(F32)
    lane = lax.broadcasted_iota(jnp.int32, tail.shape, 1)
    g_lo = DECAY_RANK + ICL_RANK
    g_hi = g_lo + GATE_RANK
    act = jnp.where(lane < DECAY_RANK, jnp.tanh(tail),
                    jnp.where((lane >= g_lo) & (lane < g_hi), _sigmoid(tail), tail))
    lo = _dot(act.astype(BF16), w2_ref[...])
    z = -(w0_ref[...] + lo[:, 0:rw])
    softplus = jnp.maximum(z, 0.0) + jnp.log(1.0 + jnp.exp(-jnp.abs(z)))
    lw_o[...] = -jnp.exp(-softplus - 0.5)
    a = _sigmoid(a0_ref[...] + lo[:, rw:2 * rw])
    g_o[...] = lo[:, 2 * rw:3 * rw].astype(BF16)
    if not first:
        mix = _sigmoid(v0_ref[...] + lo[:, 3 * rw:4 * rw])
        v = v + (vf_ref[...].astype(F32) - v) * mix
    v_o[...] = v.astype(BF16)
    kk = k * kk_ref_p[...]
    sq = kk * kk
    sq_hi = sq.astype(BF16)
    sq_lo = (sq - sq_hi.astype(F32)).astype(BF16)
    ss = _dot(sq_hi, bd_ref[...]) + _dot(sq_lo, bd_ref[...])
    kk = kk / jnp.maximum(jnp.sqrt(ss), 1e-12)
    kk_o[...] = kk.astype(BF16)
    kka_o[...] = (kk * a).astype(BF16)
    k_o[...] = (k * (1.0 + (a - 1.0) * ka_ref[...])).astype(BF16)


def _rwkv_prep(x, vf, w2, bd, w0, a0, k_k, k_a, v0, tm):
    t = x.shape[0]
    rw = RWKV_WIDTH
    first = vf is None
    row = lambda w: pl.BlockSpec((tm, w), lambda i: (i, 0))
    vec = _full((1, rw))
    if first:
        args = (x, w2, bd, w0, a0, k_k, k_a)
        in_specs = [row(RW_C_PAD), _full(w2.shape), _full(bd.shape), vec, vec, vec, vec]
    else:
        args = (x, vf, w2, bd, w0, a0, k_k, k_a, v0)
        in_specs = [row(RW_C_PAD), row(rw), _full(w2.shape), _full(bd.shape),
                    vec, vec, vec, vec, vec]
    out_dt = [BF16, BF16, BF16, BF16, BF16, F32, BF16]
    return pl.pallas_call(
        functools.partial(_rwkv_prep_kernel, first=first),
        grid=(t // tm,),
        in_specs=in_specs,
        out_specs=[row(rw)] * 7,
        out_shape=[jax.ShapeDtypeStruct((t, rw), dt) for dt in out_dt],
        compiler_params=_params("parallel"),
        name="rwkv_prep",
    )(*args)


_DIMS = {"nn": (((1,), (0,)), ((), ())), "nt": (((1,), (1,)), ((), ())),
         "tn": (((0,), (0,)), ((), ()))}


def _split(x):
    hi = x.astype(BF16)
    return hi, (x - hi.astype(F32)).astype(BF16)


def _mm(a, b, mode, dims="nn"):
    dg = lambda x, y: lax.dot_general(x, y, _DIMS[dims], preferred_element_type=F32)
    if mode == "bf16":
        return dg(a.astype(BF16), b.astype(BF16))
    if mode == "x3":
        a_hi, a_lo = _split(a)
        b_hi, b_lo = _split(b)
        return dg(a_hi, b_hi) + (dg(a_lo, b_hi) + dg(a_hi, b_lo))
    if mode == "ax":
        a_hi = a.astype(BF16)
        b_hi, b_r = _split(b)
        b_mid, b_lo = _split(b - b_hi.astype(F32))
        del b_r
        return dg(a_hi, b_hi) + (dg(a_hi, b_mid) + dg(a_hi, b_lo))
    return lax.dot_general(a, b, _DIMS[dims], preferred_element_type=F32, precision=HI)


def _unit_lower_solve(a, rhs, row, col, eye, mode):
    return _unit_lower_solve_many([a], [rhs], row, col, eye, mode)[0]


def _unit_lower_solve_many(a_list, rhs_list, row, col, eye, mode):
    d16 = (row // 16) == (col // 16)
    d32 = (row // 32) == (col // 32)
    n = range(len(a_list))
    d = [jnp.where(d16, a, 0.0) for a in a_list]
    x = [eye + d[i] for i in n]
    d2 = [_mm(d[i], d[i], mode) for i in n]
    t = [_mm(d2[i], jnp.concatenate([x[i], d2[i]], axis=1), mode) for i in n]
    x = [x[i] + t[i][:, :128] for i in n]
    d4 = [t[i][:, 128:] for i in n]
    t = [_mm(d4[i], jnp.concatenate([x[i], d4[i]], axis=1), mode) for i in n]
    x = [x[i] + t[i][:, :128] for i in n]
    d8 = [t[i][:, 128:] for i in n]
    t = [_mm(d8[i], x[i], mode) for i in n]
    x = [x[i] + t[i] for i in n]
    e = [jnp.where(d32, a_list[i], 0.0) - d[i] for i in n]
    t = [_mm(e[i], x[i], mode) for i in n]
    t = [_mm(x[i], t[i], mode) for i in n]
    x = [x[i] + t[i] for i in n]
    f = [jnp.where(d32, 0.0, a_list[i]) for i in n]
    z = [_mm(x[i], rhs_list[i], mode) for i in n]
    t = [_mm(f[i], z[i], mode) for i in n]
    t = [_mm(x[i], t[i], mode) for i in n]
    return [z[i] + t[i] for i in n]


def _wkv_kernel(r_ref, k_ref, v_ref, kk_ref, kka_ref, lw_ref, g_ref,
                rk_ref, lnw_ref, lnb_ref, bd_ref, o_ref, s_ref, *, nch, modes):
    m_quad, m_inv, m_state, m_post = modes
    c_len = WKV_CHUNK

    npair = s_ref.shape[0]

    @pl.when(pl.program_id(1) == 0)
    def _():
        s_ref[...] = jnp.zeros_like(s_ref)

    row = lax.broadcasted_iota(jnp.int32, (128, 128), 0)
    col = lax.broadcasted_iota(jnp.int32, (128, 128), 1)
    same = (row // c_len) == (col // c_len)
    mask_s = same & (col < row)
    mask_i = same & (col <= row)
    eye = jnp.where(row == col, 1.0, 0.0)
    lane0 = lax.broadcasted_iota(jnp.int32, (c_len, 128), 1) < RWKV_HEAD
    tr = lax.broadcasted_iota(jnp.int32, (c_len, c_len), 0)
    tc = lax.broadcasted_iota(jnp.int32, (c_len, c_len), 1)
    ltri = jnp.where(tc <= tr, 1.0, 0.0)

    def stack(x):
        return jnp.concatenate([jnp.where(lane0, x, 0.0), jnp.where(lane0, 0.0, x)], axis=0)

    bd = bd_ref[...]
    par = []
    for c, hp in [(c, hp) for c in range(nch) for hp in range(npair)]:
        sl = pl.ds(c * c_len, c_len)
        ln = slice(hp * LANE, (hp + 1) * LANE)
        r = r_ref[0, sl, ln].astype(F32)
        k = k_ref[0, sl, ln].astype(F32)
        v = v_ref[0, sl, ln].astype(F32)
        kk = kk_ref[0, sl, ln].astype(F32)
        kka = kka_ref[0, sl, ln].astype(F32)
        lw = lw_ref[0, sl, ln]
        cum = _mm(ltri, lw, "ax")
        tot = cum[c_len - 1:c_len, :]
        p_in = jnp.exp(cum)
        p_ex = jnp.exp(cum - lw)
        p_inv = jnp.exp(-cum)
        p_end = jnp.exp(tot - cum)
        at = stack(-kk * p_ex)
        rt = stack(r * p_in)
        bt = kka * p_inv
        kt = k * p_inv
        par.append(dict(
            r=r, k=k, v=v, tot=tot, at=at, rt=rt, vs=stack(v),
            bkh=jnp.concatenate([stack(kka * p_end), stack(k * p_end)], axis=0),
            lhs=jnp.concatenate([at, rt], axis=0),
            rhs=jnp.concatenate([bt, bt, kt, kt], axis=0)))
    for p in par:
        quad = _mm(p["lhs"], p["rhs"], m_quad, "nt")
        p["a_ab"] = jnp.where(mask_s, quad[:128, :128], 0.0)
        p["a_ak"] = jnp.where(mask_s, quad[:128, 128:], 0.0)
        p["a_r"] = jnp.concatenate([jnp.where(mask_i, quad[128:, :128], 0.0),
                                    jnp.where(mask_i, quad[128:, 128:], 0.0)], axis=1)
    for p in par:
        p["akv"] = _mm(p["a_ak"], p["vs"], m_quad)
    wus = _unit_lower_solve_many(
        [p["a_ab"] for p in par],
        [jnp.concatenate([p["at"], p["akv"]], axis=1) for p in par], row, col, eye, m_inv)

    pairs = range(npair)
    state = [s_ref[hp] for hp in pairs]
    for c in range(nch):
        sl = pl.ds(c * c_len, c_len)
        pc = [par[c * npair + hp] for hp in pairs]
        wu = [wus[c * npair + hp] for hp in pairs]
        m1 = [_mm(jnp.concatenate([wu[hp][:, :128], pc[hp]["rt"]], axis=0), state[hp],
                  m_state, "nt") for hp in pairs]
        uv = [jnp.concatenate([m1[hp][:128] + wu[hp][:, 128:], pc[hp]["vs"]], axis=0)
              for hp in pairs]
        upd = [_mm(uv[hp], pc[hp]["bkh"], m_state, "tn") for hp in pairs]
        state = [state[hp] * jnp.exp(pc[hp]["tot"]) + upd[hp] for hp in pairs]
        ys = [m1[hp][128:] + _mm(pc[hp]["a_r"], uv[hp], m_state) for hp in pairs]
        y = [ys[hp][:c_len] + ys[hp][c_len:] for hp in pairs]
        mean = [_mm(y[hp], bd, m_post) for hp in pairs]
        yc = [y[hp] - mean[hp] for hp in pairs]
        var = [_mm(yc[hp] * yc[hp], bd, m_post) for hp in pairs]
        for hp in pairs:
            ln = slice(hp * LANE, (hp + 1) * LANE)
            r, k, v = pc[hp]["r"], pc[hp]["k"], pc[hp]["v"]
            yn = yc[hp] * lax.rsqrt(var[hp] + GN_EPS) * lnw_ref[:, ln] + lnb_ref[:, ln]
            bonus = _mm(r * k * rk_ref[:, ln], bd, m_post) * float(RWKV_HEAD)
            yn = yn + bonus * v
            o_ref[0, sl, ln] = (yn * g_ref[0, sl, ln].astype(F32)).astype(BF16)
    for hp in pairs:
        s_ref[hp] = state[hp]


def _wkv(r, k, v, kk, kka, lw, g, r_k, ln_w, ln_b, bd, rows, modes):
    b, s, w = r.shape
    blk = pl.BlockSpec((1, rows, w), lambda bi, i: (bi, i, 0))
    kern = functools.partial(_wkv_kernel, nch=rows // WKV_CHUNK, modes=modes)
    return pl.pallas_call(
        kern,
        grid=(b, s // rows),
        in_specs=[blk] * 7 + [_full((1, w))] * 3 + [_full((LANE, LANE))],
        out_specs=blk,
        out_shape=jax.ShapeDtypeStruct((b, s, w), BF16),
        scratch_shapes=[pltpu.VMEM((w // LANE, LANE, LANE), F32)],
        compiler_params=_params("parallel", "arbitrary"),
        name="wkv7",
    )(r, k, v, kk, kka, lw, g, r_k, ln_w, ln_b, bd)


def _mem_kv_kernel(m_ref, g_ref, w_ref, o_ref):
    d = m_ref.shape[-1]
    mn = _rms(m_ref[...], g_ref[...], d).astype(BF16)
    o_ref[...] = _dot(mn, w_ref[0]).astype(BF16)


def _mem_kv(mem2d, g, wkv, tm):
    t, d = mem2d.shape
    nl, _, n = wkv.shape
    return pl.pallas_call(
        _mem_kv_kernel,
        grid=(nl, t // tm),
        in_specs=[pl.BlockSpec((tm, d), lambda l, i: (i, 0)), _full((1, d)),
                  pl.BlockSpec((1, d, n), lambda l, i: (l, 0, 0))],
        out_specs=pl.BlockSpec((tm, n), lambda l, i: (i, l)),
        out_shape=jax.ShapeDtypeStruct((t, nl * n), BF16),
        compiler_params=_params("parallel", "parallel"),
        name="mem_kv",
    )(mem2d, g, wkv)


def _mix_out_kernel(h_ref, att_ref, rw_ref, woa_ref, wob_ref, g_ref, wq_ref,
                    mk_ref, mv_ref, wo_ref, o_ref, *, q_scale):
    d = h_ref.shape[-1]
    h1 = h_ref[...] + _dot(att_ref[...], woa_ref[0]) + _dot(rw_ref[...], wob_ref[0])
    hn = _rms(h1, g_ref[0], d).astype(BF16)
    q = (_dot(hn, wq_ref[0]) * q_scale).astype(BF16)
    dh = d // CA_HEADS
    outs = []
    for hd in range(CA_HEADS):
        sl = slice(hd * dh, (hd + 1) * dh)
        s = _dot_nt(q[:, sl], mk_ref[:, sl])
        m = jnp.max(s, axis=-1, keepdims=True)
        p = jnp.exp2(s - m)
        l = jnp.sum(p, axis=-1, keepdims=True)
        outs.append((_dot(p.astype(BF16), mv_ref[:, sl]) / l).astype(BF16))
    o = jnp.concatenate(outs, axis=1)
    o_ref[...] = h1 + _dot(o, wo_ref[0])


def _mix_out(h, att, rw, woa, wob, g, wq, memkv, wo, layer, seq, mem_len, tm):
    t, d = h.shape
    tiles = seq // tm
    q_scale = (d // CA_HEADS) ** -0.5 * LOG2E
    row = lambda w: pl.BlockSpec((tm, w), lambda i: (i, 0))
    lay = lambda a: pl.BlockSpec((1,) + a.shape[1:], lambda i: (layer,) + (0,) * (a.ndim - 1))
    return pl.pallas_call(
        functools.partial(_mix_out_kernel, q_scale=q_scale),
        grid=(t // tm,),
        in_specs=[row(d), row(att.shape[1]), row(rw.shape[1]), lay(woa), lay(wob), lay(g),
                  lay(wq),
                  pl.BlockSpec((mem_len, d), lambda i: (i // tiles, 2 * layer)),
                  pl.BlockSpec((mem_len, d), lambda i: (i // tiles, 2 * layer + 1)),
                  lay(wo)],
        out_specs=row(d),
        out_shape=jax.ShapeDtypeStruct((t, d), F32),
        compiler_params=_params("parallel"),
        name="mix_out_cross_attention",
    )(h, att, rw, woa, wob, g, wq, memkv, memkv, wo)


def _mlp_kernel(h_ref, g_ref, wu_ref, wd_ref, gf_ref, o_ref, *, last, ff_chunk):
    d = h_ref.shape[-1]
    h = h_ref[...]
    hn = _rms(h, g_ref[0], d).astype(BF16)
    acc = h
    for c in range(wu_ref.shape[-1] // ff_chunk):
        sl = slice(c * ff_chunk, (c + 1) * ff_chunk)
        u = jnp.maximum(_dot(hn, wu_ref[0, :, sl]), 0.0)
        acc = acc + _dot((u * u).astype(BF16), wd_ref[0, sl, :])
    if last:
        acc = _rms(acc, gf_ref[...], d)
    o_ref[...] = acc


def _mlp(h, g, wu, wd, gf, layer, last, tm):
    t, d = h.shape
    row = pl.BlockSpec((tm, d), lambda i: (i, 0))
    lay = lambda a: pl.BlockSpec((1,) + a.shape[1:], lambda i: (layer,) + (0,) * (a.ndim - 1))
    return pl.pallas_call(
        functools.partial(_mlp_kernel, last=last, ff_chunk=1024),
        grid=(t // tm,),
        in_specs=[row, lay(g), lay(wu), lay(wd), _full((1, d))],
        out_specs=row,
        out_shape=jax.ShapeDtypeStruct((t, d), F32),
        compiler_params=_params("parallel"),
        name="mlp",
    )(h, g, wu, wd, gf)


def _pad_cols(a, n):
    return jnp.pad(a, ((0, 0), (0, n - a.shape[1])))


def _swap_halves(a):
    half = a.shape[-1] // 2
    return jnp.concatenate([a[..., half:], a[..., :half]], axis=-1)


def _in_weights(w_in, mu):
    cq = w_in[:, :MLA_Q_RANK]
    ckv = w_in[:, MLA_Q_RANK:MLA_Q_RANK + MLA_KV_RANK]
    kr = w_in[:, MLA_Q_RANK + MLA_KV_RANK:MLA_COLS]
    mla = jnp.concatenate([_pad_cols(cq, 256), ckv, kr, _swap_halves(kr)], axis=1)
    w = jnp.concatenate([_pad_cols(mla, MLA_C_PAD), _pad_cols(w_in[:, MLA_COLS:], RW_C_PAD)], axis=1)
    return w.astype(BF16), _pad_cols(mu[None, :], RW_C_PAD)


def _head_pad(a, n_heads, per_head, lo, hi, at):
    rows = a.shape[0]
    a = a.reshape(rows, n_heads, per_head)[:, :, lo:hi]
    a = jnp.pad(a, ((0, 0), (0, 0), (at, HEAD_PAD - at - (hi - lo))))
    return a.reshape(rows, n_heads * HEAD_PAD)


def _mla_weights(w_uq, w_ukv):
    dq = MLA_NOPE_DIM + MLA_ROPE_DIM
    h = MLA_HEADS
    wq3 = w_uq.reshape(MLA_Q_RANK, h, dq)
    rope_sw = _swap_halves(wq3[:, :, MLA_NOPE_DIM:]).reshape(MLA_Q_RANK, h * MLA_ROPE_DIM)
    wqa = _head_pad(w_uq, h, dq, 0, dq, 0)
    wqb = _head_pad(rope_sw, h, MLA_ROPE_DIM, 0, MLA_ROPE_DIM, MLA_NOPE_DIM)
    pad_q = ((0, 256 - MLA_Q_RANK), (0, 0))
    wqa = jnp.pad(wqa, pad_q)
    wqb = jnp.pad(wqb, pad_q)
    dkv = MLA_NOPE_DIM + MLA_V_DIM
    wk = _head_pad(w_ukv, h, dkv, 0, MLA_NOPE_DIM, 0)
    wvt = w_ukv.reshape(MLA_KV_RANK, h, dkv)[:, :, MLA_NOPE_DIM:].reshape(MLA_KV_RANK, -1).T
    eye = jnp.eye(MLA_ROPE_DIM, dtype=F32)
    place = _head_pad(jnp.tile(eye, (1, h)), h, MLA_ROPE_DIM, 0, MLA_ROPE_DIM, MLA_NOPE_DIM)
    zeros = jnp.zeros_like(place)
    pad_rows = jnp.zeros((LANE - 2 * MLA_ROPE_DIM, h * HEAD_PAD), F32)
    pa = jnp.concatenate([place, zeros, pad_rows], axis=0)
    pb = jnp.concatenate([zeros, place, pad_rows], axis=0)
    wka = jnp.concatenate([wk, pa], axis=0)
    return tuple(a.astype(BF16) for a in (wqa, wqb, wka, pb, wvt))


def _lora_weights(w2, a2, g2, v2):
    rw = RWKV_WIDTH
    z = lambda r: jnp.zeros((r, rw), F32)
    rows = [
        jnp.concatenate([w2, z(DECAY_RANK), z(DECAY_RANK), z(DECAY_RANK)], axis=1),
        jnp.concatenate([z(ICL_RANK), a2, z(ICL_RANK), z(ICL_RANK)], axis=1),
        jnp.concatenate([z(GATE_RANK), z(GATE_RANK), g2, z(GATE_RANK)], axis=1),
        jnp.concatenate([z(VRES_RANK)] * 3 + [v2 if v2 is not None else z(VRES_RANK)], axis=1),
    ]
    w = jnp.concatenate(rows, axis=0)
    return jnp.pad(w, ((0, LORA_PAD - w.shape[0]), (0, 0))).astype(BF16)


def _block_diag(n, blk, value):
    i = jnp.arange(n)
    return jnp.where((i[:, None] // blk) == (i[None, :] // blk), value, 0.0)


def kernel(x, mem, positions, mix_norm, w_in_first, w_in_rest, shift_mu_first, shift_mu_rest, mla_q_norm, mla_w_uq, mla_kv_norm, mla_w_ukv, mla_out_norm, rwkv_w0, rwkv_w2, rwkv_a0, rwkv_a2, rwkv_g2, rwkv_v0, rwkv_v2, rwkv_k_k, rwkv_k_a, rwkv_r_k, rwkv_ln_w, rwkv_ln_b, w_out, ca_norm, mem_norm, ca_wq, ca_wkv, ca_wo, mlp_norm, mlp_w_up, mlp_w_down, final_norm):
    b, s, d = x.shape
    depth = mix_norm.shape[0]
    mem_len = mem.shape[1]
    t = b * s
    tm = min(512, s)
    tq = min(256, s)
    wkv_rows = min(128, s)
    rw = RWKV_WIDTH

    inv = ROPE_THETA ** (-jnp.arange(0, MLA_ROPE_DIM, 2, dtype=F32) / MLA_ROPE_DIM)
    inv_row = jnp.concatenate([jnp.zeros((MLA_NOPE_DIM,), F32), inv, inv,
                               jnp.zeros((HEAD_PAD - MLA_NOPE_DIM - MLA_ROPE_DIM,), F32)])[None, :]
    ct, st = _rope_tables(positions.astype(F32).reshape(t, 1), inv_row, tm)

    memkv = _mem_kv(mem.reshape(b * mem_len, d), mem_norm[None, :], ca_wkv.astype(BF16),
                    min(512, b * mem_len))

    woa = w_out[:, :MLA_HEADS * MLA_V_DIM].astype(BF16)
    wob = w_out[:, MLA_HEADS * MLA_V_DIM:].astype(BF16)
    wq_b = ca_wq.astype(BF16)
    wo_b = ca_wo.astype(BF16)
    wu_b = mlp_w_up.astype(BF16)
    wd_b = mlp_w_down.astype(BF16)
    ca_g = ca_norm[:, None, :]
    mlp_g = mlp_norm[:, None, :]
    bd512 = _block_diag(rw, RWKV_HEAD, 1.0).astype(BF16)
    bd128 = _block_diag(LANE, RWKV_HEAD, 1.0 / RWKV_HEAD).astype(F32)

    h = x.reshape(t, d)
    v_first = None
    for l in range(depth):
        if l == 0:
            w_in, mu, v0, v2 = w_in_first, shift_mu_first, None, None
        else:
            w_in, mu, v0, v2 = w_in_rest[l - 1], shift_mu_rest[l - 1], rwkv_v0[l - 1], rwkv_v2[l - 1]
        w_all, mu_p = _in_weights(w_in, mu)
        mla_c, rw_c = _in_proj(h, mix_norm[l][None, :], w_all, mu_p, s, tm)

        wqa, wqb, wka, pb, wvt = _mla_weights(mla_w_uq[l], mla_w_ukv[l])
        gq = _pad_cols(mla_q_norm[l][None, :], 256)
        q, k, vt = _mla_prep(mla_c, ct, st, gq, mla_kv_norm[l][None, :], wqa, wqb, wka, pb, wvt, tm)
        att = _mla_attention(q, k, vt, mla_out_norm[l][None, :], s, tq, ATTN_HEADS_PER_STEP)

        w2c = _lora_weights(rwkv_w2[l], rwkv_a2[l], rwkv_g2[l], v2)
        vec = lambda a: a.reshape(1, rw)
        r_, k_, v_, kk_, kka_, lw_, g_ = _rwkv_prep(
            rw_c, v_first, w2c, bd512, vec(rwkv_w0[l]), vec(rwkv_a0[l]), vec(rwkv_k_k[l]),
            vec(rwkv_k_a[l]), None if v0 is None else vec(v0), tm)
        if l == 0:
            v_first = v_
        sh = lambda a: a.reshape(b, s, rw)
        y = _wkv(sh(r_), sh(k_), sh(v_), sh(kk_), sh(kka_), sh(lw_), sh(g_),
                 vec(rwkv_r_k[l]), vec(rwkv_ln_w[l]), vec(rwkv_ln_b[l]), bd128,
                 wkv_rows, WKV_MODES)

        h = _mix_out(h, att.reshape(t, -1), y.reshape(t, rw), woa, wob, ca_g, wq_b, memkv, wo_b,
                     l, s, mem_len, tm)
        h = _mlp(h, mlp_g, wu_b, wd_b, final_norm[None, :], l, l == depth - 1, tm)
    return h.reshape(b, s, d)
```

```python
import functools
import math

import jax
import jax.numpy as jnp
from jax import lax
from jax.experimental import pallas as pl
from jax.experimental.pallas import tpu as pltpu

F32 = jnp.float32
BF16 = jnp.bfloat16

CHUNK = 64
NORM_EPS = 1e-6
MLA_HEADS = 8
MLA_V_DIM = 64
MLA_NOPE_DIM = 64
MLA_ROPE_DIM = 32
MLA_Q_RANK = 192
MLA_KV_RANK = 128
ROPE_THETA = 10000.0
RWKV_HEAD = 64
RWKV_HEADS = 8
RWKV_WIDTH = 512
DECAY_RANK = 32
ICL_RANK = 32
GATE_RANK = 96
VRES_RANK = 32
GN_EPS = 64e-5
CA_HEADS = 4
MLA_COLS = MLA_Q_RANK + MLA_KV_RANK + MLA_ROPE_DIM

LANE = 128
HEAD_PAD = 128
MLA_C_PAD = 512
RW_C_PAD = 1792
LORA_PAD = 256
WKV_CHUNK = 64
ATTN_HEADS_PER_STEP = 4
ATTN_ONES_ROWS = 16
LOG2E = 1.4426950408889634
NEG_BIG = -1e30
VMEM_LIMIT = 56 * 1024 * 1024

HI = lax.Precision.HIGHEST
WKV_MODES = ("bf16", "bf16", "bf16")


def _dot(a, b, precision=None):
    return jnp.dot(a, b, preferred_element_type=F32, precision=precision)


def _dot_nt(a, b, precision=None):
    return lax.dot_general(a, b, (((1,), (1,)), ((), ())),
                           preferred_element_type=F32, precision=precision)


def _dot_tn(a, b, precision=None):
    return lax.dot_general(a, b, (((0,), (0,)), ((), ())),
                           preferred_element_type=F32, precision=precision)


def _rms(x, g, n, eps=NORM_EPS):
    ms = jnp.sum(x * x, axis=-1, keepdims=True) * (1.0 / n)
    return x * lax.rsqrt(ms + eps) * g


def _sigmoid(x):
    return 0.5 + 0.5 * jnp.tanh(0.5 * x)


def _params(*sem):
    return pltpu.CompilerParams(dimension_semantics=sem, vmem_limit_bytes=VMEM_LIMIT)


def _full(shape):
    nd = len(shape)
    return pl.BlockSpec(shape, lambda *_: (0,) * nd)


def _rope_kernel(pos_ref, inv_ref, ct_ref, st_ref):
    ang = pos_ref[...] * inv_ref[...]
    c = jnp.cos(ang)
    s = jnp.sin(ang)
    lane = lax.broadcasted_iota(jnp.int32, ang.shape, 1)
    half = MLA_NOPE_DIM + MLA_ROPE_DIM // 2
    end = MLA_NOPE_DIM + MLA_ROPE_DIM
    ct_ref[...] = jnp.where(lane < MLA_NOPE_DIM, 1.0, jnp.where(lane < end, c, 0.0))
    st_ref[...] = jnp.where(lane < MLA_NOPE_DIM, 0.0,
                            jnp.where(lane < half, -s, jnp.where(lane < end, s, 0.0)))


def _rope_tables(pos_f, inv_row, tm):
    t = pos_f.shape[0]
    return pl.pallas_call(
        _rope_kernel,
        grid=(t // tm,),
        in_specs=[pl.BlockSpec((tm, 1), lambda i: (i, 0)), _full((1, LANE))],
        out_specs=[pl.BlockSpec((tm, LANE), lambda i: (i, 0))] * 2,
        out_shape=[jax.ShapeDtypeStruct((t, LANE), F32)] * 2,
        compiler_params=_params("parallel"),
        name="rope_tables",
    )(pos_f, inv_row)


def _in_proj_kernel(h_ref, g_ref, w_ref, mu_ref, mla_ref, rw_ref, carry_ref, *, tiles_per_seq):
    i = pl.program_id(0)
    d = h_ref.shape[-1]
    hn = _rms(h_ref[...], g_ref[...], d).astype(BF16)
    proj = _dot(hn, w_ref[...])
    mla_ref[...] = proj[:, :MLA_C_PAD].astype(BF16)
    z = proj[:, MLA_C_PAD:]
    tm = z.shape[0]

    @pl.when(i % tiles_per_seq == 0)
    def _():
        carry_ref[...] = jnp.zeros_like(carry_ref)

    prev = pltpu.roll(z, 1, axis=0)
    row = lax.broadcasted_iota(jnp.int32, z.shape, 0)
    prev = jnp.where(row == 0, carry_ref[7:8, :], prev)
    rw_ref[...] = (z + (prev - z) * mu_ref[...]).astype(BF16)
    carry_ref[...] = z[tm - 8:, :]


def _in_proj(h, g, w, mu, seq, tm):
    t, d = h.shape
    n = w.shape[1]
    kern = functools.partial(_in_proj_kernel, tiles_per_seq=seq // tm)
    return pl.pallas_call(
        kern,
        grid=(t // tm,),
        in_specs=[pl.BlockSpec((tm, d), lambda i: (i, 0)), _full((1, d)),
                  _full((d, n)), _full((1, RW_C_PAD))],
        out_specs=[pl.BlockSpec((tm, MLA_C_PAD), lambda i: (i, 0)),
                   pl.BlockSpec((tm, RW_C_PAD), lambda i: (i, 0))],
        out_shape=[jax.ShapeDtypeStruct((t, MLA_C_PAD), BF16),
                   jax.ShapeDtypeStruct((t, RW_C_PAD), BF16)],
        scratch_shapes=[pltpu.VMEM((8, RW_C_PAD), F32)],
        compiler_params=_params("arbitrary"),
        name="in_proj",
    )(h, g, w, mu)


def _mla_prep_kernel(c_ref, ct_ref, st_ref, gq_ref, gkv_ref, wqa_ref, wqb_ref,
                     wka_ref, pb_ref, wv_ref, q_ref, k_ref, v_ref, *, q_scale):
    c = c_ref[...].astype(F32)
    nq = _rms(c[:, :256], gq_ref[...], MLA_Q_RANK).astype(BF16)
    nkv = _rms(c[:, 256:384], gkv_ref[...], MLA_KV_RANK).astype(BF16)
    kr = c_ref[:, 384:512]
    ct = jnp.tile(ct_ref[...], (1, MLA_HEADS))
    st = jnp.tile(st_ref[...], (1, MLA_HEADS))
    q = _dot(nq, wqa_ref[...]) * ct + _dot(nq, wqb_ref[...]) * st
    q_ref[...] = (q * q_scale).astype(BF16)
    x = jnp.concatenate([nkv, kr], axis=1)
    k = _dot(x, wka_ref[...]) * ct + _dot(kr, pb_ref[...]) * st
    k_ref[...] = k.astype(BF16)
    v_ref[...] = _dot_nt(wv_ref[...], nkv).astype(BF16)


def _mla_prep(c, ct, st, gq, gkv, wqa, wqb, wka, pb, wvt, tm):
    t = c.shape[0]
    hw = MLA_HEADS * HEAD_PAD
    vw = wvt.shape[0]
    q_scale = (MLA_NOPE_DIM + MLA_ROPE_DIM) ** -0.5 * LOG2E
    row = lambda w: pl.BlockSpec((tm, w), lambda i: (i, 0))
    return pl.pallas_call(
        functools.partial(_mla_prep_kernel, q_scale=q_scale),
        grid=(t // tm,),
        in_specs=[row(MLA_C_PAD), row(LANE), row(LANE), _full(gq.shape), _full(gkv.shape),
                  _full(wqa.shape), _full(wqb.shape), _full(wka.shape), _full(pb.shape),
                  _full(wvt.shape)],
        out_specs=[row(hw), row(hw), pl.BlockSpec((vw, tm), lambda i: (0, i))],
        out_shape=[jax.ShapeDtypeStruct((t, hw), BF16), jax.ShapeDtypeStruct((t, hw), BF16),
                   jax.ShapeDtypeStruct((vw, t), BF16)],
        compiler_params=_params("parallel"),
        name="mla_prep",
    )(c, ct, st, gq, gkv, wqa, wqb, wka, pb, wvt)


def _attn_kernel(q_ref, k_ref, vt_ref, g_ref, o_ref, m_ref, a_ref, p_ref, acc_ref, s_ref, s2_ref,
                 *, tq, tk, nh):
    qi = pl.program_id(2)
    dv = MLA_V_DIM
    heads = range(nh)

    ones_rows = jnp.where(lax.broadcasted_iota(jnp.int32, (ATTN_ONES_ROWS, tk), 0) == 0,
                          1.0, 0.0).astype(BF16)

    m_ref[...] = jnp.full_like(m_ref, NEG_BIG)
    a_ref[...] = jnp.ones_like(a_ref)
    p_ref[...] = jnp.zeros_like(p_ref)
    acc_ref[...] = jnp.zeros_like(acc_ref)

    def flush_values(j):
        off = pl.multiple_of(j * tk, tk)
        pv = [_dot(jnp.concatenate([vt_ref[h * dv:(h + 1) * dv, pl.ds(off, tk)], ones_rows],
                                   axis=0), p_ref[h]) for h in heads]
        for h in heads:
            acc_ref[h] = a_ref[h] * acc_ref[h] + pv[h]

    def scores(j, h, dst):
        off = pl.multiple_of(j * tk, tk)
        dst[h] = _dot_nt(k_ref[0, pl.ds(off, tk), h * HEAD_PAD:(h + 1) * HEAD_PAD],
                         q_ref[0, :, h * HEAD_PAD:(h + 1) * HEAD_PAD])

    def softmax(h, src, key_chunk0):
        s = src[h]
        if key_chunk0 is not None:
            ck = lax.broadcasted_iota(jnp.int32, (tk, tq), 0) // CHUNK + key_chunk0
            cq = lax.broadcasted_iota(jnp.int32, (tk, tq), 1) // CHUNK
            s = jnp.where(ck <= cq, s, NEG_BIG)
        m_old = m_ref[h]
        m_new = jnp.maximum(m_old, jnp.max(s, axis=0, keepdims=True))
        p_ref[h] = jnp.exp2((s - m_new).astype(BF16))
        a_ref[h] = jnp.exp2(m_old - m_new)
        m_ref[h] = m_new

    def stage(j, src, dst, key_chunk0=None):
        flush_values(jnp.maximum(j - 1, 0))
        for h in heads:
            if dst is not None:
                scores(j + 1, h, dst)
            softmax(h, src, key_chunk0)

    for h in heads:
        scores(0, h, s_ref)

    @pl.loop(0, qi)
    def _(t):
        stage(2 * t, s_ref, s2_ref)
        stage(2 * t + 1, s2_ref, s_ref)

    stage(2 * qi, s_ref, s2_ref, 0)
    stage(2 * qi + 1, s2_ref, None, tk // CHUNK)
    flush_values(2 * qi + 1)
    outs = []
    for h in heads:
        acc = acc_ref[h]
        o = acc[:dv] / acc[dv:dv + 1]
        ms = jnp.sum(o * o, axis=0, keepdims=True) * (1.0 / dv)
        outs.append(o * lax.rsqrt(ms + NORM_EPS))
    o_t = jnp.concatenate(outs, axis=0)
    o_ref[0] = (o_t.T * g_ref[...]).astype(BF16)


def _mla_attention(q, k, vt, g, seq, tk, nh):
    tq = 2 * tk
    b = q.shape[0] // seq
    hw = q.shape[1]
    groups = hw // (nh * HEAD_PAD)
    ow = nh * MLA_V_DIM
    q3 = q.reshape(b, seq, hw)
    k3 = k.reshape(b, seq, hw)
    return pl.pallas_call(
        functools.partial(_attn_kernel, tq=tq, tk=tk, nh=nh),
        grid=(b, groups, seq // tq),
        in_specs=[pl.BlockSpec((1, tq, nh * HEAD_PAD), lambda bi, h, i: (bi, i, h)),
                  pl.BlockSpec((1, seq, nh * HEAD_PAD), lambda bi, h, i: (bi, 0, h)),
                  pl.BlockSpec((ow, seq), lambda bi, h, i: (h, bi)),
                  pl.BlockSpec((1, ow), lambda bi, h, i: (0, h))],
        out_specs=pl.BlockSpec((1, tq, ow), lambda bi, h, i: (bi, i, h)),
        out_shape=jax.ShapeDtypeStruct((b, seq, groups * ow), BF16),
        scratch_shapes=[pltpu.VMEM((nh, 1, tq), F32), pltpu.VMEM((nh, 1, tq), F32),
                        pltpu.VMEM((nh, tk, tq), BF16),
                        pltpu.VMEM((nh, MLA_V_DIM + ATTN_ONES_ROWS, tq), F32),
                        pltpu.VMEM((nh, tk, tq), F32), pltpu.VMEM((nh, tk, tq), F32)],
        compiler_params=_params("parallel", "parallel", "arbitrary"),
        name="mla_attention",
    )(q3, k3, vt, g)


def _rwkv_prep_kernel(*refs, first):
    if first:
        (x_ref, w2_ref, bd_ref, w0_ref, a0_ref, kk_ref_p, ka_ref,
         r_o, k_o, v_o, kk_o, kka_o, lw_o, g_o) = refs
        vf_ref = v0_ref = None
    else:
        (x_ref, vf_ref, w2_ref, bd_ref, w0_ref, a0_ref, kk_ref_p, ka_ref, v0_ref,
         r_o, k_o, v_o, kk_o, kka_o, lw_o, g_o) = refs
    rw = RWKV_WIDTH
    r_o[...] = x_ref[:, 0:rw]
    k = x_ref[:, rw:2 * rw].astype(F32)
    v = x_ref[:, 2 * rw:3 * rw].astype(F32)
    tail = x_ref[:, 3 * rw:3 * rw + LORA_PAD].astype(F32)
    lane = lax.broadcasted_iota(jnp.int32, tail.shape, 1)
    g_lo = DECAY_RANK + ICL_RANK
    g_hi = g_lo + GATE_RANK
    is_gate = (lane >= g_lo) & (lane < g_hi)
    th = jnp.tanh(jnp.where(is_gate, 0.5 * tail, tail))
    act = jnp.where(lane < DECAY_RANK, th, jnp.where(is_gate, 0.5 + 0.5 * th, tail))
    lo = _dot(act.astype(BF16), w2_ref[...])
    lw_o[...] = -math.exp(-0.5) * _sigmoid(w0_ref[...] + lo[:, 0:rw])
    a = _sigmoid(a0_ref[...] + lo[:, rw:2 * rw])
    g_o[...] = lo[:, 2 * rw:3 * rw].astype(BF16)
    if not first:
        mix = _sigmoid(v0_ref[...] + lo[:, 3 * rw:4 * rw])
        v = v + (vf_ref[...].astype(F32) - v) * mix
    v_o[...] = v.astype(BF16)
    kk = k * kk_ref_p[...]
    sq = kk * kk
    sq_hi = sq.astype(BF16)
    sq_lo = (sq - sq_hi.astype(F32)).astype(BF16)
    ss = _dot(sq_hi, bd_ref[...]) + _dot(sq_lo, bd_ref[...])
    kk = kk * lax.rsqrt(jnp.maximum(ss, 1e-24))
    kk_o[...] = kk.astype(BF16)
    kka_o[...] = (kk * a).astype(BF16)
    k_o[...] = (k * (1.0 + (a - 1.0) * ka_ref[...])).astype(BF16)


def _rwkv_prep(x, vf, w2, bd, w0, a0, k_k, k_a, v0, tm):
    t = x.shape[0]
    rw = RWKV_WIDTH
    first = vf is None
    row = lambda w: pl.BlockSpec((tm, w), lambda i: (i, 0))
    vec = _full((1, rw))
    if first:
        args = (x, w2, bd, w0, a0, k_k, k_a)
        in_specs = [row(RW_C_PAD), _full(w2.shape), _full(bd.shape), vec, vec, vec, vec]
    else:
        args = (x, vf, w2, bd, w0, a0, k_k, k_a, v0)
        in_specs = [row(RW_C_PAD), row(rw), _full(w2.shape), _full(bd.shape),
                    vec, vec, vec, vec, vec]
    out_dt = [BF16, BF16, BF16, BF16, BF16, F32, BF16]
    return pl.pallas_call(
        functools.partial(_rwkv_prep_kernel, first=first),
        grid=(t // tm,),
        in_specs=in_specs,
        out_specs=[row(rw)] * 7,
        out_shape=[jax.ShapeDtypeStruct((t, rw), dt) for dt in out_dt],
        compiler_params=_params("parallel"),
        name="rwkv_prep",
    )(*args)


_DIMS = {"nn": (((1,), (0,)), ((), ())), "nt": (((1,), (1,)), ((), ())),
         "tn": (((0,), (0,)), ((), ()))}


def _split(x):
    hi = x.astype(BF16)
    return hi, (x - hi.astype(F32)).astype(BF16)


def _mm(a, b, mode, dims="nn"):
    dg = lambda x, y: lax.dot_general(x, y, _DIMS[dims], preferred_element_type=F32)
    if mode == "bf16":
        return dg(a.astype(BF16), b.astype(BF16))
    if mode == "x3":
        a_hi, a_lo = _split(a)
        b_hi, b_lo = _split(b)
        return dg(a_hi, b_hi) + (dg(a_lo, b_hi) + dg(a_hi, b_lo))
    if mode == "ax":
        n = b.shape[1]
        out = dg(a.astype(BF16), jnp.concatenate(_split(b), axis=1))
        return out[:, :n] + out[:, n:]
    return lax.dot_general(a, b, _DIMS[dims], preferred_element_type=F32, precision=HI)


def _unit_lower_solve(a, rhs, row, col, eye, mode):
    return _unit_lower_solve_many([a], [rhs], row, col, eye, mode)[0]


def _unit_lower_solve_many(a_list, rhs_list, row, col, eye, mode):
    d16 = (row // 16) == (col // 16)
    d32 = (row // 32) == (col // 32)
    n = range(len(a_list))
    d = [jnp.where(d16, a, 0.0) for a in a_list]
    x = [eye + d[i] for i in n]
    d2 = [_mm(d[i], d[i], mode) for i in n]
    t = [_mm(d2[i], jnp.concatenate([x[i], d2[i]], axis=1), mode) for i in n]
    x = [x[i] + t[i][:, :128] for i in n]
    d4 = [t[i][:, 128:] for i in n]
    t = [_mm(d4[i], jnp.concatenate([x[i], d4[i]], axis=1), mode) for i in n]
    x = [x[i] + t[i][:, :128] for i in n]
    d8 = [t[i][:, 128:] for i in n]
    t = [_mm(d8[i], x[i], mode) for i in n]
    x = [x[i] + t[i] for i in n]
    e = [jnp.where(d32, a_list[i], 0.0) - d[i] for i in n]
    t = [_mm(e[i], x[i], mode) for i in n]
    t = [_mm(x[i], t[i], mode) for i in n]
    x = [x[i] + t[i] for i in n]
    f = [jnp.where(d32, 0.0, a_list[i]) for i in n]
    z = [_mm(x[i], rhs_list[i], mode) for i in n]
    t = [_mm(f[i], z[i], mode) for i in n]
    t = [_mm(x[i], t[i], mode) for i in n]
    return [z[i] + t[i] for i in n]


def _wkv_kernel(r_ref, k_ref, v_ref, kk_ref, kka_ref, lw_ref, g_ref,
                rk_ref, lnw_ref, lnb_ref, o_ref, s_ref, *, nch, modes):
    m_quad, m_inv, m_state = modes
    c_len = WKV_CHUNK

    npair = s_ref.shape[0]

    @pl.when(pl.program_id(1) == 0)
    def _():
        s_ref[...] = jnp.zeros_like(s_ref)

    row = lax.broadcasted_iota(jnp.int32, (128, 128), 0)
    col = lax.broadcasted_iota(jnp.int32, (128, 128), 1)
    same = (row // c_len) == (col // c_len)
    mask_s = same & (col < row)
    mask_i = same & (col <= row)
    eye = jnp.where(row == col, 1.0, 0.0)
    lane0 = lax.broadcasted_iota(jnp.int32, (c_len, 128), 1) < RWKV_HEAD
    tr = lax.broadcasted_iota(jnp.int32, (c_len, c_len), 0)
    tc = lax.broadcasted_iota(jnp.int32, (c_len, c_len), 1)
    ltri = jnp.where(tc <= tr, 1.0, 0.0)

    def stack(x):
        return jnp.concatenate([jnp.where(lane0, x, 0.0), jnp.where(lane0, 0.0, x)], axis=0)

    def head_sums(x):
        first = jnp.sum(jnp.where(lane0, x, 0.0), axis=-1, keepdims=True)
        both = jnp.sum(x, axis=-1, keepdims=True)
        return jnp.where(lane0, first, both - first)

    par = []
    for c, hp in [(c, hp) for c in range(nch) for hp in range(npair)]:
        sl = pl.ds(c * c_len, c_len)
        ln = slice(hp * LANE, (hp + 1) * LANE)
        r = r_ref[0, sl, ln].astype(F32)
        k = k_ref[0, sl, ln].astype(F32)
        v = v_ref[0, sl, ln].astype(F32)
        kk = kk_ref[0, sl, ln].astype(F32)
        kka = kka_ref[0, sl, ln].astype(F32)
        lw = lw_ref[0, sl, ln]
        cum = _mm(ltri, lw, "ax")
        tot = cum[c_len - 1:c_len, :]
        p_in = jnp.exp(cum)
        p_ex = jnp.exp(cum - lw)
        p_inv = jnp.exp(-cum)
        p_end = jnp.exp(tot - cum)
        at = stack(-kk * p_ex)
        rt = stack(r * p_in)
        bt = kka * p_inv
        kt = k * p_inv
        par.append(dict(
            r=r, k=k, v=v, tot=tot, at=at, rt=rt, vs=stack(v),
            bkh=jnp.concatenate([stack(kka * p_end), stack(k * p_end)], axis=0),
            lhs=jnp.concatenate([at, rt], axis=0),
            rhs=jnp.concatenate([bt, bt, kt, kt], axis=0)))
    for p in par:
        quad = _mm(p["lhs"], p["rhs"], m_quad, "nt")
        p["a_ab"] = jnp.where(mask_s, quad[:128, :128], 0.0)
        p["a_ak"] = jnp.where(mask_s, quad[:128, 128:], 0.0)
        p["a_r"] = jnp.concatenate([jnp.where(mask_i, quad[128:, :128], 0.0),
                                    jnp.where(mask_i, quad[128:, 128:], 0.0)], axis=1)
    for p in par:
        p["akv"] = _mm(p["a_ak"], p["vs"], m_quad)
    wus = _unit_lower_solve_many(
        [p["a_ab"] for p in par],
        [jnp.concatenate([p["at"], p["akv"]], axis=1) for p in par], row, col, eye, m_inv)

    pairs = range(npair)
    state = [s_ref[hp] for hp in pairs]
    for c in range(nch):
        sl = pl.ds(c * c_len, c_len)
        pc = [par[c * npair + hp] for hp in pairs]
        wu = [wus[c * npair + hp] for hp in pairs]
        m1 = [_mm(jnp.concatenate([wu[hp][:, :128], pc[hp]["rt"]], axis=0), state[hp],
                  m_state, "nt") for hp in pairs]
        uv = [jnp.concatenate([m1[hp][:128] + wu[hp][:, 128:], pc[hp]["vs"]], axis=0)
              for hp in pairs]
        upd = [_mm(uv[hp], pc[hp]["bkh"], m_state, "tn") for hp in pairs]
        state = [state[hp] * jnp.exp(pc[hp]["tot"]) + upd[hp] for hp in pairs]
        ys = [m1[hp][128:] + _mm(pc[hp]["a_r"], uv[hp], m_state) for hp in pairs]
        y = [ys[hp][:c_len] + ys[hp][c_len:] for hp in pairs]
        mean = [head_sums(y[hp]) * (1.0 / RWKV_HEAD) for hp in pairs]
        yc = [y[hp] - mean[hp] for hp in pairs]
        var = [head_sums(yc[hp] * yc[hp]) * (1.0 / RWKV_HEAD) for hp in pairs]
        for hp in pairs:
            ln = slice(hp * LANE, (hp + 1) * LANE)
            r, k, v = pc[hp]["r"], pc[hp]["k"], pc[hp]["v"]
            yn = yc[hp] * lax.rsqrt(var[hp] + GN_EPS) * lnw_ref[:, ln] + lnb_ref[:, ln]
            bonus = head_sums(r * k * rk_ref[:, ln])
            yn = yn + bonus * v
            o_ref[0, sl, ln] = (yn * g_ref[0, sl, ln].astype(F32)).astype(BF16)
    for hp in pairs:
        s_ref[hp] = state[hp]


def _wkv(r, k, v, kk, kka, lw, g, r_k, ln_w, ln_b, rows, modes):
    b, s, w = r.shape
    blk = pl.BlockSpec((1, rows, w), lambda bi, i: (bi, i, 0))
    kern = functools.partial(_wkv_kernel, nch=rows // WKV_CHUNK, modes=modes)
    return pl.pallas_call(
        kern,
        grid=(b, s // rows),
        in_specs=[blk] * 7 + [_full((1, w))] * 3,
        out_specs=blk,
        out_shape=jax.ShapeDtypeStruct((b, s, w), BF16),
        scratch_shapes=[pltpu.VMEM((w // LANE, LANE, LANE), F32)],
        compiler_params=_params("parallel", "arbitrary"),
        name="wkv7",
    )(r, k, v, kk, kka, lw, g, r_k, ln_w, ln_b)


def _mem_kv_kernel(m_ref, g_ref, w_ref, o_ref):
    d = m_ref.shape[-1]
    mn = _rms(m_ref[...], g_ref[...], d).astype(BF16)
    o_ref[...] = _dot(mn, w_ref[0]).astype(BF16)


def _mem_kv(mem2d, g, wkv, tm):
    t, d = mem2d.shape
    nl, _, n = wkv.shape
    return pl.pallas_call(
        _mem_kv_kernel,
        grid=(nl, t // tm),
        in_specs=[pl.BlockSpec((tm, d), lambda l, i: (i, 0)), _full((1, d)),
                  pl.BlockSpec((1, d, n), lambda l, i: (l, 0, 0))],
        out_specs=pl.BlockSpec((tm, n), lambda l, i: (i, l)),
        out_shape=jax.ShapeDtypeStruct((t, nl * n), BF16),
        compiler_params=_params("parallel", "parallel"),
        name="mem_kv",
    )(mem2d, g, wkv)


def _mix_out_kernel(h_ref, att_ref, rw_ref, woa_ref, wob_ref, g_ref, wq_ref,
                    mk_ref, mv_ref, wo_ref, o_ref, *, q_scale):
    d = h_ref.shape[-1]
    h1 = h_ref[...] + _dot(att_ref[...], woa_ref[0]) + _dot(rw_ref[...], wob_ref[0])
    hn = _rms(h1, g_ref[0], d).astype(BF16)
    q = (_dot(hn, wq_ref[0]) * q_scale).astype(BF16)
    dh = d // CA_HEADS
    outs = []
    for hd in range(CA_HEADS):
        sl = slice(hd * dh, (hd + 1) * dh)
        s = _dot_nt(q[:, sl], mk_ref[:, sl])
        m = jnp.max(s, axis=-1, keepdims=True)
        p = jnp.exp2(s - m)
        l = jnp.sum(p, axis=-1, keepdims=True)
        outs.append((_dot(p.astype(BF16), mv_ref[:, sl]) / l).astype(BF16))
    o = jnp.concatenate(outs, axis=1)
    o_ref[...] = h1 + _dot(o, wo_ref[0])


def _mix_out(h, att, rw, woa, wob, g, wq, memkv, wo, layer, seq, mem_len, tm):
    t, d = h.shape
    tiles = seq // tm
    q_scale = (d // CA_HEADS) ** -0.5 * LOG2E
    row = lambda w: pl.BlockSpec((tm, w), lambda i: (i, 0))
    lay = lambda a: pl.BlockSpec((1,) + a.shape[1:], lambda i: (layer,) + (0,) * (a.ndim - 1))
    return pl.pallas_call(
        functools.partial(_mix_out_kernel, q_scale=q_scale),
        grid=(t // tm,),
        in_specs=[row(d), row(att.shape[1]), row(rw.shape[1]), lay(woa), lay(wob), lay(g),
                  lay(wq),
                  pl.BlockSpec((mem_len, d), lambda i: (i // tiles, 2 * layer)),
                  pl.BlockSpec((mem_len, d), lambda i: (i // tiles, 2 * layer + 1)),
                  lay(wo)],
        out_specs=row(d),
        out_shape=jax.ShapeDtypeStruct((t, d), F32),
        compiler_params=_params("parallel"),
        name="mix_out_cross_attention",
    )(h, att, rw, woa, wob, g, wq, memkv, memkv, wo)


def _mlp_kernel(h_ref, g_ref, wu_ref, wd_ref, gf_ref, o_ref, *, last, ff_chunk):
    d = h_ref.shape[-1]
    h = h_ref[...]
    hn = _rms(h, g_ref[0], d).astype(BF16)
    acc = h
    for c in range(wu_ref.shape[-1] // ff_chunk):
        sl = slice(c * ff_chunk, (c + 1) * ff_chunk)
        u = jnp.maximum(_dot(hn, wu_ref[0, :, sl]), 0.0)
        acc = acc + _dot((u * u).astype(BF16), wd_ref[0, sl, :])
    if last:
        acc = _rms(acc, gf_ref[...], d)
    o_ref[...] = acc


def _mlp(h, g, wu, wd, gf, layer, last, tm):
    t, d = h.shape
    row = pl.BlockSpec((tm, d), lambda i: (i, 0))
    lay = lambda a: pl.BlockSpec((1,) + a.shape[1:], lambda i: (layer,) + (0,) * (a.ndim - 1))
    return pl.pallas_call(
        functools.partial(_mlp_kernel, last=last, ff_chunk=1024),
        grid=(t // tm,),
        in_specs=[row, lay(g), lay(wu), lay(wd), _full((1, d))],
        out_specs=row,
        out_shape=jax.ShapeDtypeStruct((t, d), F32),
        compiler_params=_params("parallel"),
        name="mlp",
    )(h, g, wu, wd, gf)


def _pad_cols(a, n):
    return jnp.pad(a, ((0, 0), (0, n - a.shape[1])))


def _swap_halves(a):
    half = a.shape[-1] // 2
    return jnp.concatenate([a[..., half:], a[..., :half]], axis=-1)


def _in_weights(w_in, mu):
    cq = w_in[:, :MLA_Q_RANK]
    ckv = w_in[:, MLA_Q_RANK:MLA_Q_RANK + MLA_KV_RANK]
    kr = w_in[:, MLA_Q_RANK + MLA_KV_RANK:MLA_COLS]
    mla = jnp.concatenate([_pad_cols(cq, 256), ckv, kr, _swap_halves(kr)], axis=1)
    w = jnp.concatenate([_pad_cols(mla, MLA_C_PAD), _pad_cols(w_in[:, MLA_COLS:], RW_C_PAD)], axis=1)
    return w.astype(BF16), _pad_cols(mu[None, :], RW_C_PAD)


def _head_pad(a, n_heads, per_head, lo, hi, at):
    rows = a.shape[0]
    a = a.reshape(rows, n_heads, per_head)[:, :, lo:hi]
    a = jnp.pad(a, ((0, 0), (0, 0), (at, HEAD_PAD - at - (hi - lo))))
    return a.reshape(rows, n_heads * HEAD_PAD)


def _mla_weights(w_uq, w_ukv):
    dq = MLA_NOPE_DIM + MLA_ROPE_DIM
    h = MLA_HEADS
    wq3 = w_uq.reshape(MLA_Q_RANK, h, dq)
    rope_sw = _swap_halves(wq3[:, :, MLA_NOPE_DIM:]).reshape(MLA_Q_RANK, h * MLA_ROPE_DIM)
    wqa = _head_pad(w_uq, h, dq, 0, dq, 0)
    wqb = _head_pad(rope_sw, h, MLA_ROPE_DIM, 0, MLA_ROPE_DIM, MLA_NOPE_DIM)
    pad_q = ((0, 256 - MLA_Q_RANK), (0, 0))
    wqa = jnp.pad(wqa, pad_q)
    wqb = jnp.pad(wqb, pad_q)
    dkv = MLA_NOPE_DIM + MLA_V_DIM
    wk = _head_pad(w_ukv, h, dkv, 0, MLA_NOPE_DIM, 0)
    wvt = w_ukv.reshape(MLA_KV_RANK, h, dkv)[:, :, MLA_NOPE_DIM:].reshape(MLA_KV_RANK, -1).T
    eye = jnp.eye(MLA_ROPE_DIM, dtype=F32)
    place = _head_pad(jnp.tile(eye, (1, h)), h, MLA_ROPE_DIM, 0, MLA_ROPE_DIM, MLA_NOPE_DIM)
    zeros = jnp.zeros_like(place)
    pad_rows = jnp.zeros((LANE - 2 * MLA_ROPE_DIM, h * HEAD_PAD), F32)
    pa = jnp.concatenate([place, zeros, pad_rows], axis=0)
    pb = jnp.concatenate([zeros, place, pad_rows], axis=0)
    wka = jnp.concatenate([wk, pa], axis=0)
    return tuple(a.astype(BF16) for a in (wqa, wqb, wka, pb, wvt))


def _lora_weights(w2, a2, g2, v2):
    rw = RWKV_WIDTH
    z = lambda r: jnp.zeros((r, rw), F32)
    rows = [
        jnp.concatenate([w2, z(DECAY_RANK), z(DECAY_RANK), z(DECAY_RANK)], axis=1),
        jnp.concatenate([z(ICL_RANK), a2, z(ICL_RANK), z(ICL_RANK)], axis=1),
        jnp.concatenate([z(GATE_RANK), z(GATE_RANK), g2, z(GATE_RANK)], axis=1),
        jnp.concatenate([z(VRES_RANK)] * 3 + [v2 if v2 is not None else z(VRES_RANK)], axis=1),
    ]
    w = jnp.concatenate(rows, axis=0)
    return jnp.pad(w, ((0, LORA_PAD - w.shape[0]), (0, 0))).astype(BF16)


def _block_diag(n, blk, value):
    i = jnp.arange(n)
    return jnp.where((i[:, None] // blk) == (i[None, :] // blk), value, 0.0)


def kernel(x, mem, positions, mix_norm, w_in_first, w_in_rest, shift_mu_first, shift_mu_rest, mla_q_norm, mla_w_uq, mla_kv_norm, mla_w_ukv, mla_out_norm, rwkv_w0, rwkv_w2, rwkv_a0, rwkv_a2, rwkv_g2, rwkv_v0, rwkv_v2, rwkv_k_k, rwkv_k_a, rwkv_r_k, rwkv_ln_w, rwkv_ln_b, w_out, ca_norm, mem_norm, ca_wq, ca_wkv, ca_wo, mlp_norm, mlp_w_up, mlp_w_down, final_norm):
    b, s, d = x.shape
    depth = mix_norm.shape[0]
    mem_len = mem.shape[1]
    t = b * s
    tm = min(512, s)
    tq = min(256, s)
    wkv_rows = min(128, s)
    rw = RWKV_WIDTH

    inv = ROPE_THETA ** (-jnp.arange(0, MLA_ROPE_DIM, 2, dtype=F32) / MLA_ROPE_DIM)
    inv_row = jnp.concatenate([jnp.zeros((MLA_NOPE_DIM,), F32), inv, inv,
                               jnp.zeros((HEAD_PAD - MLA_NOPE_DIM - MLA_ROPE_DIM,), F32)])[None, :]
    ct, st = _rope_tables(positions.astype(F32).reshape(t, 1), inv_row, tm)

    memkv = _mem_kv(mem.reshape(b * mem_len, d), mem_norm[None, :], ca_wkv.astype(BF16),
                    min(512, b * mem_len))

    woa = w_out[:, :MLA_HEADS * MLA_V_DIM].astype(BF16)
    wob = w_out[:, MLA_HEADS * MLA_V_DIM:].astype(BF16)
    wq_b = ca_wq.astype(BF16)
    wo_b = ca_wo.astype(BF16)
    wu_b = mlp_w_up.astype(BF16)
    wd_b = mlp_w_down.astype(BF16)
    ca_g = ca_norm[:, None, :]
    mlp_g = mlp_norm[:, None, :]
    bd512 = _block_diag(rw, RWKV_HEAD, 1.0).astype(BF16)

    h = x.reshape(t, d)
    v_first = None
    for l in range(depth):
        if l == 0:
            w_in, mu, v0, v2 = w_in_first, shift_mu_first, None, None
        else:
            w_in, mu, v0, v2 = w_in_rest[l - 1], shift_mu_rest[l - 1], rwkv_v0[l - 1], rwkv_v2[l - 1]
        w_all, mu_p = _in_weights(w_in, mu)
        mla_c, rw_c = _in_proj(h, mix_norm[l][None, :], w_all, mu_p, s, tm)

        wqa, wqb, wka, pb, wvt = _mla_weights(mla_w_uq[l], mla_w_ukv[l])
        gq = _pad_cols(mla_q_norm[l][None, :], 256)
        q, k, vt = _mla_prep(mla_c, ct, st, gq, mla_kv_norm[l][None, :], wqa, wqb, wka, pb, wvt, tm)
        att = _mla_attention(q, k, vt, mla_out_norm[l][None, :], s, tq, ATTN_HEADS_PER_STEP)

        w2c = _lora_weights(rwkv_w2[l], rwkv_a2[l], rwkv_g2[l], v2)
        vec = lambda a: a.reshape(1, rw)
        r_, k_, v_, kk_, kka_, lw_, g_ = _rwkv_prep(
            rw_c, v_first, w2c, bd512, vec(rwkv_w0[l]), vec(rwkv_a0[l]), vec(rwkv_k_k[l]),
            vec(rwkv_k_a[l]), None if v0 is None else vec(v0), tm)
        if l == 0:
            v_first = v_
        sh = lambda a: a.reshape(b, s, rw)
        y = _wkv(sh(r_), sh(k_), sh(v_), sh(kk_), sh(kka_), sh(lw_), sh(g_),
                 vec(rwkv_r_k[l]), vec(rwkv_ln_w[l]), vec(rwkv_ln_b[l]), wkv_rows, WKV_MODES)

        h = _mix_out(h, att.reshape(t, -1), y.reshape(t, rw), woa, wob, ca_g, wq_b, memkv, wo_b,
                     l, s, mem_len, tm)
        h = _mlp(h, mlp_g, wu_b, wd_b, final_norm[None, :], l, l == depth - 1, tm)
    return h.reshape(b, s, d)
```

```python
import functools
import math

import jax
import jax.numpy as jnp
from jax import lax
from jax.experimental import pallas as pl
from jax.experimental.pallas import tpu as pltpu

F32 = jnp.float32
BF16 = jnp.bfloat16

CHUNK = 64
NORM_EPS = 1e-6
MLA_HEADS = 8
MLA_V_DIM = 64
MLA_NOPE_DIM = 64
MLA_ROPE_DIM = 32
MLA_Q_RANK = 192
MLA_KV_RANK = 128
ROPE_THETA = 10000.0
RWKV_HEAD = 64
RWKV_HEADS = 8
RWKV_WIDTH = 512
DECAY_RANK = 32
ICL_RANK = 32
GATE_RANK = 96
VRES_RANK = 32
GN_EPS = 64e-5
CA_HEADS = 4
MLA_COLS = MLA_Q_RANK + MLA_KV_RANK + MLA_ROPE_DIM

LANE = 128
HEAD_PAD = 128
MLA_C_PAD = 512
RW_C_PAD = 1792
LORA_PAD = 256
WKV_CHUNK = 64
ATTN_HEADS_PER_STEP = 4
ATTN_ONES_ROWS = 16
LOG2E = 1.4426950408889634
NEG_BIG = -1e30
VMEM_LIMIT = 56 * 1024 * 1024

HI = lax.Precision.HIGHEST
WKV_MODES = ("bf16", "bf16", "bf16")


def _dot(a, b, precision=None):
    return jnp.dot(a, b, preferred_element_type=F32, precision=precision)


def _dot_nt(a, b, precision=None):
    return lax.dot_general(a, b, (((1,), (1,)), ((), ())),
                           preferred_element_type=F32, precision=precision)


def _dot_tn(a, b, precision=None):
    return lax.dot_general(a, b, (((0,), (0,)), ((), ())),
                           preferred_element_type=F32, precision=precision)


def _rms(x, g, n, eps=NORM_EPS):
    ms = jnp.sum(x * x, axis=-1, keepdims=True) * (1.0 / n)
    return x * lax.rsqrt(ms + eps) * g


def _sigmoid(x):
    return 0.5 + 0.5 * jnp.tanh(0.5 * x)


def _params(*sem):
    return pltpu.CompilerParams(dimension_semantics=sem, vmem_limit_bytes=VMEM_LIMIT)


def _full(shape):
    nd = len(shape)
    return pl.BlockSpec(shape, lambda *_: (0,) * nd)


def _rope_kernel(pos_ref, inv_ref, ct_ref, st_ref):
    ang = pos_ref[...] * inv_ref[...]
    c = jnp.cos(ang)
    s = jnp.sin(ang)
    lane = lax.broadcasted_iota(jnp.int32, ang.shape, 1)
    half = MLA_NOPE_DIM + MLA_ROPE_DIM // 2
    end = MLA_NOPE_DIM + MLA_ROPE_DIM
    ct_ref[...] = jnp.where(lane < MLA_NOPE_DIM, 1.0, jnp.where(lane < end, c, 0.0))
    st_ref[...] = jnp.where(lane < MLA_NOPE_DIM, 0.0,
                            jnp.where(lane < half, -s, jnp.where(lane < end, s, 0.0)))


def _rope_tables(pos_f, inv_row, tm):
    t = pos_f.shape[0]
    return pl.pallas_call(
        _rope_kernel,
        grid=(t // tm,),
        in_specs=[pl.BlockSpec((tm, 1), lambda i: (i, 0)), _full((1, LANE))],
        out_specs=[pl.BlockSpec((tm, LANE), lambda i: (i, 0))] * 2,
        out_shape=[jax.ShapeDtypeStruct((t, LANE), F32)] * 2,
        compiler_params=_params("parallel"),
        name="rope_tables",
    )(pos_f, inv_row)


def _mla_latents(c, ct_ref, st_ref, gq_ref, gkv_ref, wqa_ref, wqb_ref, wka_ref, pb_ref,
                 wv_ref, ones_ref, q_ref, k_ref, v_ref, q_scale):
    nq = _rms(c[:, :256], gq_ref[...], MLA_Q_RANK).astype(BF16)
    nkv = _rms(c[:, 256:384], gkv_ref[...], MLA_KV_RANK).astype(BF16)
    kr = c[:, 384:512].astype(BF16)
    ct = jnp.tile(ct_ref[...], (1, MLA_HEADS))
    st = jnp.tile(st_ref[...], (1, MLA_HEADS))
    q = _dot(nq, wqa_ref[...]) * ct + _dot(nq, wqb_ref[...]) * st
    q_ref[...] = (q * q_scale).astype(BF16)
    x = jnp.concatenate([nkv, kr], axis=1)
    k = _dot(x, wka_ref[...]) * ct + _dot(kr, pb_ref[...]) * st
    k_ref[...] = k.astype(BF16)
    v_ref[...] = (_dot_nt(wv_ref[...], nkv) + ones_ref[...]).astype(BF16)


def _rwkv_elementwise(x, vf_ref, w2_ref, bd_ref, w0_ref, a0_ref, kk_ref_p, ka_ref, v0_ref,
                      r_o, k_o, v_o, kk_o, kka_o, lw_o, g_o):
    rw = RWKV_WIDTH
    r_o[...] = x[:, 0:rw].astype(BF16)
    k = x[:, rw:2 * rw]
    v = x[:, 2 * rw:3 * rw]
    tail = x[:, 3 * rw:3 * rw + LORA_PAD]
    lane = lax.broadcasted_iota(jnp.int32, tail.shape, 1)
    g_lo = DECAY_RANK + ICL_RANK
    g_hi = g_lo + GATE_RANK
    is_gate = (lane >= g_lo) & (lane < g_hi)
    th = jnp.tanh(jnp.where(is_gate, 0.5 * tail, tail))
    act = jnp.where(lane < DECAY_RANK, th, jnp.where(is_gate, 0.5 + 0.5 * th, tail))
    lo = _dot(act.astype(BF16), w2_ref[...])
    lw_o[...] = -math.exp(-0.5) * _sigmoid(w0_ref[...] + lo[:, 0:rw])
    a = _sigmoid(a0_ref[...] + lo[:, rw:2 * rw])
    g_o[...] = lo[:, 2 * rw:3 * rw].astype(BF16)
    if vf_ref is not None:
        mix = _sigmoid(v0_ref[...] + lo[:, 3 * rw:4 * rw])
        v = v + (vf_ref[...].astype(F32) - v) * mix
    v_o[...] = v.astype(BF16)
    kk = k * kk_ref_p[...]
    sq = kk * kk
    sq_hi = sq.astype(BF16)
    sq_lo = (sq - sq_hi.astype(F32)).astype(BF16)
    ss = _dot(sq_hi, bd_ref[...]) + _dot(sq_lo, bd_ref[...])
    kk = kk * lax.rsqrt(jnp.maximum(ss, 1e-24))
    kk_o[...] = kk.astype(BF16)
    kka_o[...] = (kk * a).astype(BF16)
    k_o[...] = (k * (1.0 + (a - 1.0) * ka_ref[...])).astype(BF16)


def _layer_in_kernel(*refs, first, tiles_per_seq, q_scale):
    n_in = 20 if first else 22
    ins, outs, carry_ref = refs[:n_in], refs[n_in:n_in + 10], refs[n_in + 10]
    h_ref, g_ref, w_ref, mu_ref = ins[:4]
    mla_ins = ins[4:14]
    if first:
        w2_ref, bd_ref, w0_ref, a0_ref, kkp_ref, ka_ref = ins[14:20]
        vf_ref = v0_ref = None
    else:
        vf_ref, w2_ref, bd_ref, w0_ref, a0_ref, kkp_ref, ka_ref, v0_ref = ins[14:22]
    q_ref, k_ref, vt_ref = outs[:3]
    rw_outs = outs[3:]

    i = pl.program_id(0)
    d = h_ref.shape[-1]
    hn = _rms(h_ref[...], g_ref[...], d).astype(BF16)
    proj = _dot(hn, w_ref[...])
    _mla_latents(proj[:, :MLA_C_PAD], *mla_ins, q_ref, k_ref, vt_ref, q_scale)

    z = proj[:, MLA_C_PAD:]
    tm = z.shape[0]

    @pl.when(i % tiles_per_seq == 0)
    def _():
        carry_ref[...] = jnp.zeros_like(carry_ref)

    prev = pltpu.roll(z, 1, axis=0)
    row = lax.broadcasted_iota(jnp.int32, z.shape, 0)
    prev = jnp.where(row == 0, carry_ref[7:8, :], prev)
    x = z + (prev - z) * mu_ref[...]
    carry_ref[...] = z[tm - 8:, :]
    _rwkv_elementwise(x, vf_ref, w2_ref, bd_ref, w0_ref, a0_ref, kkp_ref, ka_ref, v0_ref,
                      *rw_outs)


def _layer_in(h, g, w, mu, ct, st, gq, gkv, wqa, wqb, wka, pb, wvt, vf, w2, bd, w0, a0, k_k,
              k_a, v0, seq, tm):
    t, d = h.shape
    rw = RWKV_WIDTH
    hw = MLA_HEADS * HEAD_PAD
    vw = wvt.shape[0]
    first = vf is None
    ones_col = (jnp.arange(vw) % (MLA_V_DIM + ATTN_ONES_ROWS) == MLA_V_DIM).astype(F32)[:, None]
    q_scale = (MLA_NOPE_DIM + MLA_ROPE_DIM) ** -0.5 * LOG2E
    row = lambda width: pl.BlockSpec((tm, width), lambda i: (i, 0))
    vec = _full((1, rw))
    args = [h, g, w, mu, ct, st, gq, gkv, wqa, wqb, wka, pb, wvt, ones_col]
    in_specs = [row(d), _full((1, d)), _full(w.shape), _full((1, RW_C_PAD)), row(LANE), row(LANE),
                _full(gq.shape), _full(gkv.shape), _full(wqa.shape), _full(wqb.shape),
                _full(wka.shape), _full(pb.shape), _full(wvt.shape), _full(ones_col.shape)]
    if first:
        args += [w2, bd, w0, a0, k_k, k_a]
        in_specs += [_full(w2.shape), _full(bd.shape), vec, vec, vec, vec]
    else:
        args += [vf, w2, bd, w0, a0, k_k, k_a, v0]
        in_specs += [row(rw), _full(w2.shape), _full(bd.shape), vec, vec, vec, vec, vec]
    rw_dt = [BF16, BF16, BF16, BF16, BF16, F32, BF16]
    kern = functools.partial(_layer_in_kernel, first=first, tiles_per_seq=seq // tm,
                             q_scale=q_scale)
    return pl.pallas_call(
        kern,
        grid=(t // tm,),
        in_specs=in_specs,
        out_specs=[row(hw), row(hw), pl.BlockSpec((vw, tm), lambda i: (0, i))] + [row(rw)] * 7,
        out_shape=[jax.ShapeDtypeStruct((t, hw), BF16), jax.ShapeDtypeStruct((t, hw), BF16),
                   jax.ShapeDtypeStruct((vw, t), BF16)]
                  + [jax.ShapeDtypeStruct((t, rw), dt) for dt in rw_dt],
        scratch_shapes=[pltpu.VMEM((8, RW_C_PAD), F32)],
        compiler_params=_params("arbitrary"),
        name="layer_in",
    )(*args)


def _attn_kernel(q_ref, k_ref, vt_ref, g_ref, o_ref, m_ref, a_ref, p_ref, acc_ref, s_ref, s2_ref,
                 *, tq, tk, nh):
    qi = pl.program_id(2)
    dv = MLA_V_DIM
    heads = range(nh)

    ext = dv + ATTN_ONES_ROWS

    m_ref[...] = jnp.full_like(m_ref, NEG_BIG)
    a_ref[...] = jnp.ones_like(a_ref)
    p_ref[...] = jnp.zeros_like(p_ref)
    acc_ref[...] = jnp.zeros_like(acc_ref)

    def flush_values(j, q0=0):
        off = pl.multiple_of(j * tk, tk)
        pv = [_dot(vt_ref[h * ext:(h + 1) * ext, pl.ds(off, tk)], p_ref[h, :, q0:])
              for h in heads]
        for h in heads:
            acc_ref[h, :, q0:] = a_ref[h, :, q0:] * acc_ref[h, :, q0:] + pv[h]

    def scores(j, h, dst, q0=0):
        off = pl.multiple_of(j * tk, tk)
        dst[h, :, q0:] = _dot_nt(k_ref[0, pl.ds(off, tk), h * HEAD_PAD:(h + 1) * HEAD_PAD],
                                 q_ref[0, q0:, h * HEAD_PAD:(h + 1) * HEAD_PAD])

    def softmax(h, src, key0=None, q0=0):
        s = src[h, :, q0:]
        if key0 is not None:
            ck = (lax.broadcasted_iota(jnp.int32, s.shape, 0) + key0) // CHUNK
            cq = (lax.broadcasted_iota(jnp.int32, s.shape, 1) + q0) // CHUNK
            s = jnp.where(ck <= cq, s, NEG_BIG)
        m_old = m_ref[h, :, q0:]
        m_new = jnp.maximum(m_old, jnp.max(s, axis=0, keepdims=True))
        p_ref[h, :, q0:] = jnp.exp2((s - m_new).astype(BF16))
        a_ref[h, :, q0:] = jnp.exp2(m_old - m_new)
        m_ref[h, :, q0:] = m_new

    def stage(j, src, dst, key0=None, q0=0, next_q0=0):
        flush_values(jnp.maximum(j - 1, 0))
        for h in heads:
            if dst is not None:
                scores(j + 1, h, dst, next_q0)
            softmax(h, src, key0, q0)

    for h in heads:
        scores(0, h, s_ref)

    @pl.loop(0, qi)
    def _(t):
        stage(2 * t, s_ref, s2_ref)
        stage(2 * t + 1, s2_ref, s_ref)

    stage(2 * qi, s_ref, s2_ref, key0=0, next_q0=tk)
    stage(2 * qi + 1, s2_ref, None, key0=tk, q0=tk)
    flush_values(2 * qi + 1, q0=tk)
    outs = []
    for h in heads:
        acc = acc_ref[h]
        o = acc[:dv] / acc[dv:dv + 1]
        ms = jnp.sum(o * o, axis=0, keepdims=True) * (1.0 / dv)
        outs.append(o * lax.rsqrt(ms + NORM_EPS))
    o_t = jnp.concatenate(outs, axis=0)
    o_ref[0] = (o_t.T * g_ref[...]).astype(BF16)


def _mla_attention(q, k, vt, g, seq, tk, nh):
    tq = 2 * tk
    b = q.shape[0] // seq
    hw = q.shape[1]
    groups = hw // (nh * HEAD_PAD)
    ow = nh * MLA_V_DIM
    q3 = q.reshape(b, seq, hw)
    k3 = k.reshape(b, seq, hw)
    return pl.pallas_call(
        functools.partial(_attn_kernel, tq=tq, tk=tk, nh=nh),
        grid=(b, groups, seq // tq),
        in_specs=[pl.BlockSpec((1, tq, nh * HEAD_PAD), lambda bi, h, i: (bi, i, h)),
                  pl.BlockSpec((1, seq, nh * HEAD_PAD), lambda bi, h, i: (bi, 0, h)),
                  pl.BlockSpec((nh * (MLA_V_DIM + ATTN_ONES_ROWS), seq), lambda bi, h, i: (h, bi)),
                  pl.BlockSpec((1, ow), lambda bi, h, i: (0, h))],
        out_specs=pl.BlockSpec((1, tq, ow), lambda bi, h, i: (bi, i, h)),
        out_shape=jax.ShapeDtypeStruct((b, seq, groups * ow), BF16),
        scratch_shapes=[pltpu.VMEM((nh, 1, tq), F32), pltpu.VMEM((nh, 1, tq), F32),
                        pltpu.VMEM((nh, tk, tq), BF16),
                        pltpu.VMEM((nh, MLA_V_DIM + ATTN_ONES_ROWS, tq), F32),
                        pltpu.VMEM((nh, tk, tq), F32), pltpu.VMEM((nh, tk, tq), F32)],
        compiler_params=_params("parallel", "parallel", "arbitrary"),
        name="mla_attention",
    )(q3, k3, vt, g)


_DIMS = {"nn": (((1,), (0,)), ((), ())), "nt": (((1,), (1,)), ((), ())),
         "tn": (((0,), (0,)), ((), ()))}


def _split(x):
    hi = x.astype(BF16)
    return hi, (x - hi.astype(F32)).astype(BF16)


def _mm(a, b, mode, dims="nn"):
    dg = lambda x, y: lax.dot_general(x, y, _DIMS[dims], preferred_element_type=F32)
    if mode == "bf16":
        return dg(a.astype(BF16), b.astype(BF16))
    if mode == "x3":
        a_hi, a_lo = _split(a)
        b_hi, b_lo = _split(b)
        return dg(a_hi, b_hi) + (dg(a_lo, b_hi) + dg(a_hi, b_lo))
    if mode == "ax":
        n = b.shape[1]
        out = dg(a.astype(BF16), jnp.concatenate(_split(b), axis=1))
        return out[:, :n] + out[:, n:]
    return lax.dot_general(a, b, _DIMS[dims], preferred_element_type=F32, precision=HI)


def _unit_lower_solve(a, rhs, row, col, eye, mode):
    return _unit_lower_solve_many([a], [rhs], row, col, eye, mode)[0]


def _unit_lower_solve_many(a_list, rhs_list, row, col, eye, mode):
    d16 = (row // 16) == (col // 16)
    d32 = (row // 32) == (col // 32)
    n = range(len(a_list))
    d = [jnp.where(d16, a, 0.0) for a in a_list]
    x = [eye + d[i] for i in n]
    d2 = [_mm(d[i], d[i], mode) for i in n]
    t = [_mm(d2[i], jnp.concatenate([x[i], d2[i]], axis=1), mode) for i in n]
    x = [x[i] + t[i][:, :128] for i in n]
    d4 = [t[i][:, 128:] for i in n]
    t = [_mm(d4[i], jnp.concatenate([x[i], d4[i]], axis=1), mode) for i in n]
    x = [x[i] + t[i][:, :128] for i in n]
    d8 = [t[i][:, 128:] for i in n]
    t = [_mm(d8[i], x[i], mode) for i in n]
    x = [x[i] + t[i] for i in n]
    e = [jnp.where(d32, a_list[i], 0.0) - d[i] for i in n]
    t = [_mm(e[i], x[i], mode) for i in n]
    t = [_mm(x[i], t[i], mode) for i in n]
    x = [x[i] + t[i] for i in n]
    f = [jnp.where(d32, 0.0, a_list[i]) for i in n]
    z = [_mm(x[i], rhs_list[i], mode) for i in n]
    t = [_mm(f[i], z[i], mode) for i in n]
    t = [_mm(x[i], t[i], mode) for i in n]
    return [z[i] + t[i] for i in n]


def _wkv_kernel(r_ref, k_ref, v_ref, kk_ref, kka_ref, lw_ref, g_ref,
                rk_ref, lnw_ref, lnb_ref, o_ref, s_ref, *, nch, modes):
    m_quad, m_inv, m_state = modes
    c_len = WKV_CHUNK

    npair = s_ref.shape[0]

    @pl.when(pl.program_id(1) == 0)
    def _():
        s_ref[...] = jnp.zeros_like(s_ref)

    row = lax.broadcasted_iota(jnp.int32, (128, 128), 0)
    col = lax.broadcasted_iota(jnp.int32, (128, 128), 1)
    same = (row // c_len) == (col // c_len)
    mask_s = same & (col < row)
    mask_i = same & (col <= row)
    eye = jnp.where(row == col, 1.0, 0.0)
    lane0 = lax.broadcasted_iota(jnp.int32, (c_len, 128), 1) < RWKV_HEAD
    tr = lax.broadcasted_iota(jnp.int32, (c_len, c_len), 0)
    tc = lax.broadcasted_iota(jnp.int32, (c_len, c_len), 1)
    ltri = jnp.where(tc <= tr, 1.0, 0.0)

    def stack(x):
        return jnp.concatenate([jnp.where(lane0, x, 0.0), jnp.where(lane0, 0.0, x)], axis=0)

    def head_sums(x):
        first = jnp.sum(jnp.where(lane0, x, 0.0), axis=-1, keepdims=True)
        both = jnp.sum(x, axis=-1, keepdims=True)
        return jnp.where(lane0, first, both - first)

    par = []
    for c, hp in [(c, hp) for c in range(nch) for hp in range(npair)]:
        sl = pl.ds(c * c_len, c_len)
        ln = slice(hp * LANE, (hp + 1) * LANE)
        r = r_ref[0, sl, ln].astype(F32)
        k = k_ref[0, sl, ln].astype(F32)
        v = v_ref[0, sl, ln].astype(F32)
        kk = kk_ref[0, sl, ln].astype(F32)
        kka = kka_ref[0, sl, ln].astype(F32)
        lw = lw_ref[0, sl, ln]
        cum = _mm(ltri, lw, "ax")
        tot = cum[c_len - 1:c_len, :]
        p_in = jnp.exp(cum)
        p_ex = jnp.exp(cum - lw)
        p_inv = jnp.exp(-cum)
        p_end = jnp.exp(tot - cum)
        at = stack(-kk * p_ex)
        rt = stack(r * p_in)
        bt = kka * p_inv
        kt = k * p_inv
        par.append(dict(
            r=r, k=k, v=v, tot=tot, at=at, rt=rt, vs=stack(v),
            bkh=jnp.concatenate([stack(kka * p_end), stack(k * p_end)], axis=0),
            lhs=jnp.concatenate([at, rt], axis=0),
            rhs=jnp.concatenate([bt, bt, kt, kt], axis=0)))
    for p in par:
        quad = _mm(p["lhs"], p["rhs"], m_quad, "nt")
        p["a_ab"] = jnp.where(mask_s, quad[:128, :128], 0.0)
        p["a_ak"] = jnp.where(mask_s, quad[:128, 128:], 0.0)
        p["a_r"] = jnp.concatenate([jnp.where(mask_i, quad[128:, :128], 0.0),
                                    jnp.where(mask_i, quad[128:, 128:], 0.0)], axis=1)
    for p in par:
        p["akv"] = _mm(p["a_ak"], p["vs"], m_quad)
    wus = _unit_lower_solve_many(
        [p["a_ab"] for p in par],
        [jnp.concatenate([p["at"], p["akv"]], axis=1) for p in par], row, col, eye, m_inv)

    pairs = range(npair)
    state = [s_ref[hp] for hp in pairs]
    for c in range(nch):
        sl = pl.ds(c * c_len, c_len)
        pc = [par[c * npair + hp] for hp in pairs]
        wu = [wus[c * npair + hp] for hp in pairs]
        m1 = [_mm(jnp.concatenate([wu[hp][:, :128], pc[hp]["rt"]], axis=0), state[hp],
                  m_state, "nt") for hp in pairs]
        uv = [jnp.concatenate([m1[hp][:128] + wu[hp][:, 128:], pc[hp]["vs"]], axis=0)
              for hp in pairs]
        upd = [_mm(uv[hp], pc[hp]["bkh"], m_state, "tn") for hp in pairs]
        state = [state[hp] * jnp.exp(pc[hp]["tot"]) + upd[hp] for hp in pairs]
        ys = [m1[hp][128:] + _mm(pc[hp]["a_r"], uv[hp], m_state) for hp in pairs]
        y = [ys[hp][:c_len] + ys[hp][c_len:] for hp in pairs]
        mean = [head_sums(y[hp]) * (1.0 / RWKV_HEAD) for hp in pairs]
        yc = [y[hp] - mean[hp] for hp in pairs]
        var = [head_sums(yc[hp] * yc[hp]) * (1.0 / RWKV_HEAD) for hp in pairs]
        for hp in pairs:
            ln = slice(hp * LANE, (hp + 1) * LANE)
            r, k, v = pc[hp]["r"], pc[hp]["k"], pc[hp]["v"]
            yn = yc[hp] * lax.rsqrt(var[hp] + GN_EPS) * lnw_ref[:, ln] + lnb_ref[:, ln]
            bonus = head_sums(r * k * rk_ref[:, ln])
            yn = yn + bonus * v
            o_ref[0, sl, ln] = (yn * g_ref[0, sl, ln].astype(F32)).astype(BF16)
    for hp in pairs:
        s_ref[hp] = state[hp]


def _wkv(r, k, v, kk, kka, lw, g, r_k, ln_w, ln_b, rows, modes):
    b, s, w = r.shape
    blk = pl.BlockSpec((1, rows, w), lambda bi, i: (bi, i, 0))
    kern = functools.partial(_wkv_kernel, nch=rows // WKV_CHUNK, modes=modes)
    return pl.pallas_call(
        kern,
        grid=(b, s // rows),
        in_specs=[blk] * 7 + [_full((1, w))] * 3,
        out_specs=blk,
        out_shape=jax.ShapeDtypeStruct((b, s, w), BF16),
        scratch_shapes=[pltpu.VMEM((w // LANE, LANE, LANE), F32)],
        compiler_params=_params("parallel", "arbitrary"),
        name="wkv7",
    )(r, k, v, kk, kka, lw, g, r_k, ln_w, ln_b)


def _mem_kv_kernel(m_ref, g_ref, w_ref, o_ref):
    d = m_ref.shape[-1]
    mn = _rms(m_ref[...], g_ref[...], d).astype(BF16)
    o_ref[...] = _dot(mn, w_ref[0]).astype(BF16)


def _mem_kv(mem2d, g, wkv, tm):
    t, d = mem2d.shape
    nl, _, n = wkv.shape
    return pl.pallas_call(
        _mem_kv_kernel,
        grid=(nl, t // tm),
        in_specs=[pl.BlockSpec((tm, d), lambda l, i: (i, 0)), _full((1, d)),
                  pl.BlockSpec((1, d, n), lambda l, i: (l, 0, 0))],
        out_specs=pl.BlockSpec((tm, n), lambda l, i: (i, l)),
        out_shape=jax.ShapeDtypeStruct((t, nl * n), BF16),
        compiler_params=_params("parallel", "parallel"),
        name="mem_kv",
    )(mem2d, g, wkv)


def _mix_out_kernel(h_ref, att_ref, rw_ref, woa_ref, wob_ref, g_ref, wq_ref,
                    mk_ref, mv_ref, wo_ref, o_ref, *, q_scale):
    d = h_ref.shape[-1]
    h1 = h_ref[...] + _dot(att_ref[...], woa_ref[0]) + _dot(rw_ref[...], wob_ref[0])
    hn = _rms(h1, g_ref[0], d).astype(BF16)
    q = (_dot(hn, wq_ref[0]) * q_scale).astype(BF16)
    dh = d // CA_HEADS
    outs = []
    for hd in range(CA_HEADS):
        sl = slice(hd * dh, (hd + 1) * dh)
        s = _dot_nt(q[:, sl], mk_ref[:, sl])
        m = jnp.max(s, axis=-1, keepdims=True)
        p = jnp.exp2(s - m)
        l = jnp.sum(p, axis=-1, keepdims=True)
        outs.append((_dot(p.astype(BF16), mv_ref[:, sl]) / l).astype(BF16))
    o = jnp.concatenate(outs, axis=1)
    o_ref[...] = h1 + _dot(o, wo_ref[0])


def _mix_out(h, att, rw, woa, wob, g, wq, memkv, wo, layer, seq, mem_len, tm):
    t, d = h.shape
    tiles = seq // tm
    q_scale = (d // CA_HEADS) ** -0.5 * LOG2E
    row = lambda w: pl.BlockSpec((tm, w), lambda i: (i, 0))
    lay = lambda a: pl.BlockSpec((1,) + a.shape[1:], lambda i: (layer,) + (0,) * (a.ndim - 1))
    return pl.pallas_call(
        functools.partial(_mix_out_kernel, q_scale=q_scale),
        grid=(t // tm,),
        in_specs=[row(d), row(att.shape[1]), row(rw.shape[1]), lay(woa), lay(wob), lay(g),
                  lay(wq),
                  pl.BlockSpec((mem_len, d), lambda i: (i // tiles, 2 * layer)),
                  pl.BlockSpec((mem_len, d), lambda i: (i // tiles, 2 * layer + 1)),
                  lay(wo)],
        out_specs=row(d),
        out_shape=jax.ShapeDtypeStruct((t, d), F32),
        compiler_params=_params("parallel"),
        name="mix_out_cross_attention",
    )(h, att, rw, woa, wob, g, wq, memkv, memkv, wo)


def _mlp_kernel(h_ref, g_ref, wu_ref, wd_ref, gf_ref, o_ref, *, last, ff_chunk):
    d = h_ref.shape[-1]
    h = h_ref[...]
    hn = _rms(h, g_ref[0], d).astype(BF16)
    acc = h
    for c in range(wu_ref.shape[-1] // ff_chunk):
        sl = slice(c * ff_chunk, (c + 1) * ff_chunk)
        u = jnp.maximum(_dot(hn, wu_ref[0, :, sl]), 0.0)
        acc = acc + _dot((u * u).astype(BF16), wd_ref[0, sl, :])
    if last:
        acc = _rms(acc, gf_ref[...], d)
    o_ref[...] = acc


def _mlp(h, g, wu, wd, gf, layer, last, tm):
    t, d = h.shape
    row = pl.BlockSpec((tm, d), lambda i: (i, 0))
    lay = lambda a: pl.BlockSpec((1,) + a.shape[1:], lambda i: (layer,) + (0,) * (a.ndim - 1))
    return pl.pallas_call(
        functools.partial(_mlp_kernel, last=last, ff_chunk=1024),
        grid=(t // tm,),
        in_specs=[row, lay(g), lay(wu), lay(wd), _full((1, d))],
        out_specs=row,
        out_shape=jax.ShapeDtypeStruct((t, d), F32),
        compiler_params=_params("parallel"),
        name="mlp",
    )(h, g, wu, wd, gf)


def _pad_cols(a, n):
    return jnp.pad(a, ((0, 0), (0, n - a.shape[1])))


def _swap_halves(a):
    half = a.shape[-1] // 2
    return jnp.concatenate([a[..., half:], a[..., :half]], axis=-1)


def _in_weights(w_in, mu):
    cq = w_in[:, :MLA_Q_RANK]
    ckv = w_in[:, MLA_Q_RANK:MLA_Q_RANK + MLA_KV_RANK]
    kr = w_in[:, MLA_Q_RANK + MLA_KV_RANK:MLA_COLS]
    mla = jnp.concatenate([_pad_cols(cq, 256), ckv, kr, _swap_halves(kr)], axis=1)
    w = jnp.concatenate([_pad_cols(mla, MLA_C_PAD), _pad_cols(w_in[:, MLA_COLS:], RW_C_PAD)], axis=1)
    return w.astype(BF16), _pad_cols(mu[None, :], RW_C_PAD)


def _head_pad(a, n_heads, per_head, lo, hi, at):
    rows = a.shape[0]
    a = a.reshape(rows, n_heads, per_head)[:, :, lo:hi]
    a = jnp.pad(a, ((0, 0), (0, 0), (at, HEAD_PAD - at - (hi - lo))))
    return a.reshape(rows, n_heads * HEAD_PAD)


def _mla_weights(w_uq, w_ukv):
    dq = MLA_NOPE_DIM + MLA_ROPE_DIM
    h = MLA_HEADS
    wq3 = w_uq.reshape(MLA_Q_RANK, h, dq)
    rope_sw = _swap_halves(wq3[:, :, MLA_NOPE_DIM:]).reshape(MLA_Q_RANK, h * MLA_ROPE_DIM)
    wqa = _head_pad(w_uq, h, dq, 0, dq, 0)
    wqb = _head_pad(rope_sw, h, MLA_ROPE_DIM, 0, MLA_ROPE_DIM, MLA_NOPE_DIM)
    pad_q = ((0, 256 - MLA_Q_RANK), (0, 0))
    wqa = jnp.pad(wqa, pad_q)
    wqb = jnp.pad(wqb, pad_q)
    dkv = MLA_NOPE_DIM + MLA_V_DIM
    wk = _head_pad(w_ukv, h, dkv, 0, MLA_NOPE_DIM, 0)
    wvt = jnp.transpose(w_ukv.reshape(MLA_KV_RANK, h, dkv)[:, :, MLA_NOPE_DIM:], (1, 2, 0))
    wvt = jnp.pad(wvt, ((0, 0), (0, ATTN_ONES_ROWS), (0, 0))).reshape(-1, MLA_KV_RANK)
    eye = jnp.eye(MLA_ROPE_DIM, dtype=F32)
    place = _head_pad(jnp.tile(eye, (1, h)), h, MLA_ROPE_DIM, 0, MLA_ROPE_DIM, MLA_NOPE_DIM)
    zeros = jnp.zeros_like(place)
    pad_rows = jnp.zeros((LANE - 2 * MLA_ROPE_DIM, h * HEAD_PAD), F32)
    pa = jnp.concatenate([place, zeros, pad_rows], axis=0)
    pb = jnp.concatenate([zeros, place, pad_rows], axis=0)
    wka = jnp.concatenate([wk, pa], axis=0)
    return tuple(a.astype(BF16) for a in (wqa, wqb, wka, pb, wvt))


def _lora_weights(w2, a2, g2, v2):
    rw = RWKV_WIDTH
    z = lambda r: jnp.zeros((r, rw), F32)
    rows = [
        jnp.concatenate([w2, z(DECAY_RANK), z(DECAY_RANK), z(DECAY_RANK)], axis=1),
        jnp.concatenate([z(ICL_RANK), a2, z(ICL_RANK), z(ICL_RANK)], axis=1),
        jnp.concatenate([z(GATE_RANK), z(GATE_RANK), g2, z(GATE_RANK)], axis=1),
        jnp.concatenate([z(VRES_RANK)] * 3 + [v2 if v2 is not None else z(VRES_RANK)], axis=1),
    ]
    w = jnp.concatenate(rows, axis=0)
    return jnp.pad(w, ((0, LORA_PAD - w.shape[0]), (0, 0))).astype(BF16)


def _block_diag(n, blk, value):
    i = jnp.arange(n)
    return jnp.where((i[:, None] // blk) == (i[None, :] // blk), value, 0.0)


def kernel(x, mem, positions, mix_norm, w_in_first, w_in_rest, shift_mu_first, shift_mu_rest, mla_q_norm, mla_w_uq, mla_kv_norm, mla_w_ukv, mla_out_norm, rwkv_w0, rwkv_w2, rwkv_a0, rwkv_a2, rwkv_g2, rwkv_v0, rwkv_v2, rwkv_k_k, rwkv_k_a, rwkv_r_k, rwkv_ln_w, rwkv_ln_b, w_out, ca_norm, mem_norm, ca_wq, ca_wkv, ca_wo, mlp_norm, mlp_w_up, mlp_w_down, final_norm):
    b, s, d = x.shape
    depth = mix_norm.shape[0]
    mem_len = mem.shape[1]
    t = b * s
    tm = min(512, s)
    tq = min(256, s)
    wkv_rows = min(128, s)
    rw = RWKV_WIDTH

    inv = ROPE_THETA ** (-jnp.arange(0, MLA_ROPE_DIM, 2, dtype=F32) / MLA_ROPE_DIM)
    inv_row = jnp.concatenate([jnp.zeros((MLA_NOPE_DIM,), F32), inv, inv,
                               jnp.zeros((HEAD_PAD - MLA_NOPE_DIM - MLA_ROPE_DIM,), F32)])[None, :]
    ct, st = _rope_tables(positions.astype(F32).reshape(t, 1), inv_row, tm)

    memkv = _mem_kv(mem.reshape(b * mem_len, d), mem_norm[None, :], ca_wkv.astype(BF16),
                    min(512, b * mem_len))

    woa = w_out[:, :MLA_HEADS * MLA_V_DIM].astype(BF16)
    wob = w_out[:, MLA_HEADS * MLA_V_DIM:].astype(BF16)
    wq_b = ca_wq.astype(BF16)
    wo_b = ca_wo.astype(BF16)
    wu_b = mlp_w_up.astype(BF16)
    wd_b = mlp_w_down.astype(BF16)
    ca_g = ca_norm[:, None, :]
    mlp_g = mlp_norm[:, None, :]
    bd512 = _block_diag(rw, RWKV_HEAD, 1.0).astype(BF16)

    h = x.reshape(t, d)
    v_first = None
    for l in range(depth):
        if l == 0:
            w_in, mu, v0, v2 = w_in_first, shift_mu_first, None, None
        else:
            w_in, mu, v0, v2 = w_in_rest[l - 1], shift_mu_rest[l - 1], rwkv_v0[l - 1], rwkv_v2[l - 1]
        w_all, mu_p = _in_weights(w_in, mu)
        wqa, wqb, wka, pb, wvt = _mla_weights(mla_w_uq[l], mla_w_ukv[l])
        gq = _pad_cols(mla_q_norm[l][None, :], 256)
        w2c = _lora_weights(rwkv_w2[l], rwkv_a2[l], rwkv_g2[l], v2)
        vec = lambda a: a.reshape(1, rw)
        q, k, vt, r_, k_, v_, kk_, kka_, lw_, g_ = _layer_in(
            h, mix_norm[l][None, :], w_all, mu_p, ct, st, gq, mla_kv_norm[l][None, :],
            wqa, wqb, wka, pb, wvt, v_first, w2c, bd512, vec(rwkv_w0[l]), vec(rwkv_a0[l]),
            vec(rwkv_k_k[l]), vec(rwkv_k_a[l]), None if v0 is None else vec(v0), s, tm)
        if l == 0:
            v_first = v_

        att = _mla_attention(q, k, vt, mla_out_norm[l][None, :], s, tq, ATTN_HEADS_PER_STEP)

        sh = lambda a: a.reshape(b, s, rw)
        y = _wkv(sh(r_), sh(k_), sh(v_), sh(kk_), sh(kka_), sh(lw_), sh(g_),
                 vec(rwkv_r_k[l]), vec(rwkv_ln_w[l]), vec(rwkv_ln_b[l]), wkv_rows, WKV_MODES)

        h = _mix_out(h, att.reshape(t, -1), y.reshape(t, rw), woa, wob, ca_g, wq_b, memkv, wo_b,
                     l, s, mem_len, tm)
        h = _mlp(h, mlp_g, wu_b, wd_b, final_norm[None, :], l, l == depth - 1, tm)
    return h.reshape(b, s, d)
```

```python
import functools
import math

import jax
import jax.numpy as jnp
from jax import lax
from jax.experimental import pallas as pl
from jax.experimental.pallas import tpu as pltpu

F32 = jnp.float32
BF16 = jnp.bfloat16

CHUNK = 64
NORM_EPS = 1e-6
MLA_HEADS = 8
MLA_V_DIM = 64
MLA_NOPE_DIM = 64
MLA_ROPE_DIM = 32
MLA_Q_RANK = 192
MLA_KV_RANK = 128
ROPE_THETA = 10000.0
RWKV_HEAD = 64
RWKV_HEADS = 8
RWKV_WIDTH = 512
DECAY_RANK = 32
ICL_RANK = 32
GATE_RANK = 96
VRES_RANK = 32
GN_EPS = 64e-5
CA_HEADS = 4
MLA_COLS = MLA_Q_RANK + MLA_KV_RANK + MLA_ROPE_DIM

LANE = 128
HEAD_PAD = 128
MLA_C_PAD = 512
RW_C_PAD = 1792
LORA_PAD = 256
WKV_CHUNK = 64
ATTN_HEADS_PER_STEP = 4
ATTN_ONES_ROWS = 16
LOG2E = 1.4426950408889634
NEG_BIG = -1e30
VMEM_LIMIT = 56 * 1024 * 1024

HI = lax.Precision.HIGHEST
WKV_MODES = ("bf16", "bf16", "bf16")


def _dot(a, b, precision=None):
    return jnp.dot(a, b, preferred_element_type=F32, precision=precision)


def _dot_nt(a, b, precision=None):
    return lax.dot_general(a, b, (((1,), (1,)), ((), ())),
                           preferred_element_type=F32, precision=precision)


def _dot_tn(a, b, precision=None):
    return lax.dot_general(a, b, (((0,), (0,)), ((), ())),
                           preferred_element_type=F32, precision=precision)


def _rms(x, g, n, eps=NORM_EPS):
    ms = jnp.sum(x * x, axis=-1, keepdims=True) * (1.0 / n)
    return x * lax.rsqrt(ms + eps) * g


def _sigmoid(x):
    return 0.5 + 0.5 * jnp.tanh(0.5 * x)


def _params(*sem):
    return pltpu.CompilerParams(dimension_semantics=sem, vmem_limit_bytes=VMEM_LIMIT)


def _full(shape):
    nd = len(shape)
    return pl.BlockSpec(shape, lambda *_: (0,) * nd)


def _rope_kernel(pos_ref, inv_ref, ct_ref, st_ref):
    ang = pos_ref[...] * inv_ref[...]
    c = jnp.cos(ang)
    s = jnp.sin(ang)
    lane = lax.broadcasted_iota(jnp.int32, ang.shape, 1)
    half = MLA_NOPE_DIM + MLA_ROPE_DIM // 2
    end = MLA_NOPE_DIM + MLA_ROPE_DIM
    ct_ref[...] = jnp.where(lane < MLA_NOPE_DIM, 1.0, jnp.where(lane < end, c, 0.0))
    st_ref[...] = jnp.where(lane < MLA_NOPE_DIM, 0.0,
                            jnp.where(lane < half, -s, jnp.where(lane < end, s, 0.0)))


def _rope_tables(pos_f, inv_row, tm):
    t = pos_f.shape[0]
    return pl.pallas_call(
        _rope_kernel,
        grid=(t // tm,),
        in_specs=[pl.BlockSpec((tm, 1), lambda i: (i, 0)), _full((1, LANE))],
        out_specs=[pl.BlockSpec((tm, LANE), lambda i: (i, 0))] * 2,
        out_shape=[jax.ShapeDtypeStruct((t, LANE), F32)] * 2,
        compiler_params=_params("parallel"),
        name="rope_tables",
    )(pos_f, inv_row)


def _mla_latents(c, ct_ref, st_ref, gq_ref, gkv_ref, wqa_ref, wqb_ref, wka_ref, pb_ref,
                 wv_ref, ones_ref, q_ref, k_ref, v_ref, q_scale):
    nq = _rms(c[:, :256], gq_ref[...], MLA_Q_RANK).astype(BF16)
    nkv = _rms(c[:, 256:384], gkv_ref[...], MLA_KV_RANK).astype(BF16)
    kr = c[:, 384:512].astype(BF16)
    ct = jnp.tile(ct_ref[...], (1, MLA_HEADS))
    st = jnp.tile(st_ref[...], (1, MLA_HEADS))
    q = _dot(nq, wqa_ref[...]) * ct + _dot(nq, wqb_ref[...]) * st
    q_ref[...] = (q * q_scale).astype(BF16)
    x = jnp.concatenate([nkv, kr], axis=1)
    k = _dot(x, wka_ref[...]) * ct + _dot(kr, pb_ref[...]) * st
    k_ref[...] = k.astype(BF16)
    v_ref[...] = (_dot_nt(wv_ref[...], nkv) + ones_ref[...]).astype(BF16)


def _rwkv_elementwise(x, vf_ref, w2_ref, bd_ref, w0_ref, a0_ref, kk_ref_p, ka_ref, v0_ref,
                      r_o, k_o, v_o, kk_o, kka_o, lw_o, g_o):
    rw = RWKV_WIDTH
    r_o[...] = x[:, 0:rw].astype(BF16)
    k = x[:, rw:2 * rw]
    v = x[:, 2 * rw:3 * rw]
    tail = x[:, 3 * rw:3 * rw + LORA_PAD]
    lane = lax.broadcasted_iota(jnp.int32, tail.shape, 1)
    g_lo = DECAY_RANK + ICL_RANK
    g_hi = g_lo + GATE_RANK
    is_gate = (lane >= g_lo) & (lane < g_hi)
    th = jnp.tanh(jnp.where(is_gate, 0.5 * tail, tail))
    act = jnp.where(lane < DECAY_RANK, th, jnp.where(is_gate, 0.5 + 0.5 * th, tail))
    lo = _dot(act.astype(BF16), w2_ref[...])
    lw_o[...] = -math.exp(-0.5) * _sigmoid(w0_ref[...] + lo[:, 0:rw])
    a = _sigmoid(a0_ref[...] + lo[:, rw:2 * rw])
    g_o[...] = lo[:, 2 * rw:3 * rw].astype(BF16)
    if vf_ref is not None:
        mix = _sigmoid(v0_ref[...] + lo[:, 3 * rw:4 * rw])
        v = v + (vf_ref[...].astype(F32) - v) * mix
    v_o[...] = v.astype(BF16)
    kk = k * kk_ref_p[...]
    sq = kk * kk
    sq_hi = sq.astype(BF16)
    sq_lo = (sq - sq_hi.astype(F32)).astype(BF16)
    ss = _dot(sq_hi, bd_ref[...]) + _dot(sq_lo, bd_ref[...])
    kk = kk * lax.rsqrt(jnp.maximum(ss, 1e-24))
    kk_o[...] = kk.astype(BF16)
    kka_o[...] = (kk * a).astype(BF16)
    k_o[...] = (k * (1.0 + (a - 1.0) * ka_ref[...])).astype(BF16)


def _layer_in_kernel(*refs, first, tiles_per_seq, q_scale):
    n_in = 20 if first else 22
    ins, outs, carry_ref = refs[:n_in], refs[n_in:n_in + 10], refs[n_in + 10]
    h_ref, g_ref, w_ref, mu_ref = ins[:4]
    mla_ins = ins[4:14]
    if first:
        w2_ref, bd_ref, w0_ref, a0_ref, kkp_ref, ka_ref = ins[14:20]
        vf_ref = v0_ref = None
    else:
        vf_ref, w2_ref, bd_ref, w0_ref, a0_ref, kkp_ref, ka_ref, v0_ref = ins[14:22]
    q_ref, k_ref, vt_ref = outs[:3]
    rw_outs = outs[3:]

    i = pl.program_id(0)
    d = h_ref.shape[-1]
    hn = _rms(h_ref[...], g_ref[...], d).astype(BF16)
    proj = _dot(hn, w_ref[...])
    _mla_latents(proj[:, :MLA_C_PAD], *mla_ins, q_ref, k_ref, vt_ref, q_scale)

    z = proj[:, MLA_C_PAD:]
    tm = z.shape[0]

    @pl.when(i % tiles_per_seq == 0)
    def _():
        carry_ref[...] = jnp.zeros_like(carry_ref)

    prev = pltpu.roll(z, 1, axis=0)
    row = lax.broadcasted_iota(jnp.int32, z.shape, 0)
    prev = jnp.where(row == 0, carry_ref[7:8, :], prev)
    x = z + (prev - z) * mu_ref[...]
    carry_ref[...] = z[tm - 8:, :]
    _rwkv_elementwise(x, vf_ref, w2_ref, bd_ref, w0_ref, a0_ref, kkp_ref, ka_ref, v0_ref,
                      *rw_outs)


def _layer_in(h, g, w, mu, ct, st, gq, gkv, wqa, wqb, wka, pb, wvt, vf, w2, bd, w0, a0, k_k,
              k_a, v0, seq, tm):
    t, d = h.shape
    rw = RWKV_WIDTH
    hw = MLA_HEADS * HEAD_PAD
    vw = wvt.shape[0]
    first = vf is None
    ones_col = (jnp.arange(vw) % (MLA_V_DIM + ATTN_ONES_ROWS) == MLA_V_DIM).astype(F32)[:, None]
    q_scale = (MLA_NOPE_DIM + MLA_ROPE_DIM) ** -0.5 * LOG2E
    row = lambda width: pl.BlockSpec((tm, width), lambda i: (i, 0))
    vec = _full((1, rw))
    args = [h, g, w, mu, ct, st, gq, gkv, wqa, wqb, wka, pb, wvt, ones_col]
    in_specs = [row(d), _full((1, d)), _full(w.shape), _full((1, RW_C_PAD)), row(LANE), row(LANE),
                _full(gq.shape), _full(gkv.shape), _full(wqa.shape), _full(wqb.shape),
                _full(wka.shape), _full(pb.shape), _full(wvt.shape), _full(ones_col.shape)]
    if first:
        args += [w2, bd, w0, a0, k_k, k_a]
        in_specs += [_full(w2.shape), _full(bd.shape), vec, vec, vec, vec]
    else:
        args += [vf, w2, bd, w0, a0, k_k, k_a, v0]
        in_specs += [row(rw), _full(w2.shape), _full(bd.shape), vec, vec, vec, vec, vec]
    rw_dt = [BF16, BF16, BF16, BF16, BF16, F32, BF16]
    kern = functools.partial(_layer_in_kernel, first=first, tiles_per_seq=seq // tm,
                             q_scale=q_scale)
    return pl.pallas_call(
        kern,
        grid=(t // tm,),
        in_specs=in_specs,
        out_specs=[row(hw), row(hw), pl.BlockSpec((vw, tm), lambda i: (0, i))] + [row(rw)] * 7,
        out_shape=[jax.ShapeDtypeStruct((t, hw), BF16), jax.ShapeDtypeStruct((t, hw), BF16),
                   jax.ShapeDtypeStruct((vw, t), BF16)]
                  + [jax.ShapeDtypeStruct((t, rw), dt) for dt in rw_dt],
        scratch_shapes=[pltpu.VMEM((8, RW_C_PAD), F32)],
        compiler_params=_params("arbitrary"),
        name="layer_in",
    )(*args)


def _attn_kernel(q_ref, k_ref, vt_ref, g_ref, o_ref, m_ref, a_ref, p_ref, acc_ref, s_ref, s2_ref,
                 *, tq, tk, nh):
    qi = pl.program_id(2)
    dv = MLA_V_DIM
    heads = range(nh)

    ext = dv + ATTN_ONES_ROWS

    m_ref[...] = jnp.full_like(m_ref, NEG_BIG)
    a_ref[...] = jnp.ones_like(a_ref)
    p_ref[...] = jnp.zeros_like(p_ref)
    acc_ref[...] = jnp.zeros_like(acc_ref)

    def flush_values(j, q0=0):
        off = pl.multiple_of(j * tk, tk)
        pv = [_dot(vt_ref[h * ext:(h + 1) * ext, pl.ds(off, tk)], p_ref[h, :, q0:])
              for h in heads]
        for h in heads:
            acc_ref[h, :, q0:] = a_ref[h, :, q0:] * acc_ref[h, :, q0:] + pv[h]

    def scores(j, h, dst, q0=0):
        off = pl.multiple_of(j * tk, tk)
        dst[h, :, q0:] = _dot_nt(k_ref[0, pl.ds(off, tk), h * HEAD_PAD:(h + 1) * HEAD_PAD],
                                 q_ref[0, q0:, h * HEAD_PAD:(h + 1) * HEAD_PAD])

    def softmax(h, src, key0=None, q0=0):
        s = src[h, :, q0:]
        if key0 is not None:
            ck = (lax.broadcasted_iota(jnp.int32, s.shape, 0) + key0) // CHUNK
            cq = (lax.broadcasted_iota(jnp.int32, s.shape, 1) + q0) // CHUNK
            s = jnp.where(ck <= cq, s, NEG_BIG)
        m_old = m_ref[h, :, q0:]
        m_new = jnp.maximum(m_old, jnp.max(s, axis=0, keepdims=True))
        p_ref[h, :, q0:] = jnp.exp2((s - m_new).astype(BF16))
        a_ref[h, :, q0:] = jnp.exp2(m_old - m_new)
        m_ref[h, :, q0:] = m_new

    def stage(j, src, dst, key0=None, q0=0, next_q0=0):
        flush_values(jnp.maximum(j - 1, 0))
        for h in heads:
            if dst is not None:
                scores(j + 1, h, dst, next_q0)
            softmax(h, src, key0, q0)

    for h in heads:
        scores(0, h, s_ref)

    @pl.loop(0, qi)
    def _(t):
        stage(2 * t, s_ref, s2_ref)
        stage(2 * t + 1, s2_ref, s_ref)

    stage(2 * qi, s_ref, s2_ref, key0=0, next_q0=tk)
    stage(2 * qi + 1, s2_ref, None, key0=tk, q0=tk)
    flush_values(2 * qi + 1, q0=tk)
    outs = []
    for h in heads:
        acc = acc_ref[h]
        o = acc[:dv] / acc[dv:dv + 1]
        ms = jnp.sum(o * o, axis=0, keepdims=True) * (1.0 / dv)
        outs.append(o * lax.rsqrt(ms + NORM_EPS))
    o_t = jnp.concatenate(outs, axis=0)
    o_ref[0] = (o_t.T * g_ref[...]).astype(BF16)


def _mla_attention(q, k, vt, g, seq, tk, nh):
    tq = 2 * tk
    b = q.shape[0] // seq
    hw = q.shape[1]
    groups = hw // (nh * HEAD_PAD)
    ow = nh * MLA_V_DIM
    q3 = q.reshape(b, seq, hw)
    k3 = k.reshape(b, seq, hw)
    return pl.pallas_call(
        functools.partial(_attn_kernel, tq=tq, tk=tk, nh=nh),
        grid=(b, groups, seq // tq),
        in_specs=[pl.BlockSpec((1, tq, nh * HEAD_PAD), lambda bi, h, i: (bi, i, h)),
                  pl.BlockSpec((1, seq, nh * HEAD_PAD), lambda bi, h, i: (bi, 0, h)),
                  pl.BlockSpec((nh * (MLA_V_DIM + ATTN_ONES_ROWS), seq), lambda bi, h, i: (h, bi)),
                  pl.BlockSpec((1, ow), lambda bi, h, i: (0, h))],
        out_specs=pl.BlockSpec((1, tq, ow), lambda bi, h, i: (bi, i, h)),
        out_shape=jax.ShapeDtypeStruct((b, seq, groups * ow), BF16),
        scratch_shapes=[pltpu.VMEM((nh, 1, tq), F32), pltpu.VMEM((nh, 1, tq), F32),
                        pltpu.VMEM((nh, tk, tq), BF16),
                        pltpu.VMEM((nh, MLA_V_DIM + ATTN_ONES_ROWS, tq), F32),
                        pltpu.VMEM((nh, tk, tq), F32), pltpu.VMEM((nh, tk, tq), F32)],
        compiler_params=_params("parallel", "parallel", "arbitrary"),
        name="mla_attention",
    )(q3, k3, vt, g)


_DIMS = {"nn": (((1,), (0,)), ((), ())), "nt": (((1,), (1,)), ((), ())),
         "tn": (((0,), (0,)), ((), ()))}


def _split(x):
    hi = x.astype(BF16)
    return hi, (x - hi.astype(F32)).astype(BF16)


def _mm(a, b, mode, dims="nn"):
    dg = lambda x, y: lax.dot_general(x, y, _DIMS[dims], preferred_element_type=F32)
    if mode == "bf16":
        return dg(a.astype(BF16), b.astype(BF16))
    if mode == "x3":
        a_hi, a_lo = _split(a)
        b_hi, b_lo = _split(b)
        return dg(a_hi, b_hi) + (dg(a_lo, b_hi) + dg(a_hi, b_lo))
    if mode == "ax":
        n = b.shape[1]
        out = dg(a.astype(BF16), jnp.concatenate(_split(b), axis=1))
        return out[:, :n] + out[:, n:]
    return lax.dot_general(a, b, _DIMS[dims], preferred_element_type=F32, precision=HI)


def _unit_lower_solve(a, rhs, row, col, eye, mode):
    return _unit_lower_solve_many([a], [rhs], row, col, eye, mode)[0]


def _unit_lower_solve_many(a_list, rhs_list, row, col, eye, mode):
    d16 = (row // 16) == (col // 16)
    d32 = (row // 32) == (col // 32)
    n = range(len(a_list))
    d = [jnp.where(d16, a, 0.0) for a in a_list]
    x = [eye + d[i] for i in n]
    d2 = [_mm(d[i], d[i], mode) for i in n]
    t = [_mm(d2[i], jnp.concatenate([x[i], d2[i]], axis=1), mode) for i in n]
    x = [x[i] + t[i][:, :128] for i in n]
    d4 = [t[i][:, 128:] for i in n]
    t = [_mm(d4[i], jnp.concatenate([x[i], d4[i]], axis=1), mode) for i in n]
    x = [x[i] + t[i][:, :128] for i in n]
    d8 = [t[i][:, 128:] for i in n]
    t = [_mm(d8[i], x[i], mode) for i in n]
    x = [x[i] + t[i] for i in n]
    e = [jnp.where(d32, a_list[i], 0.0) - d[i] for i in n]
    t = [_mm(e[i], x[i], mode) for i in n]
    t = [_mm(x[i], t[i], mode) for i in n]
    x = [x[i] + t[i] for i in n]
    f = [jnp.where(d32, 0.0, a_list[i]) for i in n]
    z = [_mm(x[i], rhs_list[i], mode) for i in n]
    t = [_mm(f[i], z[i], mode) for i in n]
    t = [_mm(x[i], t[i], mode) for i in n]
    return [z[i] + t[i] for i in n]


def _wkv_kernel(r_ref, k_ref, v_ref, kk_ref, kka_ref, lw_ref, g_ref,
                rk_ref, lnw_ref, lnb_ref, o_ref, s_ref, *, nch, modes):
    m_quad, m_inv, m_state = modes
    c_len = WKV_CHUNK

    npair = s_ref.shape[0]

    @pl.when(pl.program_id(1) == 0)
    def _():
        s_ref[...] = jnp.zeros_like(s_ref)

    row = lax.broadcasted_iota(jnp.int32, (128, 128), 0)
    col = lax.broadcasted_iota(jnp.int32, (128, 128), 1)
    same = (row // c_len) == (col // c_len)
    mask_s = same & (col < row)
    mask_i = same & (col <= row)
    eye = jnp.where(row == col, 1.0, 0.0)
    lane0 = lax.broadcasted_iota(jnp.int32, (c_len, 128), 1) < RWKV_HEAD
    tr = lax.broadcasted_iota(jnp.int32, (c_len, c_len), 0)
    tc = lax.broadcasted_iota(jnp.int32, (c_len, c_len), 1)
    ltri = jnp.where(tc <= tr, 1.0, 0.0)

    def stack(x):
        return jnp.concatenate([jnp.where(lane0, x, 0.0), jnp.where(lane0, 0.0, x)], axis=0)

    def head_sums(x):
        first = jnp.sum(jnp.where(lane0, x, 0.0), axis=-1, keepdims=True)
        both = jnp.sum(x, axis=-1, keepdims=True)
        return jnp.where(lane0, first, both - first)

    par = []
    for c, hp in [(c, hp) for c in range(nch) for hp in range(npair)]:
        sl = pl.ds(c * c_len, c_len)
        ln = slice(hp * LANE, (hp + 1) * LANE)
        r = r_ref[0, sl, ln].astype(F32)
        k = k_ref[0, sl, ln].astype(F32)
        v = v_ref[0, sl, ln].astype(F32)
        kk = kk_ref[0, sl, ln].astype(F32)
        kka = kka_ref[0, sl, ln].astype(F32)
        lw = lw_ref[0, sl, ln]
        cum = _mm(ltri, lw, "ax")
        tot = cum[c_len - 1:c_len, :]
        p_in = jnp.exp(cum)
        p_ex = jnp.exp(cum - lw)
        p_inv = jnp.exp(-cum)
        p_end = jnp.exp(tot - cum)
        at = stack(-kk * p_ex)
        rt = stack(r * p_in)
        bt = kka * p_inv
        kt = k * p_inv
        par.append(dict(
            r=r, k=k, v=v, tot=tot, at=at, rt=rt, vs=stack(v),
            bkh=jnp.concatenate([stack(kka * p_end), stack(k * p_end)], axis=0),
            lhs=jnp.concatenate([at, rt], axis=0),
            rhs=jnp.concatenate([bt, bt, kt, kt], axis=0)))
    for p in par:
        quad = _mm(p["lhs"], p["rhs"], m_quad, "nt")
        p["a_ab"] = jnp.where(mask_s, quad[:128, :128], 0.0)
        p["a_ak"] = jnp.where(mask_s, quad[:128, 128:], 0.0)
        p["a_r"] = jnp.concatenate([jnp.where(mask_i, quad[128:, :128], 0.0),
                                    jnp.where(mask_i, quad[128:, 128:], 0.0)], axis=1)
    for p in par:
        p["akv"] = _mm(p["a_ak"], p["vs"], m_quad)
    wus = _unit_lower_solve_many(
        [p["a_ab"] for p in par],
        [jnp.concatenate([p["at"], p["akv"]], axis=1) for p in par], row, col, eye, m_inv)

    pairs = range(npair)
    state = [s_ref[hp] for hp in pairs]
    def outputs(c, m1, uv):
        sl = pl.ds(c * c_len, c_len)
        pc = [par[c * npair + hp] for hp in pairs]
        ys = [m1[hp][128:] + _mm(pc[hp]["a_r"], uv[hp], m_state) for hp in pairs]
        y = [ys[hp][:c_len] + ys[hp][c_len:] for hp in pairs]
        mean = [head_sums(y[hp]) * (1.0 / RWKV_HEAD) for hp in pairs]
        yc = [y[hp] - mean[hp] for hp in pairs]
        var = [head_sums(yc[hp] * yc[hp]) * (1.0 / RWKV_HEAD) for hp in pairs]
        for hp in pairs:
            ln = slice(hp * LANE, (hp + 1) * LANE)
            r, k, v = pc[hp]["r"], pc[hp]["k"], pc[hp]["v"]
            yn = yc[hp] * lax.rsqrt(var[hp] + GN_EPS) * lnw_ref[:, ln] + lnb_ref[:, ln]
            bonus = head_sums(r * k * rk_ref[:, ln])
            yn = yn + bonus * v
            o_ref[0, sl, ln] = (yn * g_ref[0, sl, ln].astype(F32)).astype(BF16)

    pending = None
    for c in range(nch):
        pc = [par[c * npair + hp] for hp in pairs]
        wu = [wus[c * npair + hp] for hp in pairs]
        m1 = [_mm(jnp.concatenate([wu[hp][:, :128], pc[hp]["rt"]], axis=0), state[hp],
                  m_state, "nt") for hp in pairs]
        uv = [jnp.concatenate([m1[hp][:128] + wu[hp][:, 128:], pc[hp]["vs"]], axis=0)
              for hp in pairs]
        upd = [_mm(uv[hp], pc[hp]["bkh"], m_state, "tn") for hp in pairs]
        state = [state[hp] * jnp.exp(pc[hp]["tot"]) + upd[hp] for hp in pairs]
        if pending is not None:
            outputs(*pending)
        pending = (c, m1, uv)
    outputs(*pending)
    for hp in pairs:
        s_ref[hp] = state[hp]


def _wkv(r, k, v, kk, kka, lw, g, r_k, ln_w, ln_b, rows, modes):
    b, s, w = r.shape
    blk = pl.BlockSpec((1, rows, w), lambda bi, i: (bi, i, 0))
    kern = functools.partial(_wkv_kernel, nch=rows // WKV_CHUNK, modes=modes)
    return pl.pallas_call(
        kern,
        grid=(b, s // rows),
        in_specs=[blk] * 7 + [_full((1, w))] * 3,
        out_specs=blk,
        out_shape=jax.ShapeDtypeStruct((b, s, w), BF16),
        scratch_shapes=[pltpu.VMEM((w // LANE, LANE, LANE), F32)],
        compiler_params=_params("parallel", "arbitrary"),
        name="wkv7",
    )(r, k, v, kk, kka, lw, g, r_k, ln_w, ln_b)


def _mem_kv_kernel(m_ref, g_ref, w_ref, o_ref):
    d = m_ref.shape[-1]
    mn = _rms(m_ref[...], g_ref[...], d).astype(BF16)
    o_ref[...] = _dot(mn, w_ref[0]).astype(BF16)


def _mem_kv(mem2d, g, wkv, tm):
    t, d = mem2d.shape
    nl, _, n = wkv.shape
    return pl.pallas_call(
        _mem_kv_kernel,
        grid=(nl, t // tm),
        in_specs=[pl.BlockSpec((tm, d), lambda l, i: (i, 0)), _full((1, d)),
                  pl.BlockSpec((1, d, n), lambda l, i: (l, 0, 0))],
        out_specs=pl.BlockSpec((tm, n), lambda l, i: (i, l)),
        out_shape=jax.ShapeDtypeStruct((t, nl * n), BF16),
        compiler_params=_params("parallel", "parallel"),
        name="mem_kv",
    )(mem2d, g, wkv)


def _mix_out_kernel(h_ref, att_ref, rw_ref, woa_ref, wob_ref, g_ref, wq_ref,
                    mk_ref, mv_ref, wo_ref, o_ref, *, q_scale):
    d = h_ref.shape[-1]
    h1 = h_ref[...] + _dot(att_ref[...], woa_ref[0]) + _dot(rw_ref[...], wob_ref[0])
    hn = _rms(h1, g_ref[0], d).astype(BF16)
    q = (_dot(hn, wq_ref[0]) * q_scale).astype(BF16)
    dh = d // CA_HEADS
    outs = []
    for hd in range(CA_HEADS):
        sl = slice(hd * dh, (hd + 1) * dh)
        s = _dot_nt(q[:, sl], mk_ref[:, sl])
        m = jnp.max(s, axis=-1, keepdims=True)
        p = jnp.exp2(s - m)
        l = jnp.sum(p, axis=-1, keepdims=True)
        outs.append((_dot(p.astype(BF16), mv_ref[:, sl]) / l).astype(BF16))
    o = jnp.concatenate(outs, axis=1)
    o_ref[...] = h1 + _dot(o, wo_ref[0])


def _mix_out(h, att, rw, woa, wob, g, wq, memkv, wo, layer, seq, mem_len, tm):
    t, d = h.shape
    tiles = seq // tm
    q_scale = (d // CA_HEADS) ** -0.5 * LOG2E
    row = lambda w: pl.BlockSpec((tm, w), lambda i: (i, 0))
    lay = lambda a: pl.BlockSpec((1,) + a.shape[1:], lambda i: (layer,) + (0,) * (a.ndim - 1))
    return pl.pallas_call(
        functools.partial(_mix_out_kernel, q_scale=q_scale),
        grid=(t // tm,),
        in_specs=[row(d), row(att.shape[1]), row(rw.shape[1]), lay(woa), lay(wob), lay(g),
                  lay(wq),
                  pl.BlockSpec((mem_len, d), lambda i: (i // tiles, 2 * layer)),
                  pl.BlockSpec((mem_len, d), lambda i: (i // tiles, 2 * layer + 1)),
                  lay(wo)],
        out_specs=row(d),
        out_shape=jax.ShapeDtypeStruct((t, d), F32),
        compiler_params=_params("parallel"),
        name="mix_out_cross_attention",
    )(h, att, rw, woa, wob, g, wq, memkv, memkv, wo)


def _mlp_kernel(h_ref, g_ref, wu_ref, wd_ref, gf_ref, o_ref, *, last, ff_chunk):
    d = h_ref.shape[-1]
    h = h_ref[...]
    hn = _rms(h, g_ref[0], d).astype(BF16)
    acc = h
    for c in range(wu_ref.shape[-1] // ff_chunk):
        sl = slice(c * ff_chunk, (c + 1) * ff_chunk)
        u = jnp.maximum(_dot(hn, wu_ref[0, :, sl]), 0.0)
        acc = acc + _dot((u * u).astype(BF16), wd_ref[0, sl, :])
    if last:
        acc = _rms(acc, gf_ref[...], d)
    o_ref[...] = acc


def _mlp(h, g, wu, wd, gf, layer, last, tm):
    t, d = h.shape
    row = pl.BlockSpec((tm, d), lambda i: (i, 0))
    lay = lambda a: pl.BlockSpec((1,) + a.shape[1:], lambda i: (layer,) + (0,) * (a.ndim - 1))
    return pl.pallas_call(
        functools.partial(_mlp_kernel, last=last, ff_chunk=1024),
        grid=(t // tm,),
        in_specs=[row, lay(g), lay(wu), lay(wd), _full((1, d))],
        out_specs=row,
        out_shape=jax.ShapeDtypeStruct((t, d), F32),
        compiler_params=_params("parallel"),
        name="mlp",
    )(h, g, wu, wd, gf)


def _pad_cols(a, n):
    return jnp.pad(a, ((0, 0), (0, n - a.shape[1])))


def _swap_halves(a):
    half = a.shape[-1] // 2
    return jnp.concatenate([a[..., half:], a[..., :half]], axis=-1)


def _in_weights(w_in, mu):
    cq = w_in[:, :MLA_Q_RANK]
    ckv = w_in[:, MLA_Q_RANK:MLA_Q_RANK + MLA_KV_RANK]
    kr = w_in[:, MLA_Q_RANK + MLA_KV_RANK:MLA_COLS]
    mla = jnp.concatenate([_pad_cols(cq, 256), ckv, kr, _swap_halves(kr)], axis=1)
    w = jnp.concatenate([_pad_cols(mla, MLA_C_PAD), _pad_cols(w_in[:, MLA_COLS:], RW_C_PAD)], axis=1)
    return w.astype(BF16), _pad_cols(mu[None, :], RW_C_PAD)


def _head_pad(a, n_heads, per_head, lo, hi, at):
    rows = a.shape[0]
    a = a.reshape(rows, n_heads, per_head)[:, :, lo:hi]
    a = jnp.pad(a, ((0, 0), (0, 0), (at, HEAD_PAD - at - (hi - lo))))
    return a.reshape(rows, n_heads * HEAD_PAD)


def _mla_weights(w_uq, w_ukv):
    dq = MLA_NOPE_DIM + MLA_ROPE_DIM
    h = MLA_HEADS
    wq3 = w_uq.reshape(MLA_Q_RANK, h, dq)
    rope_sw = _swap_halves(wq3[:, :, MLA_NOPE_DIM:]).reshape(MLA_Q_RANK, h * MLA_ROPE_DIM)
    wqa = _head_pad(w_uq, h, dq, 0, dq, 0)
    wqb = _head_pad(rope_sw, h, MLA_ROPE_DIM, 0, MLA_ROPE_DIM, MLA_NOPE_DIM)
    pad_q = ((0, 256 - MLA_Q_RANK), (0, 0))
    wqa = jnp.pad(wqa, pad_q)
    wqb = jnp.pad(wqb, pad_q)
    dkv = MLA_NOPE_DIM + MLA_V_DIM
    wk = _head_pad(w_ukv, h, dkv, 0, MLA_NOPE_DIM, 0)
    wvt = jnp.transpose(w_ukv.reshape(MLA_KV_RANK, h, dkv)[:, :, MLA_NOPE_DIM:], (1, 2, 0))
    wvt = jnp.pad(wvt, ((0, 0), (0, ATTN_ONES_ROWS), (0, 0))).reshape(-1, MLA_KV_RANK)
    eye = jnp.eye(MLA_ROPE_DIM, dtype=F32)
    place = _head_pad(jnp.tile(eye, (1, h)), h, MLA_ROPE_DIM, 0, MLA_ROPE_DIM, MLA_NOPE_DIM)
    zeros = jnp.zeros_like(place)
    pad_rows = jnp.zeros((LANE - 2 * MLA_ROPE_DIM, h * HEAD_PAD), F32)
    pa = jnp.concatenate([place, zeros, pad_rows], axis=0)
    pb = jnp.concatenate([zeros, place, pad_rows], axis=0)
    wka = jnp.concatenate([wk, pa], axis=0)
    return tuple(a.astype(BF16) for a in (wqa, wqb, wka, pb, wvt))


def _lora_weights(w2, a2, g2, v2):
    rw = RWKV_WIDTH
    z = lambda r: jnp.zeros((r, rw), F32)
    rows = [
        jnp.concatenate([w2, z(DECAY_RANK), z(DECAY_RANK), z(DECAY_RANK)], axis=1),
        jnp.concatenate([z(ICL_RANK), a2, z(ICL_RANK), z(ICL_RANK)], axis=1),
        jnp.concatenate([z(GATE_RANK), z(GATE_RANK), g2, z(GATE_RANK)], axis=1),
        jnp.concatenate([z(VRES_RANK)] * 3 + [v2 if v2 is not None else z(VRES_RANK)], axis=1),
    ]
    w = jnp.concatenate(rows, axis=0)
    return jnp.pad(w, ((0, LORA_PAD - w.shape[0]), (0, 0))).astype(BF16)


def _block_diag(n, blk, value):
    i = jnp.arange(n)
    return jnp.where((i[:, None] // blk) == (i[None, :] // blk), value, 0.0)


def kernel(x, mem, positions, mix_norm, w_in_first, w_in_rest, shift_mu_first, shift_mu_rest, mla_q_norm, mla_w_uq, mla_kv_norm, mla_w_ukv, mla_out_norm, rwkv_w0, rwkv_w2, rwkv_a0, rwkv_a2, rwkv_g2, rwkv_v0, rwkv_v2, rwkv_k_k, rwkv_k_a, rwkv_r_k, rwkv_ln_w, rwkv_ln_b, w_out, ca_norm, mem_norm, ca_wq, ca_wkv, ca_wo, mlp_norm, mlp_w_up, mlp_w_down, final_norm):
    b, s, d = x.shape
    depth = mix_norm.shape[0]
    mem_len = mem.shape[1]
    t = b * s
    tm = min(512, s)
    tq = min(256, s)
    wkv_rows = min(256, s)
    rw = RWKV_WIDTH

    inv = ROPE_THETA ** (-jnp.arange(0, MLA_ROPE_DIM, 2, dtype=F32) / MLA_ROPE_DIM)
    inv_row = jnp.concatenate([jnp.zeros((MLA_NOPE_DIM,), F32), inv, inv,
                               jnp.zeros((HEAD_PAD - MLA_NOPE_DIM - MLA_ROPE_DIM,), F32)])[None, :]
    ct, st = _rope_tables(positions.astype(F32).reshape(t, 1), inv_row, tm)

    memkv = _mem_kv(mem.reshape(b * mem_len, d), mem_norm[None, :], ca_wkv.astype(BF16),
                    min(512, b * mem_len))

    woa = w_out[:, :MLA_HEADS * MLA_V_DIM].astype(BF16)
    wob = w_out[:, MLA_HEADS * MLA_V_DIM:].astype(BF16)
    wq_b = ca_wq.astype(BF16)
    wo_b = ca_wo.astype(BF16)
    wu_b = mlp_w_up.astype(BF16)
    wd_b = mlp_w_down.astype(BF16)
    ca_g = ca_norm[:, None, :]
    mlp_g = mlp_norm[:, None, :]
    bd512 = _block_diag(rw, RWKV_HEAD, 1.0).astype(BF16)

    h = x.reshape(t, d)
    v_first = None
    for l in range(depth):
        if l == 0:
            w_in, mu, v0, v2 = w_in_first, shift_mu_first, None, None
        else:
            w_in, mu, v0, v2 = w_in_rest[l - 1], shift_mu_rest[l - 1], rwkv_v0[l - 1], rwkv_v2[l - 1]
        w_all, mu_p = _in_weights(w_in, mu)
        wqa, wqb, wka, pb, wvt = _mla_weights(mla_w_uq[l], mla_w_ukv[l])
        gq = _pad_cols(mla_q_norm[l][None, :], 256)
        w2c = _lora_weights(rwkv_w2[l], rwkv_a2[l], rwkv_g2[l], v2)
        vec = lambda a: a.reshape(1, rw)
        q, k, vt, r_, k_, v_, kk_, kka_, lw_, g_ = _layer_in(
            h, mix_norm[l][None, :], w_all, mu_p, ct, st, gq, mla_kv_norm[l][None, :],
            wqa, wqb, wka, pb, wvt, v_first, w2c, bd512, vec(rwkv_w0[l]), vec(rwkv_a0[l]),
            vec(rwkv_k_k[l]), vec(rwkv_k_a[l]), None if v0 is None else vec(v0), s, tm)
        if l == 0:
            v_first = v_

        att = _mla_attention(q, k, vt, mla_out_norm[l][None, :], s, tq, ATTN_HEADS_PER_STEP)

        sh = lambda a: a.reshape(b, s, rw)
        y = _wkv(sh(r_), sh(k_), sh(v_), sh(kk_), sh(kka_), sh(lw_), sh(g_),
                 vec(rwkv_r_k[l]), vec(rwkv_ln_w[l]), vec(rwkv_ln_b[l]), wkv_rows, WKV_MODES)

        h = _mix_out(h, att.reshape(t, -1), y.reshape(t, rw), woa, wob, ca_g, wq_b, memkv, wo_b,
                     l, s, mem_len, tm)
        h = _mlp(h, mlp_g, wu_b, wd_b, final_norm[None, :], l, l == depth - 1, tm)
    return h.reshape(b, s, d)
```

```python
import functools
import math

import jax
import jax.numpy as jnp
from jax import lax
from jax.experimental import pallas as pl
from jax.experimental.pallas import tpu as pltpu

F32 = jnp.float32
BF16 = jnp.bfloat16

CHUNK = 64
NORM_EPS = 1e-6
MLA_HEADS = 8
MLA_V_DIM = 64
MLA_NOPE_DIM = 64
MLA_ROPE_DIM = 32
MLA_Q_RANK = 192
MLA_KV_RANK = 128
ROPE_THETA = 10000.0
RWKV_HEAD = 64
RWKV_HEADS = 8
RWKV_WIDTH = 512
DECAY_RANK = 32
ICL_RANK = 32
GATE_RANK = 96
VRES_RANK = 32
GN_EPS = 64e-5
CA_HEADS = 4
MLA_COLS = MLA_Q_RANK + MLA_KV_RANK + MLA_ROPE_DIM

LANE = 128
HEAD_PAD = 128
MLA_C_PAD = 512
RW_C_PAD = 1792
LORA_PAD = 256
WKV_CHUNK = 64
ATTN_HEADS_PER_STEP = 4
ATTN_ONES_ROWS = 16
LOG2E = 1.4426950408889634
NEG_BIG = -1e30
VMEM_LIMIT = 56 * 1024 * 1024

HI = lax.Precision.HIGHEST
WKV_MODES = ("bf16", "bf16", "bf16")


def _dot(a, b, precision=None):
    return jnp.dot(a, b, preferred_element_type=F32, precision=precision)


def _dot_nt(a, b, precision=None):
    return lax.dot_general(a, b, (((1,), (1,)), ((), ())),
                           preferred_element_type=F32, precision=precision)


def _dot_tn(a, b, precision=None):
    return lax.dot_general(a, b, (((0,), (0,)), ((), ())),
                           preferred_element_type=F32, precision=precision)


def _rms(x, g, n, eps=NORM_EPS):
    ms = jnp.sum(x * x, axis=-1, keepdims=True) * (1.0 / n)
    return x * lax.rsqrt(ms + eps) * g


def _sigmoid(x):
    return 0.5 + 0.5 * jnp.tanh(0.5 * x)


def _params(*sem):
    return pltpu.CompilerParams(dimension_semantics=sem, vmem_limit_bytes=VMEM_LIMIT)


def _full(shape):
    nd = len(shape)
    return pl.BlockSpec(shape, lambda *_: (0,) * nd)


def _layer_slice(a, layer):
    nd = a.ndim
    return pl.BlockSpec((1,) + a.shape[1:], lambda i: (layer,) + (0,) * (nd - 1),
                        pipeline_mode=pl.Buffered(1))


def _rope_kernel(pos_ref, inv_ref, ct_ref, st_ref):
    ang = pos_ref[...] * inv_ref[...]
    c = jnp.cos(ang)
    s = jnp.sin(ang)
    lane = lax.broadcasted_iota(jnp.int32, ang.shape, 1)
    half = MLA_NOPE_DIM + MLA_ROPE_DIM // 2
    end = MLA_NOPE_DIM + MLA_ROPE_DIM
    ct_ref[...] = jnp.where(lane < MLA_NOPE_DIM, 1.0, jnp.where(lane < end, c, 0.0))
    st_ref[...] = jnp.where(lane < MLA_NOPE_DIM, 0.0,
                            jnp.where(lane < half, -s, jnp.where(lane < end, s, 0.0)))


def _rope_tables(pos_f, inv_row, tm):
    t = pos_f.shape[0]
    return pl.pallas_call(
        _rope_kernel,
        grid=(t // tm,),
        in_specs=[pl.BlockSpec((tm, 1), lambda i: (i, 0)), _full((1, LANE))],
        out_specs=[pl.BlockSpec((tm, LANE), lambda i: (i, 0))] * 2,
        out_shape=[jax.ShapeDtypeStruct((t, LANE), F32)] * 2,
        compiler_params=_params("parallel"),
        name="rope_tables",
    )(pos_f, inv_row)


def _mla_latents(c, ct_ref, st_ref, gq_ref, gkv_ref, wqa_ref, wqb_ref, wka_ref, pb_ref,
                 wv_ref, ones_ref, q_ref, k_ref, v_ref, q_scale):
    nq = _rms(c[:, :256], gq_ref[...], MLA_Q_RANK).astype(BF16)
    nkv = _rms(c[:, 256:384], gkv_ref[...], MLA_KV_RANK).astype(BF16)
    kr = c[:, 384:512].astype(BF16)
    ct = jnp.tile(ct_ref[...], (1, MLA_HEADS))
    st = jnp.tile(st_ref[...], (1, MLA_HEADS))
    q = _dot(nq, wqa_ref[...]) * ct + _dot(nq, wqb_ref[...]) * st
    q_ref[...] = (q * q_scale).astype(BF16)
    x = jnp.concatenate([nkv, kr], axis=1)
    k = _dot(x, wka_ref[...]) * ct + _dot(kr, pb_ref[...]) * st
    k_ref[...] = k.astype(BF16)
    v_ref[...] = (_dot_nt(wv_ref[...], nkv) + ones_ref[...]).astype(BF16)


def _rwkv_elementwise(x, vf_ref, w2_ref, bd_ref, w0_ref, a0_ref, kk_ref_p, ka_ref, v0_ref,
                      r_o, k_o, v_o, kk_o, kka_o, lw_o, g_o):
    rw = RWKV_WIDTH
    r_o[...] = x[:, 0:rw].astype(BF16)
    k = x[:, rw:2 * rw]
    v = x[:, 2 * rw:3 * rw]
    tail = x[:, 3 * rw:3 * rw + LORA_PAD]
    lane = lax.broadcasted_iota(jnp.int32, tail.shape, 1)
    g_lo = DECAY_RANK + ICL_RANK
    g_hi = g_lo + GATE_RANK
    is_gate = (lane >= g_lo) & (lane < g_hi)
    th = jnp.tanh(jnp.where(is_gate, 0.5 * tail, tail))
    act = jnp.where(lane < DECAY_RANK, th, jnp.where(is_gate, 0.5 + 0.5 * th, tail))
    lo = _dot(act.astype(BF16), w2_ref[...])
    lw_o[...] = -math.exp(-0.5) * _sigmoid(w0_ref[...] + lo[:, 0:rw])
    a = _sigmoid(a0_ref[...] + lo[:, rw:2 * rw])
    g_o[...] = lo[:, 2 * rw:3 * rw].astype(BF16)
    if vf_ref is not None:
        mix = _sigmoid(v0_ref[...] + lo[:, 3 * rw:4 * rw])
        v = v + (vf_ref[...].astype(F32) - v) * mix
    v_o[...] = v.astype(BF16)
    kk = k * kk_ref_p[...]
    sq = kk * kk
    sq_hi = sq.astype(BF16)
    sq_lo = (sq - sq_hi.astype(F32)).astype(BF16)
    ss = _dot(sq_hi, bd_ref[...]) + _dot(sq_lo, bd_ref[...])
    kk = kk * lax.rsqrt(jnp.maximum(ss, 1e-24))
    kk_o[...] = kk.astype(BF16)
    kka_o[...] = (kk * a).astype(BF16)
    k_o[...] = (k * (1.0 + (a - 1.0) * ka_ref[...])).astype(BF16)


def _layer_in_kernel(*refs, first, tiles_per_seq, q_scale):
    n_in = 20 if first else 22
    ins, outs, carry_ref = refs[:n_in], refs[n_in:n_in + 10], refs[n_in + 10]
    h_ref, g_ref, w_ref, mu_ref = ins[:4]
    mla_ins = ins[4:14]
    if first:
        w2_ref, bd_ref, w0_ref, a0_ref, kkp_ref, ka_ref = ins[14:20]
        vf_ref = v0_ref = None
    else:
        vf_ref, w2_ref, bd_ref, w0_ref, a0_ref, kkp_ref, ka_ref, v0_ref = ins[14:22]
    q_ref, k_ref, vt_ref = outs[:3]
    rw_outs = outs[3:]

    i = pl.program_id(0)
    d = h_ref.shape[-1]
    hn = _rms(h_ref[...], g_ref[...], d).astype(BF16)
    proj = _dot(hn, w_ref[...])
    _mla_latents(proj[:, :MLA_C_PAD], *mla_ins, q_ref, k_ref, vt_ref, q_scale)

    z = proj[:, MLA_C_PAD:]
    tm = z.shape[0]

    @pl.when(i % tiles_per_seq == 0)
    def _():
        carry_ref[...] = jnp.zeros_like(carry_ref)

    prev = pltpu.roll(z, 1, axis=0)
    row = lax.broadcasted_iota(jnp.int32, z.shape, 0)
    prev = jnp.where(row == 0, carry_ref[7:8, :], prev)
    x = z + (prev - z) * mu_ref[...]
    carry_ref[...] = z[tm - 8:, :]
    _rwkv_elementwise(x, vf_ref, w2_ref, bd_ref, w0_ref, a0_ref, kkp_ref, ka_ref, v0_ref,
                      *rw_outs)


def _layer_in(h, g, w, mu, ct, st, gq, gkv, wqa, wqb, wka, pb, wvt, vf, w2, bd, w0, a0, k_k,
              k_a, v0, seq, tm):
    t, d = h.shape
    rw = RWKV_WIDTH
    hw = MLA_HEADS * HEAD_PAD
    vw = wvt.shape[0]
    first = vf is None
    ones_col = (jnp.arange(vw) % (MLA_V_DIM + ATTN_ONES_ROWS) == MLA_V_DIM).astype(F32)[:, None]
    q_scale = (MLA_NOPE_DIM + MLA_ROPE_DIM) ** -0.5 * LOG2E
    row = lambda width: pl.BlockSpec((tm, width), lambda i: (i, 0))
    vec = _full((1, rw))
    args = [h, g, w, mu, ct, st, gq, gkv, wqa, wqb, wka, pb, wvt, ones_col]
    in_specs = [row(d), _full((1, d)), _full(w.shape), _full((1, RW_C_PAD)), row(LANE), row(LANE),
                _full(gq.shape), _full(gkv.shape), _full(wqa.shape), _full(wqb.shape),
                _full(wka.shape), _full(pb.shape), _full(wvt.shape), _full(ones_col.shape)]
    if first:
        args += [w2, bd, w0, a0, k_k, k_a]
        in_specs += [_full(w2.shape), _full(bd.shape), vec, vec, vec, vec]
    else:
        args += [vf, w2, bd, w0, a0, k_k, k_a, v0]
        in_specs += [row(rw), _full(w2.shape), _full(bd.shape), vec, vec, vec, vec, vec]
    rw_dt = [BF16, BF16, BF16, BF16, BF16, F32, BF16]
    kern = functools.partial(_layer_in_kernel, first=first, tiles_per_seq=seq // tm,
                             q_scale=q_scale)
    return pl.pallas_call(
        kern,
        grid=(t // tm,),
        in_specs=in_specs,
        out_specs=[row(hw), row(hw), pl.BlockSpec((vw, tm), lambda i: (0, i))] + [row(rw)] * 7,
        out_shape=[jax.ShapeDtypeStruct((t, hw), BF16), jax.ShapeDtypeStruct((t, hw), BF16),
                   jax.ShapeDtypeStruct((vw, t), BF16)]
                  + [jax.ShapeDtypeStruct((t, rw), dt) for dt in rw_dt],
        scratch_shapes=[pltpu.VMEM((8, RW_C_PAD), F32)],
        compiler_params=_params("arbitrary"),
        name="layer_in",
    )(*args)


def _attn_kernel(q_ref, k_ref, vt_ref, g_ref, o_ref, m_ref, a_ref, p_ref, acc_ref, s_ref, s2_ref,
                 *, tq, tk, nh):
    qi = pl.program_id(2)
    dv = MLA_V_DIM
    heads = range(nh)

    ext = dv + ATTN_ONES_ROWS

    m_ref[...] = jnp.full_like(m_ref, NEG_BIG)
    a_ref[...] = jnp.ones_like(a_ref)
    p_ref[...] = jnp.zeros_like(p_ref)
    acc_ref[...] = jnp.zeros_like(acc_ref)

    def flush_values(j, q0=0):
        off = pl.multiple_of(j * tk, tk)
        pv = [_dot(vt_ref[h * ext:(h + 1) * ext, pl.ds(off, tk)], p_ref[h, :, q0:])
              for h in heads]
        for h in heads:
            acc_ref[h, :, q0:] = a_ref[h, :, q0:] * acc_ref[h, :, q0:] + pv[h]

    def scores(j, h, dst, q0=0):
        off = pl.multiple_of(j * tk, tk)
        dst[h, :, q0:] = _dot_nt(k_ref[0, pl.ds(off, tk), h * HEAD_PAD:(h + 1) * HEAD_PAD],
                                 q_ref[0, q0:, h * HEAD_PAD:(h + 1) * HEAD_PAD])

    def softmax(h, src, key0=None, q0=0):
        s = src[h, :, q0:]
        if key0 is not None:
            ck = (lax.broadcasted_iota(jnp.int32, s.shape, 0) + key0) // CHUNK
            cq = (lax.broadcasted_iota(jnp.int32, s.shape, 1) + q0) // CHUNK
            s = jnp.where(ck <= cq, s, NEG_BIG)
        m_old = m_ref[h, :, q0:]
        m_new = jnp.maximum(m_old, jnp.max(s, axis=0, keepdims=True))
        p_ref[h, :, q0:] = jnp.exp2((s - m_new).astype(BF16))
        a_ref[h, :, q0:] = jnp.exp2(m_old - m_new)
        m_ref[h, :, q0:] = m_new

    def stage(j, src, dst, key0=None, q0=0, next_q0=0):
        flush_values(jnp.maximum(j - 1, 0))
        for h in heads:
            if dst is not None:
                scores(j + 1, h, dst, next_q0)
            softmax(h, src, key0, q0)

    for h in heads:
        scores(0, h, s_ref)

    @pl.loop(0, qi)
    def _(t):
        stage(2 * t, s_ref, s2_ref)
        stage(2 * t + 1, s2_ref, s_ref)

    stage(2 * qi, s_ref, s2_ref, key0=0, next_q0=tk)
    stage(2 * qi + 1, s2_ref, None, key0=tk, q0=tk)
    flush_values(2 * qi + 1, q0=tk)
    outs = []
    for h in heads:
        acc = acc_ref[h]
        o = acc[:dv] / acc[dv:dv + 1]
        ms = jnp.sum(o * o, axis=0, keepdims=True) * (1.0 / dv)
        outs.append(o * lax.rsqrt(ms + NORM_EPS))
    o_t = jnp.concatenate(outs, axis=0)
    o_ref[0] = (o_t.T * g_ref[...]).astype(BF16)


def _mla_attention(q, k, vt, g, seq, tk, nh):
    tq = 2 * tk
    b = q.shape[0] // seq
    hw = q.shape[1]
    groups = hw // (nh * HEAD_PAD)
    ow = nh * MLA_V_DIM
    q3 = q.reshape(b, seq, hw)
    k3 = k.reshape(b, seq, hw)
    return pl.pallas_call(
        functools.partial(_attn_kernel, tq=tq, tk=tk, nh=nh),
        grid=(b, groups, seq // tq),
        in_specs=[pl.BlockSpec((1, tq, nh * HEAD_PAD), lambda bi, h, i: (bi, i, h)),
                  pl.BlockSpec((1, seq, nh * HEAD_PAD), lambda bi, h, i: (bi, 0, h)),
                  pl.BlockSpec((nh * (MLA_V_DIM + ATTN_ONES_ROWS), seq), lambda bi, h, i: (h, bi)),
                  pl.BlockSpec((1, ow), lambda bi, h, i: (0, h))],
        out_specs=pl.BlockSpec((1, tq, ow), lambda bi, h, i: (bi, i, h)),
        out_shape=jax.ShapeDtypeStruct((b, seq, groups * ow), BF16),
        scratch_shapes=[pltpu.VMEM((nh, 1, tq), F32), pltpu.VMEM((nh, 1, tq), F32),
                        pltpu.VMEM((nh, tk, tq), BF16),
                        pltpu.VMEM((nh, MLA_V_DIM + ATTN_ONES_ROWS, tq), F32),
                        pltpu.VMEM((nh, tk, tq), F32), pltpu.VMEM((nh, tk, tq), F32)],
        compiler_params=_params("parallel", "parallel", "arbitrary"),
        name="mla_attention",
    )(q3, k3, vt, g)


_DIMS = {"nn": (((1,), (0,)), ((), ())), "nt": (((1,), (1,)), ((), ())),
         "tn": (((0,), (0,)), ((), ()))}


def _split(x):
    hi = x.astype(BF16)
    return hi, (x - hi.astype(F32)).astype(BF16)


def _mm(a, b, mode, dims="nn"):
    dg = lambda x, y: lax.dot_general(x, y, _DIMS[dims], preferred_element_type=F32)
    if mode == "bf16":
        return dg(a.astype(BF16), b.astype(BF16))
    if mode == "x3":
        a_hi, a_lo = _split(a)
        b_hi, b_lo = _split(b)
        return dg(a_hi, b_hi) + (dg(a_lo, b_hi) + dg(a_hi, b_lo))
    if mode == "ax":
        n = b.shape[1]
        out = dg(a.astype(BF16), jnp.concatenate(_split(b), axis=1))
        return out[:, :n] + out[:, n:]
    return lax.dot_general(a, b, _DIMS[dims], preferred_element_type=F32, precision=HI)


def _unit_lower_solve(a, rhs, row, col, eye, mode):
    return _unit_lower_solve_many([a], [rhs], row, col, eye, mode)[0]


def _unit_lower_solve_many(a_list, rhs_list, row, col, eye, mode):
    d16 = (row // 16) == (col // 16)
    d32 = (row // 32) == (col // 32)
    n = range(len(a_list))
    d = [jnp.where(d16, a, 0.0) for a in a_list]
    x = [eye + d[i] for i in n]
    d2 = [_mm(d[i], d[i], mode) for i in n]
    t = [_mm(d2[i], jnp.concatenate([x[i], d2[i]], axis=1), mode) for i in n]
    x = [x[i] + t[i][:, :128] for i in n]
    d4 = [t[i][:, 128:] for i in n]
    t = [_mm(d4[i], jnp.concatenate([x[i], d4[i]], axis=1), mode) for i in n]
    x = [x[i] + t[i][:, :128] for i in n]
    d8 = [t[i][:, 128:] for i in n]
    t = [_mm(d8[i], x[i], mode) for i in n]
    x = [x[i] + t[i] for i in n]
    e = [jnp.where(d32, a_list[i], 0.0) - d[i] for i in n]
    t = [_mm(e[i], x[i], mode) for i in n]
    t = [_mm(x[i], t[i], mode) for i in n]
    x = [x[i] + t[i] for i in n]
    f = [jnp.where(d32, 0.0, a_list[i]) for i in n]
    z = [_mm(x[i], rhs_list[i], mode) for i in n]
    t = [_mm(f[i], z[i], mode) for i in n]
    t = [_mm(x[i], t[i], mode) for i in n]
    return [z[i] + t[i] for i in n]


def _wkv_kernel(r_ref, k_ref, v_ref, kk_ref, kka_ref, lw_ref, g_ref,
                rk_ref, lnw_ref, lnb_ref, o_ref, s_ref, *, nch, modes):
    m_quad, m_inv, m_state = modes
    c_len = WKV_CHUNK

    npair = s_ref.shape[0]

    @pl.when(pl.program_id(1) == 0)
    def _():
        s_ref[...] = jnp.zeros_like(s_ref)

    row = lax.broadcasted_iota(jnp.int32, (128, 128), 0)
    col = lax.broadcasted_iota(jnp.int32, (128, 128), 1)
    same = (row // c_len) == (col // c_len)
    mask_s = same & (col < row)
    mask_i = same & (col <= row)
    eye = jnp.where(row == col, 1.0, 0.0)
    lane0 = lax.broadcasted_iota(jnp.int32, (c_len, 128), 1) < RWKV_HEAD
    tr = lax.broadcasted_iota(jnp.int32, (c_len, c_len), 0)
    tc = lax.broadcasted_iota(jnp.int32, (c_len, c_len), 1)
    ltri = jnp.where(tc <= tr, 1.0, 0.0)

    def stack(x):
        return jnp.concatenate([jnp.where(lane0, x, 0.0), jnp.where(lane0, 0.0, x)], axis=0)

    def head_sums(x):
        first = jnp.sum(jnp.where(lane0, x, 0.0), axis=-1, keepdims=True)
        both = jnp.sum(x, axis=-1, keepdims=True)
        return jnp.where(lane0, first, both - first)

    par = []
    for c, hp in [(c, hp) for c in range(nch) for hp in range(npair)]:
        sl = pl.ds(c * c_len, c_len)
        ln = slice(hp * LANE, (hp + 1) * LANE)
        r = r_ref[0, sl, ln].astype(F32)
        k = k_ref[0, sl, ln].astype(F32)
        v = v_ref[0, sl, ln].astype(F32)
        kk = kk_ref[0, sl, ln].astype(F32)
        kka = kka_ref[0, sl, ln].astype(F32)
        lw = lw_ref[0, sl, ln]
        cum = _mm(ltri, lw, "ax")
        tot = cum[c_len - 1:c_len, :]
        p_in = jnp.exp(cum)
        p_ex = jnp.exp(cum - lw)
        p_inv = jnp.exp(-cum)
        p_end = jnp.exp(tot - cum)
        at = stack(-kk * p_ex)
        rt = stack(r * p_in)
        bt = kka * p_inv
        kt = k * p_inv
        par.append(dict(
            r=r, k=k, v=v, tot=tot, at=at, rt=rt, vs=stack(v),
            bkh=jnp.concatenate([stack(kka * p_end), stack(k * p_end)], axis=0),
            lhs=jnp.concatenate([at, rt], axis=0),
            rhs=jnp.concatenate([bt, bt, kt, kt], axis=0)))
    for p in par:
        quad = _mm(p["lhs"], p["rhs"], m_quad, "nt")
        p["a_ab"] = jnp.where(mask_s, quad[:128, :128], 0.0)
        p["a_ak"] = jnp.where(mask_s, quad[:128, 128:], 0.0)
        p["a_r"] = jnp.concatenate([jnp.where(mask_i, quad[128:, :128], 0.0),
                                    jnp.where(mask_i, quad[128:, 128:], 0.0)], axis=1)
    for p in par:
        p["akv"] = _mm(p["a_ak"], p["vs"], m_quad)
    wus = _unit_lower_solve_many(
        [p["a_ab"] for p in par],
        [jnp.concatenate([p["at"], p["akv"]], axis=1) for p in par], row, col, eye, m_inv)

    pairs = range(npair)
    state = [s_ref[hp] for hp in pairs]
    def outputs(c, m1, uv):
        sl = pl.ds(c * c_len, c_len)
        pc = [par[c * npair + hp] for hp in pairs]
        ys = [m1[hp][128:] + _mm(pc[hp]["a_r"], uv[hp], m_state) for hp in pairs]
        y = [ys[hp][:c_len] + ys[hp][c_len:] for hp in pairs]
        mean = [head_sums(y[hp]) * (1.0 / RWKV_HEAD) for hp in pairs]
        yc = [y[hp] - mean[hp] for hp in pairs]
        var = [head_sums(yc[hp] * yc[hp]) * (1.0 / RWKV_HEAD) for hp in pairs]
        for hp in pairs:
            ln = slice(hp * LANE, (hp + 1) * LANE)
            r, k, v = pc[hp]["r"], pc[hp]["k"], pc[hp]["v"]
            yn = yc[hp] * lax.rsqrt(var[hp] + GN_EPS) * lnw_ref[:, ln] + lnb_ref[:, ln]
            bonus = head_sums(r * k * rk_ref[:, ln])
            yn = yn + bonus * v
            o_ref[0, sl, ln] = (yn * g_ref[0, sl, ln].astype(F32)).astype(BF16)

    pending = None
    for c in range(nch):
        pc = [par[c * npair + hp] for hp in pairs]
        wu = [wus[c * npair + hp] for hp in pairs]
        m1 = [_mm(jnp.concatenate([wu[hp][:, :128], pc[hp]["rt"]], axis=0), state[hp],
                  m_state, "nt") for hp in pairs]
        uv = [jnp.concatenate([m1[hp][:128] + wu[hp][:, 128:], pc[hp]["vs"]], axis=0)
              for hp in pairs]
        upd = [_mm(uv[hp], pc[hp]["bkh"], m_state, "tn") for hp in pairs]
        state = [state[hp] * jnp.exp(pc[hp]["tot"]) + upd[hp] for hp in pairs]
        if pending is not None:
            outputs(*pending)
        pending = (c, m1, uv)
    outputs(*pending)
    for hp in pairs:
        s_ref[hp] = state[hp]


def _wkv(r, k, v, kk, kka, lw, g, r_k, ln_w, ln_b, rows, modes):
    b, s, w = r.shape
    blk = pl.BlockSpec((1, rows, w), lambda bi, i: (bi, i, 0))
    kern = functools.partial(_wkv_kernel, nch=rows // WKV_CHUNK, modes=modes)
    return pl.pallas_call(
        kern,
        grid=(b, s // rows),
        in_specs=[blk] * 7 + [_full((1, w))] * 3,
        out_specs=blk,
        out_shape=jax.ShapeDtypeStruct((b, s, w), BF16),
        scratch_shapes=[pltpu.VMEM((w // LANE, LANE, LANE), F32)],
        compiler_params=_params("parallel", "arbitrary"),
        name="wkv7",
    )(r, k, v, kk, kka, lw, g, r_k, ln_w, ln_b)


def _mem_kv_kernel(m_ref, g_ref, w_ref, o_ref):
    d = m_ref.shape[-1]
    mn = _rms(m_ref[...], g_ref[...], d).astype(BF16)
    o_ref[...] = _dot(mn, w_ref[0]).astype(BF16)


def _mem_kv(mem2d, g, wkv, tm):
    t, d = mem2d.shape
    nl, _, n = wkv.shape
    return pl.pallas_call(
        _mem_kv_kernel,
        grid=(nl, t // tm),
        in_specs=[pl.BlockSpec((tm, d), lambda l, i: (i, 0)), _full((1, d)),
                  pl.BlockSpec((1, d, n), lambda l, i: (l, 0, 0))],
        out_specs=pl.BlockSpec((tm, n), lambda l, i: (i, l)),
        out_shape=jax.ShapeDtypeStruct((t, nl * n), BF16),
        compiler_params=_params("parallel", "parallel"),
        name="mem_kv",
    )(mem2d, g, wkv)


def _mix_out_kernel(h_ref, att_ref, rw_ref, woa_ref, wob_ref, g_ref, wq_ref,
                    mk_ref, mv_ref, wo_ref, o_ref, *, q_scale):
    d = h_ref.shape[-1]
    h1 = h_ref[...] + _dot(att_ref[...], woa_ref[0]) + _dot(rw_ref[...], wob_ref[0])
    hn = _rms(h1, g_ref[0], d).astype(BF16)
    q = (_dot(hn, wq_ref[0]) * q_scale).astype(BF16)
    dh = d // CA_HEADS
    outs = []
    for hd in range(CA_HEADS):
        sl = slice(hd * dh, (hd + 1) * dh)
        s = _dot_nt(q[:, sl], mk_ref[:, sl])
        m = jnp.max(s, axis=-1, keepdims=True)
        p = jnp.exp2(s - m)
        l = jnp.sum(p, axis=-1, keepdims=True)
        outs.append((_dot(p.astype(BF16), mv_ref[:, sl]) / l).astype(BF16))
    o = jnp.concatenate(outs, axis=1)
    o_ref[...] = h1 + _dot(o, wo_ref[0])


def _mix_out(h, att, rw, woa, wob, g, wq, memkv, wo, layer, seq, mem_len, tm):
    t, d = h.shape
    tiles = seq // tm
    q_scale = (d // CA_HEADS) ** -0.5 * LOG2E
    row = lambda w: pl.BlockSpec((tm, w), lambda i: (i, 0))
    lay = functools.partial(_layer_slice, layer=layer)
    return pl.pallas_call(
        functools.partial(_mix_out_kernel, q_scale=q_scale),
        grid=(t // tm,),
        in_specs=[row(d), row(att.shape[1]), row(rw.shape[1]), lay(woa), lay(wob), lay(g),
                  lay(wq),
                  pl.BlockSpec((mem_len, d), lambda i: (i // tiles, 2 * layer)),
                  pl.BlockSpec((mem_len, d), lambda i: (i // tiles, 2 * layer + 1)),
                  lay(wo)],
        out_specs=row(d),
        out_shape=jax.ShapeDtypeStruct((t, d), F32),
        compiler_params=_params("parallel"),
        name="mix_out_cross_attention",
    )(h, att, rw, woa, wob, g, wq, memkv, memkv, wo)


def _mlp_kernel(h_ref, g_ref, wu_ref, wd_ref, gf_ref, o_ref, *, last, ff_chunk):
    d = h_ref.shape[-1]
    h = h_ref[...]
    hn = _rms(h, g_ref[0], d).astype(BF16)
    acc = h
    for c in range(wu_ref.shape[-1] // ff_chunk):
        sl = slice(c * ff_chunk, (c + 1) * ff_chunk)
        u = jnp.maximum(_dot(hn, wu_ref[0, :, sl]), 0.0)
        acc = acc + _dot((u * u).astype(BF16), wd_ref[0, sl, :])
    if last:
        acc = _rms(acc, gf_ref[...], d)
    o_ref[...] = acc


def _mlp(h, g, wu, wd, gf, layer, last, tm):
    t, d = h.shape
    row = pl.BlockSpec((tm, d), lambda i: (i, 0))
    lay = functools.partial(_layer_slice, layer=layer)
    return pl.pallas_call(
        functools.partial(_mlp_kernel, last=last, ff_chunk=1024),
        grid=(t // tm,),
        in_specs=[row, lay(g), lay(wu), lay(wd), _full((1, d))],
        out_specs=row,
        out_shape=jax.ShapeDtypeStruct((t, d), F32),
        compiler_params=_params("parallel"),
        name="mlp",
    )(h, g, wu, wd, gf)


def _pad_cols(a, n):
    return jnp.pad(a, ((0, 0), (0, n - a.shape[1])))


def _swap_halves(a):
    half = a.shape[-1] // 2
    return jnp.concatenate([a[..., half:], a[..., :half]], axis=-1)


def _in_weights(w_in, mu):
    cq = w_in[:, :MLA_Q_RANK]
    ckv = w_in[:, MLA_Q_RANK:MLA_Q_RANK + MLA_KV_RANK]
    kr = w_in[:, MLA_Q_RANK + MLA_KV_RANK:MLA_COLS]
    mla = jnp.concatenate([_pad_cols(cq, 256), ckv, kr, _swap_halves(kr)], axis=1)
    w = jnp.concatenate([_pad_cols(mla, MLA_C_PAD), _pad_cols(w_in[:, MLA_COLS:], RW_C_PAD)], axis=1)
    return w.astype(BF16), _pad_cols(mu[None, :], RW_C_PAD)


def _head_pad(a, n_heads, per_head, lo, hi, at):
    rows = a.shape[0]
    a = a.reshape(rows, n_heads, per_head)[:, :, lo:hi]
    a = jnp.pad(a, ((0, 0), (0, 0), (at, HEAD_PAD - at - (hi - lo))))
    return a.reshape(rows, n_heads * HEAD_PAD)


def _mla_weights(w_uq, w_ukv):
    dq = MLA_NOPE_DIM + MLA_ROPE_DIM
    h = MLA_HEADS
    wq3 = w_uq.reshape(MLA_Q_RANK, h, dq)
    rope_sw = _swap_halves(wq3[:, :, MLA_NOPE_DIM:]).reshape(MLA_Q_RANK, h * MLA_ROPE_DIM)
    wqa = _head_pad(w_uq, h, dq, 0, dq, 0)
    wqb = _head_pad(rope_sw, h, MLA_ROPE_DIM, 0, MLA_ROPE_DIM, MLA_NOPE_DIM)
    pad_q = ((0, 256 - MLA_Q_RANK), (0, 0))
    wqa = jnp.pad(wqa, pad_q)
    wqb = jnp.pad(wqb, pad_q)
    dkv = MLA_NOPE_DIM + MLA_V_DIM
    wk = _head_pad(w_ukv, h, dkv, 0, MLA_NOPE_DIM, 0)
    wvt = jnp.transpose(w_ukv.reshape(MLA_KV_RANK, h, dkv)[:, :, MLA_NOPE_DIM:], (1, 2, 0))
    wvt = jnp.pad(wvt, ((0, 0), (0, ATTN_ONES_ROWS), (0, 0))).reshape(-1, MLA_KV_RANK)
    eye = jnp.eye(MLA_ROPE_DIM, dtype=F32)
    place = _head_pad(jnp.tile(eye, (1, h)), h, MLA_ROPE_DIM, 0, MLA_ROPE_DIM, MLA_NOPE_DIM)
    zeros = jnp.zeros_like(place)
    pad_rows = jnp.zeros((LANE - 2 * MLA_ROPE_DIM, h * HEAD_PAD), F32)
    pa = jnp.concatenate([place, zeros, pad_rows], axis=0)
    pb = jnp.concatenate([zeros, place, pad_rows], axis=0)
    wka = jnp.concatenate([wk, pa], axis=0)
    return tuple(a.astype(BF16) for a in (wqa, wqb, wka, pb, wvt))


def _lora_weights(w2, a2, g2, v2):
    rw = RWKV_WIDTH
    z = lambda r: jnp.zeros((r, rw), F32)
    rows = [
        jnp.concatenate([w2, z(DECAY_RANK), z(DECAY_RANK), z(DECAY_RANK)], axis=1),
        jnp.concatenate([z(ICL_RANK), a2, z(ICL_RANK), z(ICL_RANK)], axis=1),
        jnp.concatenate([z(GATE_RANK), z(GATE_RANK), g2, z(GATE_RANK)], axis=1),
        jnp.concatenate([z(VRES_RANK)] * 3 + [v2 if v2 is not None else z(VRES_RANK)], axis=1),
    ]
    w = jnp.concatenate(rows, axis=0)
    return jnp.pad(w, ((0, LORA_PAD - w.shape[0]), (0, 0))).astype(BF16)


def _block_diag(n, blk, value):
    i = jnp.arange(n)
    return jnp.where((i[:, None] // blk) == (i[None, :] // blk), value, 0.0)


def kernel(x, mem, positions, mix_norm, w_in_first, w_in_rest, shift_mu_first, shift_mu_rest, mla_q_norm, mla_w_uq, mla_kv_norm, mla_w_ukv, mla_out_norm, rwkv_w0, rwkv_w2, rwkv_a0, rwkv_a2, rwkv_g2, rwkv_v0, rwkv_v2, rwkv_k_k, rwkv_k_a, rwkv_r_k, rwkv_ln_w, rwkv_ln_b, w_out, ca_norm, mem_norm, ca_wq, ca_wkv, ca_wo, mlp_norm, mlp_w_up, mlp_w_down, final_norm):
    b, s, d = x.shape
    depth = mix_norm.shape[0]
    mem_len = mem.shape[1]
    t = b * s
    tm = min(512, s)
    tm_wide = min(1024, s)
    tq = min(256, s)
    wkv_rows = min(512, s)
    rw = RWKV_WIDTH

    inv = ROPE_THETA ** (-jnp.arange(0, MLA_ROPE_DIM, 2, dtype=F32) / MLA_ROPE_DIM)
    inv_row = jnp.concatenate([jnp.zeros((MLA_NOPE_DIM,), F32), inv, inv,
                               jnp.zeros((HEAD_PAD - MLA_NOPE_DIM - MLA_ROPE_DIM,), F32)])[None, :]
    ct, st = _rope_tables(positions.astype(F32).reshape(t, 1), inv_row, tm)

    memkv = _mem_kv(mem.reshape(b * mem_len, d), mem_norm[None, :], ca_wkv.astype(BF16),
                    min(512, b * mem_len))

    woa = w_out[:, :MLA_HEADS * MLA_V_DIM].astype(BF16)
    wob = w_out[:, MLA_HEADS * MLA_V_DIM:].astype(BF16)
    wq_b = ca_wq.astype(BF16)
    wo_b = ca_wo.astype(BF16)
    wu_b = mlp_w_up.astype(BF16)
    wd_b = mlp_w_down.astype(BF16)
    ca_g = ca_norm[:, None, :]
    mlp_g = mlp_norm[:, None, :]
    bd512 = _block_diag(rw, RWKV_HEAD, 1.0).astype(BF16)

    h = x.reshape(t, d)
    v_first = None
    for l in range(depth):
        if l == 0:
            w_in, mu, v0, v2 = w_in_first, shift_mu_first, None, None
        else:
            w_in, mu, v0, v2 = w_in_rest[l - 1], shift_mu_rest[l - 1], rwkv_v0[l - 1], rwkv_v2[l - 1]
        w_all, mu_p = _in_weights(w_in, mu)
        wqa, wqb, wka, pb, wvt = _mla_weights(mla_w_uq[l], mla_w_ukv[l])
        gq = _pad_cols(mla_q_norm[l][None, :], 256)
        w2c = _lora_weights(rwkv_w2[l], rwkv_a2[l], rwkv_g2[l], v2)
        vec = lambda a: a.reshape(1, rw)
        q, k, vt, r_, k_, v_, kk_, kka_, lw_, g_ = _layer_in(
            h, mix_norm[l][None, :], w_all, mu_p, ct, st, gq, mla_kv_norm[l][None, :],
            wqa, wqb, wka, pb, wvt, v_first, w2c, bd512, vec(rwkv_w0[l]), vec(rwkv_a0[l]),
            vec(rwkv_k_k[l]), vec(rwkv_k_a[l]), None if v0 is None else vec(v0), s, tm)
        if l == 0:
            v_first = v_

        att = _mla_attention(q, k, vt, mla_out_norm[l][None, :], s, tq, ATTN_HEADS_PER_STEP)

        sh = lambda a: a.reshape(b, s, rw)
        y = _wkv(sh(r_), sh(k_), sh(v_), sh(kk_), sh(kka_), sh(lw_), sh(g_),
                 vec(rwkv_r_k[l]), vec(rwkv_ln_w[l]), vec(rwkv_ln_b[l]), wkv_rows, WKV_MODES)

        h = _mix_out(h, att.reshape(t, -1), y.reshape(t, rw), woa, wob, ca_g, wq_b, memkv, wo_b,
                     l, s, mem_len, tm_wide)
        h = _mlp(h, mlp_g, wu_b, wd_b, final_norm[None, :], l, l == depth - 1, tm_wide)
    return h.reshape(b, s, d)
```

```python
import functools
import math

import jax
import jax.numpy as jnp
from jax import lax
from jax.experimental import pallas as pl
from jax.experimental.pallas import tpu as pltpu

F32 = jnp.float32
BF16 = jnp.bfloat16

CHUNK = 64
NORM_EPS = 1e-6
MLA_HEADS = 8
MLA_V_DIM = 64
MLA_NOPE_DIM = 64
MLA_ROPE_DIM = 32
MLA_Q_RANK = 192
MLA_KV_RANK = 128
ROPE_THETA = 10000.0
RWKV_HEAD = 64
RWKV_HEADS = 8
RWKV_WIDTH = 512
DECAY_RANK = 32
ICL_RANK = 32
GATE_RANK = 96
VRES_RANK = 32
GN_EPS = 64e-5
CA_HEADS = 4
MLA_COLS = MLA_Q_RANK + MLA_KV_RANK + MLA_ROPE_DIM

LANE = 128
HEAD_PAD = 128
MLA_C_PAD = 512
RW_C_PAD = 1792
LORA_PAD = 256
WKV_CHUNK = 64
LAYER_IN_PARTS = 2
ATTN_HEADS_PER_STEP = 4
ATTN_ONES_ROWS = 16
LOG2E = 1.4426950408889634
NEG_BIG = -1e30
VMEM_LIMIT = 56 * 1024 * 1024

HI = lax.Precision.HIGHEST
WKV_MODES = ("bf16", "bf16", "bf16")


def _dot(a, b, precision=None):
    return jnp.dot(a, b, preferred_element_type=F32, precision=precision)


def _dot_nt(a, b, precision=None):
    return lax.dot_general(a, b, (((1,), (1,)), ((), ())),
                           preferred_element_type=F32, precision=precision)


def _dot_tn(a, b, precision=None):
    return lax.dot_general(a, b, (((0,), (0,)), ((), ())),
                           preferred_element_type=F32, precision=precision)


def _rms(x, g, n, eps=NORM_EPS):
    ms = jnp.sum(x * x, axis=-1, keepdims=True) * (1.0 / n)
    return x * lax.rsqrt(ms + eps) * g


def _sigmoid(x):
    return 0.5 + 0.5 * jnp.tanh(0.5 * x)


def _params(*sem):
    return pltpu.CompilerParams(dimension_semantics=sem, vmem_limit_bytes=VMEM_LIMIT)


def _full(shape):
    nd = len(shape)
    return pl.BlockSpec(shape, lambda *_: (0,) * nd)


def _layer_slice(a, layer):
    nd = a.ndim
    return pl.BlockSpec((1,) + a.shape[1:], lambda i: (layer,) + (0,) * (nd - 1),
                        pipeline_mode=pl.Buffered(1))


def _rope_kernel(pos_ref, inv_ref, ct_ref, st_ref):
    ang = pos_ref[...] * inv_ref[...]
    c = jnp.cos(ang)
    s = jnp.sin(ang)
    lane = lax.broadcasted_iota(jnp.int32, ang.shape, 1)
    half = MLA_NOPE_DIM + MLA_ROPE_DIM // 2
    end = MLA_NOPE_DIM + MLA_ROPE_DIM
    ct_ref[...] = jnp.where(lane < MLA_NOPE_DIM, 1.0, jnp.where(lane < end, c, 0.0))
    st_ref[...] = jnp.where(lane < MLA_NOPE_DIM, 0.0,
                            jnp.where(lane < half, -s, jnp.where(lane < end, s, 0.0)))


def _rope_tables(pos_f, inv_row, tm):
    t = pos_f.shape[0]
    return pl.pallas_call(
        _rope_kernel,
        grid=(t // tm,),
        in_specs=[pl.BlockSpec((tm, 1), lambda i: (i, 0)), _full((1, LANE))],
        out_specs=[pl.BlockSpec((tm, LANE), lambda i: (i, 0))] * 2,
        out_shape=[jax.ShapeDtypeStruct((t, LANE), F32)] * 2,
        compiler_params=_params("parallel"),
        name="rope_tables",
    )(pos_f, inv_row)


def _mla_latents(c, ct_ref, st_ref, gq_ref, gkv_ref, wqa_ref, wqb_ref, wka_ref, pb_ref,
                 wv_ref, ones_ref, q_ref, k_ref, v_ref, q_scale):
    nq = _rms(c[:, :256], gq_ref[...], MLA_Q_RANK).astype(BF16)
    nkv = _rms(c[:, 256:384], gkv_ref[...], MLA_KV_RANK).astype(BF16)
    kr = c[:, 384:512].astype(BF16)
    ct = jnp.tile(ct_ref[...], (1, MLA_HEADS))
    st = jnp.tile(st_ref[...], (1, MLA_HEADS))
    q = _dot(nq, wqa_ref[...]) * ct + _dot(nq, wqb_ref[...]) * st
    q_ref[...] = (q * q_scale).astype(BF16)
    x = jnp.concatenate([nkv, kr], axis=1)
    k = _dot(x, wka_ref[...]) * ct + _dot(kr, pb_ref[...]) * st
    k_ref[...] = k.astype(BF16)
    v_ref[...] = (_dot_nt(wv_ref[...], nkv) + ones_ref[...]).astype(BF16)


def _rwkv_elementwise(x, vf_ref, w2_ref, bd_ref, w0_ref, a0_ref, kk_ref_p, ka_ref, v0_ref,
                      r_o, k_o, v_o, kk_o, kka_o, lw_o, g_o):
    rw = RWKV_WIDTH
    r_o[...] = x[:, 0:rw].astype(BF16)
    k = x[:, rw:2 * rw]
    v = x[:, 2 * rw:3 * rw]
    tail = x[:, 3 * rw:3 * rw + LORA_PAD]
    lane = lax.broadcasted_iota(jnp.int32, tail.shape, 1)
    g_lo = DECAY_RANK + ICL_RANK
    g_hi = g_lo + GATE_RANK
    is_gate = (lane >= g_lo) & (lane < g_hi)
    th = jnp.tanh(jnp.where(is_gate, 0.5 * tail, tail))
    act = jnp.where(lane < DECAY_RANK, th, jnp.where(is_gate, 0.5 + 0.5 * th, tail))
    lo = _dot(act.astype(BF16), w2_ref[...])
    lw_o[...] = -math.exp(-0.5) * _sigmoid(w0_ref[...] + lo[:, 0:rw])
    a = _sigmoid(a0_ref[...] + lo[:, rw:2 * rw])
    g_o[...] = lo[:, 2 * rw:3 * rw].astype(BF16)
    if vf_ref is not None:
        mix = _sigmoid(v0_ref[...] + lo[:, 3 * rw:4 * rw])
        v = v + (vf_ref[...].astype(F32) - v) * mix
    v_o[...] = v.astype(BF16)
    kk = k * kk_ref_p[...]
    sq = kk * kk
    sq_hi = sq.astype(BF16)
    sq_lo = (sq - sq_hi.astype(F32)).astype(BF16)
    ss = _dot(sq_hi, bd_ref[...]) + _dot(sq_lo, bd_ref[...])
    kk = kk * lax.rsqrt(jnp.maximum(ss, 1e-24))
    kk_o[...] = kk.astype(BF16)
    kka_o[...] = (kk * a).astype(BF16)
    k_o[...] = (k * (1.0 + (a - 1.0) * ka_ref[...])).astype(BF16)


def _layer_in_kernel(*refs, first, tiles_per_seq, q_scale):
    n_in = 20 if first else 22
    ins, outs, carry_ref = refs[:n_in], refs[n_in:n_in + 10], refs[n_in + 10]
    h_ref, g_ref, w_ref, mu_ref = ins[:4]
    mla_ins = ins[4:14]
    if first:
        w2_ref, bd_ref, w0_ref, a0_ref, kkp_ref, ka_ref = ins[14:20]
        vf_ref = v0_ref = None
    else:
        vf_ref, w2_ref, bd_ref, w0_ref, a0_ref, kkp_ref, ka_ref, v0_ref = ins[14:22]
    q_ref, k_ref, vt_ref = outs[:3]
    rw_outs = outs[3:]

    i = pl.program_id(0)
    tm, d = h_ref.shape
    hr = tm // LAYER_IN_PARTS

    @pl.when(i % tiles_per_seq == 0)
    def _():
        carry_ref[...] = jnp.zeros_like(carry_ref)

    parts = [pl.ds(p * hr, hr) for p in range(LAYER_IN_PARTS)]
    hn = [_rms(h_ref[rows, :], g_ref[...], d).astype(BF16) for rows in parts]
    proj = [_dot(x, w_ref[...]) for x in hn]
    last = carry_ref[7:8, :]
    for rows, pr in zip(parts, proj):
        ct_v, st_v = mla_ins[0].at[rows], mla_ins[1].at[rows]
        _mla_latents(pr[:, :MLA_C_PAD], ct_v, st_v, *mla_ins[2:], q_ref.at[rows], k_ref.at[rows],
                     vt_ref.at[:, rows], q_scale)
        z = pr[:, MLA_C_PAD:]
        prev = pltpu.roll(z, 1, axis=0)
        row = lax.broadcasted_iota(jnp.int32, z.shape, 0)
        prev = jnp.where(row == 0, last, prev)
        x = z + (prev - z) * mu_ref[...]
        last = z[hr - 1:hr, :]
        _rwkv_elementwise(x, None if vf_ref is None else vf_ref.at[rows], w2_ref, bd_ref, w0_ref,
                          a0_ref, kkp_ref, ka_ref, v0_ref, *[o.at[rows] for o in rw_outs])
    carry_ref[...] = z[hr - 8:, :]


def _layer_in(h, g, w, mu, ct, st, gq, gkv, wqa, wqb, wka, pb, wvt, vf, w2, bd, w0, a0, k_k,
              k_a, v0, seq, tm):
    t, d = h.shape
    rw = RWKV_WIDTH
    hw = MLA_HEADS * HEAD_PAD
    vw = wvt.shape[0]
    first = vf is None
    ones_col = (jnp.arange(vw) % (MLA_V_DIM + ATTN_ONES_ROWS) == MLA_V_DIM).astype(F32)[:, None]
    q_scale = (MLA_NOPE_DIM + MLA_ROPE_DIM) ** -0.5 * LOG2E
    row = lambda width: pl.BlockSpec((tm, width), lambda i: (i, 0))
    vec = _full((1, rw))
    args = [h, g, w, mu, ct, st, gq, gkv, wqa, wqb, wka, pb, wvt, ones_col]
    in_specs = [row(d), _full((1, d)), _full(w.shape), _full((1, RW_C_PAD)), row(LANE), row(LANE),
                _full(gq.shape), _full(gkv.shape), _full(wqa.shape), _full(wqb.shape),
                _full(wka.shape), _full(pb.shape), _full(wvt.shape), _full(ones_col.shape)]
    if first:
        args += [w2, bd, w0, a0, k_k, k_a]
        in_specs += [_full(w2.shape), _full(bd.shape), vec, vec, vec, vec]
    else:
        args += [vf, w2, bd, w0, a0, k_k, k_a, v0]
        in_specs += [row(rw), _full(w2.shape), _full(bd.shape), vec, vec, vec, vec, vec]
    rw_dt = [BF16, BF16, BF16, BF16, BF16, F32, BF16]
    kern = functools.partial(_layer_in_kernel, first=first, tiles_per_seq=seq // tm,
                             q_scale=q_scale)
    return pl.pallas_call(
        kern,
        grid=(t // tm,),
        in_specs=in_specs,
        out_specs=[row(hw), row(hw), pl.BlockSpec((vw, tm), lambda i: (0, i))] + [row(rw)] * 7,
        out_shape=[jax.ShapeDtypeStruct((t, hw), BF16), jax.ShapeDtypeStruct((t, hw), BF16),
                   jax.ShapeDtypeStruct((vw, t), BF16)]
                  + [jax.ShapeDtypeStruct((t, rw), dt) for dt in rw_dt],
        scratch_shapes=[pltpu.VMEM((8, RW_C_PAD), F32)],
        compiler_params=_params("arbitrary"),
        name="layer_in",
    )(*args)


def _attn_kernel(q_ref, k_ref, vt_ref, g_ref, o_ref, m_ref, a_ref, p_ref, acc_ref, s_ref, s2_ref,
                 *, tq, tk, nh):
    qi = pl.program_id(2)
    dv = MLA_V_DIM
    heads = range(nh)

    ext = dv + ATTN_ONES_ROWS

    m_ref[...] = jnp.full_like(m_ref, NEG_BIG)
    a_ref[...] = jnp.ones_like(a_ref)
    p_ref[...] = jnp.zeros_like(p_ref)
    acc_ref[...] = jnp.zeros_like(acc_ref)

    def flush_values(j, q0=0):
        off = pl.multiple_of(j * tk, tk)
        pv = [_dot(vt_ref[h * ext:(h + 1) * ext, pl.ds(off, tk)], p_ref[h, :, q0:])
              for h in heads]
        for h in heads:
            acc_ref[h, :, q0:] = a_ref[h, :, q0:] * acc_ref[h, :, q0:] + pv[h]

    def scores(j, h, dst, q0=0):
        off = pl.multiple_of(j * tk, tk)
        dst[h, :, q0:] = _dot_nt(k_ref[0, pl.ds(off, tk), h * HEAD_PAD:(h + 1) * HEAD_PAD],
                                 q_ref[0, q0:, h * HEAD_PAD:(h + 1) * HEAD_PAD])

    def softmax(h, src, key0=None, q0=0):
        s = src[h, :, q0:]
        if key0 is not None:
            ck = (lax.broadcasted_iota(jnp.int32, s.shape, 0) + key0) // CHUNK
            cq = (lax.broadcasted_iota(jnp.int32, s.shape, 1) + q0) // CHUNK
            s = jnp.where(ck <= cq, s, NEG_BIG)
        m_old = m_ref[h, :, q0:]
        m_new = jnp.maximum(m_old, jnp.max(s, axis=0, keepdims=True))
        p_ref[h, :, q0:] = jnp.exp2((s - m_new).astype(BF16))
        a_ref[h, :, q0:] = jnp.exp2(m_old - m_new)
        m_ref[h, :, q0:] = m_new

    def stage(j, src, dst, key0=None, q0=0, next_q0=0):
        flush_values(jnp.maximum(j - 1, 0))
        for h in heads:
            if dst is not None:
                scores(j + 1, h, dst, next_q0)
            softmax(h, src, key0, q0)

    for h in heads:
        scores(0, h, s_ref)

    @pl.loop(0, qi)
    def _(t):
        stage(2 * t, s_ref, s2_ref)
        stage(2 * t + 1, s2_ref, s_ref)

    stage(2 * qi, s_ref, s2_ref, key0=0, next_q0=tk)
    stage(2 * qi + 1, s2_ref, None, key0=tk, q0=tk)
    flush_values(2 * qi + 1, q0=tk)
    outs = []
    for h in heads:
        acc = acc_ref[h]
        o = acc[:dv] / acc[dv:dv + 1]
        ms = jnp.sum(o * o, axis=0, keepdims=True) * (1.0 / dv)
        outs.append(o * lax.rsqrt(ms + NORM_EPS))
    o_t = jnp.concatenate(outs, axis=0)
    o_ref[0] = (o_t.T * g_ref[...]).astype(BF16)


def _mla_attention(q, k, vt, g, seq, tk, nh):
    tq = 2 * tk
    b = q.shape[0] // seq
    hw = q.shape[1]
    groups = hw // (nh * HEAD_PAD)
    ow = nh * MLA_V_DIM
    q3 = q.reshape(b, seq, hw)
    k3 = k.reshape(b, seq, hw)
    return pl.pallas_call(
        functools.partial(_attn_kernel, tq=tq, tk=tk, nh=nh),
        grid=(b, groups, seq // tq),
        in_specs=[pl.BlockSpec((1, tq, nh * HEAD_PAD), lambda bi, h, i: (bi, i, h)),
                  pl.BlockSpec((1, seq, nh * HEAD_PAD), lambda bi, h, i: (bi, 0, h)),
                  pl.BlockSpec((nh * (MLA_V_DIM + ATTN_ONES_ROWS), seq), lambda bi, h, i: (h, bi)),
                  pl.BlockSpec((1, ow), lambda bi, h, i: (0, h))],
        out_specs=pl.BlockSpec((1, tq, ow), lambda bi, h, i: (bi, i, h)),
        out_shape=jax.ShapeDtypeStruct((b, seq, groups * ow), BF16),
        scratch_shapes=[pltpu.VMEM((nh, 1, tq), F32), pltpu.VMEM((nh, 1, tq), F32),
                        pltpu.VMEM((nh, tk, tq), BF16),
                        pltpu.VMEM((nh, MLA_V_DIM + ATTN_ONES_ROWS, tq), F32),
                        pltpu.VMEM((nh, tk, tq), F32), pltpu.VMEM((nh, tk, tq), F32)],
        compiler_params=_params("parallel", "parallel", "arbitrary"),
        name="mla_attention",
    )(q3, k3, vt, g)


_DIMS = {"nn": (((1,), (0,)), ((), ())), "nt": (((1,), (1,)), ((), ())),
         "tn": (((0,), (0,)), ((), ()))}


def _split(x):
    hi = x.astype(BF16)
    return hi, (x - hi.astype(F32)).astype(BF16)


def _mm(a, b, mode, dims="nn"):
    dg = lambda x, y: lax.dot_general(x, y, _DIMS[dims], preferred_element_type=F32)
    if mode == "bf16":
        return dg(a.astype(BF16), b.astype(BF16))
    if mode == "x3":
        a_hi, a_lo = _split(a)
        b_hi, b_lo = _split(b)
        return dg(a_hi, b_hi) + (dg(a_lo, b_hi) + dg(a_hi, b_lo))
    if mode == "ax":
        n = b.shape[1]
        out = dg(a.astype(BF16), jnp.concatenate(_split(b), axis=1))
        return out[:, :n] + out[:, n:]
    return lax.dot_general(a, b, _DIMS[dims], preferred_element_type=F32, precision=HI)


def _unit_lower_solve(a, rhs, row, col, eye, mode):
    return _unit_lower_solve_many([a], [rhs], row, col, eye, mode)[0]


def _unit_lower_solve_many(a_list, rhs_list, row, col, eye, mode):
    d16 = (row // 16) == (col // 16)
    d32 = (row // 32) == (col // 32)
    n = range(len(a_list))
    d = [jnp.where(d16, a, 0.0) for a in a_list]
    x = [eye + d[i] for i in n]
    d2 = [_mm(d[i], d[i], mode) for i in n]
    t = [_mm(d2[i], jnp.concatenate([x[i], d2[i]], axis=1), mode) for i in n]
    x = [x[i] + t[i][:, :128] for i in n]
    d4 = [t[i][:, 128:] for i in n]
    t = [_mm(d4[i], jnp.concatenate([x[i], d4[i]], axis=1), mode) for i in n]
    x = [x[i] + t[i][:, :128] for i in n]
    d8 = [t[i][:, 128:] for i in n]
    t = [_mm(d8[i], x[i], mode) for i in n]
    x = [x[i] + t[i] for i in n]
    e = [jnp.where(d32, a_list[i], 0.0) - d[i] for i in n]
    t = [_mm(e[i], x[i], mode) for i in n]
    t = [_mm(x[i], t[i], mode) for i in n]
    x = [x[i] + t[i] for i in n]
    f = [jnp.where(d32, 0.0, a_list[i]) for i in n]
    z = [_mm(x[i], rhs_list[i], mode) for i in n]
    t = [_mm(f[i], z[i], mode) for i in n]
    t = [_mm(x[i], t[i], mode) for i in n]
    return [z[i] + t[i] for i in n]


def _wkv_kernel(r_ref, k_ref, v_ref, kk_ref, kka_ref, lw_ref, g_ref,
                rk_ref, lnw_ref, lnb_ref, o_ref, s_ref, *, nch, modes):
    m_quad, m_inv, m_state = modes
    c_len = WKV_CHUNK

    npair = s_ref.shape[0]

    @pl.when(pl.program_id(1) == 0)
    def _():
        s_ref[...] = jnp.zeros_like(s_ref)

    row = lax.broadcasted_iota(jnp.int32, (128, 128), 0)
    col = lax.broadcasted_iota(jnp.int32, (128, 128), 1)
    same = (row // c_len) == (col // c_len)
    mask_s = same & (col < row)
    mask_i = same & (col <= row)
    eye = jnp.where(row == col, 1.0, 0.0)
    lane0 = lax.broadcasted_iota(jnp.int32, (c_len, 128), 1) < RWKV_HEAD
    tr = lax.broadcasted_iota(jnp.int32, (c_len, c_len), 0)
    tc = lax.broadcasted_iota(jnp.int32, (c_len, c_len), 1)
    ltri = jnp.where(tc <= tr, 1.0, 0.0)

    def stack(x):
        return jnp.concatenate([jnp.where(lane0, x, 0.0), jnp.where(lane0, 0.0, x)], axis=0)

    def head_sums(x):
        first = jnp.sum(jnp.where(lane0, x, 0.0), axis=-1, keepdims=True)
        both = jnp.sum(x, axis=-1, keepdims=True)
        return jnp.where(lane0, first, both - first)

    par = []
    for c, hp in [(c, hp) for c in range(nch) for hp in range(npair)]:
        sl = pl.ds(c * c_len, c_len)
        ln = slice(hp * LANE, (hp + 1) * LANE)
        r = r_ref[0, sl, ln].astype(F32)
        k = k_ref[0, sl, ln].astype(F32)
        v = v_ref[0, sl, ln].astype(F32)
        kk = kk_ref[0, sl, ln].astype(F32)
        kka = kka_ref[0, sl, ln].astype(F32)
        lw = lw_ref[0, sl, ln]
        cum = _mm(ltri, lw, "ax")
        tot = cum[c_len - 1:c_len, :]
        p_in = jnp.exp(cum)
        p_ex = jnp.exp(cum - lw)
        p_inv = jnp.exp(-cum)
        p_end = jnp.exp(tot - cum)
        at = stack(-kk * p_ex)
        rt = stack(r * p_in)
        bt = kka * p_inv
        kt = k * p_inv
        par.append(dict(
            r=r, k=k, v=v, tot=tot, at=at, rt=rt, vs=stack(v),
            bkh=jnp.concatenate([stack(kka * p_end), stack(k * p_end)], axis=0),
            lhs=jnp.concatenate([at, rt], axis=0),
            rhs=jnp.concatenate([bt, bt, kt, kt], axis=0)))
    for p in par:
        quad = _mm(p["lhs"], p["rhs"], m_quad, "nt")
        p["a_ab"] = jnp.where(mask_s, quad[:128, :128], 0.0)
        p["a_ak"] = jnp.where(mask_s, quad[:128, 128:], 0.0)
        p["a_r"] = jnp.concatenate([jnp.where(mask_i, quad[128:, :128], 0.0),
                                    jnp.where(mask_i, quad[128:, 128:], 0.0)], axis=1)
    for p in par:
        p["akv"] = _mm(p["a_ak"], p["vs"], m_quad)
    wus = _unit_lower_solve_many(
        [p["a_ab"] for p in par],
        [jnp.concatenate([p["at"], p["akv"]], axis=1) for p in par], row, col, eye, m_inv)

    pairs = range(npair)
    state = [s_ref[hp] for hp in pairs]
    def outputs(c, m1, uv):
        sl = pl.ds(c * c_len, c_len)
        pc = [par[c * npair + hp] for hp in pairs]
        ys = [m1[hp][128:] + _mm(pc[hp]["a_r"], uv[hp], m_state) for hp in pairs]
        y = [ys[hp][:c_len] + ys[hp][c_len:] for hp in pairs]
        mean = [head_sums(y[hp]) * (1.0 / RWKV_HEAD) for hp in pairs]
        yc = [y[hp] - mean[hp] for hp in pairs]
        var = [head_sums(yc[hp] * yc[hp]) * (1.0 / RWKV_HEAD) for hp in pairs]
        for hp in pairs:
            ln = slice(hp * LANE, (hp + 1) * LANE)
            r, k, v = pc[hp]["r"], pc[hp]["k"], pc[hp]["v"]
            yn = yc[hp] * lax.rsqrt(var[hp] + GN_EPS) * lnw_ref[:, ln] + lnb_ref[:, ln]
            bonus = head_sums(r * k * rk_ref[:, ln])
            yn = yn + bonus * v
            o_ref[0, sl, ln] = (yn * g_ref[0, sl, ln].astype(F32)).astype(BF16)

    pending = None
    for c in range(nch):
        pc = [par[c * npair + hp] for hp in pairs]
        wu = [wus[c * npair + hp] for hp in pairs]
        m1 = [_mm(jnp.concatenate([wu[hp][:, :128], pc[hp]["rt"]], axis=0), state[hp],
                  m_state, "nt") for hp in pairs]
        uv = [jnp.concatenate([m1[hp][:128] + wu[hp][:, 128:], pc[hp]["vs"]], axis=0)
              for hp in pairs]
        upd = [_mm(uv[hp], pc[hp]["bkh"], m_state, "tn") for hp in pairs]
        state = [state[hp] * jnp.exp(pc[hp]["tot"]) + upd[hp] for hp in pairs]
        if pending is not None:
            outputs(*pending)
        pending = (c, m1, uv)
    outputs(*pending)
    for hp in pairs:
        s_ref[hp] = state[hp]


def _wkv(r, k, v, kk, kka, lw, g, r_k, ln_w, ln_b, rows, modes):
    b, s, w = r.shape
    blk = pl.BlockSpec((1, rows, w), lambda bi, i: (bi, i, 0))
    kern = functools.partial(_wkv_kernel, nch=rows // WKV_CHUNK, modes=modes)
    return pl.pallas_call(
        kern,
        grid=(b, s // rows),
        in_specs=[blk] * 7 + [_full((1, w))] * 3,
        out_specs=blk,
        out_shape=jax.ShapeDtypeStruct((b, s, w), BF16),
        scratch_shapes=[pltpu.VMEM((w // LANE, LANE, LANE), F32)],
        compiler_params=_params("parallel", "arbitrary"),
        name="wkv7",
    )(r, k, v, kk, kka, lw, g, r_k, ln_w, ln_b)


def _mem_kv_kernel(m_ref, g_ref, w_ref, o_ref):
    d = m_ref.shape[-1]
    mn = _rms(m_ref[...], g_ref[...], d).astype(BF16)
    o_ref[...] = _dot(mn, w_ref[0]).astype(BF16)


def _mem_kv(mem2d, g, wkv, tm):
    t, d = mem2d.shape
    nl, _, n = wkv.shape
    return pl.pallas_call(
        _mem_kv_kernel,
        grid=(nl, t // tm),
        in_specs=[pl.BlockSpec((tm, d), lambda l, i: (i, 0)), _full((1, d)),
                  pl.BlockSpec((1, d, n), lambda l, i: (l, 0, 0))],
        out_specs=pl.BlockSpec((tm, n), lambda l, i: (i, l)),
        out_shape=jax.ShapeDtypeStruct((t, nl * n), BF16),
        compiler_params=_params("parallel", "parallel"),
        name="mem_kv",
    )(mem2d, g, wkv)


def _mix_out_kernel(h_ref, att_ref, rw_ref, woa_ref, wob_ref, g_ref, wq_ref,
                    mk_ref, mv_ref, wo_ref, o_ref, *, q_scale):
    d = h_ref.shape[-1]
    h1 = h_ref[...] + _dot(att_ref[...], woa_ref[0]) + _dot(rw_ref[...], wob_ref[0])
    hn = _rms(h1, g_ref[0], d).astype(BF16)
    q = (_dot(hn, wq_ref[0]) * q_scale).astype(BF16)
    dh = d // CA_HEADS
    outs = []
    for hd in range(CA_HEADS):
        sl = slice(hd * dh, (hd + 1) * dh)
        s = _dot_nt(q[:, sl], mk_ref[:, sl])
        m = jnp.max(s, axis=-1, keepdims=True)
        p = jnp.exp2(s - m)
        l = jnp.sum(p, axis=-1, keepdims=True)
        outs.append((_dot(p.astype(BF16), mv_ref[:, sl]) / l).astype(BF16))
    o = jnp.concatenate(outs, axis=1)
    o_ref[...] = h1 + _dot(o, wo_ref[0])


def _mix_out(h, att, rw, woa, wob, g, wq, memkv, wo, layer, seq, mem_len, tm):
    t, d = h.shape
    tiles = seq // tm
    q_scale = (d // CA_HEADS) ** -0.5 * LOG2E
    row = lambda w: pl.BlockSpec((tm, w), lambda i: (i, 0))
    lay = functools.partial(_layer_slice, layer=layer)
    return pl.pallas_call(
        functools.partial(_mix_out_kernel, q_scale=q_scale),
        grid=(t // tm,),
        in_specs=[row(d), row(att.shape[1]), row(rw.shape[1]), lay(woa), lay(wob), lay(g),
                  lay(wq),
                  pl.BlockSpec((mem_len, d), lambda i: (i // tiles, 2 * layer)),
                  pl.BlockSpec((mem_len, d), lambda i: (i // tiles, 2 * layer + 1)),
                  lay(wo)],
        out_specs=row(d),
        out_shape=jax.ShapeDtypeStruct((t, d), F32),
        compiler_params=_params("parallel"),
        name="mix_out_cross_attention",
    )(h, att, rw, woa, wob, g, wq, memkv, memkv, wo)


def _mlp_kernel(h_ref, g_ref, wu_ref, wd_ref, gf_ref, o_ref, *, last, ff_chunk):
    d = h_ref.shape[-1]
    h = h_ref[...]
    hn = _rms(h, g_ref[0], d).astype(BF16)
    acc = h
    for c in range(wu_ref.shape[-1] // ff_chunk):
        sl = slice(c * ff_chunk, (c + 1) * ff_chunk)
        u = jnp.maximum(_dot(hn, wu_ref[0, :, sl]), 0.0)
        acc = acc + _dot((u * u).astype(BF16), wd_ref[0, sl, :])
    if last:
        acc = _rms(acc, gf_ref[...], d)
    o_ref[...] = acc


def _mlp(h, g, wu, wd, gf, layer, last, tm):
    t, d = h.shape
    row = pl.BlockSpec((tm, d), lambda i: (i, 0))
    lay = functools.partial(_layer_slice, layer=layer)
    return pl.pallas_call(
        functools.partial(_mlp_kernel, last=last, ff_chunk=1024),
        grid=(t // tm,),
        in_specs=[row, lay(g), lay(wu), lay(wd), _full((1, d))],
        out_specs=row,
        out_shape=jax.ShapeDtypeStruct((t, d), F32),
        compiler_params=_params("parallel"),
        name="mlp",
    )(h, g, wu, wd, gf)


def _pad_cols(a, n):
    return jnp.pad(a, ((0, 0), (0, n - a.shape[1])))


def _swap_halves(a):
    half = a.shape[-1] // 2
    return jnp.concatenate([a[..., half:], a[..., :half]], axis=-1)


def _in_weights(w_in, mu):
    cq = w_in[:, :MLA_Q_RANK]
    ckv = w_in[:, MLA_Q_RANK:MLA_Q_RANK + MLA_KV_RANK]
    kr = w_in[:, MLA_Q_RANK + MLA_KV_RANK:MLA_COLS]
    mla = jnp.concatenate([_pad_cols(cq, 256), ckv, kr, _swap_halves(kr)], axis=1)
    w = jnp.concatenate([_pad_cols(mla, MLA_C_PAD), _pad_cols(w_in[:, MLA_COLS:], RW_C_PAD)], axis=1)
    return w.astype(BF16), _pad_cols(mu[None, :], RW_C_PAD)


def _head_pad(a, n_heads, per_head, lo, hi, at):
    rows = a.shape[0]
    a = a.reshape(rows, n_heads, per_head)[:, :, lo:hi]
    a = jnp.pad(a, ((0, 0), (0, 0), (at, HEAD_PAD - at - (hi - lo))))
    return a.reshape(rows, n_heads * HEAD_PAD)


def _mla_weights(w_uq, w_ukv):
    dq = MLA_NOPE_DIM + MLA_ROPE_DIM
    h = MLA_HEADS
    wq3 = w_uq.reshape(MLA_Q_RANK, h, dq)
    rope_sw = _swap_halves(wq3[:, :, MLA_NOPE_DIM:]).reshape(MLA_Q_RANK, h * MLA_ROPE_DIM)
    wqa = _head_pad(w_uq, h, dq, 0, dq, 0)
    wqb = _head_pad(rope_sw, h, MLA_ROPE_DIM, 0, MLA_ROPE_DIM, MLA_NOPE_DIM)
    pad_q = ((0, 256 - MLA_Q_RANK), (0, 0))
    wqa = jnp.pad(wqa, pad_q)
    wqb = jnp.pad(wqb, pad_q)
    dkv = MLA_NOPE_DIM + MLA_V_DIM
    wk = _head_pad(w_ukv, h, dkv, 0, MLA_NOPE_DIM, 0)
    wvt = jnp.transpose(w_ukv.reshape(MLA_KV_RANK, h, dkv)[:, :, MLA_NOPE_DIM:], (1, 2, 0))
    wvt = jnp.pad(wvt, ((0, 0), (0, ATTN_ONES_ROWS), (0, 0))).reshape(-1, MLA_KV_RANK)
    eye = jnp.eye(MLA_ROPE_DIM, dtype=F32)
    place = _head_pad(jnp.tile(eye, (1, h)), h, MLA_ROPE_DIM, 0, MLA_ROPE_DIM, MLA_NOPE_DIM)
    zeros = jnp.zeros_like(place)
    pad_rows = jnp.zeros((LANE - 2 * MLA_ROPE_DIM, h * HEAD_PAD), F32)
    pa = jnp.concatenate([place, zeros, pad_rows], axis=0)
    pb = jnp.concatenate([zeros, place, pad_rows], axis=0)
    wka = jnp.concatenate([wk, pa], axis=0)
    return tuple(a.astype(BF16) for a in (wqa, wqb, wka, pb, wvt))


def _lora_weights(w2, a2, g2, v2):
    rw = RWKV_WIDTH
    z = lambda r: jnp.zeros((r, rw), F32)
    rows = [
        jnp.concatenate([w2, z(DECAY_RANK), z(DECAY_RANK), z(DECAY_RANK)], axis=1),
        jnp.concatenate([z(ICL_RANK), a2, z(ICL_RANK), z(ICL_RANK)], axis=1),
        jnp.concatenate([z(GATE_RANK), z(GATE_RANK), g2, z(GATE_RANK)], axis=1),
        jnp.concatenate([z(VRES_RANK)] * 3 + [v2 if v2 is not None else z(VRES_RANK)], axis=1),
    ]
    w = jnp.concatenate(rows, axis=0)
    return jnp.pad(w, ((0, LORA_PAD - w.shape[0]), (0, 0))).astype(BF16)


def _block_diag(n, blk, value):
    i = jnp.arange(n)
    return jnp.where((i[:, None] // blk) == (i[None, :] // blk), value, 0.0)


def kernel(x, mem, positions, mix_norm, w_in_first, w_in_rest, shift_mu_first, shift_mu_rest, mla_q_norm, mla_w_uq, mla_kv_norm, mla_w_ukv, mla_out_norm, rwkv_w0, rwkv_w2, rwkv_a0, rwkv_a2, rwkv_g2, rwkv_v0, rwkv_v2, rwkv_k_k, rwkv_k_a, rwkv_r_k, rwkv_ln_w, rwkv_ln_b, w_out, ca_norm, mem_norm, ca_wq, ca_wkv, ca_wo, mlp_norm, mlp_w_up, mlp_w_down, final_norm):
    b, s, d = x.shape
    depth = mix_norm.shape[0]
    mem_len = mem.shape[1]
    t = b * s
    tm = min(512, s)
    tm_wide = min(1024, s)
    tq = min(256, s)
    wkv_rows = min(512, s)
    rw = RWKV_WIDTH

    inv = ROPE_THETA ** (-jnp.arange(0, MLA_ROPE_DIM, 2, dtype=F32) / MLA_ROPE_DIM)
    inv_row = jnp.concatenate([jnp.zeros((MLA_NOPE_DIM,), F32), inv, inv,
                               jnp.zeros((HEAD_PAD - MLA_NOPE_DIM - MLA_ROPE_DIM,), F32)])[None, :]
    ct, st = _rope_tables(positions.astype(F32).reshape(t, 1), inv_row, tm)

    memkv = _mem_kv(mem.reshape(b * mem_len, d), mem_norm[None, :], ca_wkv.astype(BF16),
                    min(512, b * mem_len))

    woa = w_out[:, :MLA_HEADS * MLA_V_DIM].astype(BF16)
    wob = w_out[:, MLA_HEADS * MLA_V_DIM:].astype(BF16)
    wq_b = ca_wq.astype(BF16)
    wo_b = ca_wo.astype(BF16)
    wu_b = mlp_w_up.astype(BF16)
    wd_b = mlp_w_down.astype(BF16)
    ca_g = ca_norm[:, None, :]
    mlp_g = mlp_norm[:, None, :]
    bd512 = _block_diag(rw, RWKV_HEAD, 1.0).astype(BF16)

    h = x.reshape(t, d)
    v_first = None
    for l in range(depth):
        if l == 0:
            w_in, mu, v0, v2 = w_in_first, shift_mu_first, None, None
        else:
            w_in, mu, v0, v2 = w_in_rest[l - 1], shift_mu_rest[l - 1], rwkv_v0[l - 1], rwkv_v2[l - 1]
        w_all, mu_p = _in_weights(w_in, mu)
        wqa, wqb, wka, pb, wvt = _mla_weights(mla_w_uq[l], mla_w_ukv[l])
        gq = _pad_cols(mla_q_norm[l][None, :], 256)
        w2c = _lora_weights(rwkv_w2[l], rwkv_a2[l], rwkv_g2[l], v2)
        vec = lambda a: a.reshape(1, rw)
        q, k, vt, r_, k_, v_, kk_, kka_, lw_, g_ = _layer_in(
            h, mix_norm[l][None, :], w_all, mu_p, ct, st, gq, mla_kv_norm[l][None, :],
            wqa, wqb, wka, pb, wvt, v_first, w2c, bd512, vec(rwkv_w0[l]), vec(rwkv_a0[l]),
            vec(rwkv_k_k[l]), vec(rwkv_k_a[l]), None if v0 is None else vec(v0), s, tm)
        if l == 0:
            v_first = v_

        att = _mla_attention(q, k, vt, mla_out_norm[l][None, :], s, tq, ATTN_HEADS_PER_STEP)

        sh = lambda a: a.reshape(b, s, rw)
        y = _wkv(sh(r_), sh(k_), sh(v_), sh(kk_), sh(kka_), sh(lw_), sh(g_),
                 vec(rwkv_r_k[l]), vec(rwkv_ln_w[l]), vec(rwkv_ln_b[l]), wkv_rows, WKV_MODES)

        h = _mix_out(h, att.reshape(t, -1), y.reshape(t, rw), woa, wob, ca_g, wq_b, memkv, wo_b,
                     l, s, mem_len, tm_wide)
        h = _mlp(h, mlp_g, wu_b, wd_b, final_norm[None, :], l, l == depth - 1, tm_wide)
    return h.reshape(b, s, d)
```

```python
import functools
import math

import jax
import jax.numpy as jnp
from jax import lax
from jax.experimental import pallas as pl
from jax.experimental.pallas import tpu as pltpu

F32 = jnp.float32
BF16 = jnp.bfloat16

CHUNK = 64
NORM_EPS = 1e-6
MLA_HEADS = 8
MLA_V_DIM = 64
MLA_NOPE_DIM = 64
MLA_ROPE_DIM = 32
MLA_Q_RANK = 192
MLA_KV_RANK = 128
ROPE_THETA = 10000.0
RWKV_HEAD = 64
RWKV_HEADS = 8
RWKV_WIDTH = 512
DECAY_RANK = 32
ICL_RANK = 32
GATE_RANK = 96
VRES_RANK = 32
GN_EPS = 64e-5
CA_HEADS = 4
MLA_COLS = MLA_Q_RANK + MLA_KV_RANK + MLA_ROPE_DIM

LANE = 128
HEAD_PAD = 128
MLA_C_PAD = 512
RW_C_PAD = 1792
LORA_PAD = 256
WKV_CHUNK = 64
LAYER_IN_PART_ROWS = 256
ATTN_HEADS_PER_STEP = 4
ATTN_ONES_ROWS = 16
LOG2E = 1.4426950408889634
NEG_BIG = -1e30
VMEM_LIMIT = 56 * 1024 * 1024

HI = lax.Precision.HIGHEST
WKV_MODES = ("bf16", "bf16", "bf16")


def _dot(a, b, precision=None):
    return jnp.dot(a, b, preferred_element_type=F32, precision=precision)


def _dot_nt(a, b, precision=None):
    return lax.dot_general(a, b, (((1,), (1,)), ((), ())),
                           preferred_element_type=F32, precision=precision)


def _dot_tn(a, b, precision=None):
    return lax.dot_general(a, b, (((0,), (0,)), ((), ())),
                           preferred_element_type=F32, precision=precision)


def _rms(x, g, n, eps=NORM_EPS):
    ms = jnp.sum(x * x, axis=-1, keepdims=True) * (1.0 / n)
    return x * lax.rsqrt(ms + eps) * g


def _sigmoid(x):
    return 0.5 + 0.5 * jnp.tanh(0.5 * x)


def _params(*sem):
    return pltpu.CompilerParams(dimension_semantics=sem, vmem_limit_bytes=VMEM_LIMIT)


def _full(shape):
    nd = len(shape)
    return pl.BlockSpec(shape, lambda *_: (0,) * nd, pipeline_mode=pl.Buffered(1))


def _layer_slice(a, layer):
    nd = a.ndim
    return pl.BlockSpec((1,) + a.shape[1:], lambda i: (layer,) + (0,) * (nd - 1),
                        pipeline_mode=pl.Buffered(1))


def _rope_kernel(pos_ref, inv_ref, ct_ref, st_ref):
    ang = pos_ref[...] * inv_ref[...]
    c = jnp.cos(ang)
    s = jnp.sin(ang)
    lane = lax.broadcasted_iota(jnp.int32, ang.shape, 1)
    half = MLA_NOPE_DIM + MLA_ROPE_DIM // 2
    end = MLA_NOPE_DIM + MLA_ROPE_DIM
    ct_ref[...] = jnp.where(lane < MLA_NOPE_DIM, 1.0, jnp.where(lane < end, c, 0.0))
    st_ref[...] = jnp.where(lane < MLA_NOPE_DIM, 0.0,
                            jnp.where(lane < half, -s, jnp.where(lane < end, s, 0.0)))


def _rope_tables(pos_f, inv_row, tm):
    t = pos_f.shape[0]
    return pl.pallas_call(
        _rope_kernel,
        grid=(t // tm,),
        in_specs=[pl.BlockSpec((tm, 1), lambda i: (i, 0)), _full((1, LANE))],
        out_specs=[pl.BlockSpec((tm, LANE), lambda i: (i, 0))] * 2,
        out_shape=[jax.ShapeDtypeStruct((t, LANE), F32)] * 2,
        compiler_params=_params("parallel"),
        name="rope_tables",
    )(pos_f, inv_row)


def _mla_latents(c, ct_ref, st_ref, gq_ref, gkv_ref, wqa_ref, wqb_ref, wka_ref, pb_ref,
                 wv_ref, ones_ref, q_ref, k_ref, v_ref, q_scale):
    nq = _rms(c[:, :256], gq_ref[...], MLA_Q_RANK).astype(BF16)
    nkv = _rms(c[:, 256:384], gkv_ref[...], MLA_KV_RANK).astype(BF16)
    kr = c[:, 384:512].astype(BF16)
    ct = jnp.tile(ct_ref[...], (1, MLA_HEADS))
    st = jnp.tile(st_ref[...], (1, MLA_HEADS))
    q = _dot(nq, wqa_ref[...]) * ct + _dot(nq, wqb_ref[...]) * st
    q_ref[...] = (q * q_scale).astype(BF16)
    x = jnp.concatenate([nkv, kr], axis=1)
    k = _dot(x, wka_ref[...]) * ct + _dot(kr, pb_ref[...]) * st
    k_ref[...] = k.astype(BF16)
    v_ref[...] = (_dot_nt(wv_ref[...], nkv) + ones_ref[...]).astype(BF16)


def _rwkv_elementwise(x, vf_ref, w2_ref, bd_ref, w0_ref, a0_ref, kk_ref_p, ka_ref, v0_ref,
                      r_o, k_o, v_o, kk_o, kka_o, lw_o, g_o):
    rw = RWKV_WIDTH
    r_o[...] = x[:, 0:rw].astype(BF16)
    k = x[:, rw:2 * rw]
    v = x[:, 2 * rw:3 * rw]
    tail = x[:, 3 * rw:3 * rw + LORA_PAD]
    lane = lax.broadcasted_iota(jnp.int32, tail.shape, 1)
    g_lo = DECAY_RANK + ICL_RANK
    g_hi = g_lo + GATE_RANK
    is_gate = (lane >= g_lo) & (lane < g_hi)
    th = jnp.tanh(jnp.where(is_gate, 0.5 * tail, tail))
    act = jnp.where(lane < DECAY_RANK, th, jnp.where(is_gate, 0.5 + 0.5 * th, tail))
    lo = _dot(act.astype(BF16), w2_ref[...])
    lw_o[...] = -math.exp(-0.5) * _sigmoid(w0_ref[...] + lo[:, 0:rw])
    a = _sigmoid(a0_ref[...] + lo[:, rw:2 * rw])
    g_o[...] = lo[:, 2 * rw:3 * rw].astype(BF16)
    if vf_ref is not None:
        mix = _sigmoid(v0_ref[...] + lo[:, 3 * rw:4 * rw])
        v = v + (vf_ref[...].astype(F32) - v) * mix
    v_o[...] = v.astype(BF16)
    kk = k * kk_ref_p[...]
    sq = kk * kk
    sq_hi = sq.astype(BF16)
    sq_lo = (sq - sq_hi.astype(F32)).astype(BF16)
    ss = _dot(sq_hi, bd_ref[...]) + _dot(sq_lo, bd_ref[...])
    kk = kk * lax.rsqrt(jnp.maximum(ss, 1e-24))
    kk_o[...] = kk.astype(BF16)
    kka_o[...] = (kk * a).astype(BF16)
    k_o[...] = (k * (1.0 + (a - 1.0) * ka_ref[...])).astype(BF16)


def _layer_in_kernel(*refs, first, tiles_per_seq, q_scale):
    n_in = 20 if first else 22
    ins, outs, carry_ref = refs[:n_in], refs[n_in:n_in + 10], refs[n_in + 10]
    h_ref, g_ref, w_ref, mu_ref = ins[:4]
    mla_ins = ins[4:14]
    if first:
        w2_ref, bd_ref, w0_ref, a0_ref, kkp_ref, ka_ref = ins[14:20]
        vf_ref = v0_ref = None
    else:
        vf_ref, w2_ref, bd_ref, w0_ref, a0_ref, kkp_ref, ka_ref, v0_ref = ins[14:22]
    q_ref, k_ref, vt_ref = outs[:3]
    rw_outs = outs[3:]

    i = pl.program_id(0)
    tm, d = h_ref.shape
    hr = min(LAYER_IN_PART_ROWS, tm)

    @pl.when(i % tiles_per_seq == 0)
    def _():
        carry_ref[...] = jnp.zeros_like(carry_ref)

    parts = [pl.ds(p * hr, hr) for p in range(tm // hr)]
    hn = [_rms(h_ref[rows, :], g_ref[...], d).astype(BF16) for rows in parts]
    proj = [_dot(x, w_ref[...]) for x in hn]
    last = carry_ref[7:8, :]
    for rows, pr in zip(parts, proj):
        ct_v, st_v = mla_ins[0].at[rows], mla_ins[1].at[rows]
        _mla_latents(pr[:, :MLA_C_PAD], ct_v, st_v, *mla_ins[2:], q_ref.at[rows], k_ref.at[rows],
                     vt_ref.at[:, rows], q_scale)
        z = pr[:, MLA_C_PAD:]
        prev = pltpu.roll(z, 1, axis=0)
        row = lax.broadcasted_iota(jnp.int32, z.shape, 0)
        prev = jnp.where(row == 0, last, prev)
        x = z + (prev - z) * mu_ref[...]
        last = z[hr - 1:hr, :]
        _rwkv_elementwise(x, None if vf_ref is None else vf_ref.at[rows], w2_ref, bd_ref, w0_ref,
                          a0_ref, kkp_ref, ka_ref, v0_ref, *[o.at[rows] for o in rw_outs])
    carry_ref[...] = z[hr - 8:, :]


def _layer_in(h, g, w, mu, ct, st, gq, gkv, wqa, wqb, wka, pb, wvt, vf, w2, bd, w0, a0, k_k,
              k_a, v0, seq, tm):
    t, d = h.shape
    rw = RWKV_WIDTH
    hw = MLA_HEADS * HEAD_PAD
    vw = wvt.shape[0]
    first = vf is None
    ones_col = (jnp.arange(vw) % (MLA_V_DIM + ATTN_ONES_ROWS) == MLA_V_DIM).astype(F32)[:, None]
    q_scale = (MLA_NOPE_DIM + MLA_ROPE_DIM) ** -0.5 * LOG2E
    row = lambda width: pl.BlockSpec((tm, width), lambda i: (i, 0))
    vec = _full((1, rw))
    args = [h, g, w, mu, ct, st, gq, gkv, wqa, wqb, wka, pb, wvt, ones_col]
    in_specs = [row(d), _full((1, d)), _full(w.shape), _full((1, RW_C_PAD)), row(LANE), row(LANE),
                _full(gq.shape), _full(gkv.shape), _full(wqa.shape), _full(wqb.shape),
                _full(wka.shape), _full(pb.shape), _full(wvt.shape), _full(ones_col.shape)]
    if first:
        args += [w2, bd, w0, a0, k_k, k_a]
        in_specs += [_full(w2.shape), _full(bd.shape), vec, vec, vec, vec]
    else:
        args += [vf, w2, bd, w0, a0, k_k, k_a, v0]
        in_specs += [row(rw), _full(w2.shape), _full(bd.shape), vec, vec, vec, vec, vec]
    rw_dt = [BF16, BF16, BF16, BF16, BF16, F32, BF16]
    kern = functools.partial(_layer_in_kernel, first=first, tiles_per_seq=seq // tm,
                             q_scale=q_scale)
    return pl.pallas_call(
        kern,
        grid=(t // tm,),
        in_specs=in_specs,
        out_specs=[row(hw), row(hw), pl.BlockSpec((vw, tm), lambda i: (0, i))] + [row(rw)] * 7,
        out_shape=[jax.ShapeDtypeStruct((t, hw), BF16), jax.ShapeDtypeStruct((t, hw), BF16),
                   jax.ShapeDtypeStruct((vw, t), BF16)]
                  + [jax.ShapeDtypeStruct((t, rw), dt) for dt in rw_dt],
        scratch_shapes=[pltpu.VMEM((8, RW_C_PAD), F32)],
        compiler_params=_params("arbitrary"),
        name="layer_in",
    )(*args)


def _attn_kernel(q_ref, k_ref, vt_ref, g_ref, o_ref, m_ref, a_ref, p_ref, acc_ref, s_ref, s2_ref,
                 *, tq, tk, nh):
    qi = pl.program_id(2)
    dv = MLA_V_DIM
    heads = range(nh)

    ext = dv + ATTN_ONES_ROWS

    m_ref[...] = jnp.full_like(m_ref, NEG_BIG)
    a_ref[...] = jnp.ones_like(a_ref)
    p_ref[...] = jnp.zeros_like(p_ref)
    acc_ref[...] = jnp.zeros_like(acc_ref)

    def flush_values(j, q0=0):
        off = pl.multiple_of(j * tk, tk)
        pv = [_dot(vt_ref[h * ext:(h + 1) * ext, pl.ds(off, tk)], p_ref[h, :, q0:])
              for h in heads]
        for h in heads:
            acc_ref[h, :, q0:] = a_ref[h, :, q0:] * acc_ref[h, :, q0:] + pv[h]

    def scores(j, h, dst, q0=0):
        off = pl.multiple_of(j * tk, tk)
        dst[h, :, q0:] = _dot_nt(k_ref[0, pl.ds(off, tk), h * HEAD_PAD:(h + 1) * HEAD_PAD],
                                 q_ref[0, q0:, h * HEAD_PAD:(h + 1) * HEAD_PAD])

    def softmax(h, src, key0=None, q0=0):
        s = src[h, :, q0:]
        if key0 is not None:
            ck = (lax.broadcasted_iota(jnp.int32, s.shape, 0) + key0) // CHUNK
            cq = (lax.broadcasted_iota(jnp.int32, s.shape, 1) + q0) // CHUNK
            s = jnp.where(ck <= cq, s, NEG_BIG)
        m_old = m_ref[h, :, q0:]
        m_new = jnp.maximum(m_old, jnp.max(s, axis=0, keepdims=True))
        p_ref[h, :, q0:] = jnp.exp2((s - m_new).astype(BF16))
        a_ref[h, :, q0:] = jnp.exp2(m_old - m_new)
        m_ref[h, :, q0:] = m_new

    def stage(j, src, dst, key0=None, q0=0, next_q0=0):
        flush_values(jnp.maximum(j - 1, 0))
        for h in heads:
            if dst is not None:
                scores(j + 1, h, dst, next_q0)
            softmax(h, src, key0, q0)

    for h in heads:
        scores(0, h, s_ref)

    @pl.loop(0, qi)
    def _(t):
        stage(2 * t, s_ref, s2_ref)
        stage(2 * t + 1, s2_ref, s_ref)

    stage(2 * qi, s_ref, s2_ref, key0=0, next_q0=tk)
    stage(2 * qi + 1, s2_ref, None, key0=tk, q0=tk)
    flush_values(2 * qi + 1, q0=tk)
    outs = []
    for h in heads:
        acc = acc_ref[h]
        o = acc[:dv] / acc[dv:dv + 1]
        ms = jnp.sum(o * o, axis=0, keepdims=True) * (1.0 / dv)
        outs.append(o * lax.rsqrt(ms + NORM_EPS))
    o_t = jnp.concatenate(outs, axis=0)
    o_ref[0] = (o_t.T * g_ref[...]).astype(BF16)


def _mla_attention(q, k, vt, g, seq, tk, nh):
    tq = 2 * tk
    b = q.shape[0] // seq
    hw = q.shape[1]
    groups = hw // (nh * HEAD_PAD)
    ow = nh * MLA_V_DIM
    q3 = q.reshape(b, seq, hw)
    k3 = k.reshape(b, seq, hw)
    return pl.pallas_call(
        functools.partial(_attn_kernel, tq=tq, tk=tk, nh=nh),
        grid=(b, groups, seq // tq),
        in_specs=[pl.BlockSpec((1, tq, nh * HEAD_PAD), lambda bi, h, i: (bi, i, h)),
                  pl.BlockSpec((1, seq, nh * HEAD_PAD), lambda bi, h, i: (bi, 0, h)),
                  pl.BlockSpec((nh * (MLA_V_DIM + ATTN_ONES_ROWS), seq), lambda bi, h, i: (h, bi)),
                  pl.BlockSpec((1, ow), lambda bi, h, i: (0, h))],
        out_specs=pl.BlockSpec((1, tq, ow), lambda bi, h, i: (bi, i, h)),
        out_shape=jax.ShapeDtypeStruct((b, seq, groups * ow), BF16),
        scratch_shapes=[pltpu.VMEM((nh, 1, tq), F32), pltpu.VMEM((nh, 1, tq), F32),
                        pltpu.VMEM((nh, tk, tq), BF16),
                        pltpu.VMEM((nh, MLA_V_DIM + ATTN_ONES_ROWS, tq), F32),
                        pltpu.VMEM((nh, tk, tq), F32), pltpu.VMEM((nh, tk, tq), F32)],
        compiler_params=_params("parallel", "parallel", "arbitrary"),
        name="mla_attention",
    )(q3, k3, vt, g)


_DIMS = {"nn": (((1,), (0,)), ((), ())), "nt": (((1,), (1,)), ((), ())),
         "tn": (((0,), (0,)), ((), ()))}


def _split(x):
    hi = x.astype(BF16)
    return hi, (x - hi.astype(F32)).astype(BF16)


def _mm(a, b, mode, dims="nn"):
    dg = lambda x, y: lax.dot_general(x, y, _DIMS[dims], preferred_element_type=F32)
    if mode == "bf16":
        return dg(a.astype(BF16), b.astype(BF16))
    if mode == "x3":
        a_hi, a_lo = _split(a)
        b_hi, b_lo = _split(b)
        return dg(a_hi, b_hi) + (dg(a_lo, b_hi) + dg(a_hi, b_lo))
    if mode == "ax":
        n = b.shape[1]
        out = dg(a.astype(BF16), jnp.concatenate(_split(b), axis=1))
        return out[:, :n] + out[:, n:]
    return lax.dot_general(a, b, _DIMS[dims], preferred_element_type=F32, precision=HI)


def _unit_lower_solve(a, rhs, row, col, eye, mode):
    return _unit_lower_solve_many([a], [rhs], row, col, eye, mode)[0]


def _unit_lower_solve_many(a_list, rhs_list, row, col, eye, mode):
    d16 = (row // 16) == (col // 16)
    d32 = (row // 32) == (col // 32)
    n = range(len(a_list))
    d = [jnp.where(d16, a, 0.0) for a in a_list]
    x = [eye + d[i] for i in n]
    d2 = [_mm(d[i], d[i], mode) for i in n]
    t = [_mm(d2[i], jnp.concatenate([x[i], d2[i]], axis=1), mode) for i in n]
    x = [x[i] + t[i][:, :128] for i in n]
    d4 = [t[i][:, 128:] for i in n]
    t = [_mm(d4[i], jnp.concatenate([x[i], d4[i]], axis=1), mode) for i in n]
    x = [x[i] + t[i][:, :128] for i in n]
    d8 = [t[i][:, 128:] for i in n]
    t = [_mm(d8[i], x[i], mode) for i in n]
    x = [x[i] + t[i] for i in n]
    e = [jnp.where(d32, a_list[i], 0.0) - d[i] for i in n]
    t = [_mm(e[i], x[i], mode) for i in n]
    t = [_mm(x[i], t[i], mode) for i in n]
    x = [x[i] + t[i] for i in n]
    f = [jnp.where(d32, 0.0, a_list[i]) for i in n]
    z = [_mm(x[i], rhs_list[i], mode) for i in n]
    t = [_mm(f[i], z[i], mode) for i in n]
    t = [_mm(x[i], t[i], mode) for i in n]
    return [z[i] + t[i] for i in n]


def _wkv_kernel(r_ref, k_ref, v_ref, kk_ref, kka_ref, lw_ref, g_ref,
                rk_ref, lnw_ref, lnb_ref, o_ref, s_ref, *, nch, modes):
    m_quad, m_inv, m_state = modes
    c_len = WKV_CHUNK

    npair = s_ref.shape[0]

    @pl.when(pl.program_id(1) == 0)
    def _():
        s_ref[...] = jnp.zeros_like(s_ref)

    row = lax.broadcasted_iota(jnp.int32, (128, 128), 0)
    col = lax.broadcasted_iota(jnp.int32, (128, 128), 1)
    same = (row // c_len) == (col // c_len)
    mask_s = same & (col < row)
    mask_i = same & (col <= row)
    eye = jnp.where(row == col, 1.0, 0.0)
    lane0 = lax.broadcasted_iota(jnp.int32, (c_len, 128), 1) < RWKV_HEAD
    tr = lax.broadcasted_iota(jnp.int32, (c_len, c_len), 0)
    tc = lax.broadcasted_iota(jnp.int32, (c_len, c_len), 1)
    ltri = jnp.where(tc <= tr, 1.0, 0.0)

    def stack(x):
        return jnp.concatenate([jnp.where(lane0, x, 0.0), jnp.where(lane0, 0.0, x)], axis=0)

    def head_sums(x):
        first = jnp.sum(jnp.where(lane0, x, 0.0), axis=-1, keepdims=True)
        both = jnp.sum(x, axis=-1, keepdims=True)
        return jnp.where(lane0, first, both - first)

    par = []
    for c, hp in [(c, hp) for c in range(nch) for hp in range(npair)]:
        sl = pl.ds(c * c_len, c_len)
        ln = slice(hp * LANE, (hp + 1) * LANE)
        r = r_ref[0, sl, ln].astype(F32)
        k = k_ref[0, sl, ln].astype(F32)
        v = v_ref[0, sl, ln].astype(F32)
        kk = kk_ref[0, sl, ln].astype(F32)
        kka = kka_ref[0, sl, ln].astype(F32)
        lw = lw_ref[0, sl, ln]
        cum = _mm(ltri, lw, "ax")
        tot = cum[c_len - 1:c_len, :]
        p_in = jnp.exp(cum)
        p_ex = jnp.exp(cum - lw)
        p_inv = jnp.exp(-cum)
        p_end = jnp.exp(tot - cum)
        at = stack(-kk * p_ex)
        rt = stack(r * p_in)
        bt = kka * p_inv
        kt = k * p_inv
        par.append(dict(
            r=r, k=k, v=v, tot=tot, at=at, rt=rt, vs=stack(v),
            bkh=jnp.concatenate([stack(kka * p_end), stack(k * p_end)], axis=0),
            lhs=jnp.concatenate([at, rt], axis=0),
            rhs=jnp.concatenate([bt, bt, kt, kt], axis=0)))
    for p in par:
        quad = _mm(p["lhs"], p["rhs"], m_quad, "nt")
        p["a_ab"] = jnp.where(mask_s, quad[:128, :128], 0.0)
        p["a_ak"] = jnp.where(mask_s, quad[:128, 128:], 0.0)
        p["a_r"] = jnp.concatenate([jnp.where(mask_i, quad[128:, :128], 0.0),
                                    jnp.where(mask_i, quad[128:, 128:], 0.0)], axis=1)
    for p in par:
        p["akv"] = _mm(p["a_ak"], p["vs"], m_quad)
    wus = _unit_lower_solve_many(
        [p["a_ab"] for p in par],
        [jnp.concatenate([p["at"], p["akv"]], axis=1) for p in par], row, col, eye, m_inv)

    pairs = range(npair)
    state = [s_ref[hp] for hp in pairs]
    def outputs(c, m1, uv):
        sl = pl.ds(c * c_len, c_len)
        pc = [par[c * npair + hp] for hp in pairs]
        ys = [m1[hp][128:] + _mm(pc[hp]["a_r"], uv[hp], m_state) for hp in pairs]
        y = [ys[hp][:c_len] + ys[hp][c_len:] for hp in pairs]
        mean = [head_sums(y[hp]) * (1.0 / RWKV_HEAD) for hp in pairs]
        yc = [y[hp] - mean[hp] for hp in pairs]
        var = [head_sums(yc[hp] * yc[hp]) * (1.0 / RWKV_HEAD) for hp in pairs]
        for hp in pairs:
            ln = slice(hp * LANE, (hp + 1) * LANE)
            r, k, v = pc[hp]["r"], pc[hp]["k"], pc[hp]["v"]
            yn = yc[hp] * lax.rsqrt(var[hp] + GN_EPS) * lnw_ref[:, ln] + lnb_ref[:, ln]
            bonus = head_sums(r * k * rk_ref[:, ln])
            yn = yn + bonus * v
            o_ref[0, sl, ln] = (yn * g_ref[0, sl, ln].astype(F32)).astype(BF16)

    pending = None
    for c in range(nch):
        pc = [par[c * npair + hp] for hp in pairs]
        wu = [wus[c * npair + hp] for hp in pairs]
        m1 = [_mm(jnp.concatenate([wu[hp][:, :128], pc[hp]["rt"]], axis=0), state[hp],
                  m_state, "nt") for hp in pairs]
        uv = [jnp.concatenate([m1[hp][:128] + wu[hp][:, 128:], pc[hp]["vs"]], axis=0)
              for hp in pairs]
        upd = [_mm(uv[hp], pc[hp]["bkh"], m_state, "tn") for hp in pairs]
        state = [state[hp] * jnp.exp(pc[hp]["tot"]) + upd[hp] for hp in pairs]
        if pending is not None:
            outputs(*pending)
        pending = (c, m1, uv)
    outputs(*pending)
    for hp in pairs:
        s_ref[hp] = state[hp]


def _wkv(r, k, v, kk, kka, lw, g, r_k, ln_w, ln_b, rows, modes):
    b, s, w = r.shape
    blk = pl.BlockSpec((1, rows, w), lambda bi, i: (bi, i, 0))
    kern = functools.partial(_wkv_kernel, nch=rows // WKV_CHUNK, modes=modes)
    return pl.pallas_call(
        kern,
        grid=(b, s // rows),
        in_specs=[blk] * 7 + [_full((1, w))] * 3,
        out_specs=blk,
        out_shape=jax.ShapeDtypeStruct((b, s, w), BF16),
        scratch_shapes=[pltpu.VMEM((w // LANE, LANE, LANE), F32)],
        compiler_params=_params("parallel", "arbitrary"),
        name="wkv7",
    )(r, k, v, kk, kka, lw, g, r_k, ln_w, ln_b)


def _mem_kv_kernel(m_ref, g_ref, w_ref, o_ref):
    d = m_ref.shape[-1]
    mn = _rms(m_ref[...], g_ref[...], d).astype(BF16)
    o_ref[...] = _dot(mn, w_ref[0]).astype(BF16)


def _mem_kv(mem2d, g, wkv, tm):
    t, d = mem2d.shape
    nl, _, n = wkv.shape
    return pl.pallas_call(
        _mem_kv_kernel,
        grid=(nl, t // tm),
        in_specs=[pl.BlockSpec((tm, d), lambda l, i: (i, 0)), _full((1, d)),
                  pl.BlockSpec((1, d, n), lambda l, i: (l, 0, 0))],
        out_specs=pl.BlockSpec((tm, n), lambda l, i: (i, l)),
        out_shape=jax.ShapeDtypeStruct((t, nl * n), BF16),
        compiler_params=_params("parallel", "parallel"),
        name="mem_kv",
    )(mem2d, g, wkv)


def _mix_out_kernel(h_ref, att_ref, rw_ref, woa_ref, wob_ref, g_ref, wq_ref,
                    mk_ref, mv_ref, wo_ref, o_ref, *, q_scale):
    d = h_ref.shape[-1]
    h1 = h_ref[...] + _dot(att_ref[...], woa_ref[0]) + _dot(rw_ref[...], wob_ref[0])
    hn = _rms(h1, g_ref[0], d).astype(BF16)
    q = (_dot(hn, wq_ref[0]) * q_scale).astype(BF16)
    dh = d // CA_HEADS
    outs = []
    for hd in range(CA_HEADS):
        sl = slice(hd * dh, (hd + 1) * dh)
        s = _dot_nt(q[:, sl], mk_ref[:, sl])
        m = jnp.max(s, axis=-1, keepdims=True)
        p = jnp.exp2(s - m)
        l = jnp.sum(p, axis=-1, keepdims=True)
        outs.append((_dot(p.astype(BF16), mv_ref[:, sl]) / l).astype(BF16))
    o = jnp.concatenate(outs, axis=1)
    o_ref[...] = h1 + _dot(o, wo_ref[0])


def _mix_out(h, att, rw, woa, wob, g, wq, memkv, wo, layer, seq, mem_len, tm):
    t, d = h.shape
    tiles = seq // tm
    q_scale = (d // CA_HEADS) ** -0.5 * LOG2E
    row = lambda w: pl.BlockSpec((tm, w), lambda i: (i, 0))
    lay = functools.partial(_layer_slice, layer=layer)
    return pl.pallas_call(
        functools.partial(_mix_out_kernel, q_scale=q_scale),
        grid=(t // tm,),
        in_specs=[row(d), row(att.shape[1]), row(rw.shape[1]), lay(woa), lay(wob), lay(g),
                  lay(wq),
                  pl.BlockSpec((mem_len, d), lambda i: (i // tiles, 2 * layer)),
                  pl.BlockSpec((mem_len, d), lambda i: (i // tiles, 2 * layer + 1)),
                  lay(wo)],
        out_specs=row(d),
        out_shape=jax.ShapeDtypeStruct((t, d), F32),
        compiler_params=_params("parallel"),
        name="mix_out_cross_attention",
    )(h, att, rw, woa, wob, g, wq, memkv, memkv, wo)


def _mlp_kernel(h_ref, g_ref, wu_ref, wd_ref, gf_ref, o_ref, *, last, ff_chunk):
    d = h_ref.shape[-1]
    h = h_ref[...]
    hn = _rms(h, g_ref[0], d).astype(BF16)
    acc = h
    for c in range(wu_ref.shape[-1] // ff_chunk):
        sl = slice(c * ff_chunk, (c + 1) * ff_chunk)
        u = jnp.maximum(_dot(hn, wu_ref[0, :, sl]), 0.0)
        acc = acc + _dot((u * u).astype(BF16), wd_ref[0, sl, :])
    if last:
        acc = _rms(acc, gf_ref[...], d)
    o_ref[...] = acc


def _mlp(h, g, wu, wd, gf, layer, last, tm):
    t, d = h.shape
    row = pl.BlockSpec((tm, d), lambda i: (i, 0))
    lay = functools.partial(_layer_slice, layer=layer)
    return pl.pallas_call(
        functools.partial(_mlp_kernel, last=last, ff_chunk=1024),
        grid=(t // tm,),
        in_specs=[row, lay(g), lay(wu), lay(wd), _full((1, d))],
        out_specs=row,
        out_shape=jax.ShapeDtypeStruct((t, d), F32),
        compiler_params=_params("parallel"),
        name="mlp",
    )(h, g, wu, wd, gf)


def _pad_cols(a, n):
    return jnp.pad(a, ((0, 0), (0, n - a.shape[1])))


def _swap_halves(a):
    half = a.shape[-1] // 2
    return jnp.concatenate([a[..., half:], a[..., :half]], axis=-1)


def _in_weights(w_in, mu):
    cq = w_in[:, :MLA_Q_RANK]
    ckv = w_in[:, MLA_Q_RANK:MLA_Q_RANK + MLA_KV_RANK]
    kr = w_in[:, MLA_Q_RANK + MLA_KV_RANK:MLA_COLS]
    mla = jnp.concatenate([_pad_cols(cq, 256), ckv, kr, _swap_halves(kr)], axis=1)
    w = jnp.concatenate([_pad_cols(mla, MLA_C_PAD), _pad_cols(w_in[:, MLA_COLS:], RW_C_PAD)], axis=1)
    return w.astype(BF16), _pad_cols(mu[None, :], RW_C_PAD)


def _head_pad(a, n_heads, per_head, lo, hi, at):
    rows = a.shape[0]
    a = a.reshape(rows, n_heads, per_head)[:, :, lo:hi]
    a = jnp.pad(a, ((0, 0), (0, 0), (at, HEAD_PAD - at - (hi - lo))))
    return a.reshape(rows, n_heads * HEAD_PAD)


def _mla_weights(w_uq, w_ukv):
    dq = MLA_NOPE_DIM + MLA_ROPE_DIM
    h = MLA_HEADS
    wq3 = w_uq.reshape(MLA_Q_RANK, h, dq)
    rope_sw = _swap_halves(wq3[:, :, MLA_NOPE_DIM:]).reshape(MLA_Q_RANK, h * MLA_ROPE_DIM)
    wqa = _head_pad(w_uq, h, dq, 0, dq, 0)
    wqb = _head_pad(rope_sw, h, MLA_ROPE_DIM, 0, MLA_ROPE_DIM, MLA_NOPE_DIM)
    pad_q = ((0, 256 - MLA_Q_RANK), (0, 0))
    wqa = jnp.pad(wqa, pad_q)
    wqb = jnp.pad(wqb, pad_q)
    dkv = MLA_NOPE_DIM + MLA_V_DIM
    wk = _head_pad(w_ukv, h, dkv, 0, MLA_NOPE_DIM, 0)
    wvt = jnp.transpose(w_ukv.reshape(MLA_KV_RANK, h, dkv)[:, :, MLA_NOPE_DIM:], (1, 2, 0))
    wvt = jnp.pad(wvt, ((0, 0), (0, ATTN_ONES_ROWS), (0, 0))).reshape(-1, MLA_KV_RANK)
    eye = jnp.eye(MLA_ROPE_DIM, dtype=F32)
    place = _head_pad(jnp.tile(eye, (1, h)), h, MLA_ROPE_DIM, 0, MLA_ROPE_DIM, MLA_NOPE_DIM)
    zeros = jnp.zeros_like(place)
    pad_rows = jnp.zeros((LANE - 2 * MLA_ROPE_DIM, h * HEAD_PAD), F32)
    pa = jnp.concatenate([place, zeros, pad_rows], axis=0)
    pb = jnp.concatenate([zeros, place, pad_rows], axis=0)
    wka = jnp.concatenate([wk, pa], axis=0)
    return tuple(a.astype(BF16) for a in (wqa, wqb, wka, pb, wvt))


def _lora_weights(w2, a2, g2, v2):
    rw = RWKV_WIDTH
    z = lambda r: jnp.zeros((r, rw), F32)
    rows = [
        jnp.concatenate([w2, z(DECAY_RANK), z(DECAY_RANK), z(DECAY_RANK)], axis=1),
        jnp.concatenate([z(ICL_RANK), a2, z(ICL_RANK), z(ICL_RANK)], axis=1),
        jnp.concatenate([z(GATE_RANK), z(GATE_RANK), g2, z(GATE_RANK)], axis=1),
        jnp.concatenate([z(VRES_RANK)] * 3 + [v2 if v2 is not None else z(VRES_RANK)], axis=1),
    ]
    w = jnp.concatenate(rows, axis=0)
    return jnp.pad(w, ((0, LORA_PAD - w.shape[0]), (0, 0))).astype(BF16)


def _block_diag(n, blk, value):
    i = jnp.arange(n)
    return jnp.where((i[:, None] // blk) == (i[None, :] // blk), value, 0.0)


def kernel(x, mem, positions, mix_norm, w_in_first, w_in_rest, shift_mu_first, shift_mu_rest, mla_q_norm, mla_w_uq, mla_kv_norm, mla_w_ukv, mla_out_norm, rwkv_w0, rwkv_w2, rwkv_a0, rwkv_a2, rwkv_g2, rwkv_v0, rwkv_v2, rwkv_k_k, rwkv_k_a, rwkv_r_k, rwkv_ln_w, rwkv_ln_b, w_out, ca_norm, mem_norm, ca_wq, ca_wkv, ca_wo, mlp_norm, mlp_w_up, mlp_w_down, final_norm):
    b, s, d = x.shape
    depth = mix_norm.shape[0]
    mem_len = mem.shape[1]
    t = b * s
    tm = min(512, s)
    tm_wide = min(1024, s)
    tq = min(256, s)
    wkv_rows = min(512, s)
    rw = RWKV_WIDTH

    inv = ROPE_THETA ** (-jnp.arange(0, MLA_ROPE_DIM, 2, dtype=F32) / MLA_ROPE_DIM)
    inv_row = jnp.concatenate([jnp.zeros((MLA_NOPE_DIM,), F32), inv, inv,
                               jnp.zeros((HEAD_PAD - MLA_NOPE_DIM - MLA_ROPE_DIM,), F32)])[None, :]
    ct, st = _rope_tables(positions.astype(F32).reshape(t, 1), inv_row, tm)

    memkv = _mem_kv(mem.reshape(b * mem_len, d), mem_norm[None, :], ca_wkv.astype(BF16),
                    min(512, b * mem_len))

    woa = w_out[:, :MLA_HEADS * MLA_V_DIM].astype(BF16)
    wob = w_out[:, MLA_HEADS * MLA_V_DIM:].astype(BF16)
    wq_b = ca_wq.astype(BF16)
    wo_b = ca_wo.astype(BF16)
    wu_b = mlp_w_up.astype(BF16)
    wd_b = mlp_w_down.astype(BF16)
    ca_g = ca_norm[:, None, :]
    mlp_g = mlp_norm[:, None, :]
    bd512 = _block_diag(rw, RWKV_HEAD, 1.0).astype(BF16)

    h = x.reshape(t, d)
    v_first = None
    for l in range(depth):
        if l == 0:
            w_in, mu, v0, v2 = w_in_first, shift_mu_first, None, None
        else:
            w_in, mu, v0, v2 = w_in_rest[l - 1], shift_mu_rest[l - 1], rwkv_v0[l - 1], rwkv_v2[l - 1]
        w_all, mu_p = _in_weights(w_in, mu)
        wqa, wqb, wka, pb, wvt = _mla_weights(mla_w_uq[l], mla_w_ukv[l])
        gq = _pad_cols(mla_q_norm[l][None, :], 256)
        w2c = _lora_weights(rwkv_w2[l], rwkv_a2[l], rwkv_g2[l], v2)
        vec = lambda a: a.reshape(1, rw)
        q, k, vt, r_, k_, v_, kk_, kka_, lw_, g_ = _layer_in(
            h, mix_norm[l][None, :], w_all, mu_p, ct, st, gq, mla_kv_norm[l][None, :],
            wqa, wqb, wka, pb, wvt, v_first, w2c, bd512, vec(rwkv_w0[l]), vec(rwkv_a0[l]),
            vec(rwkv_k_k[l]), vec(rwkv_k_a[l]), None if v0 is None else vec(v0), s, tm_wide)
        if l == 0:
            v_first = v_

        att = _mla_attention(q, k, vt, mla_out_norm[l][None, :], s, tq, ATTN_HEADS_PER_STEP)

        sh = lambda a: a.reshape(b, s, rw)
        y = _wkv(sh(r_), sh(k_), sh(v_), sh(kk_), sh(kka_), sh(lw_), sh(g_),
                 vec(rwkv_r_k[l]), vec(rwkv_ln_w[l]), vec(rwkv_ln_b[l]), wkv_rows, WKV_MODES)

        h = _mix_out(h, att.reshape(t, -1), y.reshape(t, rw), woa, wob, ca_g, wq_b, memkv, wo_b,
                     l, s, mem_len, tm_wide)
        h = _mlp(h, mlp_g, wu_b, wd_b, final_norm[None, :], l, l == depth - 1, tm_wide)
    return h.reshape(b, s, d)
```

```python
import functools
import math

import jax
import jax.numpy as jnp
from jax import lax
from jax.experimental import pallas as pl
from jax.experimental.pallas import tpu as pltpu

F32 = jnp.float32
BF16 = jnp.bfloat16

CHUNK = 64
NORM_EPS = 1e-6
MLA_HEADS = 8
MLA_V_DIM = 64
MLA_NOPE_DIM = 64
MLA_ROPE_DIM = 32
MLA_Q_RANK = 192
MLA_KV_RANK = 128
ROPE_THETA = 10000.0
RWKV_HEAD = 64
RWKV_HEADS = 8
RWKV_WIDTH = 512
DECAY_RANK = 32
ICL_RANK = 32
GATE_RANK = 96
VRES_RANK = 32
GN_EPS = 64e-5
CA_HEADS = 4
MLA_COLS = MLA_Q_RANK + MLA_KV_RANK + MLA_ROPE_DIM

LANE = 128
HEAD_PAD = 128
MLA_C_PAD = 512
RW_C_PAD = 1792
LORA_PAD = 256
WKV_CHUNK = 64
WKV_GROUP = 4
WKV_AHEAD_STAGES = 2
LAYER_IN_PART_ROWS = 256
ATTN_HEADS_PER_STEP = 4
ATTN_ONES_ROWS = 16
LOG2E = 1.4426950408889634
NEG_BIG = -1e30
VMEM_LIMIT = 56 * 1024 * 1024

HI = lax.Precision.HIGHEST
WKV_MODES = ("bf16", "bf16", "bf16")


def _dot(a, b, precision=None):
    return jnp.dot(a, b, preferred_element_type=F32, precision=precision)


def _dot_nt(a, b, precision=None):
    return lax.dot_general(a, b, (((1,), (1,)), ((), ())),
                           preferred_element_type=F32, precision=precision)


def _dot_tn(a, b, precision=None):
    return lax.dot_general(a, b, (((0,), (0,)), ((), ())),
                           preferred_element_type=F32, precision=precision)


def _rms(x, g, n, eps=NORM_EPS):
    ms = jnp.sum(x * x, axis=-1, keepdims=True) * (1.0 / n)
    return x * lax.rsqrt(ms + eps) * g


def _sigmoid(x):
    return 0.5 + 0.5 * jnp.tanh(0.5 * x)


def _params(*sem):
    return pltpu.CompilerParams(dimension_semantics=sem, vmem_limit_bytes=VMEM_LIMIT)


def _full(shape):
    nd = len(shape)
    return pl.BlockSpec(shape, lambda *_: (0,) * nd, pipeline_mode=pl.Buffered(1))


def _layer_slice(a, layer):
    nd = a.ndim
    return pl.BlockSpec((1,) + a.shape[1:], lambda i: (layer,) + (0,) * (nd - 1),
                        pipeline_mode=pl.Buffered(1))


def _rope_kernel(pos_ref, inv_ref, ct_ref, st_ref):
    ang = pos_ref[...] * inv_ref[...]
    c = jnp.cos(ang)
    s = jnp.sin(ang)
    lane = lax.broadcasted_iota(jnp.int32, ang.shape, 1)
    half = MLA_NOPE_DIM + MLA_ROPE_DIM // 2
    end = MLA_NOPE_DIM + MLA_ROPE_DIM
    ct_ref[...] = jnp.where(lane < MLA_NOPE_DIM, 1.0, jnp.where(lane < end, c, 0.0))
    st_ref[...] = jnp.where(lane < MLA_NOPE_DIM, 0.0,
                            jnp.where(lane < half, -s, jnp.where(lane < end, s, 0.0)))


def _rope_tables(pos_f, inv_row, tm):
    t = pos_f.shape[0]
    return pl.pallas_call(
        _rope_kernel,
        grid=(t // tm,),
        in_specs=[pl.BlockSpec((tm, 1), lambda i: (i, 0)), _full((1, LANE))],
        out_specs=[pl.BlockSpec((tm, LANE), lambda i: (i, 0))] * 2,
        out_shape=[jax.ShapeDtypeStruct((t, LANE), F32)] * 2,
        compiler_params=_params("parallel"),
        name="rope_tables",
    )(pos_f, inv_row)


def _mla_latents(c, ct_ref, st_ref, gq_ref, gkv_ref, wqa_ref, wqb_ref, wka_ref, pb_ref,
                 wv_ref, ones_ref, q_ref, k_ref, v_ref, q_scale):
    nq = _rms(c[:, :256], gq_ref[...], MLA_Q_RANK).astype(BF16)
    nkv = _rms(c[:, 256:384], gkv_ref[...], MLA_KV_RANK).astype(BF16)
    kr = c[:, 384:512].astype(BF16)
    ct = jnp.tile(ct_ref[...], (1, MLA_HEADS))
    st = jnp.tile(st_ref[...], (1, MLA_HEADS))
    q = _dot(nq, wqa_ref[...]) * ct + _dot(nq, wqb_ref[...]) * st
    q_ref[...] = (q * q_scale).astype(BF16)
    x = jnp.concatenate([nkv, kr], axis=1)
    k = _dot(x, wka_ref[...]) * ct + _dot(kr, pb_ref[...]) * st
    k_ref[...] = k.astype(BF16)
    v_ref[...] = (_dot_nt(wv_ref[...], nkv) + ones_ref[...]).astype(BF16)


def _rwkv_elementwise(x, vf_ref, w2_ref, bd_ref, w0_ref, a0_ref, kk_ref_p, ka_ref, v0_ref,
                      r_o, k_o, v_o, kk_o, kka_o, lw_o, g_o):
    rw = RWKV_WIDTH
    r_o[...] = x[:, 0:rw].astype(BF16)
    k = x[:, rw:2 * rw]
    v = x[:, 2 * rw:3 * rw]
    tail = x[:, 3 * rw:3 * rw + LORA_PAD]
    lane = lax.broadcasted_iota(jnp.int32, tail.shape, 1)
    g_lo = DECAY_RANK + ICL_RANK
    g_hi = g_lo + GATE_RANK
    is_gate = (lane >= g_lo) & (lane < g_hi)
    th = jnp.tanh(jnp.where(is_gate, 0.5 * tail, tail))
    act = jnp.where(lane < DECAY_RANK, th, jnp.where(is_gate, 0.5 + 0.5 * th, tail))
    lo = _dot(act.astype(BF16), w2_ref[...])
    lw_o[...] = -math.exp(-0.5) * _sigmoid(w0_ref[...] + lo[:, 0:rw])
    a = _sigmoid(a0_ref[...] + lo[:, rw:2 * rw])
    g_o[...] = lo[:, 2 * rw:3 * rw].astype(BF16)
    if vf_ref is not None:
        mix = _sigmoid(v0_ref[...] + lo[:, 3 * rw:4 * rw])
        v = v + (vf_ref[...].astype(F32) - v) * mix
    v_o[...] = v.astype(BF16)
    kk = k * kk_ref_p[...]
    sq = kk * kk
    sq_hi = sq.astype(BF16)
    sq_lo = (sq - sq_hi.astype(F32)).astype(BF16)
    ss = _dot(sq_hi, bd_ref[...]) + _dot(sq_lo, bd_ref[...])
    kk = kk * lax.rsqrt(jnp.maximum(ss, 1e-24))
    kk_o[...] = kk.astype(BF16)
    kka_o[...] = (kk * a).astype(BF16)
    k_o[...] = (k * (1.0 + (a - 1.0) * ka_ref[...])).astype(BF16)


def _layer_in_kernel(*refs, first, tiles_per_seq, q_scale):
    n_in = 20 if first else 22
    ins, outs, carry_ref = refs[:n_in], refs[n_in:n_in + 10], refs[n_in + 10]
    h_ref, g_ref, w_ref, mu_ref = ins[:4]
    mla_ins = ins[4:14]
    if first:
        w2_ref, bd_ref, w0_ref, a0_ref, kkp_ref, ka_ref = ins[14:20]
        vf_ref = v0_ref = None
    else:
        vf_ref, w2_ref, bd_ref, w0_ref, a0_ref, kkp_ref, ka_ref, v0_ref = ins[14:22]
    q_ref, k_ref, vt_ref = outs[:3]
    rw_outs = outs[3:]

    i = pl.program_id(0)
    tm, d = h_ref.shape
    hr = min(LAYER_IN_PART_ROWS, tm)

    @pl.when(i % tiles_per_seq == 0)
    def _():
        carry_ref[...] = jnp.zeros_like(carry_ref)

    parts = [pl.ds(p * hr, hr) for p in range(tm // hr)]
    hn = [_rms(h_ref[rows, :], g_ref[...], d).astype(BF16) for rows in parts]
    proj = [_dot(x, w_ref[...]) for x in hn]
    last = carry_ref[7:8, :]
    for rows, pr in zip(parts, proj):
        ct_v, st_v = mla_ins[0].at[rows], mla_ins[1].at[rows]
        _mla_latents(pr[:, :MLA_C_PAD], ct_v, st_v, *mla_ins[2:], q_ref.at[rows], k_ref.at[rows],
                     vt_ref.at[:, rows], q_scale)
        z = pr[:, MLA_C_PAD:]
        prev = pltpu.roll(z, 1, axis=0)
        row = lax.broadcasted_iota(jnp.int32, z.shape, 0)
        prev = jnp.where(row == 0, last, prev)
        x = z + (prev - z) * mu_ref[...]
        last = z[hr - 1:hr, :]
        _rwkv_elementwise(x, None if vf_ref is None else vf_ref.at[rows], w2_ref, bd_ref, w0_ref,
                          a0_ref, kkp_ref, ka_ref, v0_ref, *[o.at[rows] for o in rw_outs])
    carry_ref[...] = z[hr - 8:, :]


def _layer_in(h, g, w, mu, ct, st, gq, gkv, wqa, wqb, wka, pb, wvt, vf, w2, bd, w0, a0, k_k,
              k_a, v0, seq, tm):
    t, d = h.shape
    rw = RWKV_WIDTH
    hw = MLA_HEADS * HEAD_PAD
    vw = wvt.shape[0]
    first = vf is None
    ones_col = (jnp.arange(vw) % (MLA_V_DIM + ATTN_ONES_ROWS) == MLA_V_DIM).astype(F32)[:, None]
    q_scale = (MLA_NOPE_DIM + MLA_ROPE_DIM) ** -0.5 * LOG2E
    row = lambda width: pl.BlockSpec((tm, width), lambda i: (i, 0))
    vec = _full((1, rw))
    args = [h, g, w, mu, ct, st, gq, gkv, wqa, wqb, wka, pb, wvt, ones_col]
    in_specs = [row(d), _full((1, d)), _full(w.shape), _full((1, RW_C_PAD)), row(LANE), row(LANE),
                _full(gq.shape), _full(gkv.shape), _full(wqa.shape), _full(wqb.shape),
                _full(wka.shape), _full(pb.shape), _full(wvt.shape), _full(ones_col.shape)]
    if first:
        args += [w2, bd, w0, a0, k_k, k_a]
        in_specs += [_full(w2.shape), _full(bd.shape), vec, vec, vec, vec]
    else:
        args += [vf, w2, bd, w0, a0, k_k, k_a, v0]
        in_specs += [row(rw), _full(w2.shape), _full(bd.shape), vec, vec, vec, vec, vec]
    rw_dt = [BF16, BF16, BF16, BF16, BF16, F32, BF16]
    kern = functools.partial(_layer_in_kernel, first=first, tiles_per_seq=seq // tm,
                             q_scale=q_scale)
    return pl.pallas_call(
        kern,
        grid=(t // tm,),
        in_specs=in_specs,
        out_specs=[row(hw), row(hw), pl.BlockSpec((vw, tm), lambda i: (0, i))] + [row(rw)] * 7,
        out_shape=[jax.ShapeDtypeStruct((t, hw), BF16), jax.ShapeDtypeStruct((t, hw), BF16),
                   jax.ShapeDtypeStruct((vw, t), BF16)]
                  + [jax.ShapeDtypeStruct((t, rw), dt) for dt in rw_dt],
        scratch_shapes=[pltpu.VMEM((8, RW_C_PAD), F32)],
        compiler_params=_params("arbitrary"),
        name="layer_in",
    )(*args)


def _attn_kernel(q_ref, k_ref, vt_ref, g_ref, o_ref, m_ref, a_ref, p_ref, acc_ref, s_ref, s2_ref,
                 *, tq, tk, nh):
    qi = pl.program_id(2)
    dv = MLA_V_DIM
    heads = range(nh)

    ext = dv + ATTN_ONES_ROWS

    m_ref[...] = jnp.full_like(m_ref, NEG_BIG)
    a_ref[...] = jnp.ones_like(a_ref)
    p_ref[...] = jnp.zeros_like(p_ref)
    acc_ref[...] = jnp.zeros_like(acc_ref)

    def flush_values(j, q0=0):
        off = pl.multiple_of(j * tk, tk)
        pv = [_dot(vt_ref[h * ext:(h + 1) * ext, pl.ds(off, tk)], p_ref[h, :, q0:])
              for h in heads]
        for h in heads:
            acc_ref[h, :, q0:] = a_ref[h, :, q0:] * acc_ref[h, :, q0:] + pv[h]

    def scores(j, h, dst, q0=0):
        off = pl.multiple_of(j * tk, tk)
        dst[h, :, q0:] = _dot_nt(k_ref[0, pl.ds(off, tk), h * HEAD_PAD:(h + 1) * HEAD_PAD],
                                 q_ref[0, q0:, h * HEAD_PAD:(h + 1) * HEAD_PAD])

    def softmax(h, src, key0=None, q0=0):
        s = src[h, :, q0:]
        if key0 is not None:
            ck = (lax.broadcasted_iota(jnp.int32, s.shape, 0) + key0) // CHUNK
            cq = (lax.broadcasted_iota(jnp.int32, s.shape, 1) + q0) // CHUNK
            s = jnp.where(ck <= cq, s, NEG_BIG)
        m_old = m_ref[h, :, q0:]
        m_new = jnp.maximum(m_old, jnp.max(s, axis=0, keepdims=True))
        p_ref[h, :, q0:] = jnp.exp2((s - m_new).astype(BF16))
        a_ref[h, :, q0:] = jnp.exp2(m_old - m_new)
        m_ref[h, :, q0:] = m_new

    def stage(j, src, dst, key0=None, q0=0, next_q0=0):
        flush_values(jnp.maximum(j - 1, 0))
        for h in heads:
            if dst is not None:
                scores(j + 1, h, dst, next_q0)
            softmax(h, src, key0, q0)

    for h in heads:
        scores(0, h, s_ref)

    @pl.loop(0, qi)
    def _(t):
        stage(2 * t, s_ref, s2_ref)
        stage(2 * t + 1, s2_ref, s_ref)

    stage(2 * qi, s_ref, s2_ref, key0=0, next_q0=tk)
    stage(2 * qi + 1, s2_ref, None, key0=tk, q0=tk)
    flush_values(2 * qi + 1, q0=tk)
    outs = []
    for h in heads:
        acc = acc_ref[h]
        o = acc[:dv] / acc[dv:dv + 1]
        ms = jnp.sum(o * o, axis=0, keepdims=True) * (1.0 / dv)
        outs.append(o * lax.rsqrt(ms + NORM_EPS))
    o_t = jnp.concatenate(outs, axis=0)
    o_ref[0] = (o_t.T * g_ref[...]).astype(BF16)


def _mla_attention(q, k, vt, g, seq, tk, nh):
    tq = 2 * tk
    b = q.shape[0] // seq
    hw = q.shape[1]
    groups = hw // (nh * HEAD_PAD)
    ow = nh * MLA_V_DIM
    q3 = q.reshape(b, seq, hw)
    k3 = k.reshape(b, seq, hw)
    return pl.pallas_call(
        functools.partial(_attn_kernel, tq=tq, tk=tk, nh=nh),
        grid=(b, groups, seq // tq),
        in_specs=[pl.BlockSpec((1, tq, nh * HEAD_PAD), lambda bi, h, i: (bi, i, h)),
                  pl.BlockSpec((1, seq, nh * HEAD_PAD), lambda bi, h, i: (bi, 0, h)),
                  pl.BlockSpec((nh * (MLA_V_DIM + ATTN_ONES_ROWS), seq), lambda bi, h, i: (h, bi)),
                  pl.BlockSpec((1, ow), lambda bi, h, i: (0, h))],
        out_specs=pl.BlockSpec((1, tq, ow), lambda bi, h, i: (bi, i, h)),
        out_shape=jax.ShapeDtypeStruct((b, seq, groups * ow), BF16),
        scratch_shapes=[pltpu.VMEM((nh, 1, tq), F32), pltpu.VMEM((nh, 1, tq), F32),
                        pltpu.VMEM((nh, tk, tq), BF16),
                        pltpu.VMEM((nh, MLA_V_DIM + ATTN_ONES_ROWS, tq), F32),
                        pltpu.VMEM((nh, tk, tq), F32), pltpu.VMEM((nh, tk, tq), F32)],
        compiler_params=_params("parallel", "parallel", "arbitrary"),
        name="mla_attention",
    )(q3, k3, vt, g)


_DIMS = {"nn": (((1,), (0,)), ((), ())), "nt": (((1,), (1,)), ((), ())),
         "tn": (((0,), (0,)), ((), ()))}


def _split(x):
    hi = x.astype(BF16)
    return hi, (x - hi.astype(F32)).astype(BF16)


def _mm(a, b, mode, dims="nn"):
    dg = lambda x, y: lax.dot_general(x, y, _DIMS[dims], preferred_element_type=F32)
    if mode == "bf16":
        return dg(a.astype(BF16), b.astype(BF16))
    if mode == "x3":
        a_hi, a_lo = _split(a)
        b_hi, b_lo = _split(b)
        return dg(a_hi, b_hi) + (dg(a_lo, b_hi) + dg(a_hi, b_lo))
    if mode == "ax":
        n = b.shape[1]
        out = dg(a.astype(BF16), jnp.concatenate(_split(b), axis=1))
        return out[:, :n] + out[:, n:]
    return lax.dot_general(a, b, _DIMS[dims], preferred_element_type=F32, precision=HI)


def _unit_lower_solve(a, rhs, row, col, eye, mode):
    out = []
    for _ in _unit_lower_solve_stages([a], [rhs], row, col, eye, mode, out):
        pass
    return out[0]


def _unit_lower_solve_stages(a_list, rhs_list, row, col, eye, mode, out):
    d16 = (row // 16) == (col // 16)
    d32 = (row // 32) == (col // 32)
    n = range(len(a_list))
    d = [jnp.where(d16, a, 0.0) for a in a_list]
    x = [eye + d[i] for i in n]
    d2 = [_mm(d[i], d[i], mode) for i in n]
    yield
    t = [_mm(d2[i], jnp.concatenate([x[i], d2[i]], axis=1), mode) for i in n]
    x = [x[i] + t[i][:, :128] for i in n]
    d4 = [t[i][:, 128:] for i in n]
    yield
    t = [_mm(d4[i], jnp.concatenate([x[i], d4[i]], axis=1), mode) for i in n]
    x = [x[i] + t[i][:, :128] for i in n]
    d8 = [t[i][:, 128:] for i in n]
    yield
    t = [_mm(d8[i], x[i], mode) for i in n]
    x = [x[i] + t[i] for i in n]
    yield
    e = [jnp.where(d32, a_list[i], 0.0) - d[i] for i in n]
    t = [_mm(e[i], x[i], mode) for i in n]
    yield
    t = [_mm(x[i], t[i], mode) for i in n]
    x = [x[i] + t[i] for i in n]
    yield
    f = [jnp.where(d32, 0.0, a_list[i]) for i in n]
    z = [_mm(x[i], rhs_list[i], mode) for i in n]
    yield
    t = [_mm(f[i], z[i], mode) for i in n]
    yield
    t = [_mm(x[i], t[i], mode) for i in n]
    out.extend(z[i] + t[i] for i in n)
    yield


def _wkv_kernel(r_ref, k_ref, v_ref, kk_ref, kka_ref, lw_ref, g_ref,
                rk_ref, lnw_ref, lnb_ref, o_ref, s_ref, *, nch, modes):
    m_quad, m_inv, m_state = modes
    c_len = WKV_CHUNK

    npair = s_ref.shape[0]

    @pl.when(pl.program_id(1) == 0)
    def _():
        s_ref[...] = jnp.zeros_like(s_ref)

    row = lax.broadcasted_iota(jnp.int32, (128, 128), 0)
    col = lax.broadcasted_iota(jnp.int32, (128, 128), 1)
    same = (row // c_len) == (col // c_len)
    mask_s = same & (col < row)
    mask_i = same & (col <= row)
    eye = jnp.where(row == col, 1.0, 0.0)
    lane0 = lax.broadcasted_iota(jnp.int32, (c_len, 128), 1) < RWKV_HEAD
    tr = lax.broadcasted_iota(jnp.int32, (c_len, c_len), 0)
    tc = lax.broadcasted_iota(jnp.int32, (c_len, c_len), 1)
    ltri = jnp.where(tc <= tr, 1.0, 0.0)

    def stack(x):
        return jnp.concatenate([jnp.where(lane0, x, 0.0), jnp.where(lane0, 0.0, x)], axis=0)

    def head_sums(x):
        first = jnp.sum(jnp.where(lane0, x, 0.0), axis=-1, keepdims=True)
        both = jnp.sum(x, axis=-1, keepdims=True)
        return jnp.where(lane0, first, both - first)

    pairs = range(npair)
    par = {}
    wus = {}

    def independent_part(chunks):
        keys = [(c, hp) for c in chunks for hp in pairs]
        for c, hp in keys:
            sl = pl.ds(c * c_len, c_len)
            ln = slice(hp * LANE, (hp + 1) * LANE)
            r = r_ref[0, sl, ln].astype(F32)
            k = k_ref[0, sl, ln].astype(F32)
            v = v_ref[0, sl, ln].astype(F32)
            kk = kk_ref[0, sl, ln].astype(F32)
            kka = kka_ref[0, sl, ln].astype(F32)
            lw = lw_ref[0, sl, ln]
            cum = _mm(ltri, lw, "ax")
            tot = cum[c_len - 1:c_len, :]
            p_in = jnp.exp(cum)
            p_ex = jnp.exp(cum - lw)
            p_inv = jnp.exp(-cum)
            p_end = jnp.exp(tot - cum)
            at = stack(-kk * p_ex)
            rt = stack(r * p_in)
            bt = kka * p_inv
            kt = k * p_inv
            par[c, hp] = dict(
                r=r, k=k, v=v, tot=tot, at=at, rt=rt, vs=stack(v),
                bkh=jnp.concatenate([stack(kka * p_end), stack(k * p_end)], axis=0),
                lhs=jnp.concatenate([at, rt], axis=0),
                rhs=jnp.concatenate([bt, bt, kt, kt], axis=0))
        yield
        for key in keys:
            p = par[key]
            quad = _mm(p["lhs"], p["rhs"], m_quad, "nt")
            p["a_ab"] = jnp.where(mask_s, quad[:128, :128], 0.0)
            p["a_ak"] = jnp.where(mask_s, quad[:128, 128:], 0.0)
            p["a_r"] = jnp.concatenate([jnp.where(mask_i, quad[128:, :128], 0.0),
                                        jnp.where(mask_i, quad[128:, 128:], 0.0)], axis=1)
        yield
        for key in keys:
            par[key]["akv"] = _mm(par[key]["a_ak"], par[key]["vs"], m_quad)
        yield
        solved = []
        yield from _unit_lower_solve_stages(
            [par[key]["a_ab"] for key in keys],
            [jnp.concatenate([par[key]["at"], par[key]["akv"]], axis=1) for key in keys],
            row, col, eye, m_inv, solved)
        wus.update(zip(keys, solved))

    state = [s_ref[hp] for hp in pairs]

    def outputs(c, m1, uv):
        sl = pl.ds(c * c_len, c_len)
        pc = [par[c, hp] for hp in pairs]
        ys = [m1[hp][128:] + _mm(pc[hp]["a_r"], uv[hp], m_state) for hp in pairs]
        y = [ys[hp][:c_len] + ys[hp][c_len:] for hp in pairs]
        mean = [head_sums(y[hp]) * (1.0 / RWKV_HEAD) for hp in pairs]
        yc = [y[hp] - mean[hp] for hp in pairs]
        var = [head_sums(yc[hp] * yc[hp]) * (1.0 / RWKV_HEAD) for hp in pairs]
        for hp in pairs:
            ln = slice(hp * LANE, (hp + 1) * LANE)
            r, k, v = pc[hp]["r"], pc[hp]["k"], pc[hp]["v"]
            yn = yc[hp] * lax.rsqrt(var[hp] + GN_EPS) * lnw_ref[:, ln] + lnb_ref[:, ln]
            bonus = head_sums(r * k * rk_ref[:, ln])
            yn = yn + bonus * v
            o_ref[0, sl, ln] = (yn * g_ref[0, sl, ln].astype(F32)).astype(BF16)

    pending = []

    def state_part(chunks):
        for c in chunks:
            pc = [par[c, hp] for hp in pairs]
            wu = [wus[c, hp] for hp in pairs]
            m1 = [_mm(jnp.concatenate([wu[hp][:, :128], pc[hp]["rt"]], axis=0), state[hp],
                      m_state, "nt") for hp in pairs]
            uv = [jnp.concatenate([m1[hp][:128] + wu[hp][:, 128:], pc[hp]["vs"]], axis=0)
                  for hp in pairs]
            upd = [_mm(uv[hp], pc[hp]["bkh"], m_state, "tn") for hp in pairs]
            for hp in pairs:
                state[hp] = state[hp] * jnp.exp(pc[hp]["tot"]) + upd[hp]
            yield
            if pending:
                outputs(*pending.pop())
                yield
            pending.append((c, m1, uv))

    groups = [list(range(g, min(g + WKV_GROUP, nch))) for g in range(0, nch, WKV_GROUP)]
    for _ in independent_part(groups[0]):
        pass
    for g, chunks in enumerate(groups):
        chain = state_part(chunks)
        ahead = independent_part(groups[g + 1]) if g + 1 < len(groups) else iter(())
        for _ in chain:
            for _ in range(WKV_AHEAD_STAGES):
                next(ahead, None)
        for _ in ahead:
            pass
    outputs(*pending.pop())
    for hp in pairs:
        s_ref[hp] = state[hp]


def _wkv(r, k, v, kk, kka, lw, g, r_k, ln_w, ln_b, rows, modes):
    b, s, w = r.shape
    blk = pl.BlockSpec((1, rows, w), lambda bi, i: (bi, i, 0))
    kern = functools.partial(_wkv_kernel, nch=rows // WKV_CHUNK, modes=modes)
    return pl.pallas_call(
        kern,
        grid=(b, s // rows),
        in_specs=[blk] * 7 + [_full((1, w))] * 3,
        out_specs=blk,
        out_shape=jax.ShapeDtypeStruct((b, s, w), BF16),
        scratch_shapes=[pltpu.VMEM((w // LANE, LANE, LANE), F32)],
        compiler_params=_params("parallel", "arbitrary"),
        name="wkv7",
    )(r, k, v, kk, kka, lw, g, r_k, ln_w, ln_b)


def _mem_kv_kernel(m_ref, g_ref, w_ref, o_ref):
    d = m_ref.shape[-1]
    mn = _rms(m_ref[...], g_ref[...], d).astype(BF16)
    o_ref[...] = _dot(mn, w_ref[0]).astype(BF16)


def _mem_kv(mem2d, g, wkv, tm):
    t, d = mem2d.shape
    nl, _, n = wkv.shape
    return pl.pallas_call(
        _mem_kv_kernel,
        grid=(nl, t // tm),
        in_specs=[pl.BlockSpec((tm, d), lambda l, i: (i, 0)), _full((1, d)),
                  pl.BlockSpec((1, d, n), lambda l, i: (l, 0, 0))],
        out_specs=pl.BlockSpec((tm, n), lambda l, i: (i, l)),
        out_shape=jax.ShapeDtypeStruct((t, nl * n), BF16),
        compiler_params=_params("parallel", "parallel"),
        name="mem_kv",
    )(mem2d, g, wkv)


def _mix_out_kernel(h_ref, att_ref, rw_ref, woa_ref, wob_ref, g_ref, wq_ref,
                    mk_ref, mv_ref, wo_ref, o_ref, *, q_scale):
    d = h_ref.shape[-1]
    h1 = h_ref[...] + _dot(att_ref[...], woa_ref[0]) + _dot(rw_ref[...], wob_ref[0])
    hn = _rms(h1, g_ref[0], d).astype(BF16)
    q = (_dot(hn, wq_ref[0]) * q_scale).astype(BF16)
    dh = d // CA_HEADS
    outs = []
    for hd in range(CA_HEADS):
        sl = slice(hd * dh, (hd + 1) * dh)
        s = _dot_nt(q[:, sl], mk_ref[:, sl])
        m = jnp.max(s, axis=-1, keepdims=True)
        p = jnp.exp2(s - m)
        l = jnp.sum(p, axis=-1, keepdims=True)
        outs.append((_dot(p.astype(BF16), mv_ref[:, sl]) / l).astype(BF16))
    o = jnp.concatenate(outs, axis=1)
    o_ref[...] = h1 + _dot(o, wo_ref[0])


def _mix_out(h, att, rw, woa, wob, g, wq, memkv, wo, layer, seq, mem_len, tm):
    t, d = h.shape
    tiles = seq // tm
    q_scale = (d // CA_HEADS) ** -0.5 * LOG2E
    row = lambda w: pl.BlockSpec((tm, w), lambda i: (i, 0))
    lay = functools.partial(_layer_slice, layer=layer)
    return pl.pallas_call(
        functools.partial(_mix_out_kernel, q_scale=q_scale),
        grid=(t // tm,),
        in_specs=[row(d), row(att.shape[1]), row(rw.shape[1]), lay(woa), lay(wob), lay(g),
                  lay(wq),
                  pl.BlockSpec((mem_len, d), lambda i: (i // tiles, 2 * layer)),
                  pl.BlockSpec((mem_len, d), lambda i: (i // tiles, 2 * layer + 1)),
                  lay(wo)],
        out_specs=row(d),
        out_shape=jax.ShapeDtypeStruct((t, d), F32),
        compiler_params=_params("parallel"),
        name="mix_out_cross_attention",
    )(h, att, rw, woa, wob, g, wq, memkv, memkv, wo)


def _mlp_kernel(h_ref, g_ref, wu_ref, wd_ref, gf_ref, o_ref, *, last, ff_chunk):
    d = h_ref.shape[-1]
    h = h_ref[...]
    hn = _rms(h, g_ref[0], d).astype(BF16)
    acc = h
    for c in range(wu_ref.shape[-1] // ff_chunk):
        sl = slice(c * ff_chunk, (c + 1) * ff_chunk)
        u = jnp.maximum(_dot(hn, wu_ref[0, :, sl]), 0.0)
        acc = acc + _dot((u * u).astype(BF16), wd_ref[0, sl, :])
    if last:
        acc = _rms(acc, gf_ref[...], d)
    o_ref[...] = acc


def _mlp(h, g, wu, wd, gf, layer, last, tm):
    t, d = h.shape
    row = pl.BlockSpec((tm, d), lambda i: (i, 0))
    lay = functools.partial(_layer_slice, layer=layer)
    return pl.pallas_call(
        functools.partial(_mlp_kernel, last=last, ff_chunk=1024),
        grid=(t // tm,),
        in_specs=[row, lay(g), lay(wu), lay(wd), _full((1, d))],
        out_specs=row,
        out_shape=jax.ShapeDtypeStruct((t, d), F32),
        compiler_params=_params("parallel"),
        name="mlp",
    )(h, g, wu, wd, gf)


def _pad_cols(a, n):
    return jnp.pad(a, ((0, 0), (0, n - a.shape[1])))


def _swap_halves(a):
    half = a.shape[-1] // 2
    return jnp.concatenate([a[..., half:], a[..., :half]], axis=-1)


def _in_weights(w_in, mu):
    cq = w_in[:, :MLA_Q_RANK]
    ckv = w_in[:, MLA_Q_RANK:MLA_Q_RANK + MLA_KV_RANK]
    kr = w_in[:, MLA_Q_RANK + MLA_KV_RANK:MLA_COLS]
    mla = jnp.concatenate([_pad_cols(cq, 256), ckv, kr, _swap_halves(kr)], axis=1)
    w = jnp.concatenate([_pad_cols(mla, MLA_C_PAD), _pad_cols(w_in[:, MLA_COLS:], RW_C_PAD)], axis=1)
    return w.astype(BF16), _pad_cols(mu[None, :], RW_C_PAD)


def _head_pad(a, n_heads, per_head, lo, hi, at):
    rows = a.shape[0]
    a = a.reshape(rows, n_heads, per_head)[:, :, lo:hi]
    a = jnp.pad(a, ((0, 0), (0, 0), (at, HEAD_PAD - at - (hi - lo))))
    return a.reshape(rows, n_heads * HEAD_PAD)


def _mla_weights(w_uq, w_ukv):
    dq = MLA_NOPE_DIM + MLA_ROPE_DIM
    h = MLA_HEADS
    wq3 = w_uq.reshape(MLA_Q_RANK, h, dq)
    rope_sw = _swap_halves(wq3[:, :, MLA_NOPE_DIM:]).reshape(MLA_Q_RANK, h * MLA_ROPE_DIM)
    wqa = _head_pad(w_uq, h, dq, 0, dq, 0)
    wqb = _head_pad(rope_sw, h, MLA_ROPE_DIM, 0, MLA_ROPE_DIM, MLA_NOPE_DIM)
    pad_q = ((0, 256 - MLA_Q_RANK), (0, 0))
    wqa = jnp.pad(wqa, pad_q)
    wqb = jnp.pad(wqb, pad_q)
    dkv = MLA_NOPE_DIM + MLA_V_DIM
    wk = _head_pad(w_ukv, h, dkv, 0, MLA_NOPE_DIM, 0)
    wvt = jnp.transpose(w_ukv.reshape(MLA_KV_RANK, h, dkv)[:, :, MLA_NOPE_DIM:], (1, 2, 0))
    wvt = jnp.pad(wvt, ((0, 0), (0, ATTN_ONES_ROWS), (0, 0))).reshape(-1, MLA_KV_RANK)
    eye = jnp.eye(MLA_ROPE_DIM, dtype=F32)
    place = _head_pad(jnp.tile(eye, (1, h)), h, MLA_ROPE_DIM, 0, MLA_ROPE_DIM, MLA_NOPE_DIM)
    zeros = jnp.zeros_like(place)
    pad_rows = jnp.zeros((LANE - 2 * MLA_ROPE_DIM, h * HEAD_PAD), F32)
    pa = jnp.concatenate([place, zeros, pad_rows], axis=0)
    pb = jnp.concatenate([zeros, place, pad_rows], axis=0)
    wka = jnp.concatenate([wk, pa], axis=0)
    return tuple(a.astype(BF16) for a in (wqa, wqb, wka, pb, wvt))


def _lora_weights(w2, a2, g2, v2):
    rw = RWKV_WIDTH
    z = lambda r: jnp.zeros((r, rw), F32)
    rows = [
        jnp.concatenate([w2, z(DECAY_RANK), z(DECAY_RANK), z(DECAY_RANK)], axis=1),
        jnp.concatenate([z(ICL_RANK), a2, z(ICL_RANK), z(ICL_RANK)], axis=1),
        jnp.concatenate([z(GATE_RANK), z(GATE_RANK), g2, z(GATE_RANK)], axis=1),
        jnp.concatenate([z(VRES_RANK)] * 3 + [v2 if v2 is not None else z(VRES_RANK)], axis=1),
    ]
    w = jnp.concatenate(rows, axis=0)
    return jnp.pad(w, ((0, LORA_PAD - w.shape[0]), (0, 0))).astype(BF16)


def _block_diag(n, blk, value):
    i = jnp.arange(n)
    return jnp.where((i[:, None] // blk) == (i[None, :] // blk), value, 0.0)


def kernel(x, mem, positions, mix_norm, w_in_first, w_in_rest, shift_mu_first, shift_mu_rest, mla_q_norm, mla_w_uq, mla_kv_norm, mla_w_ukv, mla_out_norm, rwkv_w0, rwkv_w2, rwkv_a0, rwkv_a2, rwkv_g2, rwkv_v0, rwkv_v2, rwkv_k_k, rwkv_k_a, rwkv_r_k, rwkv_ln_w, rwkv_ln_b, w_out, ca_norm, mem_norm, ca_wq, ca_wkv, ca_wo, mlp_norm, mlp_w_up, mlp_w_down, final_norm):
    b, s, d = x.shape
    depth = mix_norm.shape[0]
    mem_len = mem.shape[1]
    t = b * s
    tm = min(512, s)
    tm_wide = min(1024, s)
    tq = min(256, s)
    wkv_rows = min(1024, s)
    rw = RWKV_WIDTH

    inv = ROPE_THETA ** (-jnp.arange(0, MLA_ROPE_DIM, 2, dtype=F32) / MLA_ROPE_DIM)
    inv_row = jnp.concatenate([jnp.zeros((MLA_NOPE_DIM,), F32), inv, inv,
                               jnp.zeros((HEAD_PAD - MLA_NOPE_DIM - MLA_ROPE_DIM,), F32)])[None, :]
    ct, st = _rope_tables(positions.astype(F32).reshape(t, 1), inv_row, tm)

    memkv = _mem_kv(mem.reshape(b * mem_len, d), mem_norm[None, :], ca_wkv.astype(BF16),
                    min(512, b * mem_len))

    woa = w_out[:, :MLA_HEADS * MLA_V_DIM].astype(BF16)
    wob = w_out[:, MLA_HEADS * MLA_V_DIM:].astype(BF16)
    wq_b = ca_wq.astype(BF16)
    wo_b = ca_wo.astype(BF16)
    wu_b = mlp_w_up.astype(BF16)
    wd_b = mlp_w_down.astype(BF16)
    ca_g = ca_norm[:, None, :]
    mlp_g = mlp_norm[:, None, :]
    bd512 = _block_diag(rw, RWKV_HEAD, 1.0).astype(BF16)

    h = x.reshape(t, d)
    v_first = None
    for l in range(depth):
        if l == 0:
            w_in, mu, v0, v2 = w_in_first, shift_mu_first, None, None
        else:
            w_in, mu, v0, v2 = w_in_rest[l - 1], shift_mu_rest[l - 1], rwkv_v0[l - 1], rwkv_v2[l - 1]
        w_all, mu_p = _in_weights(w_in, mu)
        wqa, wqb, wka, pb, wvt = _mla_weights(mla_w_uq[l], mla_w_ukv[l])
        gq = _pad_cols(mla_q_norm[l][None, :], 256)
        w2c = _lora_weights(rwkv_w2[l], rwkv_a2[l], rwkv_g2[l], v2)
        vec = lambda a: a.reshape(1, rw)
        q, k, vt, r_, k_, v_, kk_, kka_, lw_, g_ = _layer_in(
            h, mix_norm[l][None, :], w_all, mu_p, ct, st, gq, mla_kv_norm[l][None, :],
            wqa, wqb, wka, pb, wvt, v_first, w2c, bd512, vec(rwkv_w0[l]), vec(rwkv_a0[l]),
            vec(rwkv_k_k[l]), vec(rwkv_k_a[l]), None if v0 is None else vec(v0), s, tm_wide)
        if l == 0:
            v_first = v_

        att = _mla_attention(q, k, vt, mla_out_norm[l][None, :], s, tq, ATTN_HEADS_PER_STEP)

        sh = lambda a: a.reshape(b, s, rw)
        y = _wkv(sh(r_), sh(k_), sh(v_), sh(kk_), sh(kka_), sh(lw_), sh(g_),
                 vec(rwkv_r_k[l]), vec(rwkv_ln_w[l]), vec(rwkv_ln_b[l]), wkv_rows, WKV_MODES)

        h = _mix_out(h, att.reshape(t, -1), y.reshape(t, rw), woa, wob, ca_g, wq_b, memkv, wo_b,
                     l, s, mem_len, tm_wide)
        h = _mlp(h, mlp_g, wu_b, wd_b, final_norm[None, :], l, l == depth - 1, tm_wide)
    return h.reshape(b, s, d)
```

```python
import functools
import math

import jax
import jax.numpy as jnp
from jax import lax
from jax.experimental import pallas as pl
from jax.experimental.pallas import tpu as pltpu

F32 = jnp.float32
BF16 = jnp.bfloat16

CHUNK = 64
NORM_EPS = 1e-6
MLA_HEADS = 8
MLA_V_DIM = 64
MLA_NOPE_DIM = 64
MLA_ROPE_DIM = 32
MLA_Q_RANK = 192
MLA_KV_RANK = 128
ROPE_THETA = 10000.0
RWKV_HEAD = 64
RWKV_HEADS = 8
RWKV_WIDTH = 512
DECAY_RANK = 32
ICL_RANK = 32
GATE_RANK = 96
VRES_RANK = 32
GN_EPS = 64e-5
CA_HEADS = 4
MLA_COLS = MLA_Q_RANK + MLA_KV_RANK + MLA_ROPE_DIM

LANE = 128
HEAD_PAD = 128
MLA_C_PAD = 512
MLA_Q_PAD = 256
MLA_KV_END = MLA_Q_PAD + MLA_KV_RANK
RW_C_PAD = 1792
LORA_PAD = 256
WKV_CHUNK = 64
WKV_GROUP = 4
WKV_AHEAD_STAGES = 2
LAYER_IN_PART_ROWS = 256
ATTN_HEADS_PER_STEP = 4
ATTN_ONES_ROWS = 16
LOG2E = 1.4426950408889634
NEG_BIG = -1e30
VMEM_LIMIT = 56 * 1024 * 1024

HI = lax.Precision.HIGHEST
WKV_MODES = ("bf16", "bf16", "bf16")


def _dot(a, b, precision=None):
    return jnp.dot(a, b, preferred_element_type=F32, precision=precision)


def _dot_nt(a, b, precision=None):
    return lax.dot_general(a, b, (((1,), (1,)), ((), ())),
                           preferred_element_type=F32, precision=precision)


def _rms(x, g, n, eps=NORM_EPS):
    ms = jnp.sum(x * x, axis=-1, keepdims=True) * (1.0 / n)
    return x * lax.rsqrt(ms + eps) * g


def _sigmoid(x):
    return 0.5 + 0.5 * jnp.tanh(0.5 * x)


def _params(*sem):
    return pltpu.CompilerParams(dimension_semantics=sem, vmem_limit_bytes=VMEM_LIMIT)


def _full(shape):
    nd = len(shape)
    return pl.BlockSpec(shape, lambda *_: (0,) * nd, pipeline_mode=pl.Buffered(1))


def _layer_slice(a, layer):
    nd = a.ndim
    return pl.BlockSpec((1,) + a.shape[1:], lambda i: (layer,) + (0,) * (nd - 1),
                        pipeline_mode=pl.Buffered(1))


def _rope_kernel(pos_ref, inv_ref, ct_ref, st_ref):
    ang = pos_ref[...] * inv_ref[...]
    c = jnp.cos(ang)
    s = jnp.sin(ang)
    lane = lax.broadcasted_iota(jnp.int32, ang.shape, 1)
    half = MLA_NOPE_DIM + MLA_ROPE_DIM // 2
    end = MLA_NOPE_DIM + MLA_ROPE_DIM
    ct_ref[...] = jnp.where(lane < MLA_NOPE_DIM, 1.0, jnp.where(lane < end, c, 0.0))
    st_ref[...] = jnp.where(lane < MLA_NOPE_DIM, 0.0,
                            jnp.where(lane < half, -s, jnp.where(lane < end, s, 0.0)))


def _rope_tables(pos_f, inv_row, tm):
    t = pos_f.shape[0]
    return pl.pallas_call(
        _rope_kernel,
        grid=(t // tm,),
        in_specs=[pl.BlockSpec((tm, 1), lambda i: (i, 0)), _full((1, LANE))],
        out_specs=[pl.BlockSpec((tm, LANE), lambda i: (i, 0))] * 2,
        out_shape=[jax.ShapeDtypeStruct((t, LANE), F32)] * 2,
        compiler_params=_params("parallel"),
        name="rope_tables",
    )(pos_f, inv_row)


def _mla_latents(c, ct_ref, st_ref, gq_ref, gkv_ref, wqa_ref, wqb_ref, wka_ref, pb_ref,
                 wv_ref, ones_ref, q_ref, k_ref, v_ref, q_scale):
    nq = _rms(c[:, :MLA_Q_PAD], gq_ref[...], MLA_Q_RANK).astype(BF16)
    nkv = _rms(c[:, MLA_Q_PAD:MLA_KV_END], gkv_ref[...], MLA_KV_RANK).astype(BF16)
    kr = c[:, MLA_KV_END:MLA_C_PAD].astype(BF16)
    ct = jnp.tile(ct_ref[...], (1, MLA_HEADS))
    st = jnp.tile(st_ref[...], (1, MLA_HEADS))
    q = _dot(nq, wqa_ref[...]) * ct + _dot(nq, wqb_ref[...]) * st
    q_ref[...] = (q * q_scale).astype(BF16)
    x = jnp.concatenate([nkv, kr], axis=1)
    k = _dot(x, wka_ref[...]) * ct + _dot(kr, pb_ref[...]) * st
    k_ref[...] = k.astype(BF16)
    v_ref[...] = (_dot_nt(wv_ref[...], nkv) + ones_ref[...]).astype(BF16)


def _rwkv_elementwise(x, vf_ref, w2_ref, bd_ref, w0_ref, a0_ref, kk_ref_p, ka_ref, v0_ref,
                      r_o, k_o, v_o, kk_o, kka_o, lw_o, g_o):
    rw = RWKV_WIDTH
    r_o[...] = x[:, 0:rw].astype(BF16)
    k = x[:, rw:2 * rw]
    v = x[:, 2 * rw:3 * rw]
    tail = x[:, 3 * rw:3 * rw + LORA_PAD]
    lane = lax.broadcasted_iota(jnp.int32, tail.shape, 1)
    g_lo = DECAY_RANK + ICL_RANK
    g_hi = g_lo + GATE_RANK
    is_gate = (lane >= g_lo) & (lane < g_hi)
    th = jnp.tanh(jnp.where(is_gate, 0.5 * tail, tail))
    act = jnp.where(lane < DECAY_RANK, th, jnp.where(is_gate, 0.5 + 0.5 * th, tail))
    lo = _dot(act.astype(BF16), w2_ref[...])
    lw_o[...] = -math.exp(-0.5) * _sigmoid(w0_ref[...] + lo[:, 0:rw])
    a = _sigmoid(a0_ref[...] + lo[:, rw:2 * rw])
    g_o[...] = lo[:, 2 * rw:3 * rw].astype(BF16)
    if vf_ref is not None:
        mix = _sigmoid(v0_ref[...] + lo[:, 3 * rw:4 * rw])
        v = v + (vf_ref[...].astype(F32) - v) * mix
    v_o[...] = v.astype(BF16)
    kk = k * kk_ref_p[...]
    sq = kk * kk
    sq_hi = sq.astype(BF16)
    sq_lo = (sq - sq_hi.astype(F32)).astype(BF16)
    ss = _dot(sq_hi, bd_ref[...]) + _dot(sq_lo, bd_ref[...])
    kk = kk * lax.rsqrt(jnp.maximum(ss, 1e-24))
    kk_o[...] = kk.astype(BF16)
    kka_o[...] = (kk * a).astype(BF16)
    k_o[...] = (k * (1.0 + (a - 1.0) * ka_ref[...])).astype(BF16)


def _layer_in_kernel(*refs, first, tiles_per_seq, q_scale):
    n_in = 20 if first else 22
    ins, outs, carry_ref = refs[:n_in], refs[n_in:n_in + 10], refs[n_in + 10]
    h_ref, g_ref, w_ref, mu_ref = ins[:4]
    mla_ins = ins[4:14]
    if first:
        w2_ref, bd_ref, w0_ref, a0_ref, kkp_ref, ka_ref = ins[14:20]
        vf_ref = v0_ref = None
    else:
        vf_ref, w2_ref, bd_ref, w0_ref, a0_ref, kkp_ref, ka_ref, v0_ref = ins[14:22]
    q_ref, k_ref, vt_ref = outs[:3]
    rw_outs = outs[3:]

    i = pl.program_id(0)
    tm, d = h_ref.shape
    hr = min(LAYER_IN_PART_ROWS, tm)

    @pl.when(i % tiles_per_seq == 0)
    def _():
        carry_ref[...] = jnp.zeros_like(carry_ref)

    parts = [pl.ds(p * hr, hr) for p in range(tm // hr)]
    hn = [_rms(h_ref[rows, :], g_ref[...], d).astype(BF16) for rows in parts]
    proj = [_dot(x, w_ref[...]) for x in hn]
    last = carry_ref[7:8, :]
    for rows, pr in zip(parts, proj):
        ct_v, st_v = mla_ins[0].at[rows], mla_ins[1].at[rows]
        _mla_latents(pr[:, :MLA_C_PAD], ct_v, st_v, *mla_ins[2:], q_ref.at[rows], k_ref.at[rows],
                     vt_ref.at[:, rows], q_scale)
        z = pr[:, MLA_C_PAD:]
        prev = pltpu.roll(z, 1, axis=0)
        row = lax.broadcasted_iota(jnp.int32, z.shape, 0)
        prev = jnp.where(row == 0, last, prev)
        x = z + (prev - z) * mu_ref[...]
        last = z[hr - 1:hr, :]
        _rwkv_elementwise(x, None if vf_ref is None else vf_ref.at[rows], w2_ref, bd_ref, w0_ref,
                          a0_ref, kkp_ref, ka_ref, v0_ref, *[o.at[rows] for o in rw_outs])
    carry_ref[...] = z[hr - 8:, :]


def _layer_in(h, g, w, mu, ct, st, gq, gkv, wqa, wqb, wka, pb, wvt, vf, w2, bd, w0, a0, k_k,
              k_a, v0, seq, tm):
    t, d = h.shape
    rw = RWKV_WIDTH
    hw = MLA_HEADS * HEAD_PAD
    vw = wvt.shape[0]
    first = vf is None
    ones_col = (jnp.arange(vw) % (MLA_V_DIM + ATTN_ONES_ROWS) == MLA_V_DIM).astype(F32)[:, None]
    q_scale = (MLA_NOPE_DIM + MLA_ROPE_DIM) ** -0.5 * LOG2E
    row = lambda width: pl.BlockSpec((tm, width), lambda i: (i, 0))
    vec = _full((1, rw))
    args = [h, g, w, mu, ct, st, gq, gkv, wqa, wqb, wka, pb, wvt, ones_col]
    in_specs = [row(d), _full((1, d)), _full(w.shape), _full((1, RW_C_PAD)), row(LANE), row(LANE),
                _full(gq.shape), _full(gkv.shape), _full(wqa.shape), _full(wqb.shape),
                _full(wka.shape), _full(pb.shape), _full(wvt.shape), _full(ones_col.shape)]
    if first:
        args += [w2, bd, w0, a0, k_k, k_a]
        in_specs += [_full(w2.shape), _full(bd.shape), vec, vec, vec, vec]
    else:
        args += [vf, w2, bd, w0, a0, k_k, k_a, v0]
        in_specs += [row(rw), _full(w2.shape), _full(bd.shape), vec, vec, vec, vec, vec]
    rw_dt = [BF16, BF16, BF16, BF16, BF16, F32, BF16]
    kern = functools.partial(_layer_in_kernel, first=first, tiles_per_seq=seq // tm,
                             q_scale=q_scale)
    return pl.pallas_call(
        kern,
        grid=(t // tm,),
        in_specs=in_specs,
        out_specs=[row(hw), row(hw), pl.BlockSpec((vw, tm), lambda i: (0, i))] + [row(rw)] * 7,
        out_shape=[jax.ShapeDtypeStruct((t, hw), BF16), jax.ShapeDtypeStruct((t, hw), BF16),
                   jax.ShapeDtypeStruct((vw, t), BF16)]
                  + [jax.ShapeDtypeStruct((t, rw), dt) for dt in rw_dt],
        scratch_shapes=[pltpu.VMEM((8, RW_C_PAD), F32)],
        compiler_params=_params("arbitrary"),
        name="layer_in",
    )(*args)


def _attn_kernel(q_ref, k_ref, vt_ref, g_ref, o_ref, m_ref, a_ref, p_ref, acc_ref, s_ref, s2_ref,
                 *, tq, tk, nh):
    qi = pl.program_id(2)
    dv = MLA_V_DIM
    heads = range(nh)

    ext = dv + ATTN_ONES_ROWS

    m_ref[...] = jnp.full_like(m_ref, NEG_BIG)
    a_ref[...] = jnp.ones_like(a_ref)
    p_ref[...] = jnp.zeros_like(p_ref)
    acc_ref[...] = jnp.zeros_like(acc_ref)

    def flush_values(j, q0=0):
        off = pl.multiple_of(j * tk, tk)
        pv = [_dot(vt_ref[h * ext:(h + 1) * ext, pl.ds(off, tk)], p_ref[h, :, q0:])
              for h in heads]
        for h in heads:
            acc_ref[h, :, q0:] = a_ref[h, :, q0:] * acc_ref[h, :, q0:] + pv[h]

    def scores(j, h, dst, q0=0):
        off = pl.multiple_of(j * tk, tk)
        dst[h, :, q0:] = _dot_nt(k_ref[0, pl.ds(off, tk), h * HEAD_PAD:(h + 1) * HEAD_PAD],
                                 q_ref[0, q0:, h * HEAD_PAD:(h + 1) * HEAD_PAD])

    def softmax(h, src, key0=None, q0=0):
        s = src[h, :, q0:]
        if key0 is not None:
            ck = (lax.broadcasted_iota(jnp.int32, s.shape, 0) + key0) // CHUNK
            cq = (lax.broadcasted_iota(jnp.int32, s.shape, 1) + q0) // CHUNK
            s = jnp.where(ck <= cq, s, NEG_BIG)
        m_old = m_ref[h, :, q0:]
        m_new = jnp.maximum(m_old, jnp.max(s, axis=0, keepdims=True))
        p_ref[h, :, q0:] = jnp.exp2((s - m_new).astype(BF16))
        a_ref[h, :, q0:] = jnp.exp2(m_old - m_new)
        m_ref[h, :, q0:] = m_new

    def stage(j, src, dst, key0=None, q0=0, next_q0=0):
        flush_values(jnp.maximum(j - 1, 0))
        for h in heads:
            if dst is not None:
                scores(j + 1, h, dst, next_q0)
            softmax(h, src, key0, q0)

    for h in heads:
        scores(0, h, s_ref)

    @pl.loop(0, qi)
    def _(t):
        stage(2 * t, s_ref, s2_ref)
        stage(2 * t + 1, s2_ref, s_ref)

    stage(2 * qi, s_ref, s2_ref, key0=0, next_q0=tk)
    stage(2 * qi + 1, s2_ref, None, key0=tk, q0=tk)
    flush_values(2 * qi + 1, q0=tk)
    outs = []
    for h in heads:
        acc = acc_ref[h]
        o = acc[:dv] / acc[dv:dv + 1]
        ms = jnp.sum(o * o, axis=0, keepdims=True) * (1.0 / dv)
        outs.append(o * lax.rsqrt(ms + NORM_EPS))
    o_t = jnp.concatenate(outs, axis=0)
    o_ref[0] = (o_t.T * g_ref[...]).astype(BF16)


def _mla_attention(q, k, vt, g, seq, tk, nh):
    tq = 2 * tk
    b = q.shape[0] // seq
    hw = q.shape[1]
    groups = hw // (nh * HEAD_PAD)
    ow = nh * MLA_V_DIM
    q3 = q.reshape(b, seq, hw)
    k3 = k.reshape(b, seq, hw)
    return pl.pallas_call(
        functools.partial(_attn_kernel, tq=tq, tk=tk, nh=nh),
        grid=(b, groups, seq // tq),
        in_specs=[pl.BlockSpec((1, tq, nh * HEAD_PAD), lambda bi, h, i: (bi, i, h)),
                  pl.BlockSpec((1, seq, nh * HEAD_PAD), lambda bi, h, i: (bi, 0, h)),
                  pl.BlockSpec((nh * (MLA_V_DIM + ATTN_ONES_ROWS), seq), lambda bi, h, i: (h, bi)),
                  pl.BlockSpec((1, ow), lambda bi, h, i: (0, h))],
        out_specs=pl.BlockSpec((1, tq, ow), lambda bi, h, i: (bi, i, h)),
        out_shape=jax.ShapeDtypeStruct((b, seq, groups * ow), BF16),
        scratch_shapes=[pltpu.VMEM((nh, 1, tq), F32), pltpu.VMEM((nh, 1, tq), F32),
                        pltpu.VMEM((nh, tk, tq), BF16),
                        pltpu.VMEM((nh, MLA_V_DIM + ATTN_ONES_ROWS, tq), F32),
                        pltpu.VMEM((nh, tk, tq), F32), pltpu.VMEM((nh, tk, tq), F32)],
        compiler_params=_params("parallel", "parallel", "arbitrary"),
        name="mla_attention",
    )(q3, k3, vt, g)


_DIMS = {"nn": (((1,), (0,)), ((), ())), "nt": (((1,), (1,)), ((), ())),
         "tn": (((0,), (0,)), ((), ()))}


def _split(x):
    hi = x.astype(BF16)
    return hi, (x - hi.astype(F32)).astype(BF16)


def _mm(a, b, mode, dims="nn"):
    dg = lambda x, y: lax.dot_general(x, y, _DIMS[dims], preferred_element_type=F32)
    if mode == "bf16":
        return dg(a.astype(BF16), b.astype(BF16))
    if mode == "x3":
        a_hi, a_lo = _split(a)
        b_hi, b_lo = _split(b)
        return dg(a_hi, b_hi) + (dg(a_lo, b_hi) + dg(a_hi, b_lo))
    if mode == "ax":
        n = b.shape[1]
        out = dg(a.astype(BF16), jnp.concatenate(_split(b), axis=1))
        return out[:, :n] + out[:, n:]
    return lax.dot_general(a, b, _DIMS[dims], preferred_element_type=F32, precision=HI)


def _unit_lower_solve_stages(a_list, rhs_list, row, col, eye, mode, out):
    d16 = (row // 16) == (col // 16)
    d32 = (row // 32) == (col // 32)
    n = range(len(a_list))
    d = [jnp.where(d16, a, 0.0) for a in a_list]
    x = [eye + d[i] for i in n]
    d2 = [_mm(d[i], d[i], mode) for i in n]
    yield
    t = [_mm(d2[i], jnp.concatenate([x[i], d2[i]], axis=1), mode) for i in n]
    x = [x[i] + t[i][:, :128] for i in n]
    d4 = [t[i][:, 128:] for i in n]
    yield
    t = [_mm(d4[i], jnp.concatenate([x[i], d4[i]], axis=1), mode) for i in n]
    x = [x[i] + t[i][:, :128] for i in n]
    d8 = [t[i][:, 128:] for i in n]
    yield
    t = [_mm(d8[i], x[i], mode) for i in n]
    x = [x[i] + t[i] for i in n]
    yield
    e = [jnp.where(d32, a_list[i], 0.0) - d[i] for i in n]
    t = [_mm(e[i], x[i], mode) for i in n]
    yield
    t = [_mm(x[i], t[i], mode) for i in n]
    x = [x[i] + t[i] for i in n]
    yield
    f = [jnp.where(d32, 0.0, a_list[i]) for i in n]
    z = [_mm(x[i], rhs_list[i], mode) for i in n]
    yield
    t = [_mm(f[i], z[i], mode) for i in n]
    yield
    t = [_mm(x[i], t[i], mode) for i in n]
    out.extend(z[i] + t[i] for i in n)
    yield


def _wkv_kernel(r_ref, k_ref, v_ref, kk_ref, kka_ref, lw_ref, g_ref,
                rk_ref, lnw_ref, lnb_ref, o_ref, s_ref, *, nch, modes):
    m_quad, m_inv, m_state = modes
    c_len = WKV_CHUNK

    npair = s_ref.shape[0]

    @pl.when(pl.program_id(1) == 0)
    def _():
        s_ref[...] = jnp.zeros_like(s_ref)

    row = lax.broadcasted_iota(jnp.int32, (128, 128), 0)
    col = lax.broadcasted_iota(jnp.int32, (128, 128), 1)
    same = (row // c_len) == (col // c_len)
    mask_s = same & (col < row)
    mask_i = same & (col <= row)
    eye = jnp.where(row == col, 1.0, 0.0)
    lane0 = lax.broadcasted_iota(jnp.int32, (c_len, 128), 1) < RWKV_HEAD
    tr = lax.broadcasted_iota(jnp.int32, (c_len, c_len), 0)
    tc = lax.broadcasted_iota(jnp.int32, (c_len, c_len), 1)
    ltri = jnp.where(tc <= tr, 1.0, 0.0)

    def stack(x):
        return jnp.concatenate([jnp.where(lane0, x, 0.0), jnp.where(lane0, 0.0, x)], axis=0)

    def head_sums(x):
        first = jnp.sum(jnp.where(lane0, x, 0.0), axis=-1, keepdims=True)
        both = jnp.sum(x, axis=-1, keepdims=True)
        return jnp.where(lane0, first, both - first)

    pairs = range(npair)
    par = {}
    wus = {}

    def independent_part(chunks):
        keys = [(c, hp) for c in chunks for hp in pairs]
        for c, hp in keys:
            sl = pl.ds(c * c_len, c_len)
            ln = slice(hp * LANE, (hp + 1) * LANE)
            r = r_ref[0, sl, ln].astype(F32)
            k = k_ref[0, sl, ln].astype(F32)
            v = v_ref[0, sl, ln].astype(F32)
            kk = kk_ref[0, sl, ln].astype(F32)
            kka = kka_ref[0, sl, ln].astype(F32)
            lw = lw_ref[0, sl, ln]
            cum = _mm(ltri, lw, "ax")
            tot = cum[c_len - 1:c_len, :]
            p_in = jnp.exp(cum)
            p_ex = jnp.exp(cum - lw)
            p_inv = jnp.exp(-cum)
            p_end = jnp.exp(tot - cum)
            at = stack(-kk * p_ex)
            rt = stack(r * p_in)
            bt = kka * p_inv
            kt = k * p_inv
            par[c, hp] = dict(
                r=r, k=k, v=v, tot=tot, at=at, rt=rt, vs=stack(v),
                bkh=jnp.concatenate([stack(kka * p_end), stack(k * p_end)], axis=0),
                lhs=jnp.concatenate([at, rt], axis=0),
                rhs=jnp.concatenate([bt, bt, kt, kt], axis=0))
        yield
        for key in keys:
            p = par[key]
            quad = _mm(p["lhs"], p["rhs"], m_quad, "nt")
            p["a_ab"] = jnp.where(mask_s, quad[:128, :128], 0.0)
            p["a_ak"] = jnp.where(mask_s, quad[:128, 128:], 0.0)
            p["a_r"] = jnp.concatenate([jnp.where(mask_i, quad[128:, :128], 0.0),
                                        jnp.where(mask_i, quad[128:, 128:], 0.0)], axis=1)
        yield
        for key in keys:
            par[key]["akv"] = _mm(par[key]["a_ak"], par[key]["vs"], m_quad)
        yield
        solved = []
        yield from _unit_lower_solve_stages(
            [par[key]["a_ab"] for key in keys],
            [jnp.concatenate([par[key]["at"], par[key]["akv"]], axis=1) for key in keys],
            row, col, eye, m_inv, solved)
        wus.update(zip(keys, solved))

    state = [s_ref[hp] for hp in pairs]

    def outputs(c, m1, uv):
        sl = pl.ds(c * c_len, c_len)
        pc = [par[c, hp] for hp in pairs]
        ys = [m1[hp][128:] + _mm(pc[hp]["a_r"], uv[hp], m_state) for hp in pairs]
        y = [ys[hp][:c_len] + ys[hp][c_len:] for hp in pairs]
        mean = [head_sums(y[hp]) * (1.0 / RWKV_HEAD) for hp in pairs]
        yc = [y[hp] - mean[hp] for hp in pairs]
        var = [head_sums(yc[hp] * yc[hp]) * (1.0 / RWKV_HEAD) for hp in pairs]
        for hp in pairs:
            ln = slice(hp * LANE, (hp + 1) * LANE)
            r, k, v = pc[hp]["r"], pc[hp]["k"], pc[hp]["v"]
            yn = yc[hp] * lax.rsqrt(var[hp] + GN_EPS) * lnw_ref[:, ln] + lnb_ref[:, ln]
            bonus = head_sums(r * k * rk_ref[:, ln])
            yn = yn + bonus * v
            o_ref[0, sl, ln] = (yn * g_ref[0, sl, ln].astype(F32)).astype(BF16)

    pending = []

    def state_part(chunks):
        for c in chunks:
            pc = [par[c, hp] for hp in pairs]
            wu = [wus[c, hp] for hp in pairs]
            m1 = [_mm(jnp.concatenate([wu[hp][:, :128], pc[hp]["rt"]], axis=0), state[hp],
                      m_state, "nt") for hp in pairs]
            uv = [jnp.concatenate([m1[hp][:128] + wu[hp][:, 128:], pc[hp]["vs"]], axis=0)
                  for hp in pairs]
            upd = [_mm(uv[hp], pc[hp]["bkh"], m_state, "tn") for hp in pairs]
            for hp in pairs:
                state[hp] = state[hp] * jnp.exp(pc[hp]["tot"]) + upd[hp]
            yield
            if pending:
                outputs(*pending.pop())
                yield
            pending.append((c, m1, uv))

    groups = [list(range(g, min(g + WKV_GROUP, nch))) for g in range(0, nch, WKV_GROUP)]
    for _ in independent_part(groups[0]):
        pass
    for g, chunks in enumerate(groups):
        chain = state_part(chunks)
        ahead = independent_part(groups[g + 1]) if g + 1 < len(groups) else iter(())
        for _ in chain:
            for _ in range(WKV_AHEAD_STAGES):
                next(ahead, None)
        for _ in ahead:
            pass
    outputs(*pending.pop())
    for hp in pairs:
        s_ref[hp] = state[hp]


def _wkv(r, k, v, kk, kka, lw, g, r_k, ln_w, ln_b, rows, modes):
    b, s, w = r.shape
    blk = pl.BlockSpec((1, rows, w), lambda bi, i: (bi, i, 0))
    kern = functools.partial(_wkv_kernel, nch=rows // WKV_CHUNK, modes=modes)
    return pl.pallas_call(
        kern,
        grid=(b, s // rows),
        in_specs=[blk] * 7 + [_full((1, w))] * 3,
        out_specs=blk,
        out_shape=jax.ShapeDtypeStruct((b, s, w), BF16),
        scratch_shapes=[pltpu.VMEM((w // LANE, LANE, LANE), F32)],
        compiler_params=_params("parallel", "arbitrary"),
        name="wkv7",
    )(r, k, v, kk, kka, lw, g, r_k, ln_w, ln_b)


def _mem_kv_kernel(m_ref, g_ref, w_ref, o_ref):
    d = m_ref.shape[-1]
    mn = _rms(m_ref[...], g_ref[...], d).astype(BF16)
    o_ref[...] = _dot(mn, w_ref[0]).astype(BF16)


def _mem_kv(mem2d, g, wkv, tm):
    t, d = mem2d.shape
    nl, _, n = wkv.shape
    return pl.pallas_call(
        _mem_kv_kernel,
        grid=(nl, t // tm),
        in_specs=[pl.BlockSpec((tm, d), lambda l, i: (i, 0)), _full((1, d)),
                  pl.BlockSpec((1, d, n), lambda l, i: (l, 0, 0))],
        out_specs=pl.BlockSpec((tm, n), lambda l, i: (i, l)),
        out_shape=jax.ShapeDtypeStruct((t, nl * n), BF16),
        compiler_params=_params("parallel", "parallel"),
        name="mem_kv",
    )(mem2d, g, wkv)


def _mix_out_kernel(h_ref, att_ref, rw_ref, woa_ref, wob_ref, g_ref, wq_ref,
                    mk_ref, mv_ref, wo_ref, o_ref, *, q_scale):
    d = h_ref.shape[-1]
    h1 = h_ref[...] + _dot(att_ref[...], woa_ref[0]) + _dot(rw_ref[...], wob_ref[0])
    hn = _rms(h1, g_ref[0], d).astype(BF16)
    q = (_dot(hn, wq_ref[0]) * q_scale).astype(BF16)
    dh = d // CA_HEADS
    outs = []
    for hd in range(CA_HEADS):
        sl = slice(hd * dh, (hd + 1) * dh)
        s = _dot_nt(q[:, sl], mk_ref[:, sl])
        m = jnp.max(s, axis=-1, keepdims=True)
        p = jnp.exp2(s - m)
        l = jnp.sum(p, axis=-1, keepdims=True)
        outs.append((_dot(p.astype(BF16), mv_ref[:, sl]) / l).astype(BF16))
    o = jnp.concatenate(outs, axis=1)
    o_ref[...] = h1 + _dot(o, wo_ref[0])


def _mix_out(h, att, rw, woa, wob, g, wq, memkv, wo, layer, seq, mem_len, tm):
    t, d = h.shape
    tiles = seq // tm
    q_scale = (d // CA_HEADS) ** -0.5 * LOG2E
    row = lambda w: pl.BlockSpec((tm, w), lambda i: (i, 0))
    lay = functools.partial(_layer_slice, layer=layer)
    return pl.pallas_call(
        functools.partial(_mix_out_kernel, q_scale=q_scale),
        grid=(t // tm,),
        in_specs=[row(d), row(att.shape[1]), row(rw.shape[1]), lay(woa), lay(wob), lay(g),
                  lay(wq),
                  pl.BlockSpec((mem_len, d), lambda i: (i // tiles, 2 * layer)),
                  pl.BlockSpec((mem_len, d), lambda i: (i // tiles, 2 * layer + 1)),
                  lay(wo)],
        out_specs=row(d),
        out_shape=jax.ShapeDtypeStruct((t, d), F32),
        compiler_params=_params("parallel"),
        name="mix_out_cross_attention",
    )(h, att, rw, woa, wob, g, wq, memkv, memkv, wo)


def _mlp_kernel(h_ref, g_ref, wu_ref, wd_ref, gf_ref, o_ref, *, last, ff_chunk):
    d = h_ref.shape[-1]
    h = h_ref[...]
    hn = _rms(h, g_ref[0], d).astype(BF16)
    acc = h
    for c in range(wu_ref.shape[-1] // ff_chunk):
        sl = slice(c * ff_chunk, (c + 1) * ff_chunk)
        u = jnp.maximum(_dot(hn, wu_ref[0, :, sl]), 0.0)
        acc = acc + _dot((u * u).astype(BF16), wd_ref[0, sl, :])
    if last:
        acc = _rms(acc, gf_ref[...], d)
    o_ref[...] = acc


def _mlp(h, g, wu, wd, gf, layer, last, tm):
    t, d = h.shape
    row = pl.BlockSpec((tm, d), lambda i: (i, 0))
    lay = functools.partial(_layer_slice, layer=layer)
    return pl.pallas_call(
        functools.partial(_mlp_kernel, last=last, ff_chunk=1024),
        grid=(t // tm,),
        in_specs=[row, lay(g), lay(wu), lay(wd), _full((1, d))],
        out_specs=row,
        out_shape=jax.ShapeDtypeStruct((t, d), F32),
        compiler_params=_params("parallel"),
        name="mlp",
    )(h, g, wu, wd, gf)


def _pad_cols(a, n):
    return jnp.pad(a, ((0, 0), (0, n - a.shape[1])))


def _swap_halves(a):
    half = a.shape[-1] // 2
    return jnp.concatenate([a[..., half:], a[..., :half]], axis=-1)


def _in_weights(w_in, mu):
    cq = w_in[:, :MLA_Q_RANK]
    ckv = w_in[:, MLA_Q_RANK:MLA_Q_RANK + MLA_KV_RANK]
    kr = w_in[:, MLA_Q_RANK + MLA_KV_RANK:MLA_COLS]
    mla = jnp.concatenate([_pad_cols(cq, MLA_Q_PAD), ckv, kr, _swap_halves(kr)], axis=1)
    w = jnp.concatenate([_pad_cols(mla, MLA_C_PAD), _pad_cols(w_in[:, MLA_COLS:], RW_C_PAD)], axis=1)
    return w.astype(BF16), _pad_cols(mu[None, :], RW_C_PAD)


def _head_pad(a, n_heads, per_head, lo, hi, at):
    rows = a.shape[0]
    a = a.reshape(rows, n_heads, per_head)[:, :, lo:hi]
    a = jnp.pad(a, ((0, 0), (0, 0), (at, HEAD_PAD - at - (hi - lo))))
    return a.reshape(rows, n_heads * HEAD_PAD)


def _mla_weights(w_uq, w_ukv):
    dq = MLA_NOPE_DIM + MLA_ROPE_DIM
    h = MLA_HEADS
    wq3 = w_uq.reshape(MLA_Q_RANK, h, dq)
    rope_sw = _swap_halves(wq3[:, :, MLA_NOPE_DIM:]).reshape(MLA_Q_RANK, h * MLA_ROPE_DIM)
    wqa = _head_pad(w_uq, h, dq, 0, dq, 0)
    wqb = _head_pad(rope_sw, h, MLA_ROPE_DIM, 0, MLA_ROPE_DIM, MLA_NOPE_DIM)
    pad_q = ((0, MLA_Q_PAD - MLA_Q_RANK), (0, 0))
    wqa = jnp.pad(wqa, pad_q)
    wqb = jnp.pad(wqb, pad_q)
    dkv = MLA_NOPE_DIM + MLA_V_DIM
    wk = _head_pad(w_ukv, h, dkv, 0, MLA_NOPE_DIM, 0)
    wvt = jnp.transpose(w_ukv.reshape(MLA_KV_RANK, h, dkv)[:, :, MLA_NOPE_DIM:], (1, 2, 0))
    wvt = jnp.pad(wvt, ((0, 0), (0, ATTN_ONES_ROWS), (0, 0))).reshape(-1, MLA_KV_RANK)
    eye = jnp.eye(MLA_ROPE_DIM, dtype=F32)
    place = _head_pad(jnp.tile(eye, (1, h)), h, MLA_ROPE_DIM, 0, MLA_ROPE_DIM, MLA_NOPE_DIM)
    zeros = jnp.zeros_like(place)
    pad_rows = jnp.zeros((LANE - 2 * MLA_ROPE_DIM, h * HEAD_PAD), F32)
    pa = jnp.concatenate([place, zeros, pad_rows], axis=0)
    pb = jnp.concatenate([zeros, place, pad_rows], axis=0)
    wka = jnp.concatenate([wk, pa], axis=0)
    return tuple(a.astype(BF16) for a in (wqa, wqb, wka, pb, wvt))


def _lora_weights(w2, a2, g2, v2):
    rw = RWKV_WIDTH
    z = lambda r: jnp.zeros((r, rw), F32)
    rows = [
        jnp.concatenate([w2, z(DECAY_RANK), z(DECAY_RANK), z(DECAY_RANK)], axis=1),
        jnp.concatenate([z(ICL_RANK), a2, z(ICL_RANK), z(ICL_RANK)], axis=1),
        jnp.concatenate([z(GATE_RANK), z(GATE_RANK), g2, z(GATE_RANK)], axis=1),
        jnp.concatenate([z(VRES_RANK)] * 3 + [v2 if v2 is not None else z(VRES_RANK)], axis=1),
    ]
    w = jnp.concatenate(rows, axis=0)
    return jnp.pad(w, ((0, LORA_PAD - w.shape[0]), (0, 0))).astype(BF16)


def _block_diag(n, blk, value):
    i = jnp.arange(n)
    return jnp.where((i[:, None] // blk) == (i[None, :] // blk), value, 0.0)


def kernel(x, mem, positions, mix_norm, w_in_first, w_in_rest, shift_mu_first, shift_mu_rest, mla_q_norm, mla_w_uq, mla_kv_norm, mla_w_ukv, mla_out_norm, rwkv_w0, rwkv_w2, rwkv_a0, rwkv_a2, rwkv_g2, rwkv_v0, rwkv_v2, rwkv_k_k, rwkv_k_a, rwkv_r_k, rwkv_ln_w, rwkv_ln_b, w_out, ca_norm, mem_norm, ca_wq, ca_wkv, ca_wo, mlp_norm, mlp_w_up, mlp_w_down, final_norm):
    b, s, d = x.shape
    depth = mix_norm.shape[0]
    mem_len = mem.shape[1]
    t = b * s
    tm = min(512, s)
    tm_wide = min(1024, s)
    attn_key_block = min(256, s)
    wkv_rows = min(1024, s)
    rw = RWKV_WIDTH

    inv = ROPE_THETA ** (-jnp.arange(0, MLA_ROPE_DIM, 2, dtype=F32) / MLA_ROPE_DIM)
    inv_row = jnp.concatenate([jnp.zeros((MLA_NOPE_DIM,), F32), inv, inv,
                               jnp.zeros((HEAD_PAD - MLA_NOPE_DIM - MLA_ROPE_DIM,), F32)])[None, :]
    ct, st = _rope_tables(positions.astype(F32).reshape(t, 1), inv_row, tm)

    memkv = _mem_kv(mem.reshape(b * mem_len, d), mem_norm[None, :], ca_wkv.astype(BF16),
                    min(512, b * mem_len))

    woa = w_out[:, :MLA_HEADS * MLA_V_DIM].astype(BF16)
    wob = w_out[:, MLA_HEADS * MLA_V_DIM:].astype(BF16)
    wq_b = ca_wq.astype(BF16)
    wo_b = ca_wo.astype(BF16)
    wu_b = mlp_w_up.astype(BF16)
    wd_b = mlp_w_down.astype(BF16)
    ca_g = ca_norm[:, None, :]
    mlp_g = mlp_norm[:, None, :]
    bd512 = _block_diag(rw, RWKV_HEAD, 1.0).astype(BF16)

    h = x.reshape(t, d)
    v_first = None
    for l in range(depth):
        if l == 0:
            w_in, mu, v0, v2 = w_in_first, shift_mu_first, None, None
        else:
            w_in, mu, v0, v2 = w_in_rest[l - 1], shift_mu_rest[l - 1], rwkv_v0[l - 1], rwkv_v2[l - 1]
        w_all, mu_p = _in_weights(w_in, mu)
        wqa, wqb, wka, pb, wvt = _mla_weights(mla_w_uq[l], mla_w_ukv[l])
        gq = _pad_cols(mla_q_norm[l][None, :], MLA_Q_PAD)
        w2c = _lora_weights(rwkv_w2[l], rwkv_a2[l], rwkv_g2[l], v2)
        vec = lambda a: a.reshape(1, rw)
        q, k, vt, r_, k_, v_, kk_, kka_, lw_, g_ = _layer_in(
            h, mix_norm[l][None, :], w_all, mu_p, ct, st, gq, mla_kv_norm[l][None, :],
            wqa, wqb, wka, pb, wvt, v_first, w2c, bd512, vec(rwkv_w0[l]), vec(rwkv_a0[l]),
            vec(rwkv_k_k[l]), vec(rwkv_k_a[l]), None if v0 is None else vec(v0), s, tm_wide)
        if l == 0:
            v_first = v_

        att = _mla_attention(q, k, vt, mla_out_norm[l][None, :], s, attn_key_block,
                             ATTN_HEADS_PER_STEP)

        sh = lambda a: a.reshape(b, s, rw)
        y = _wkv(sh(r_), sh(k_), sh(v_), sh(kk_), sh(kka_), sh(lw_), sh(g_),
                 vec(rwkv_r_k[l]), vec(rwkv_ln_w[l]), vec(rwkv_ln_b[l]), wkv_rows, WKV_MODES)

        h = _mix_out(h, att.reshape(t, -1), y.reshape(t, rw), woa, wob, ca_g, wq_b, memkv, wo_b,
                     l, s, mem_len, tm_wide)
        h = _mlp(h, mlp_g, wu_b, wd_b, final_norm[None, :], l, l == depth - 1, tm_wide)
    return h.reshape(b, s, d)
```

```python
import functools
import math

import jax
import jax.numpy as jnp
from jax import lax
from jax.experimental import pallas as pl
from jax.experimental.pallas import tpu as pltpu

F32 = jnp.float32
BF16 = jnp.bfloat16

CHUNK = 64
NORM_EPS = 1e-6
MLA_HEADS = 8
MLA_V_DIM = 64
MLA_NOPE_DIM = 64
MLA_ROPE_DIM = 32
MLA_Q_RANK = 192
MLA_KV_RANK = 128
ROPE_THETA = 10000.0
RWKV_HEAD = 64
RWKV_HEADS = 8
RWKV_WIDTH = 512
DECAY_RANK = 32
ICL_RANK = 32
GATE_RANK = 96
VRES_RANK = 32
GN_EPS = 64e-5
CA_HEADS = 4
MLA_COLS = MLA_Q_RANK + MLA_KV_RANK + MLA_ROPE_DIM

LANE = 128
HEAD_PAD = 128
MLA_C_PAD = 512
MLA_Q_PAD = 256
MLA_KV_END = MLA_Q_PAD + MLA_KV_RANK
RW_C_PAD = 1792
LORA_PAD = 256
WKV_CHUNK = 64
WKV_GROUP = 4
WKV_AHEAD_STAGES = 2
LAYER_IN_PART_ROWS = 256
ATTN_HEADS_PER_STEP = 4
ATTN_ONES_ROWS = 16
LOG2E = 1.4426950408889634
NEG_BIG = -1e30
VMEM_LIMIT = 56 * 1024 * 1024

HI = lax.Precision.HIGHEST
WKV_MODES = ("bf16", "bf16", "bf16")


def _dot(a, b, precision=None):
    return jnp.dot(a, b, preferred_element_type=F32, precision=precision)


def _dot_nt(a, b, precision=None):
    return lax.dot_general(a, b, (((1,), (1,)), ((), ())),
                           preferred_element_type=F32, precision=precision)


def _rms(x, g, n, eps=NORM_EPS):
    ms = jnp.sum(x * x, axis=-1, keepdims=True) * (1.0 / n)
    return x * lax.rsqrt(ms + eps) * g


def _sigmoid(x):
    return 0.5 + 0.5 * jnp.tanh(0.5 * x)


def _params(*sem):
    return pltpu.CompilerParams(dimension_semantics=sem, vmem_limit_bytes=VMEM_LIMIT)


def _full(shape):
    nd = len(shape)
    return pl.BlockSpec(shape, lambda *_: (0,) * nd, pipeline_mode=pl.Buffered(1))


def _layer_slice(a, layer):
    nd = a.ndim
    return pl.BlockSpec((1,) + a.shape[1:], lambda i: (layer,) + (0,) * (nd - 1),
                        pipeline_mode=pl.Buffered(1))


def _rope_kernel(pos_ref, inv_ref, ct_ref, st_ref):
    ang = pos_ref[...] * inv_ref[...]
    c = jnp.cos(ang)
    s = jnp.sin(ang)
    lane = lax.broadcasted_iota(jnp.int32, ang.shape, 1)
    half = MLA_NOPE_DIM + MLA_ROPE_DIM // 2
    end = MLA_NOPE_DIM + MLA_ROPE_DIM
    ct_ref[...] = jnp.where(lane < MLA_NOPE_DIM, 1.0, jnp.where(lane < end, c, 0.0))
    st_ref[...] = jnp.where(lane < MLA_NOPE_DIM, 0.0,
                            jnp.where(lane < half, -s, jnp.where(lane < end, s, 0.0)))


def _rope_tables(pos_f, inv_row, tm):
    t = pos_f.shape[0]
    return pl.pallas_call(
        _rope_kernel,
        grid=(t // tm,),
        in_specs=[pl.BlockSpec((tm, 1), lambda i: (i, 0)), _full((1, LANE))],
        out_specs=[pl.BlockSpec((tm, LANE), lambda i: (i, 0))] * 2,
        out_shape=[jax.ShapeDtypeStruct((t, LANE), F32)] * 2,
        compiler_params=_params("parallel"),
        name="rope_tables",
    )(pos_f, inv_row)


def _mla_latents(c, ct_ref, st_ref, gq_ref, gkv_ref, wqa_ref, wqb_ref, wka_ref, pb_ref,
                 wv_ref, ones_ref, q_ref, k_ref, v_ref, q_scale):
    nq = _rms(c[:, :MLA_Q_PAD], gq_ref[...], MLA_Q_RANK).astype(BF16)
    nkv = _rms(c[:, MLA_Q_PAD:MLA_KV_END], gkv_ref[...], MLA_KV_RANK).astype(BF16)
    kr = c[:, MLA_KV_END:MLA_C_PAD].astype(BF16)
    ct = jnp.tile(ct_ref[...], (1, MLA_HEADS))
    st = jnp.tile(st_ref[...], (1, MLA_HEADS))
    q = _dot(nq, wqa_ref[...]) * ct + _dot(nq, wqb_ref[...]) * st
    q_ref[...] = (q * q_scale).astype(BF16)
    x = jnp.concatenate([nkv, kr], axis=1)
    k = _dot(x, wka_ref[...]) * ct + _dot(kr, pb_ref[...]) * st
    k_ref[...] = k.astype(BF16)
    v_ref[...] = (_dot_nt(wv_ref[...], nkv) + ones_ref[...]).astype(BF16)


def _rwkv_elementwise(x, vf_ref, w2_ref, bd_ref, w0_ref, a0_ref, kk_ref_p, ka_ref, v0_ref,
                      r_o, k_o, v_o, kk_o, kka_o, lw_o, g_o):
    rw = RWKV_WIDTH
    r_o[...] = x[:, 0:rw].astype(BF16)
    k = x[:, rw:2 * rw]
    v = x[:, 2 * rw:3 * rw]
    tail = x[:, 3 * rw:3 * rw + LORA_PAD]
    lane = lax.broadcasted_iota(jnp.int32, tail.shape, 1)
    g_lo = DECAY_RANK + ICL_RANK
    g_hi = g_lo + GATE_RANK
    is_gate = (lane >= g_lo) & (lane < g_hi)
    th = jnp.tanh(jnp.where(is_gate, 0.5 * tail, tail))
    act = jnp.where(lane < DECAY_RANK, th, jnp.where(is_gate, 0.5 + 0.5 * th, tail))
    lo = _dot(act.astype(BF16), w2_ref[...])
    lw_o[...] = -math.exp(-0.5) * _sigmoid(w0_ref[...] + lo[:, 0:rw])
    a = _sigmoid(a0_ref[...] + lo[:, rw:2 * rw])
    g_o[...] = lo[:, 2 * rw:3 * rw].astype(BF16)
    if vf_ref is not None:
        mix = _sigmoid(v0_ref[...] + lo[:, 3 * rw:4 * rw])
        v = v + (vf_ref[...].astype(F32) - v) * mix
    v_o[...] = v.astype(BF16)
    kk = k * kk_ref_p[...]
    sq = kk * kk
    sq_hi = sq.astype(BF16)
    sq_lo = (sq - sq_hi.astype(F32)).astype(BF16)
    ss = _dot(sq_hi, bd_ref[...]) + _dot(sq_lo, bd_ref[...])
    kk = kk * lax.rsqrt(jnp.maximum(ss, 1e-24))
    kk_o[...] = kk.astype(BF16)
    kka_o[...] = (kk * a).astype(BF16)
    k_o[...] = (k * (1.0 + (a - 1.0) * ka_ref[...])).astype(BF16)


def _layer_in_kernel(*refs, first, tiles_per_seq, q_scale):
    n_in = 20 if first else 22
    ins, outs, carry_ref = refs[:n_in], refs[n_in:n_in + 10], refs[n_in + 10]
    h_ref, g_ref, w_ref, mu_ref = ins[:4]
    mla_ins = ins[4:14]
    if first:
        w2_ref, bd_ref, w0_ref, a0_ref, kkp_ref, ka_ref = ins[14:20]
        vf_ref = v0_ref = None
    else:
        vf_ref, w2_ref, bd_ref, w0_ref, a0_ref, kkp_ref, ka_ref, v0_ref = ins[14:22]
    q_ref, k_ref, vt_ref = outs[:3]
    rw_outs = outs[3:]

    i = pl.program_id(0)
    tm, d = h_ref.shape
    hr = min(LAYER_IN_PART_ROWS, tm)

    @pl.when(i % tiles_per_seq == 0)
    def _():
        carry_ref[...] = jnp.zeros_like(carry_ref)

    parts = [pl.ds(p * hr, hr) for p in range(tm // hr)]
    hn = [_rms(h_ref[rows, :], g_ref[...], d).astype(BF16) for rows in parts]
    proj = [_dot(x, w_ref[...]) for x in hn]
    last = carry_ref[7:8, :]
    for rows, pr in zip(parts, proj):
        ct_v, st_v = mla_ins[0].at[rows], mla_ins[1].at[rows]
        _mla_latents(pr[:, :MLA_C_PAD], ct_v, st_v, *mla_ins[2:], q_ref.at[rows], k_ref.at[rows],
                     vt_ref.at[:, rows], q_scale)
        z = pr[:, MLA_C_PAD:]
        prev = pltpu.roll(z, 1, axis=0)
        row = lax.broadcasted_iota(jnp.int32, z.shape, 0)
        prev = jnp.where(row == 0, last, prev)
        x = z + (prev - z) * mu_ref[...]
        last = z[hr - 1:hr, :]
        _rwkv_elementwise(x, None if vf_ref is None else vf_ref.at[rows], w2_ref, bd_ref, w0_ref,
                          a0_ref, kkp_ref, ka_ref, v0_ref, *[o.at[rows] for o in rw_outs])
    carry_ref[...] = z[hr - 8:, :]


def _layer_in(h, g, w, mu, ct, st, gq, gkv, wqa, wqb, wka, pb, wvt, vf, w2, bd, w0, a0, k_k,
              k_a, v0, seq, tm):
    t, d = h.shape
    rw = RWKV_WIDTH
    hw = MLA_HEADS * HEAD_PAD
    vw = wvt.shape[0]
    first = vf is None
    ones_col = (jnp.arange(vw) % (MLA_V_DIM + ATTN_ONES_ROWS) == MLA_V_DIM).astype(F32)[:, None]
    q_scale = (MLA_NOPE_DIM + MLA_ROPE_DIM) ** -0.5 * LOG2E
    row = lambda width: pl.BlockSpec((tm, width), lambda i: (i, 0))
    vec = _full((1, rw))
    args = [h, g, w, mu, ct, st, gq, gkv, wqa, wqb, wka, pb, wvt, ones_col]
    in_specs = [row(d), _full((1, d)), _full(w.shape), _full((1, RW_C_PAD)), row(LANE), row(LANE),
                _full(gq.shape), _full(gkv.shape), _full(wqa.shape), _full(wqb.shape),
                _full(wka.shape), _full(pb.shape), _full(wvt.shape), _full(ones_col.shape)]
    if first:
        args += [w2, bd, w0, a0, k_k, k_a]
        in_specs += [_full(w2.shape), _full(bd.shape), vec, vec, vec, vec]
    else:
        args += [vf, w2, bd, w0, a0, k_k, k_a, v0]
        in_specs += [row(rw), _full(w2.shape), _full(bd.shape), vec, vec, vec, vec, vec]
    rw_dt = [BF16, BF16, BF16, BF16, BF16, F32, BF16]
    kern = functools.partial(_layer_in_kernel, first=first, tiles_per_seq=seq // tm,
                             q_scale=q_scale)
    return pl.pallas_call(
        kern,
        grid=(t // tm,),
        in_specs=in_specs,
        out_specs=[row(hw), row(hw), pl.BlockSpec((vw, tm), lambda i: (0, i))] + [row(rw)] * 7,
        out_shape=[jax.ShapeDtypeStruct((t, hw), BF16), jax.ShapeDtypeStruct((t, hw), BF16),
                   jax.ShapeDtypeStruct((vw, t), BF16)]
                  + [jax.ShapeDtypeStruct((t, rw), dt) for dt in rw_dt],
        scratch_shapes=[pltpu.VMEM((8, RW_C_PAD), F32)],
        compiler_params=_params("arbitrary"),
        name="layer_in",
    )(*args)


def _attn_kernel(q_ref, k_ref, vt_ref, g_ref, o_ref, m_ref, a_ref, p_ref, acc_ref, s_ref, s2_ref,
                 *, tq, tk, nh):
    qi = pl.program_id(2)
    dv = MLA_V_DIM
    heads = range(nh)

    ext = dv + ATTN_ONES_ROWS

    m_ref[...] = jnp.full_like(m_ref, NEG_BIG)
    acc_ref[...] = jnp.zeros_like(acc_ref)

    def flush_values(j, q0=0):
        off = pl.multiple_of(j * tk, tk)
        pv = [_dot(vt_ref[h * ext:(h + 1) * ext, pl.ds(off, tk)], p_ref[h, :, q0:])
              for h in heads]
        for h in heads:
            acc_ref[h, :, q0:] = a_ref[h, :, q0:] * acc_ref[h, :, q0:] + pv[h]

    def scores(j, h, dst, q0=0):
        off = pl.multiple_of(j * tk, tk)
        dst[h, :, q0:] = _dot_nt(k_ref[0, pl.ds(off, tk), h * HEAD_PAD:(h + 1) * HEAD_PAD],
                                 q_ref[0, q0:, h * HEAD_PAD:(h + 1) * HEAD_PAD])

    def softmax(h, src, key0=None, q0=0):
        s = src[h, :, q0:]
        if key0 is not None:
            ck = (lax.broadcasted_iota(jnp.int32, s.shape, 0) + key0) // CHUNK
            cq = (lax.broadcasted_iota(jnp.int32, s.shape, 1) + q0) // CHUNK
            s = jnp.where(ck <= cq, s, NEG_BIG)
        m_old = m_ref[h, :, q0:]
        m_new = jnp.maximum(m_old, jnp.max(s, axis=0, keepdims=True))
        p_ref[h, :, q0:] = jnp.exp2((s - m_new).astype(BF16))
        a_ref[h, :, q0:] = jnp.exp2(m_old - m_new)
        m_ref[h, :, q0:] = m_new

    def stage(j, src, dst, key0=None, q0=0, next_q0=0, first=False):
        if not first:
            flush_values(j - 1)
        for h in heads:
            if dst is not None:
                scores(j + 1, h, dst, next_q0)
            softmax(h, src, key0, q0)

    for h in heads:
        scores(0, h, s_ref)

    @pl.when(qi > 0)
    def _():
        stage(0, s_ref, s2_ref, first=True)
        stage(1, s2_ref, s_ref)

    @pl.loop(1, qi)
    def _(t):
        stage(2 * t, s_ref, s2_ref)
        stage(2 * t + 1, s2_ref, s_ref)

    @pl.when(qi > 0)
    def _():
        stage(2 * qi, s_ref, s2_ref, key0=0, next_q0=tk)
        stage(2 * qi + 1, s2_ref, None, key0=tk, q0=tk)

    @pl.when(qi == 0)
    def _():
        stage(0, s_ref, s2_ref, key0=0, next_q0=tk, first=True)
        stage(1, s2_ref, None, key0=tk, q0=tk)

    flush_values(2 * qi + 1, q0=tk)
    outs = []
    for h in heads:
        acc = acc_ref[h]
        o = acc[:dv] / acc[dv:dv + 1]
        ms = jnp.sum(o * o, axis=0, keepdims=True) * (1.0 / dv)
        outs.append(o * lax.rsqrt(ms + NORM_EPS))
    o_t = jnp.concatenate(outs, axis=0)
    o_ref[0] = (o_t.T * g_ref[...]).astype(BF16)


def _mla_attention(q, k, vt, g, seq, tk, nh):
    tq = 2 * tk
    b = q.shape[0] // seq
    hw = q.shape[1]
    groups = hw // (nh * HEAD_PAD)
    ow = nh * MLA_V_DIM
    q3 = q.reshape(b, seq, hw)
    k3 = k.reshape(b, seq, hw)
    return pl.pallas_call(
        functools.partial(_attn_kernel, tq=tq, tk=tk, nh=nh),
        grid=(b, groups, seq // tq),
        in_specs=[pl.BlockSpec((1, tq, nh * HEAD_PAD), lambda bi, h, i: (bi, i, h)),
                  pl.BlockSpec((1, seq, nh * HEAD_PAD), lambda bi, h, i: (bi, 0, h)),
                  pl.BlockSpec((nh * (MLA_V_DIM + ATTN_ONES_ROWS), seq), lambda bi, h, i: (h, bi)),
                  pl.BlockSpec((1, ow), lambda bi, h, i: (0, h))],
        out_specs=pl.BlockSpec((1, tq, ow), lambda bi, h, i: (bi, i, h)),
        out_shape=jax.ShapeDtypeStruct((b, seq, groups * ow), BF16),
        scratch_shapes=[pltpu.VMEM((nh, 1, tq), F32), pltpu.VMEM((nh, 1, tq), F32),
                        pltpu.VMEM((nh, tk, tq), BF16),
                        pltpu.VMEM((nh, MLA_V_DIM + ATTN_ONES_ROWS, tq), F32),
                        pltpu.VMEM((nh, tk, tq), F32), pltpu.VMEM((nh, tk, tq), F32)],
        compiler_params=_params("parallel", "parallel", "arbitrary"),
        name="mla_attention",
    )(q3, k3, vt, g)


_DIMS = {"nn": (((1,), (0,)), ((), ())), "nt": (((1,), (1,)), ((), ())),
         "tn": (((0,), (0,)), ((), ()))}


def _split(x):
    hi = x.astype(BF16)
    return hi, (x - hi.astype(F32)).astype(BF16)


def _mm(a, b, mode, dims="nn"):
    dg = lambda x, y: lax.dot_general(x, y, _DIMS[dims], preferred_element_type=F32)
    if mode == "bf16":
        return dg(a.astype(BF16), b.astype(BF16))
    if mode == "x3":
        a_hi, a_lo = _split(a)
        b_hi, b_lo = _split(b)
        return dg(a_hi, b_hi) + (dg(a_lo, b_hi) + dg(a_hi, b_lo))
    if mode == "ax":
        n = b.shape[1]
        out = dg(a.astype(BF16), jnp.concatenate(_split(b), axis=1))
        return out[:, :n] + out[:, n:]
    return lax.dot_general(a, b, _DIMS[dims], preferred_element_type=F32, precision=HI)


def _unit_lower_solve_stages(a_list, rhs_list, row, col, eye, mode, out):
    d16 = (row // 16) == (col // 16)
    d32 = (row // 32) == (col // 32)
    n = range(len(a_list))
    d = [jnp.where(d16, a, 0.0) for a in a_list]
    x = [eye + d[i] for i in n]
    d2 = [_mm(d[i], d[i], mode) for i in n]
    yield
    t = [_mm(d2[i], jnp.concatenate([x[i], d2[i]], axis=1), mode) for i in n]
    x = [x[i] + t[i][:, :128] for i in n]
    d4 = [t[i][:, 128:] for i in n]
    yield
    t = [_mm(d4[i], jnp.concatenate([x[i], d4[i]], axis=1), mode) for i in n]
    x = [x[i] + t[i][:, :128] for i in n]
    d8 = [t[i][:, 128:] for i in n]
    yield
    t = [_mm(d8[i], x[i], mode) for i in n]
    x = [x[i] + t[i] for i in n]
    yield
    e = [jnp.where(d32, a_list[i], 0.0) - d[i] for i in n]
    t = [_mm(e[i], x[i], mode) for i in n]
    yield
    t = [_mm(x[i], t[i], mode) for i in n]
    x = [x[i] + t[i] for i in n]
    yield
    f = [jnp.where(d32, 0.0, a_list[i]) for i in n]
    z = [_mm(x[i], rhs_list[i], mode) for i in n]
    yield
    t = [_mm(f[i], z[i], mode) for i in n]
    yield
    t = [_mm(x[i], t[i], mode) for i in n]
    out.extend(z[i] + t[i] for i in n)
    yield


def _wkv_kernel(r_ref, k_ref, v_ref, kk_ref, kka_ref, lw_ref, g_ref,
                rk_ref, lnw_ref, lnb_ref, o_ref, s_ref, *, nch, modes):
    m_quad, m_inv, m_state = modes
    c_len = WKV_CHUNK

    npair = s_ref.shape[0]

    @pl.when(pl.program_id(1) == 0)
    def _():
        s_ref[...] = jnp.zeros_like(s_ref)

    row = lax.broadcasted_iota(jnp.int32, (128, 128), 0)
    col = lax.broadcasted_iota(jnp.int32, (128, 128), 1)
    same = (row // c_len) == (col // c_len)
    mask_s = same & (col < row)
    mask_i = same & (col <= row)
    eye = jnp.where(row == col, 1.0, 0.0)
    lane0 = lax.broadcasted_iota(jnp.int32, (c_len, 128), 1) < RWKV_HEAD
    tr = lax.broadcasted_iota(jnp.int32, (c_len, c_len), 0)
    tc = lax.broadcasted_iota(jnp.int32, (c_len, c_len), 1)
    ltri = jnp.where(tc <= tr, 1.0, 0.0)

    def stack(x):
        return jnp.concatenate([jnp.where(lane0, x, 0.0), jnp.where(lane0, 0.0, x)], axis=0)

    def head_sums(x):
        first = jnp.sum(jnp.where(lane0, x, 0.0), axis=-1, keepdims=True)
        both = jnp.sum(x, axis=-1, keepdims=True)
        return jnp.where(lane0, first, both - first)

    pairs = range(npair)
    par = {}
    wus = {}

    def independent_part(chunks):
        keys = [(c, hp) for c in chunks for hp in pairs]
        for c, hp in keys:
            sl = pl.ds(c * c_len, c_len)
            ln = slice(hp * LANE, (hp + 1) * LANE)
            r = r_ref[0, sl, ln].astype(F32)
            k = k_ref[0, sl, ln].astype(F32)
            v = v_ref[0, sl, ln].astype(F32)
            kk = kk_ref[0, sl, ln].astype(F32)
            kka = kka_ref[0, sl, ln].astype(F32)
            lw = lw_ref[0, sl, ln]
            cum = _mm(ltri, lw, "ax")
            tot = cum[c_len - 1:c_len, :]
            p_in = jnp.exp(cum)
            p_ex = jnp.exp(cum - lw)
            p_inv = jnp.exp(-cum)
            p_end = jnp.exp(tot - cum)
            at = stack(-kk * p_ex)
            rt = stack(r * p_in)
            bt = kka * p_inv
            kt = k * p_inv
            par[c, hp] = dict(
                r=r, k=k, v=v, tot=tot, at=at, rt=rt, vs=stack(v),
                bkh=jnp.concatenate([stack(kka * p_end), stack(k * p_end)], axis=0),
                lhs=jnp.concatenate([at, rt], axis=0),
                rhs=jnp.concatenate([bt, bt, kt, kt], axis=0))
        yield
        for key in keys:
            p = par[key]
            quad = _mm(p["lhs"], p["rhs"], m_quad, "nt")
            p["a_ab"] = jnp.where(mask_s, quad[:128, :128], 0.0)
            p["a_ak"] = jnp.where(mask_s, quad[:128, 128:], 0.0)
            p["a_r"] = jnp.concatenate([jnp.where(mask_i, quad[128:, :128], 0.0),
                                        jnp.where(mask_i, quad[128:, 128:], 0.0)], axis=1)
        yield
        for key in keys:
            par[key]["akv"] = _mm(par[key]["a_ak"], par[key]["vs"], m_quad)
        yield
        solved = []
        yield from _unit_lower_solve_stages(
            [par[key]["a_ab"] for key in keys],
            [jnp.concatenate([par[key]["at"], par[key]["akv"]], axis=1) for key in keys],
            row, col, eye, m_inv, solved)
        wus.update(zip(keys, solved))

    state = [s_ref[hp] for hp in pairs]

    def outputs(c, m1, uv):
        sl = pl.ds(c * c_len, c_len)
        pc = [par[c, hp] for hp in pairs]
        ys = [m1[hp][128:] + _mm(pc[hp]["a_r"], uv[hp], m_state) for hp in pairs]
        y = [ys[hp][:c_len] + ys[hp][c_len:] for hp in pairs]
        mean = [head_sums(y[hp]) * (1.0 / RWKV_HEAD) for hp in pairs]
        yc = [y[hp] - mean[hp] for hp in pairs]
        var = [head_sums(yc[hp] * yc[hp]) * (1.0 / RWKV_HEAD) for hp in pairs]
        for hp in pairs:
            ln = slice(hp * LANE, (hp + 1) * LANE)
            r, k, v = pc[hp]["r"], pc[hp]["k"], pc[hp]["v"]
            yn = yc[hp] * lax.rsqrt(var[hp] + GN_EPS) * lnw_ref[:, ln] + lnb_ref[:, ln]
            bonus = head_sums(r * k * rk_ref[:, ln])
            yn = yn + bonus * v
            o_ref[0, sl, ln] = (yn * g_ref[0, sl, ln].astype(F32)).astype(BF16)

    pending = []

    def state_part(chunks):
        for c in chunks:
            pc = [par[c, hp] for hp in pairs]
            wu = [wus[c, hp] for hp in pairs]
            m1 = [_mm(jnp.concatenate([wu[hp][:, :128], pc[hp]["rt"]], axis=0), state[hp],
                      m_state, "nt") for hp in pairs]
            uv = [jnp.concatenate([m1[hp][:128] + wu[hp][:, 128:], pc[hp]["vs"]], axis=0)
                  for hp in pairs]
            upd = [_mm(uv[hp], pc[hp]["bkh"], m_state, "tn") for hp in pairs]
            for hp in pairs:
                state[hp] = state[hp] * jnp.exp(pc[hp]["tot"]) + upd[hp]
            yield
            if pending:
                outputs(*pending.pop())
                yield
            pending.append((c, m1, uv))

    groups = [list(range(g, min(g + WKV_GROUP, nch))) for g in range(0, nch, WKV_GROUP)]
    for _ in independent_part(groups[0]):
        pass
    for g, chunks in enumerate(groups):
        chain = state_part(chunks)
        ahead = independent_part(groups[g + 1]) if g + 1 < len(groups) else iter(())
        for _ in chain:
            for _ in range(WKV_AHEAD_STAGES):
                next(ahead, None)
        for _ in ahead:
            pass
    outputs(*pending.pop())
    for hp in pairs:
        s_ref[hp] = state[hp]


def _wkv(r, k, v, kk, kka, lw, g, r_k, ln_w, ln_b, rows, modes):
    b, s, w = r.shape
    blk = pl.BlockSpec((1, rows, w), lambda bi, i: (bi, i, 0))
    kern = functools.partial(_wkv_kernel, nch=rows // WKV_CHUNK, modes=modes)
    return pl.pallas_call(
        kern,
        grid=(b, s // rows),
        in_specs=[blk] * 7 + [_full((1, w))] * 3,
        out_specs=blk,
        out_shape=jax.ShapeDtypeStruct((b, s, w), BF16),
        scratch_shapes=[pltpu.VMEM((w // LANE, LANE, LANE), F32)],
        compiler_params=_params("parallel", "arbitrary"),
        name="wkv7",
    )(r, k, v, kk, kka, lw, g, r_k, ln_w, ln_b)


def _mem_kv_kernel(m_ref, g_ref, w_ref, o_ref):
    d = m_ref.shape[-1]
    mn = _rms(m_ref[...], g_ref[...], d).astype(BF16)
    o_ref[...] = _dot(mn, w_ref[0]).astype(BF16)


def _mem_kv(mem2d, g, wkv, tm):
    t, d = mem2d.shape
    nl, _, n = wkv.shape
    return pl.pallas_call(
        _mem_kv_kernel,
        grid=(nl, t // tm),
        in_specs=[pl.BlockSpec((tm, d), lambda l, i: (i, 0)), _full((1, d)),
                  pl.BlockSpec((1, d, n), lambda l, i: (l, 0, 0))],
        out_specs=pl.BlockSpec((tm, n), lambda l, i: (i, l)),
        out_shape=jax.ShapeDtypeStruct((t, nl * n), BF16),
        compiler_params=_params("parallel", "parallel"),
        name="mem_kv",
    )(mem2d, g, wkv)


def _mix_out_kernel(h_ref, att_ref, rw_ref, woa_ref, wob_ref, g_ref, wq_ref,
                    mk_ref, mv_ref, wo_ref, o_ref, *, q_scale):
    d = h_ref.shape[-1]
    h1 = h_ref[...] + _dot(att_ref[...], woa_ref[0]) + _dot(rw_ref[...], wob_ref[0])
    hn = _rms(h1, g_ref[0], d).astype(BF16)
    q = (_dot(hn, wq_ref[0]) * q_scale).astype(BF16)
    dh = d // CA_HEADS
    outs = []
    for hd in range(CA_HEADS):
        sl = slice(hd * dh, (hd + 1) * dh)
        s = _dot_nt(q[:, sl], mk_ref[:, sl])
        m = jnp.max(s, axis=-1, keepdims=True)
        p = jnp.exp2(s - m)
        l = jnp.sum(p, axis=-1, keepdims=True)
        outs.append((_dot(p.astype(BF16), mv_ref[:, sl]) / l).astype(BF16))
    o = jnp.concatenate(outs, axis=1)
    o_ref[...] = h1 + _dot(o, wo_ref[0])


def _mix_out(h, att, rw, woa, wob, g, wq, memkv, wo, layer, seq, mem_len, tm):
    t, d = h.shape
    tiles = seq // tm
    q_scale = (d // CA_HEADS) ** -0.5 * LOG2E
    row = lambda w: pl.BlockSpec((tm, w), lambda i: (i, 0))
    lay = functools.partial(_layer_slice, layer=layer)
    return pl.pallas_call(
        functools.partial(_mix_out_kernel, q_scale=q_scale),
        grid=(t // tm,),
        in_specs=[row(d), row(att.shape[1]), row(rw.shape[1]), lay(woa), lay(wob), lay(g),
                  lay(wq),
                  pl.BlockSpec((mem_len, d), lambda i: (i // tiles, 2 * layer)),
                  pl.BlockSpec((mem_len, d), lambda i: (i // tiles, 2 * layer + 1)),
                  lay(wo)],
        out_specs=row(d),
        out_shape=jax.ShapeDtypeStruct((t, d), F32),
        compiler_params=_params("parallel"),
        name="mix_out_cross_attention",
    )(h, att, rw, woa, wob, g, wq, memkv, memkv, wo)


def _mlp_kernel(h_ref, g_ref, wu_ref, wd_ref, gf_ref, o_ref, *, last, ff_chunk):
    d = h_ref.shape[-1]
    h = h_ref[...]
    hn = _rms(h, g_ref[0], d).astype(BF16)
    acc = h
    for c in range(wu_ref.shape[-1] // ff_chunk):
        sl = slice(c * ff_chunk, (c + 1) * ff_chunk)
        u = jnp.maximum(_dot(hn, wu_ref[0, :, sl]), 0.0)
        acc = acc + _dot((u * u).astype(BF16), wd_ref[0, sl, :])
    if last:
        acc = _rms(acc, gf_ref[...], d)
    o_ref[...] = acc


def _mlp(h, g, wu, wd, gf, layer, last, tm):
    t, d = h.shape
    row = pl.BlockSpec((tm, d), lambda i: (i, 0))
    lay = functools.partial(_layer_slice, layer=layer)
    return pl.pallas_call(
        functools.partial(_mlp_kernel, last=last, ff_chunk=1024),
        grid=(t // tm,),
        in_specs=[row, lay(g), lay(wu), lay(wd), _full((1, d))],
        out_specs=row,
        out_shape=jax.ShapeDtypeStruct((t, d), F32),
        compiler_params=_params("parallel"),
        name="mlp",
    )(h, g, wu, wd, gf)


def _pad_cols(a, n):
    return jnp.pad(a, ((0, 0), (0, n - a.shape[1])))


def _swap_halves(a):
    half = a.shape[-1] // 2
    return jnp.concatenate([a[..., half:], a[..., :half]], axis=-1)


def _in_weights(w_in, mu):
    cq = w_in[:, :MLA_Q_RANK]
    ckv = w_in[:, MLA_Q_RANK:MLA_Q_RANK + MLA_KV_RANK]
    kr = w_in[:, MLA_Q_RANK + MLA_KV_RANK:MLA_COLS]
    mla = jnp.concatenate([_pad_cols(cq, MLA_Q_PAD), ckv, kr, _swap_halves(kr)], axis=1)
    w = jnp.concatenate([_pad_cols(mla, MLA_C_PAD), _pad_cols(w_in[:, MLA_COLS:], RW_C_PAD)], axis=1)
    return w.astype(BF16), _pad_cols(mu[None, :], RW_C_PAD)


def _head_pad(a, n_heads, per_head, lo, hi, at):
    rows = a.shape[0]
    a = a.reshape(rows, n_heads, per_head)[:, :, lo:hi]
    a = jnp.pad(a, ((0, 0), (0, 0), (at, HEAD_PAD - at - (hi - lo))))
    return a.reshape(rows, n_heads * HEAD_PAD)


def _mla_weights(w_uq, w_ukv):
    dq = MLA_NOPE_DIM + MLA_ROPE_DIM
    h = MLA_HEADS
    wq3 = w_uq.reshape(MLA_Q_RANK, h, dq)
    rope_sw = _swap_halves(wq3[:, :, MLA_NOPE_DIM:]).reshape(MLA_Q_RANK, h * MLA_ROPE_DIM)
    wqa = _head_pad(w_uq, h, dq, 0, dq, 0)
    wqb = _head_pad(rope_sw, h, MLA_ROPE_DIM, 0, MLA_ROPE_DIM, MLA_NOPE_DIM)
    pad_q = ((0, MLA_Q_PAD - MLA_Q_RANK), (0, 0))
    wqa = jnp.pad(wqa, pad_q)
    wqb = jnp.pad(wqb, pad_q)
    dkv = MLA_NOPE_DIM + MLA_V_DIM
    wk = _head_pad(w_ukv, h, dkv, 0, MLA_NOPE_DIM, 0)
    wvt = jnp.transpose(w_ukv.reshape(MLA_KV_RANK, h, dkv)[:, :, MLA_NOPE_DIM:], (1, 2, 0))
    wvt = jnp.pad(wvt, ((0, 0), (0, ATTN_ONES_ROWS), (0, 0))).reshape(-1, MLA_KV_RANK)
    eye = jnp.eye(MLA_ROPE_DIM, dtype=F32)
    place = _head_pad(jnp.tile(eye, (1, h)), h, MLA_ROPE_DIM, 0, MLA_ROPE_DIM, MLA_NOPE_DIM)
    zeros = jnp.zeros_like(place)
    pad_rows = jnp.zeros((LANE - 2 * MLA_ROPE_DIM, h * HEAD_PAD), F32)
    pa = jnp.concatenate([place, zeros, pad_rows], axis=0)
    pb = jnp.concatenate([zeros, place, pad_rows], axis=0)
    wka = jnp.concatenate([wk, pa], axis=0)
    return tuple(a.astype(BF16) for a in (wqa, wqb, wka, pb, wvt))


def _lora_weights(w2, a2, g2, v2):
    rw = RWKV_WIDTH
    z = lambda r: jnp.zeros((r, rw), F32)
    rows = [
        jnp.concatenate([w2, z(DECAY_RANK), z(DECAY_RANK), z(DECAY_RANK)], axis=1),
        jnp.concatenate([z(ICL_RANK), a2, z(ICL_RANK), z(ICL_RANK)], axis=1),
        jnp.concatenate([z(GATE_RANK), z(GATE_RANK), g2, z(GATE_RANK)], axis=1),
        jnp.concatenate([z(VRES_RANK)] * 3 + [v2 if v2 is not None else z(VRES_RANK)], axis=1),
    ]
    w = jnp.concatenate(rows, axis=0)
    return jnp.pad(w, ((0, LORA_PAD - w.shape[0]), (0, 0))).astype(BF16)


def _block_diag(n, blk, value):
    i = jnp.arange(n)
    return jnp.where((i[:, None] // blk) == (i[None, :] // blk), value, 0.0)


def kernel(x, mem, positions, mix_norm, w_in_first, w_in_rest, shift_mu_first, shift_mu_rest, mla_q_norm, mla_w_uq, mla_kv_norm, mla_w_ukv, mla_out_norm, rwkv_w0, rwkv_w2, rwkv_a0, rwkv_a2, rwkv_g2, rwkv_v0, rwkv_v2, rwkv_k_k, rwkv_k_a, rwkv_r_k, rwkv_ln_w, rwkv_ln_b, w_out, ca_norm, mem_norm, ca_wq, ca_wkv, ca_wo, mlp_norm, mlp_w_up, mlp_w_down, final_norm):
    b, s, d = x.shape
    depth = mix_norm.shape[0]
    mem_len = mem.shape[1]
    t = b * s
    tm = min(512, s)
    tm_wide = min(1024, s)
    attn_key_block = min(256, s)
    wkv_rows = min(1024, s)
    rw = RWKV_WIDTH

    inv = ROPE_THETA ** (-jnp.arange(0, MLA_ROPE_DIM, 2, dtype=F32) / MLA_ROPE_DIM)
    inv_row = jnp.concatenate([jnp.zeros((MLA_NOPE_DIM,), F32), inv, inv,
                               jnp.zeros((HEAD_PAD - MLA_NOPE_DIM - MLA_ROPE_DIM,), F32)])[None, :]
    ct, st = _rope_tables(positions.astype(F32).reshape(t, 1), inv_row, tm)

    memkv = _mem_kv(mem.reshape(b * mem_len, d), mem_norm[None, :], ca_wkv.astype(BF16),
                    min(512, b * mem_len))

    woa = w_out[:, :MLA_HEADS * MLA_V_DIM].astype(BF16)
    wob = w_out[:, MLA_HEADS * MLA_V_DIM:].astype(BF16)
    wq_b = ca_wq.astype(BF16)
    wo_b = ca_wo.astype(BF16)
    wu_b = mlp_w_up.astype(BF16)
    wd_b = mlp_w_down.astype(BF16)
    ca_g = ca_norm[:, None, :]
    mlp_g = mlp_norm[:, None, :]
    bd512 = _block_diag(rw, RWKV_HEAD, 1.0).astype(BF16)

    h = x.reshape(t, d)
    v_first = None
    for l in range(depth):
        if l == 0:
            w_in, mu, v0, v2 = w_in_first, shift_mu_first, None, None
        else:
            w_in, mu, v0, v2 = w_in_rest[l - 1], shift_mu_rest[l - 1], rwkv_v0[l - 1], rwkv_v2[l - 1]
        w_all, mu_p = _in_weights(w_in, mu)
        wqa, wqb, wka, pb, wvt = _mla_weights(mla_w_uq[l], mla_w_ukv[l])
        gq = _pad_cols(mla_q_norm[l][None, :], MLA_Q_PAD)
        w2c = _lora_weights(rwkv_w2[l], rwkv_a2[l], rwkv_g2[l], v2)
        vec = lambda a: a.reshape(1, rw)
        q, k, vt, r_, k_, v_, kk_, kka_, lw_, g_ = _layer_in(
            h, mix_norm[l][None, :], w_all, mu_p, ct, st, gq, mla_kv_norm[l][None, :],
            wqa, wqb, wka, pb, wvt, v_first, w2c, bd512, vec(rwkv_w0[l]), vec(rwkv_a0[l]),
            vec(rwkv_k_k[l]), vec(rwkv_k_a[l]), None if v0 is None else vec(v0), s, tm_wide)
        if l == 0:
            v_first = v_

        att = _mla_attention(q, k, vt, mla_out_norm[l][None, :], s, attn_key_block,
                             ATTN_HEADS_PER_STEP)

        sh = lambda a: a.reshape(b, s, rw)
        y = _wkv(sh(r_), sh(k_), sh(v_), sh(kk_), sh(kka_), sh(lw_), sh(g_),
                 vec(rwkv_r_k[l]), vec(rwkv_ln_w[l]), vec(rwkv_ln_b[l]), wkv_rows, WKV_MODES)

        h = _mix_out(h, att.reshape(t, -1), y.reshape(t, rw), woa, wob, ca_g, wq_b, memkv, wo_b,
                     l, s, mem_len, tm_wide)
        h = _mlp(h, mlp_g, wu_b, wd_b, final_norm[None, :], l, l == depth - 1, tm_wide)
    return h.reshape(b, s, d)
```

```python
import functools
import math

import jax
import jax.numpy as jnp
from jax import lax
from jax.experimental import pallas as pl
from jax.experimental.pallas import tpu as pltpu

F32 = jnp.float32
BF16 = jnp.bfloat16

CHUNK = 64
NORM_EPS = 1e-6
MLA_HEADS = 8
MLA_V_DIM = 64
MLA_NOPE_DIM = 64
MLA_ROPE_DIM = 32
MLA_Q_RANK = 192
MLA_KV_RANK = 128
ROPE_THETA = 10000.0
RWKV_HEAD = 64
RWKV_HEADS = 8
RWKV_WIDTH = 512
DECAY_RANK = 32
ICL_RANK = 32
GATE_RANK = 96
VRES_RANK = 32
GN_EPS = 64e-5
CA_HEADS = 4
MLA_COLS = MLA_Q_RANK + MLA_KV_RANK + MLA_ROPE_DIM

LANE = 128
HEAD_PAD = 128
MLA_C_PAD = 512
MLA_Q_PAD = 256
MLA_KV_END = MLA_Q_PAD + MLA_KV_RANK
RW_C_PAD = 1792
LORA_PAD = 256
WKV_CHUNK = 64
WKV_GROUP = 4
WKV_AHEAD_STAGES = 2
LAYER_IN_PART_ROWS = 256
ATTN_HEADS_PER_STEP = 4
ATTN_ONES_ROWS = 16
LOG2E = 1.4426950408889634
NEG_BIG = -1e30
VMEM_LIMIT = 56 * 1024 * 1024

HI = lax.Precision.HIGHEST
WKV_MODES = ("bf16", "bf16", "bf16")


def _dot(a, b, precision=None):
    return jnp.dot(a, b, preferred_element_type=F32, precision=precision)


def _dot_nt(a, b, precision=None):
    return lax.dot_general(a, b, (((1,), (1,)), ((), ())),
                           preferred_element_type=F32, precision=precision)


def _rms(x, g, n, eps=NORM_EPS):
    ms = jnp.sum(x * x, axis=-1, keepdims=True) * (1.0 / n)
    return x * lax.rsqrt(ms + eps) * g


def _sigmoid(x):
    return 0.5 + 0.5 * jnp.tanh(0.5 * x)


def _params(*sem):
    return pltpu.CompilerParams(dimension_semantics=sem, vmem_limit_bytes=VMEM_LIMIT)


def _full(shape):
    nd = len(shape)
    return pl.BlockSpec(shape, lambda *_: (0,) * nd, pipeline_mode=pl.Buffered(1))


def _layer_slice(a, layer):
    nd = a.ndim
    return pl.BlockSpec((1,) + a.shape[1:], lambda i: (layer,) + (0,) * (nd - 1),
                        pipeline_mode=pl.Buffered(1))


def _rope_kernel(pos_ref, inv_ref, ct_ref, st_ref):
    ang = pos_ref[...] * inv_ref[...]
    c = jnp.cos(ang)
    s = jnp.sin(ang)
    lane = lax.broadcasted_iota(jnp.int32, ang.shape, 1)
    half = MLA_NOPE_DIM + MLA_ROPE_DIM // 2
    end = MLA_NOPE_DIM + MLA_ROPE_DIM
    ct_ref[...] = jnp.where(lane < MLA_NOPE_DIM, 1.0, jnp.where(lane < end, c, 0.0))
    st_ref[...] = jnp.where(lane < MLA_NOPE_DIM, 0.0,
                            jnp.where(lane < half, -s, jnp.where(lane < end, s, 0.0)))


def _rope_tables(pos_f, inv_row, tm):
    t = pos_f.shape[0]
    return pl.pallas_call(
        _rope_kernel,
        grid=(t // tm,),
        in_specs=[pl.BlockSpec((tm, 1), lambda i: (i, 0)), _full((1, LANE))],
        out_specs=[pl.BlockSpec((tm, LANE), lambda i: (i, 0))] * 2,
        out_shape=[jax.ShapeDtypeStruct((t, LANE), F32)] * 2,
        compiler_params=_params("parallel"),
        name="rope_tables",
    )(pos_f, inv_row)


def _mla_latents(c, ct_ref, st_ref, gq_ref, gkv_ref, wqa_ref, wqb_ref, wka_ref, pb_ref,
                 wv_ref, ones_ref, q_ref, k_ref, v_ref, q_scale):
    nq = _rms(c[:, :MLA_Q_PAD], gq_ref[...], MLA_Q_RANK).astype(BF16)
    nkv = _rms(c[:, MLA_Q_PAD:MLA_KV_END], gkv_ref[...], MLA_KV_RANK).astype(BF16)
    kr = c[:, MLA_KV_END:MLA_C_PAD].astype(BF16)
    ct = jnp.tile(ct_ref[...], (1, MLA_HEADS))
    st = jnp.tile(st_ref[...], (1, MLA_HEADS))
    q = _dot(nq, wqa_ref[...]) * ct + _dot(nq, wqb_ref[...]) * st
    q_ref[...] = (q * q_scale).astype(BF16)
    x = jnp.concatenate([nkv, kr], axis=1)
    k = _dot(x, wka_ref[...]) * ct + _dot(kr, pb_ref[...]) * st
    k_ref[...] = k.astype(BF16)
    v_ref[...] = (_dot_nt(wv_ref[...], nkv) + ones_ref[...]).astype(BF16)


def _rwkv_elementwise(x, vf_ref, w2_ref, bd_ref, w0_ref, a0_ref, kk_ref_p, ka_ref, v0_ref,
                      r_o, k_o, v_o, kk_o, kka_o, lw_o, g_o):
    rw = RWKV_WIDTH
    r_o[...] = x[:, 0:rw].astype(BF16)
    k = x[:, rw:2 * rw]
    v = x[:, 2 * rw:3 * rw]
    tail = x[:, 3 * rw:3 * rw + LORA_PAD]
    lane = lax.broadcasted_iota(jnp.int32, tail.shape, 1)
    g_lo = DECAY_RANK + ICL_RANK
    g_hi = g_lo + GATE_RANK
    is_gate = (lane >= g_lo) & (lane < g_hi)
    th = jnp.tanh(jnp.where(is_gate, 0.5 * tail, tail))
    act = jnp.where(lane < DECAY_RANK, th, jnp.where(is_gate, 0.5 + 0.5 * th, tail))
    lo = _dot(act.astype(BF16), w2_ref[...])
    lw_o[...] = -math.exp(-0.5) * _sigmoid(w0_ref[...] + lo[:, 0:rw])
    a = _sigmoid(a0_ref[...] + lo[:, rw:2 * rw])
    g_o[...] = lo[:, 2 * rw:3 * rw].astype(BF16)
    if vf_ref is not None:
        mix = _sigmoid(v0_ref[...] + lo[:, 3 * rw:4 * rw])
        v = v + (vf_ref[...].astype(F32) - v) * mix
    v_o[...] = v.astype(BF16)
    kk = k * kk_ref_p[...]
    sq = kk * kk
    sq_hi = sq.astype(BF16)
    sq_lo = (sq - sq_hi.astype(F32)).astype(BF16)
    ss = _dot(sq_hi, bd_ref[...]) + _dot(sq_lo, bd_ref[...])
    kk = kk * lax.rsqrt(jnp.maximum(ss, 1e-24))
    kk_o[...] = kk.astype(BF16)
    kka_o[...] = (kk * a).astype(BF16)
    k_o[...] = (k * (1.0 + (a - 1.0) * ka_ref[...])).astype(BF16)


def _layer_in_kernel(*refs, first, tiles_per_seq, q_scale):
    n_in = 20 if first else 22
    ins, outs, carry_ref = refs[:n_in], refs[n_in:n_in + 10], refs[n_in + 10]
    h_ref, g_ref, w_ref, mu_ref = ins[:4]
    mla_ins = ins[4:14]
    if first:
        w2_ref, bd_ref, w0_ref, a0_ref, kkp_ref, ka_ref = ins[14:20]
        vf_ref = v0_ref = None
    else:
        vf_ref, w2_ref, bd_ref, w0_ref, a0_ref, kkp_ref, ka_ref, v0_ref = ins[14:22]
    q_ref, k_ref, vt_ref = outs[:3]
    rw_outs = outs[3:]

    i = pl.program_id(0)
    tm, d = h_ref.shape
    hr = min(LAYER_IN_PART_ROWS, tm)

    @pl.when(i % tiles_per_seq == 0)
    def _():
        carry_ref[...] = jnp.zeros_like(carry_ref)

    parts = [pl.ds(p * hr, hr) for p in range(tm // hr)]
    hn = [_rms(h_ref[rows, :], g_ref[...], d).astype(BF16) for rows in parts]
    proj = [_dot(x, w_ref[...]) for x in hn]
    last = carry_ref[7:8, :]
    for rows, pr in zip(parts, proj):
        ct_v, st_v = mla_ins[0].at[rows], mla_ins[1].at[rows]
        _mla_latents(pr[:, :MLA_C_PAD], ct_v, st_v, *mla_ins[2:], q_ref.at[rows], k_ref.at[rows],
                     vt_ref.at[:, rows], q_scale)
        z = pr[:, MLA_C_PAD:]
        prev = pltpu.roll(z, 1, axis=0)
        row = lax.broadcasted_iota(jnp.int32, z.shape, 0)
        prev = jnp.where(row == 0, last, prev)
        x = z + (prev - z) * mu_ref[...]
        last = z[hr - 1:hr, :]
        _rwkv_elementwise(x, None if vf_ref is None else vf_ref.at[rows], w2_ref, bd_ref, w0_ref,
                          a0_ref, kkp_ref, ka_ref, v0_ref, *[o.at[rows] for o in rw_outs])
    carry_ref[...] = z[hr - 8:, :]


def _layer_in(h, g, w, mu, ct, st, gq, gkv, wqa, wqb, wka, pb, wvt, vf, w2, bd, w0, a0, k_k,
              k_a, v0, seq, tm):
    t, d = h.shape
    rw = RWKV_WIDTH
    hw = MLA_HEADS * HEAD_PAD
    vw = wvt.shape[0]
    first = vf is None
    ones_col = (jnp.arange(vw) % (MLA_V_DIM + ATTN_ONES_ROWS) == MLA_V_DIM).astype(F32)[:, None]
    q_scale = (MLA_NOPE_DIM + MLA_ROPE_DIM) ** -0.5 * LOG2E
    row = lambda width: pl.BlockSpec((tm, width), lambda i: (i, 0))
    vec = _full((1, rw))
    args = [h, g, w, mu, ct, st, gq, gkv, wqa, wqb, wka, pb, wvt, ones_col]
    in_specs = [row(d), _full((1, d)), _full(w.shape), _full((1, RW_C_PAD)), row(LANE), row(LANE),
                _full(gq.shape), _full(gkv.shape), _full(wqa.shape), _full(wqb.shape),
                _full(wka.shape), _full(pb.shape), _full(wvt.shape), _full(ones_col.shape)]
    if first:
        args += [w2, bd, w0, a0, k_k, k_a]
        in_specs += [_full(w2.shape), _full(bd.shape), vec, vec, vec, vec]
    else:
        args += [vf, w2, bd, w0, a0, k_k, k_a, v0]
        in_specs += [row(rw), _full(w2.shape), _full(bd.shape), vec, vec, vec, vec, vec]
    rw_dt = [BF16, BF16, BF16, BF16, BF16, F32, BF16]
    kern = functools.partial(_layer_in_kernel, first=first, tiles_per_seq=seq // tm,
                             q_scale=q_scale)
    return pl.pallas_call(
        kern,
        grid=(t // tm,),
        in_specs=in_specs,
        out_specs=[row(hw), row(hw), pl.BlockSpec((vw, tm), lambda i: (0, i))] + [row(rw)] * 7,
        out_shape=[jax.ShapeDtypeStruct((t, hw), BF16), jax.ShapeDtypeStruct((t, hw), BF16),
                   jax.ShapeDtypeStruct((vw, t), BF16)]
                  + [jax.ShapeDtypeStruct((t, rw), dt) for dt in rw_dt],
        scratch_shapes=[pltpu.VMEM((8, RW_C_PAD), F32)],
        compiler_params=_params("arbitrary"),
        name="layer_in",
    )(*args)


def _attn_kernel(q_ref, k_ref, vt_ref, g_ref, o_ref, m_ref, a_ref, p_ref, acc_ref, s_ref, s2_ref,
                 *, tq, tk, nh):
    qi = pl.program_id(2)
    dv = MLA_V_DIM
    heads = range(nh)

    ext = dv + ATTN_ONES_ROWS

    m_ref[...] = jnp.full_like(m_ref, NEG_BIG)
    a_ref[...] = jnp.ones_like(a_ref)
    p_ref[...] = jnp.zeros_like(p_ref)
    acc_ref[...] = jnp.zeros_like(acc_ref)

    def flush_values(j, q0=0):
        off = pl.multiple_of(j * tk, tk)
        pv = [_dot(vt_ref[h * ext:(h + 1) * ext, pl.ds(off, tk)], p_ref[h, :, q0:])
              for h in heads]
        for h in heads:
            acc_ref[h, :, q0:] = a_ref[h, :, q0:] * acc_ref[h, :, q0:] + pv[h]

    def scores(j, h, dst, q0=0):
        off = pl.multiple_of(j * tk, tk)
        dst[h, :, q0:] = _dot_nt(k_ref[0, pl.ds(off, tk), h * HEAD_PAD:(h + 1) * HEAD_PAD],
                                 q_ref[0, q0:, h * HEAD_PAD:(h + 1) * HEAD_PAD])

    def softmax(h, src, key0=None, q0=0):
        for c0 in range(q0, tq, LANE):
            cols = slice(c0, c0 + LANE)
            s = src[h, :, cols]
            if key0 is not None:
                ck = (lax.broadcasted_iota(jnp.int32, s.shape, 0) + key0) // CHUNK
                cq = (lax.broadcasted_iota(jnp.int32, s.shape, 1) + c0) // CHUNK
                s = jnp.where(ck <= cq, s, NEG_BIG)
            m_old = m_ref[h, :, cols]
            m_new = jnp.maximum(m_old, jnp.max(s, axis=0, keepdims=True))
            p_ref[h, :, cols] = jnp.exp2((s - m_new).astype(BF16))
            a_ref[h, :, cols] = jnp.exp2(m_old - m_new)
            m_ref[h, :, cols] = m_new

    def stage(j, src, dst, key0=None, q0=0, next_q0=0):
        flush_values(jnp.maximum(j - 1, 0))
        for h in heads:
            if dst is not None:
                scores(j + 1, h, dst, next_q0)
            softmax(h, src, key0, q0)

    for h in heads:
        scores(0, h, s_ref)

    @pl.loop(0, qi)
    def _(t):
        stage(2 * t, s_ref, s2_ref)
        stage(2 * t + 1, s2_ref, s_ref)

    stage(2 * qi, s_ref, s2_ref, key0=0, next_q0=tk)
    stage(2 * qi + 1, s2_ref, None, key0=tk, q0=tk)
    flush_values(2 * qi + 1, q0=tk)
    outs = []
    for h in heads:
        acc = acc_ref[h]
        o = acc[:dv] / acc[dv:dv + 1]
        ms = jnp.sum(o * o, axis=0, keepdims=True) * (1.0 / dv)
        outs.append(o * lax.rsqrt(ms + NORM_EPS))
    o_t = jnp.concatenate(outs, axis=0)
    o_ref[0] = (o_t.T * g_ref[...]).astype(BF16)


def _mla_attention(q, k, vt, g, seq, tk, nh):
    tq = 2 * tk
    b = q.shape[0] // seq
    hw = q.shape[1]
    groups = hw // (nh * HEAD_PAD)
    ow = nh * MLA_V_DIM
    q3 = q.reshape(b, seq, hw)
    k3 = k.reshape(b, seq, hw)
    return pl.pallas_call(
        functools.partial(_attn_kernel, tq=tq, tk=tk, nh=nh),
        grid=(b, groups, seq // tq),
        in_specs=[pl.BlockSpec((1, tq, nh * HEAD_PAD), lambda bi, h, i: (bi, i, h)),
                  pl.BlockSpec((1, seq, nh * HEAD_PAD), lambda bi, h, i: (bi, 0, h)),
                  pl.BlockSpec((nh * (MLA_V_DIM + ATTN_ONES_ROWS), seq), lambda bi, h, i: (h, bi)),
                  pl.BlockSpec((1, ow), lambda bi, h, i: (0, h))],
        out_specs=pl.BlockSpec((1, tq, ow), lambda bi, h, i: (bi, i, h)),
        out_shape=jax.ShapeDtypeStruct((b, seq, groups * ow), BF16),
        scratch_shapes=[pltpu.VMEM((nh, 1, tq), F32), pltpu.VMEM((nh, 1, tq), F32),
                        pltpu.VMEM((nh, tk, tq), BF16),
                        pltpu.VMEM((nh, MLA_V_DIM + ATTN_ONES_ROWS, tq), F32),
                        pltpu.VMEM((nh, tk, tq), F32), pltpu.VMEM((nh, tk, tq), F32)],
        compiler_params=_params("parallel", "parallel", "arbitrary"),
        name="mla_attention",
    )(q3, k3, vt, g)


_DIMS = {"nn": (((1,), (0,)), ((), ())), "nt": (((1,), (1,)), ((), ())),
         "tn": (((0,), (0,)), ((), ()))}


def _split(x):
    hi = x.astype(BF16)
    return hi, (x - hi.astype(F32)).astype(BF16)


def _mm(a, b, mode, dims="nn"):
    dg = lambda x, y: lax.dot_general(x, y, _DIMS[dims], preferred_element_type=F32)
    if mode == "bf16":
        return dg(a.astype(BF16), b.astype(BF16))
    if mode == "x3":
        a_hi, a_lo = _split(a)
        b_hi, b_lo = _split(b)
        return dg(a_hi, b_hi) + (dg(a_lo, b_hi) + dg(a_hi, b_lo))
    if mode == "ax":
        n = b.shape[1]
        out = dg(a.astype(BF16), jnp.concatenate(_split(b), axis=1))
        return out[:, :n] + out[:, n:]
    return lax.dot_general(a, b, _DIMS[dims], preferred_element_type=F32, precision=HI)


def _unit_lower_solve_stages(a_list, rhs_list, row, col, eye, mode, out):
    d16 = (row // 16) == (col // 16)
    d32 = (row // 32) == (col // 32)
    n = range(len(a_list))
    d = [jnp.where(d16, a, 0.0) for a in a_list]
    x = [eye + d[i] for i in n]
    d2 = [_mm(d[i], d[i], mode) for i in n]
    yield
    t = [_mm(d2[i], jnp.concatenate([x[i], d2[i]], axis=1), mode) for i in n]
    x = [x[i] + t[i][:, :128] for i in n]
    d4 = [t[i][:, 128:] for i in n]
    yield
    t = [_mm(d4[i], jnp.concatenate([x[i], d4[i]], axis=1), mode) for i in n]
    x = [x[i] + t[i][:, :128] for i in n]
    d8 = [t[i][:, 128:] for i in n]
    yield
    t = [_mm(d8[i], x[i], mode) for i in n]
    x = [x[i] + t[i] for i in n]
    yield
    e = [jnp.where(d32, a_list[i], 0.0) - d[i] for i in n]
    t = [_mm(e[i], x[i], mode) for i in n]
    yield
    t = [_mm(x[i], t[i], mode) for i in n]
    x = [x[i] + t[i] for i in n]
    yield
    f = [jnp.where(d32, 0.0, a_list[i]) for i in n]
    z = [_mm(x[i], rhs_list[i], mode) for i in n]
    yield
    t = [_mm(f[i], z[i], mode) for i in n]
    yield
    t = [_mm(x[i], t[i], mode) for i in n]
    out.extend(z[i] + t[i] for i in n)
    yield


def _wkv_kernel(r_ref, k_ref, v_ref, kk_ref, kka_ref, lw_ref, g_ref,
                rk_ref, lnw_ref, lnb_ref, o_ref, s_ref, *, nch, modes):
    m_quad, m_inv, m_state = modes
    c_len = WKV_CHUNK

    npair = s_ref.shape[0]

    @pl.when(pl.program_id(1) == 0)
    def _():
        s_ref[...] = jnp.zeros_like(s_ref)

    row = lax.broadcasted_iota(jnp.int32, (128, 128), 0)
    col = lax.broadcasted_iota(jnp.int32, (128, 128), 1)
    same = (row // c_len) == (col // c_len)
    mask_s = same & (col < row)
    mask_i = same & (col <= row)
    eye = jnp.where(row == col, 1.0, 0.0)
    lane0 = lax.broadcasted_iota(jnp.int32, (c_len, 128), 1) < RWKV_HEAD
    tr = lax.broadcasted_iota(jnp.int32, (c_len, c_len), 0)
    tc = lax.broadcasted_iota(jnp.int32, (c_len, c_len), 1)
    ltri = jnp.where(tc <= tr, 1.0, 0.0)

    def stack(x):
        return jnp.concatenate([jnp.where(lane0, x, 0.0), jnp.where(lane0, 0.0, x)], axis=0)

    def head_sums(x):
        first = jnp.sum(jnp.where(lane0, x, 0.0), axis=-1, keepdims=True)
        both = jnp.sum(x, axis=-1, keepdims=True)
        return jnp.where(lane0, first, both - first)

    pairs = range(npair)
    par = {}
    wus = {}

    def independent_part(chunks):
        keys = [(c, hp) for c in chunks for hp in pairs]
        for c, hp in keys:
            sl = pl.ds(c * c_len, c_len)
            ln = slice(hp * LANE, (hp + 1) * LANE)
            r = r_ref[0, sl, ln].astype(F32)
            k = k_ref[0, sl, ln].astype(F32)
            v = v_ref[0, sl, ln].astype(F32)
            kk = kk_ref[0, sl, ln].astype(F32)
            kka = kka_ref[0, sl, ln].astype(F32)
            lw = lw_ref[0, sl, ln]
            cum = _mm(ltri, lw, "ax")
            tot = cum[c_len - 1:c_len, :]
            p_in = jnp.exp(cum)
            p_ex = jnp.exp(cum - lw)
            p_inv = jnp.exp(-cum)
            p_end = jnp.exp(tot - cum)
            at = stack(-kk * p_ex)
            rt = stack(r * p_in)
            bt = kka * p_inv
            kt = k * p_inv
            par[c, hp] = dict(
                r=r, k=k, v=v, tot=tot, at=at, rt=rt, vs=stack(v),
                bkh=jnp.concatenate([stack(kka * p_end), stack(k * p_end)], axis=0),
                lhs=jnp.concatenate([at, rt], axis=0),
                rhs=jnp.concatenate([bt, bt, kt, kt], axis=0))
        yield
        for key in keys:
            p = par[key]
            quad = _mm(p["lhs"], p["rhs"], m_quad, "nt")
            p["a_ab"] = jnp.where(mask_s, quad[:128, :128], 0.0)
            p["a_ak"] = jnp.where(mask_s, quad[:128, 128:], 0.0)
            p["a_r"] = jnp.concatenate([jnp.where(mask_i, quad[128:, :128], 0.0),
                                        jnp.where(mask_i, quad[128:, 128:], 0.0)], axis=1)
        yield
        for key in keys:
            par[key]["akv"] = _mm(par[key]["a_ak"], par[key]["vs"], m_quad)
        yield
        solved = []
        yield from _unit_lower_solve_stages(
            [par[key]["a_ab"] for key in keys],
            [jnp.concatenate([par[key]["at"], par[key]["akv"]], axis=1) for key in keys],
            row, col, eye, m_inv, solved)
        wus.update(zip(keys, solved))

    state = [s_ref[hp] for hp in pairs]

    def outputs(c, m1, uv):
        sl = pl.ds(c * c_len, c_len)
        pc = [par[c, hp] for hp in pairs]
        ys = [m1[hp][128:] + _mm(pc[hp]["a_r"], uv[hp], m_state) for hp in pairs]
        y = [ys[hp][:c_len] + ys[hp][c_len:] for hp in pairs]
        mean = [head_sums(y[hp]) * (1.0 / RWKV_HEAD) for hp in pairs]
        yc = [y[hp] - mean[hp] for hp in pairs]
        var = [head_sums(yc[hp] * yc[hp]) * (1.0 / RWKV_HEAD) for hp in pairs]
        for hp in pairs:
            ln = slice(hp * LANE, (hp + 1) * LANE)
            r, k, v = pc[hp]["r"], pc[hp]["k"], pc[hp]["v"]
            yn = yc[hp] * lax.rsqrt(var[hp] + GN_EPS) * lnw_ref[:, ln] + lnb_ref[:, ln]
            bonus = head_sums(r * k * rk_ref[:, ln])
            yn = yn + bonus * v
            o_ref[0, sl, ln] = (yn * g_ref[0, sl, ln].astype(F32)).astype(BF16)

    pending = []

    def state_part(chunks):
        for c in chunks:
            pc = [par[c, hp] for hp in pairs]
            wu = [wus[c, hp] for hp in pairs]
            m1 = [_mm(jnp.concatenate([wu[hp][:, :128], pc[hp]["rt"]], axis=0), state[hp],
                      m_state, "nt") for hp in pairs]
            uv = [jnp.concatenate([m1[hp][:128] + wu[hp][:, 128:], pc[hp]["vs"]], axis=0)
                  for hp in pairs]
            upd = [_mm(uv[hp], pc[hp]["bkh"], m_state, "tn") for hp in pairs]
            for hp in pairs:
                state[hp] = state[hp] * jnp.exp(pc[hp]["tot"]) + upd[hp]
            yield
            if pending:
                outputs(*pending.pop())
                yield
            pending.append((c, m1, uv))

    groups = [list(range(g, min(g + WKV_GROUP, nch))) for g in range(0, nch, WKV_GROUP)]
    for _ in independent_part(groups[0]):
        pass
    for g, chunks in enumerate(groups):
        chain = state_part(chunks)
        ahead = independent_part(groups[g + 1]) if g + 1 < len(groups) else iter(())
        for _ in chain:
            for _ in range(WKV_AHEAD_STAGES):
                next(ahead, None)
        for _ in ahead:
            pass
    outputs(*pending.pop())
    for hp in pairs:
        s_ref[hp] = state[hp]


def _wkv(r, k, v, kk, kka, lw, g, r_k, ln_w, ln_b, rows, modes):
    b, s, w = r.shape
    blk = pl.BlockSpec((1, rows, w), lambda bi, i: (bi, i, 0))
    kern = functools.partial(_wkv_kernel, nch=rows // WKV_CHUNK, modes=modes)
    return pl.pallas_call(
        kern,
        grid=(b, s // rows),
        in_specs=[blk] * 7 + [_full((1, w))] * 3,
        out_specs=blk,
        out_shape=jax.ShapeDtypeStruct((b, s, w), BF16),
        scratch_shapes=[pltpu.VMEM((w // LANE, LANE, LANE), F32)],
        compiler_params=_params("parallel", "arbitrary"),
        name="wkv7",
    )(r, k, v, kk, kka, lw, g, r_k, ln_w, ln_b)


def _mem_kv_kernel(m_ref, g_ref, w_ref, o_ref):
    d = m_ref.shape[-1]
    mn = _rms(m_ref[...], g_ref[...], d).astype(BF16)
    o_ref[...] = _dot(mn, w_ref[0]).astype(BF16)


def _mem_kv(mem2d, g, wkv, tm):
    t, d = mem2d.shape
    nl, _, n = wkv.shape
    return pl.pallas_call(
        _mem_kv_kernel,
        grid=(nl, t // tm),
        in_specs=[pl.BlockSpec((tm, d), lambda l, i: (i, 0)), _full((1, d)),
                  pl.BlockSpec((1, d, n), lambda l, i: (l, 0, 0))],
        out_specs=pl.BlockSpec((tm, n), lambda l, i: (i, l)),
        out_shape=jax.ShapeDtypeStruct((t, nl * n), BF16),
        compiler_params=_params("parallel", "parallel"),
        name="mem_kv",
    )(mem2d, g, wkv)


def _mix_out_kernel(h_ref, att_ref, rw_ref, woa_ref, wob_ref, g_ref, wq_ref,
                    mk_ref, mv_ref, wo_ref, o_ref, *, q_scale):
    d = h_ref.shape[-1]
    h1 = h_ref[...] + _dot(att_ref[...], woa_ref[0]) + _dot(rw_ref[...], wob_ref[0])
    hn = _rms(h1, g_ref[0], d).astype(BF16)
    q = (_dot(hn, wq_ref[0]) * q_scale).astype(BF16)
    dh = d // CA_HEADS
    outs = []
    for hd in range(CA_HEADS):
        sl = slice(hd * dh, (hd + 1) * dh)
        s = _dot_nt(q[:, sl], mk_ref[:, sl])
        m = jnp.max(s, axis=-1, keepdims=True)
        p = jnp.exp2(s - m)
        l = jnp.sum(p, axis=-1, keepdims=True)
        outs.append((_dot(p.astype(BF16), mv_ref[:, sl]) / l).astype(BF16))
    o = jnp.concatenate(outs, axis=1)
    o_ref[...] = h1 + _dot(o, wo_ref[0])


def _mix_out(h, att, rw, woa, wob, g, wq, memkv, wo, layer, seq, mem_len, tm):
    t, d = h.shape
    tiles = seq // tm
    q_scale = (d // CA_HEADS) ** -0.5 * LOG2E
    row = lambda w: pl.BlockSpec((tm, w), lambda i: (i, 0))
    lay = functools.partial(_layer_slice, layer=layer)
    return pl.pallas_call(
        functools.partial(_mix_out_kernel, q_scale=q_scale),
        grid=(t // tm,),
        in_specs=[row(d), row(att.shape[1]), row(rw.shape[1]), lay(woa), lay(wob), lay(g),
                  lay(wq),
                  pl.BlockSpec((mem_len, d), lambda i: (i // tiles, 2 * layer)),
                  pl.BlockSpec((mem_len, d), lambda i: (i // tiles, 2 * layer + 1)),
                  lay(wo)],
        out_specs=row(d),
        out_shape=jax.ShapeDtypeStruct((t, d), F32),
        compiler_params=_params("parallel"),
        name="mix_out_cross_attention",
    )(h, att, rw, woa, wob, g, wq, memkv, memkv, wo)


def _mlp_kernel(h_ref, g_ref, wu_ref, wd_ref, gf_ref, o_ref, *, last, ff_chunk):
    d = h_ref.shape[-1]
    h = h_ref[...]
    hn = _rms(h, g_ref[0], d).astype(BF16)
    acc = h
    for c in range(wu_ref.shape[-1] // ff_chunk):
        sl = slice(c * ff_chunk, (c + 1) * ff_chunk)
        u = jnp.maximum(_dot(hn, wu_ref[0, :, sl]), 0.0)
        acc = acc + _dot((u * u).astype(BF16), wd_ref[0, sl, :])
    if last:
        acc = _rms(acc, gf_ref[...], d)
    o_ref[...] = acc


def _mlp(h, g, wu, wd, gf, layer, last, tm):
    t, d = h.shape
    row = pl.BlockSpec((tm, d), lambda i: (i, 0))
    lay = functools.partial(_layer_slice, layer=layer)
    return pl.pallas_call(
        functools.partial(_mlp_kernel, last=last, ff_chunk=1024),
        grid=(t // tm,),
        in_specs=[row, lay(g), lay(wu), lay(wd), _full((1, d))],
        out_specs=row,
        out_shape=jax.ShapeDtypeStruct((t, d), F32),
        compiler_params=_params("parallel"),
        name="mlp",
    )(h, g, wu, wd, gf)


def _pad_cols(a, n):
    return jnp.pad(a, ((0, 0), (0, n - a.shape[1])))


def _swap_halves(a):
    half = a.shape[-1] // 2
    return jnp.concatenate([a[..., half:], a[..., :half]], axis=-1)


def _in_weights(w_in, mu):
    cq = w_in[:, :MLA_Q_RANK]
    ckv = w_in[:, MLA_Q_RANK:MLA_Q_RANK + MLA_KV_RANK]
    kr = w_in[:, MLA_Q_RANK + MLA_KV_RANK:MLA_COLS]
    mla = jnp.concatenate([_pad_cols(cq, MLA_Q_PAD), ckv, kr, _swap_halves(kr)], axis=1)
    w = jnp.concatenate([_pad_cols(mla, MLA_C_PAD), _pad_cols(w_in[:, MLA_COLS:], RW_C_PAD)], axis=1)
    return w.astype(BF16), _pad_cols(mu[None, :], RW_C_PAD)


def _head_pad(a, n_heads, per_head, lo, hi, at):
    rows = a.shape[0]
    a = a.reshape(rows, n_heads, per_head)[:, :, lo:hi]
    a = jnp.pad(a, ((0, 0), (0, 0), (at, HEAD_PAD - at - (hi - lo))))
    return a.reshape(rows, n_heads * HEAD_PAD)


def _mla_weights(w_uq, w_ukv):
    dq = MLA_NOPE_DIM + MLA_ROPE_DIM
    h = MLA_HEADS
    wq3 = w_uq.reshape(MLA_Q_RANK, h, dq)
    rope_sw = _swap_halves(wq3[:, :, MLA_NOPE_DIM:]).reshape(MLA_Q_RANK, h * MLA_ROPE_DIM)
    wqa = _head_pad(w_uq, h, dq, 0, dq, 0)
    wqb = _head_pad(rope_sw, h, MLA_ROPE_DIM, 0, MLA_ROPE_DIM, MLA_NOPE_DIM)
    pad_q = ((0, MLA_Q_PAD - MLA_Q_RANK), (0, 0))
    wqa = jnp.pad(wqa, pad_q)
    wqb = jnp.pad(wqb, pad_q)
    dkv = MLA_NOPE_DIM + MLA_V_DIM
    wk = _head_pad(w_ukv, h, dkv, 0, MLA_NOPE_DIM, 0)
    wvt = jnp.transpose(w_ukv.reshape(MLA_KV_RANK, h, dkv)[:, :, MLA_NOPE_DIM:], (1, 2, 0))
    wvt = jnp.pad(wvt, ((0, 0), (0, ATTN_ONES_ROWS), (0, 0))).reshape(-1, MLA_KV_RANK)
    eye = jnp.eye(MLA_ROPE_DIM, dtype=F32)
    place = _head_pad(jnp.tile(eye, (1, h)), h, MLA_ROPE_DIM, 0, MLA_ROPE_DIM, MLA_NOPE_DIM)
    zeros = jnp.zeros_like(place)
    pad_rows = jnp.zeros((LANE - 2 * MLA_ROPE_DIM, h * HEAD_PAD), F32)
    pa = jnp.concatenate([place, zeros, pad_rows], axis=0)
    pb = jnp.concatenate([zeros, place, pad_rows], axis=0)
    wka = jnp.concatenate([wk, pa], axis=0)
    return tuple(a.astype(BF16) for a in (wqa, wqb, wka, pb, wvt))


def _lora_weights(w2, a2, g2, v2):
    rw = RWKV_WIDTH
    z = lambda r: jnp.zeros((r, rw), F32)
    rows = [
        jnp.concatenate([w2, z(DECAY_RANK), z(DECAY_RANK), z(DECAY_RANK)], axis=1),
        jnp.concatenate([z(ICL_RANK), a2, z(ICL_RANK), z(ICL_RANK)], axis=1),
        jnp.concatenate([z(GATE_RANK), z(GATE_RANK), g2, z(GATE_RANK)], axis=1),
        jnp.concatenate([z(VRES_RANK)] * 3 + [v2 if v2 is not None else z(VRES_RANK)], axis=1),
    ]
    w = jnp.concatenate(rows, axis=0)
    return jnp.pad(w, ((0, LORA_PAD - w.shape[0]), (0, 0))).astype(BF16)


def _block_diag(n, blk, value):
    i = jnp.arange(n)
    return jnp.where((i[:, None] // blk) == (i[None, :] // blk), value, 0.0)


def kernel(x, mem, positions, mix_norm, w_in_first, w_in_rest, shift_mu_first, shift_mu_rest, mla_q_norm, mla_w_uq, mla_kv_norm, mla_w_ukv, mla_out_norm, rwkv_w0, rwkv_w2, rwkv_a0, rwkv_a2, rwkv_g2, rwkv_v0, rwkv_v2, rwkv_k_k, rwkv_k_a, rwkv_r_k, rwkv_ln_w, rwkv_ln_b, w_out, ca_norm, mem_norm, ca_wq, ca_wkv, ca_wo, mlp_norm, mlp_w_up, mlp_w_down, final_norm):
    b, s, d = x.shape
    depth = mix_norm.shape[0]
    mem_len = mem.shape[1]
    t = b * s
    tm = min(512, s)
    tm_wide = min(1024, s)
    attn_key_block = min(256, s)
    wkv_rows = min(1024, s)
    rw = RWKV_WIDTH

    inv = ROPE_THETA ** (-jnp.arange(0, MLA_ROPE_DIM, 2, dtype=F32) / MLA_ROPE_DIM)
    inv_row = jnp.concatenate([jnp.zeros((MLA_NOPE_DIM,), F32), inv, inv,
                               jnp.zeros((HEAD_PAD - MLA_NOPE_DIM - MLA_ROPE_DIM,), F32)])[None, :]
    ct, st = _rope_tables(positions.astype(F32).reshape(t, 1), inv_row, tm)

    memkv = _mem_kv(mem.reshape(b * mem_len, d), mem_norm[None, :], ca_wkv.astype(BF16),
                    min(512, b * mem_len))

    woa = w_out[:, :MLA_HEADS * MLA_V_DIM].astype(BF16)
    wob = w_out[:, MLA_HEADS * MLA_V_DIM:].astype(BF16)
    wq_b = ca_wq.astype(BF16)
    wo_b = ca_wo.astype(BF16)
    wu_b = mlp_w_up.astype(BF16)
    wd_b = mlp_w_down.astype(BF16)
    ca_g = ca_norm[:, None, :]
    mlp_g = mlp_norm[:, None, :]
    bd512 = _block_diag(rw, RWKV_HEAD, 1.0).astype(BF16)

    h = x.reshape(t, d)
    v_first = None
    for l in range(depth):
        if l == 0:
            w_in, mu, v0, v2 = w_in_first, shift_mu_first, None, None
        else:
            w_in, mu, v0, v2 = w_in_rest[l - 1], shift_mu_rest[l - 1], rwkv_v0[l - 1], rwkv_v2[l - 1]
        w_all, mu_p = _in_weights(w_in, mu)
        wqa, wqb, wka, pb, wvt = _mla_weights(mla_w_uq[l], mla_w_ukv[l])
        gq = _pad_cols(mla_q_norm[l][None, :], MLA_Q_PAD)
        w2c = _lora_weights(rwkv_w2[l], rwkv_a2[l], rwkv_g2[l], v2)
        vec = lambda a: a.reshape(1, rw)
        q, k, vt, r_, k_, v_, kk_, kka_, lw_, g_ = _layer_in(
            h, mix_norm[l][None, :], w_all, mu_p, ct, st, gq, mla_kv_norm[l][None, :],
            wqa, wqb, wka, pb, wvt, v_first, w2c, bd512, vec(rwkv_w0[l]), vec(rwkv_a0[l]),
            vec(rwkv_k_k[l]), vec(rwkv_k_a[l]), None if v0 is None else vec(v0), s, tm_wide)
        if l == 0:
            v_first = v_

        att = _mla_attention(q, k, vt, mla_out_norm[l][None, :], s, attn_key_block,
                             ATTN_HEADS_PER_STEP)

        sh = lambda a: a.reshape(b, s, rw)
        y = _wkv(sh(r_), sh(k_), sh(v_), sh(kk_), sh(kka_), sh(lw_), sh(g_),
                 vec(rwkv_r_k[l]), vec(rwkv_ln_w[l]), vec(rwkv_ln_b[l]), wkv_rows, WKV_MODES)

        h = _mix_out(h, att.reshape(t, -1), y.reshape(t, rw), woa, wob, ca_g, wq_b, memkv, wo_b,
                     l, s, mem_len, tm_wide)
        h = _mlp(h, mlp_g, wu_b, wd_b, final_norm[None, :], l, l == depth - 1, tm_wide)
    return h.reshape(b, s, d)
```

```python
import functools
import math

import jax
import jax.numpy as jnp
from jax import lax
from jax.experimental import pallas as pl
from jax.experimental.pallas import tpu as pltpu

F32 = jnp.float32
BF16 = jnp.bfloat16

CHUNK = 64
NORM_EPS = 1e-6
MLA_HEADS = 8
MLA_V_DIM = 64
MLA_NOPE_DIM = 64
MLA_ROPE_DIM = 32
MLA_Q_RANK = 192
MLA_KV_RANK = 128
ROPE_THETA = 10000.0
RWKV_HEAD = 64
RWKV_HEADS = 8
RWKV_WIDTH = 512
DECAY_RANK = 32
ICL_RANK = 32
GATE_RANK = 96
VRES_RANK = 32
GN_EPS = 64e-5
CA_HEADS = 4
MLA_COLS = MLA_Q_RANK + MLA_KV_RANK + MLA_ROPE_DIM

LANE = 128
HEAD_PAD = 128
MLA_C_PAD = 512
MLA_Q_PAD = 256
MLA_KV_END = MLA_Q_PAD + MLA_KV_RANK
RW_C_PAD = 1792
LORA_PAD = 256
WKV_CHUNK = 64
WKV_GROUP = 4
WKV_AHEAD_STAGES = 2
LAYER_IN_PART_ROWS = 256
ATTN_HEADS_PER_STEP = 4
ATTN_ONES_ROWS = 16
LOG2E = 1.4426950408889634
NEG_BIG = -1e30
VMEM_LIMIT = 56 * 1024 * 1024

HI = lax.Precision.HIGHEST
WKV_MODES = ("bf16", "bf16", "bf16")


def _dot(a, b, precision=None):
    return jnp.dot(a, b, preferred_element_type=F32, precision=precision)


def _dot_nt(a, b, precision=None):
    return lax.dot_general(a, b, (((1,), (1,)), ((), ())),
                           preferred_element_type=F32, precision=precision)


def _rms(x, g, n, eps=NORM_EPS):
    ms = jnp.sum(x * x, axis=-1, keepdims=True) * (1.0 / n)
    return x * lax.rsqrt(ms + eps) * g


def _sigmoid(x):
    return 0.5 + 0.5 * jnp.tanh(0.5 * x)


def _params(*sem):
    return pltpu.CompilerParams(dimension_semantics=sem, vmem_limit_bytes=VMEM_LIMIT)


def _full(shape):
    nd = len(shape)
    return pl.BlockSpec(shape, lambda *_: (0,) * nd, pipeline_mode=pl.Buffered(1))


def _layer_slice(a, layer):
    nd = a.ndim
    return pl.BlockSpec((1,) + a.shape[1:], lambda i: (layer,) + (0,) * (nd - 1),
                        pipeline_mode=pl.Buffered(1))


def _rope_kernel(pos_ref, inv_ref, ct_ref, st_ref):
    ang = pos_ref[...] * inv_ref[...]
    c = jnp.cos(ang)
    s = jnp.sin(ang)
    lane = lax.broadcasted_iota(jnp.int32, ang.shape, 1)
    half = MLA_NOPE_DIM + MLA_ROPE_DIM // 2
    end = MLA_NOPE_DIM + MLA_ROPE_DIM
    ct_ref[...] = jnp.where(lane < MLA_NOPE_DIM, 1.0, jnp.where(lane < end, c, 0.0))
    st_ref[...] = jnp.where(lane < MLA_NOPE_DIM, 0.0,
                            jnp.where(lane < half, -s, jnp.where(lane < end, s, 0.0)))


def _rope_tables(pos_f, inv_row, tm):
    t = pos_f.shape[0]
    return pl.pallas_call(
        _rope_kernel,
        grid=(t // tm,),
        in_specs=[pl.BlockSpec((tm, 1), lambda i: (i, 0)), _full((1, LANE))],
        out_specs=[pl.BlockSpec((tm, LANE), lambda i: (i, 0))] * 2,
        out_shape=[jax.ShapeDtypeStruct((t, LANE), F32)] * 2,
        compiler_params=_params("parallel"),
        name="rope_tables",
    )(pos_f, inv_row)


def _mla_latents(c, ct_ref, st_ref, gq_ref, gkv_ref, wqa_ref, wqb_ref, wka_ref, pb_ref,
                 wv_ref, ones_ref, q_ref, k_ref, v_ref, q_scale):
    nq = _rms(c[:, :MLA_Q_PAD], gq_ref[...], MLA_Q_RANK).astype(BF16)
    nkv = _rms(c[:, MLA_Q_PAD:MLA_KV_END], gkv_ref[...], MLA_KV_RANK).astype(BF16)
    kr = c[:, MLA_KV_END:MLA_C_PAD].astype(BF16)
    ct = jnp.tile(ct_ref[...], (1, MLA_HEADS))
    st = jnp.tile(st_ref[...], (1, MLA_HEADS))
    q = _dot(nq, wqa_ref[...]) * ct + _dot(nq, wqb_ref[...]) * st
    q_ref[...] = (q * q_scale).astype(BF16)
    x = jnp.concatenate([nkv, kr], axis=1)
    k = _dot(x, wka_ref[...]) * ct + _dot(kr, pb_ref[...]) * st
    k_ref[...] = k.astype(BF16)
    v_ref[...] = (_dot_nt(wv_ref[...], nkv) + ones_ref[...]).astype(BF16)


def _rwkv_elementwise(x, vf_ref, w2_ref, bd_ref, w0_ref, a0_ref, kk_ref_p, ka_ref, v0_ref,
                      r_o, k_o, v_o, kk_o, kka_o, lw_o, g_o):
    rw = RWKV_WIDTH
    r_o[...] = x[:, 0:rw].astype(BF16)
    k = x[:, rw:2 * rw]
    v = x[:, 2 * rw:3 * rw]
    tail = x[:, 3 * rw:3 * rw + LORA_PAD]
    lane = lax.broadcasted_iota(jnp.int32, tail.shape, 1)
    g_lo = DECAY_RANK + ICL_RANK
    g_hi = g_lo + GATE_RANK
    is_gate = (lane >= g_lo) & (lane < g_hi)
    th = jnp.tanh(jnp.where(is_gate, 0.5 * tail, tail))
    act = jnp.where(lane < DECAY_RANK, th, jnp.where(is_gate, 0.5 + 0.5 * th, tail))
    lo = _dot(act.astype(BF16), w2_ref[...])
    lw_o[...] = -math.exp(-0.5) * _sigmoid(w0_ref[...] + lo[:, 0:rw])
    a = _sigmoid(a0_ref[...] + lo[:, rw:2 * rw])
    g_o[...] = lo[:, 2 * rw:3 * rw].astype(BF16)
    if vf_ref is not None:
        mix = _sigmoid(v0_ref[...] + lo[:, 3 * rw:4 * rw])
        v = v + (vf_ref[...].astype(F32) - v) * mix
    v_o[...] = v.astype(BF16)
    kk = k * kk_ref_p[...]
    sq = kk * kk
    sq_hi = sq.astype(BF16)
    sq_lo = (sq - sq_hi.astype(F32)).astype(BF16)
    ss = _dot(sq_hi, bd_ref[...]) + _dot(sq_lo, bd_ref[...])
    kk = kk * lax.rsqrt(jnp.maximum(ss, 1e-24))
    kk_o[...] = kk.astype(BF16)
    kka_o[...] = (kk * a).astype(BF16)
    k_o[...] = (k * (1.0 + (a - 1.0) * ka_ref[...])).astype(BF16)


def _layer_in_kernel(*refs, first, tiles_per_seq, q_scale):
    n_in = 20 if first else 22
    ins, outs, carry_ref = refs[:n_in], refs[n_in:n_in + 10], refs[n_in + 10]
    h_ref, g_ref, w_ref, mu_ref = ins[:4]
    mla_ins = ins[4:14]
    if first:
        w2_ref, bd_ref, w0_ref, a0_ref, kkp_ref, ka_ref = ins[14:20]
        vf_ref = v0_ref = None
    else:
        vf_ref, w2_ref, bd_ref, w0_ref, a0_ref, kkp_ref, ka_ref, v0_ref = ins[14:22]
    q_ref, k_ref, vt_ref = outs[:3]
    rw_outs = outs[3:]

    i = pl.program_id(0)
    tm, d = h_ref.shape
    hr = min(LAYER_IN_PART_ROWS, tm)

    @pl.when(i % tiles_per_seq == 0)
    def _():
        carry_ref[...] = jnp.zeros_like(carry_ref)

    parts = [pl.ds(p * hr, hr) for p in range(tm // hr)]
    hn = [_rms(h_ref[rows, :], g_ref[...], d).astype(BF16) for rows in parts]
    proj = [_dot(x, w_ref[...]) for x in hn]
    last = carry_ref[7:8, :]
    for rows, pr in zip(parts, proj):
        ct_v, st_v = mla_ins[0].at[rows], mla_ins[1].at[rows]
        _mla_latents(pr[:, :MLA_C_PAD], ct_v, st_v, *mla_ins[2:], q_ref.at[rows], k_ref.at[rows],
                     vt_ref.at[:, rows], q_scale)
        z = pr[:, MLA_C_PAD:]
        prev = pltpu.roll(z, 1, axis=0)
        row = lax.broadcasted_iota(jnp.int32, z.shape, 0)
        prev = jnp.where(row == 0, last, prev)
        x = z + (prev - z) * mu_ref[...]
        last = z[hr - 1:hr, :]
        _rwkv_elementwise(x, None if vf_ref is None else vf_ref.at[rows], w2_ref, bd_ref, w0_ref,
                          a0_ref, kkp_ref, ka_ref, v0_ref, *[o.at[rows] for o in rw_outs])
    carry_ref[...] = z[hr - 8:, :]


def _layer_in(h, g, w, mu, ct, st, gq, gkv, wqa, wqb, wka, pb, wvt, vf, w2, bd, w0, a0, k_k,
              k_a, v0, seq, tm):
    t, d = h.shape
    rw = RWKV_WIDTH
    hw = MLA_HEADS * HEAD_PAD
    vw = wvt.shape[0]
    first = vf is None
    ones_col = (jnp.arange(vw) % (MLA_V_DIM + ATTN_ONES_ROWS) == MLA_V_DIM).astype(F32)[:, None]
    q_scale = (MLA_NOPE_DIM + MLA_ROPE_DIM) ** -0.5 * LOG2E
    row = lambda width: pl.BlockSpec((tm, width), lambda i: (i, 0))
    vec = _full((1, rw))
    args = [h, g, w, mu, ct, st, gq, gkv, wqa, wqb, wka, pb, wvt, ones_col]
    in_specs = [row(d), _full((1, d)), _full(w.shape), _full((1, RW_C_PAD)), row(LANE), row(LANE),
                _full(gq.shape), _full(gkv.shape), _full(wqa.shape), _full(wqb.shape),
                _full(wka.shape), _full(pb.shape), _full(wvt.shape), _full(ones_col.shape)]
    if first:
        args += [w2, bd, w0, a0, k_k, k_a]
        in_specs += [_full(w2.shape), _full(bd.shape), vec, vec, vec, vec]
    else:
        args += [vf, w2, bd, w0, a0, k_k, k_a, v0]
        in_specs += [row(rw), _full(w2.shape), _full(bd.shape), vec, vec, vec, vec, vec]
    rw_dt = [BF16, BF16, BF16, BF16, BF16, F32, BF16]
    kern = functools.partial(_layer_in_kernel, first=first, tiles_per_seq=seq // tm,
                             q_scale=q_scale)
    return pl.pallas_call(
        kern,
        grid=(t // tm,),
        in_specs=in_specs,
        out_specs=[row(hw), row(hw), pl.BlockSpec((vw, tm), lambda i: (0, i))] + [row(rw)] * 7,
        out_shape=[jax.ShapeDtypeStruct((t, hw), BF16), jax.ShapeDtypeStruct((t, hw), BF16),
                   jax.ShapeDtypeStruct((vw, t), BF16)]
                  + [jax.ShapeDtypeStruct((t, rw), dt) for dt in rw_dt],
        scratch_shapes=[pltpu.VMEM((8, RW_C_PAD), F32)],
        compiler_params=_params("arbitrary"),
        name="layer_in",
    )(*args)


def _attn_kernel(q_ref, k_ref, vt_ref, g_ref, o_ref, m_ref, a_ref, p_ref, acc_ref, s_ref, s2_ref,
                 *, tq, tk, nh):
    qi = pl.program_id(2)
    dv = MLA_V_DIM
    heads = range(nh)

    ext = dv + ATTN_ONES_ROWS

    m_ref[...] = jnp.full_like(m_ref, NEG_BIG)
    a_ref[...] = jnp.ones_like(a_ref)
    p_ref[...] = jnp.zeros_like(p_ref)
    acc_ref[...] = jnp.zeros_like(acc_ref)

    def flush_values(j, q0=0):
        off = pl.multiple_of(j * tk, tk)
        pv = [_dot(vt_ref[h * ext:(h + 1) * ext, pl.ds(off, tk)], p_ref[h, :, q0:])
              for h in heads]
        for h in heads:
            acc_ref[h, :, q0:] = a_ref[h, :, q0:] * acc_ref[h, :, q0:] + pv[h]

    def scores(j, h, dst, q0=0):
        off = pl.multiple_of(j * tk, tk)
        dst[h, :, q0:] = _dot_nt(k_ref[0, pl.ds(off, tk), h * HEAD_PAD:(h + 1) * HEAD_PAD],
                                 q_ref[0, q0:, h * HEAD_PAD:(h + 1) * HEAD_PAD])

    def softmax(h, src, key0=None, q0=0):
        s = src[h, :, q0:]
        if key0 is not None:
            ck = (lax.broadcasted_iota(jnp.int32, s.shape, 0) + key0) // CHUNK
            cq = (lax.broadcasted_iota(jnp.int32, s.shape, 1) + q0) // CHUNK
            s = jnp.where(ck <= cq, s, NEG_BIG)
        m_old = m_ref[h, :, q0:]
        m_new = jnp.maximum(m_old, jnp.max(s, axis=0, keepdims=True))
        p_ref[h, :, q0:] = jnp.exp2((s - m_new).astype(BF16))
        a_ref[h, :, q0:] = jnp.exp2(m_old - m_new)
        m_ref[h, :, q0:] = m_new

    def stage(j, src, dst, key0=None, q0=0, next_q0=0):
        flush_values(jnp.maximum(j - 1, 0))
        for h in heads:
            if dst is not None:
                scores(j + 1, h, dst, next_q0)
            softmax(h, src, key0, q0)

    for h in heads:
        scores(0, h, s_ref)

    @pl.loop(0, qi)
    def _(t):
        stage(2 * t, s_ref, s2_ref)
        stage(2 * t + 1, s2_ref, s_ref)

    stage(2 * qi, s_ref, s2_ref, key0=0, next_q0=tk)
    stage(2 * qi + 1, s2_ref, None, key0=tk, q0=tk)
    flush_values(2 * qi + 1, q0=tk)
    outs = []
    for h in heads:
        acc = acc_ref[h]
        o = acc[:dv] / acc[dv:dv + 1]
        ms = jnp.sum(o * o, axis=0, keepdims=True) * (1.0 / dv)
        outs.append(o * lax.rsqrt(ms + NORM_EPS))
    o_t = jnp.concatenate(outs, axis=0)
    o_ref[0] = (o_t.T * g_ref[...]).astype(BF16)


def _mla_attention(q, k, vt, g, seq, tk, nh):
    tq = 2 * tk
    b = q.shape[0] // seq
    hw = q.shape[1]
    groups = hw // (nh * HEAD_PAD)
    ow = nh * MLA_V_DIM
    q3 = q.reshape(b, seq, hw)
    k3 = k.reshape(b, seq, hw)
    return pl.pallas_call(
        functools.partial(_attn_kernel, tq=tq, tk=tk, nh=nh),
        grid=(b, groups, seq // tq),
        in_specs=[pl.BlockSpec((1, tq, nh * HEAD_PAD), lambda bi, h, i: (bi, i, h)),
                  pl.BlockSpec((1, seq, nh * HEAD_PAD), lambda bi, h, i: (bi, 0, h)),
                  pl.BlockSpec((nh * (MLA_V_DIM + ATTN_ONES_ROWS), seq), lambda bi, h, i: (h, bi)),
                  pl.BlockSpec((1, ow), lambda bi, h, i: (0, h))],
        out_specs=pl.BlockSpec((1, tq, ow), lambda bi, h, i: (bi, i, h)),
        out_shape=jax.ShapeDtypeStruct((b, seq, groups * ow), BF16),
        scratch_shapes=[pltpu.VMEM((nh, 1, tq), F32), pltpu.VMEM((nh, 1, tq), F32),
                        pltpu.VMEM((nh, tk, tq), BF16),
                        pltpu.VMEM((nh, MLA_V_DIM + ATTN_ONES_ROWS, tq), F32),
                        pltpu.VMEM((nh, tk, tq), F32), pltpu.VMEM((nh, tk, tq), F32)],
        compiler_params=_params("parallel", "parallel", "arbitrary"),
        name="mla_attention",
    )(q3, k3, vt, g)


_DIMS = {"nn": (((1,), (0,)), ((), ())), "nt": (((1,), (1,)), ((), ())),
         "tn": (((0,), (0,)), ((), ()))}


def _split(x):
    hi = x.astype(BF16)
    return hi, (x - hi.astype(F32)).astype(BF16)


def _mm(a, b, mode, dims="nn"):
    dg = lambda x, y: lax.dot_general(x, y, _DIMS[dims], preferred_element_type=F32)
    if mode == "bf16":
        return dg(a.astype(BF16), b.astype(BF16))
    if mode == "x3":
        a_hi, a_lo = _split(a)
        b_hi, b_lo = _split(b)
        return dg(a_hi, b_hi) + (dg(a_lo, b_hi) + dg(a_hi, b_lo))
    if mode == "ax":
        n = b.shape[1]
        out = dg(a.astype(BF16), jnp.concatenate(_split(b), axis=1))
        return out[:, :n] + out[:, n:]
    return lax.dot_general(a, b, _DIMS[dims], preferred_element_type=F32, precision=HI)


def _unit_lower_solve_stages(a_list, rhs_list, row, col, eye, mode, out):
    d16 = (row // 16) == (col // 16)
    d32 = (row // 32) == (col // 32)
    n = range(len(a_list))
    d = [jnp.where(d16, a, 0.0) for a in a_list]
    x = [eye + d[i] for i in n]
    d2 = [_mm(d[i], d[i], mode) for i in n]
    yield
    t = [_mm(d2[i], jnp.concatenate([x[i], d2[i]], axis=1), mode) for i in n]
    x = [x[i] + t[i][:, :128] for i in n]
    d4 = [t[i][:, 128:] for i in n]
    yield
    t = [_mm(d4[i], jnp.concatenate([x[i], d4[i]], axis=1), mode) for i in n]
    x = [x[i] + t[i][:, :128] for i in n]
    d8 = [t[i][:, 128:] for i in n]
    yield
    t = [_mm(d8[i], x[i], mode) for i in n]
    x = [x[i] + t[i] for i in n]
    yield
    e = [jnp.where(d32, a_list[i], 0.0) - d[i] for i in n]
    t = [_mm(e[i], x[i], mode) for i in n]
    yield
    t = [_mm(x[i], t[i], mode) for i in n]
    x = [x[i] + t[i] for i in n]
    yield
    f = [jnp.where(d32, 0.0, a_list[i]) for i in n]
    z = [_mm(x[i], rhs_list[i], mode) for i in n]
    yield
    t = [_mm(f[i], z[i], mode) for i in n]
    yield
    t = [_mm(x[i], t[i], mode) for i in n]
    out.extend(z[i] + t[i] for i in n)
    yield


def _wkv_kernel(r_ref, k_ref, v_ref, kk_ref, kka_ref, lw_ref, g_ref,
                rk_ref, lnw_ref, lnb_ref, o_ref, s_ref, *, nch, modes):
    m_quad, m_inv, m_state = modes
    c_len = WKV_CHUNK

    npair = s_ref.shape[0]

    @pl.when(pl.program_id(1) == 0)
    def _():
        s_ref[...] = jnp.zeros_like(s_ref)

    row = lax.broadcasted_iota(jnp.int32, (128, 128), 0)
    col = lax.broadcasted_iota(jnp.int32, (128, 128), 1)
    same = (row // c_len) == (col // c_len)
    mask_s = same & (col < row)
    mask_i = same & (col <= row)
    eye = jnp.where(row == col, 1.0, 0.0)
    lane0 = lax.broadcasted_iota(jnp.int32, (c_len, 128), 1) < RWKV_HEAD
    tr = lax.broadcasted_iota(jnp.int32, (c_len, c_len), 0)
    tc = lax.broadcasted_iota(jnp.int32, (c_len, c_len), 1)
    ltri = jnp.where(tc <= tr, 1.0, 0.0)

    def stack(x):
        return jnp.concatenate([jnp.where(lane0, x, 0.0), jnp.where(lane0, 0.0, x)], axis=0)

    def head_sums(x):
        first = jnp.sum(jnp.where(lane0, x, 0.0), axis=-1, keepdims=True)
        both = jnp.sum(x, axis=-1, keepdims=True)
        return jnp.where(lane0, first, both - first)

    pairs = range(npair)
    par = {}
    wus = {}

    def independent_part(chunks):
        keys = [(c, hp) for c in chunks for hp in pairs]
        for c, hp in keys:
            sl = pl.ds(c * c_len, c_len)
            ln = slice(hp * LANE, (hp + 1) * LANE)
            r = r_ref[0, sl, ln].astype(F32)
            k = k_ref[0, sl, ln].astype(F32)
            v = v_ref[0, sl, ln].astype(F32)
            kk = kk_ref[0, sl, ln].astype(F32)
            kka = kka_ref[0, sl, ln].astype(F32)
            lw = lw_ref[0, sl, ln]
            cum = _mm(ltri, lw, "ax")
            tot = cum[c_len - 1:c_len, :]
            p_in = jnp.exp(cum)
            p_ex = jnp.exp(cum - lw)
            p_inv = jnp.exp(-cum)
            p_end = jnp.exp(tot - cum)
            at = stack(-kk * p_ex)
            rt = stack(r * p_in)
            bt = kka * p_inv
            kt = k * p_inv
            par[c, hp] = dict(
                r=r, k=k, v=v, tot=tot, at=at, rt=rt, vs=stack(v),
                bkh=jnp.concatenate([stack(kka * p_end), stack(k * p_end)], axis=0),
                lhs=jnp.concatenate([at, rt], axis=0),
                rhs=jnp.concatenate([bt, bt, kt, kt], axis=0))
        yield
        for key in keys:
            p = par[key]
            quad = _mm(p["lhs"], p["rhs"], m_quad, "nt")
            p["a_ab"] = jnp.where(mask_s, quad[:128, :128], 0.0)
            p["a_ak"] = jnp.where(mask_s, quad[:128, 128:], 0.0)
            p["a_r"] = jnp.concatenate([jnp.where(mask_i, quad[128:, :128], 0.0),
                                        jnp.where(mask_i, quad[128:, 128:], 0.0)], axis=1)
        yield
        for key in keys:
            par[key]["akv"] = _mm(par[key]["a_ak"], par[key]["vs"], m_quad)
        yield
        solved = []
        yield from _unit_lower_solve_stages(
            [par[key]["a_ab"] for key in keys],
            [jnp.concatenate([par[key]["at"], par[key]["akv"]], axis=1) for key in keys],
            row, col, eye, m_inv, solved)
        wus.update(zip(keys, solved))

    state = [s_ref[hp] for hp in pairs]

    def outputs(c, m1, uv):
        sl = pl.ds(c * c_len, c_len)
        pc = [par[c, hp] for hp in pairs]
        ys = [m1[hp][128:] + _mm(pc[hp]["a_r"], uv[hp], m_state) for hp in pairs]
        y = [ys[hp][:c_len] + ys[hp][c_len:] for hp in pairs]
        mean = [head_sums(y[hp]) * (1.0 / RWKV_HEAD) for hp in pairs]
        yc = [y[hp] - mean[hp] for hp in pairs]
        var = [head_sums(yc[hp] * yc[hp]) * (1.0 / RWKV_HEAD) for hp in pairs]
        for hp in pairs:
            ln = slice(hp * LANE, (hp + 1) * LANE)
            r, k, v = pc[hp]["r"], pc[hp]["k"], pc[hp]["v"]
            yn = yc[hp] * lax.rsqrt(var[hp] + GN_EPS) * lnw_ref[:, ln] + lnb_ref[:, ln]
            bonus = head_sums(r * k * rk_ref[:, ln])
            yn = yn + bonus * v
            o_ref[0, sl, ln] = (yn * g_ref[0, sl, ln].astype(F32)).astype(BF16)

    pending = []

    def state_part(chunks):
        for c in chunks:
            pc = [par[c, hp] for hp in pairs]
            wu = [wus[c, hp] for hp in pairs]
            m1 = [_mm(jnp.concatenate([wu[hp][:, :128], pc[hp]["rt"]], axis=0), state[hp],
                      m_state, "nt") for hp in pairs]
            uv = [jnp.concatenate([m1[hp][:128] + wu[hp][:, 128:], pc[hp]["vs"]], axis=0)
                  for hp in pairs]
            upd = [_mm(uv[hp], pc[hp]["bkh"], m_state, "tn") for hp in pairs]
            for hp in pairs:
                state[hp] = state[hp] * jnp.exp(pc[hp]["tot"]) + upd[hp]
            yield
            if pending:
                outputs(*pending.pop())
                yield
            pending.append((c, m1, uv))

    groups = [list(range(g, min(g + WKV_GROUP, nch))) for g in range(0, nch, WKV_GROUP)]
    for _ in independent_part(groups[0]):
        pass
    for g, chunks in enumerate(groups):
        chain = state_part(chunks)
        ahead = independent_part(groups[g + 1]) if g + 1 < len(groups) else iter(())
        for _ in chain:
            for _ in range(WKV_AHEAD_STAGES):
                next(ahead, None)
        for _ in ahead:
            pass
    outputs(*pending.pop())
    for hp in pairs:
        s_ref[hp] = state[hp]


def _wkv(r, k, v, kk, kka, lw, g, r_k, ln_w, ln_b, rows, modes):
    b, s, w = r.shape
    blk = pl.BlockSpec((1, rows, w), lambda bi, i: (bi, i, 0))
    kern = functools.partial(_wkv_kernel, nch=rows // WKV_CHUNK, modes=modes)
    return pl.pallas_call(
        kern,
        grid=(b, s // rows),
        in_specs=[blk] * 7 + [_full((1, w))] * 3,
        out_specs=blk,
        out_shape=jax.ShapeDtypeStruct((b, s, w), BF16),
        scratch_shapes=[pltpu.VMEM((w // LANE, LANE, LANE), F32)],
        compiler_params=_params("parallel", "arbitrary"),
        name="wkv7",
    )(r, k, v, kk, kka, lw, g, r_k, ln_w, ln_b)


def _mem_kv_kernel(m_ref, g_ref, w_ref, o_ref):
    d = m_ref.shape[-1]
    mn = _rms(m_ref[...], g_ref[...], d).astype(BF16)
    o_ref[...] = _dot(mn, w_ref[0]).astype(BF16)


def _mem_kv(mem2d, g, wkv, tm):
    t, d = mem2d.shape
    nl, _, n = wkv.shape
    return pl.pallas_call(
        _mem_kv_kernel,
        grid=(nl, t // tm),
        in_specs=[pl.BlockSpec((tm, d), lambda l, i: (i, 0)), _full((1, d)),
                  pl.BlockSpec((1, d, n), lambda l, i: (l, 0, 0))],
        out_specs=pl.BlockSpec((tm, n), lambda l, i: (i, l)),
        out_shape=jax.ShapeDtypeStruct((t, nl * n), BF16),
        compiler_params=_params("parallel", "parallel"),
        name="mem_kv",
    )(mem2d, g, wkv)


def _mix_out_kernel(h_ref, att_ref, rw_ref, woa_ref, wob_ref, g_ref, wq_ref,
                    mk_ref, mv_ref, wo_ref, o_ref, *, q_scale):
    d = h_ref.shape[-1]
    h1 = h_ref[...] + _dot(att_ref[...], woa_ref[0]) + _dot(rw_ref[...], wob_ref[0])
    hn = _rms(h1, g_ref[0], d).astype(BF16)
    q = (_dot(hn, wq_ref[0]) * q_scale).astype(BF16)
    dh = d // CA_HEADS
    outs = []
    for hd in range(CA_HEADS):
        sl = slice(hd * dh, (hd + 1) * dh)
        s = _dot_nt(q[:, sl], mk_ref[:, sl])
        m = jnp.max(s, axis=-1, keepdims=True)
        p = jnp.exp2(s - m)
        l = jnp.sum(p, axis=-1, keepdims=True)
        outs.append((_dot(p.astype(BF16), mv_ref[:, sl]) / l).astype(BF16))
    o = jnp.concatenate(outs, axis=1)
    o_ref[...] = h1 + _dot(o, wo_ref[0])


def _mix_out(h, att, rw, woa, wob, g, wq, memkv, wo, layer, seq, mem_len, tm):
    t, d = h.shape
    tiles = seq // tm
    q_scale = (d // CA_HEADS) ** -0.5 * LOG2E
    row = lambda w: pl.BlockSpec((tm, w), lambda i: (i, 0))
    lay = functools.partial(_layer_slice, layer=layer)
    return pl.pallas_call(
        functools.partial(_mix_out_kernel, q_scale=q_scale),
        grid=(t // tm,),
        in_specs=[row(d), row(att.shape[1]), row(rw.shape[1]), lay(woa), lay(wob), lay(g),
                  lay(wq),
                  pl.BlockSpec((mem_len, d), lambda i: (i // tiles, 2 * layer)),
                  pl.BlockSpec((mem_len, d), lambda i: (i // tiles, 2 * layer + 1)),
                  lay(wo)],
        out_specs=row(d),
        out_shape=jax.ShapeDtypeStruct((t, d), F32),
        compiler_params=_params("parallel"),
        name="mix_out_cross_attention",
    )(h, att, rw, woa, wob, g, wq, memkv, memkv, wo)


def _mlp_kernel(h_ref, g_ref, wu_ref, wd_ref, gf_ref, o_ref, *, last, ff_chunk):
    d = h_ref.shape[-1]
    h = h_ref[...]
    hn = _rms(h, g_ref[0], d).astype(BF16)
    acc = h
    for c in range(wu_ref.shape[-1] // ff_chunk):
        sl = slice(c * ff_chunk, (c + 1) * ff_chunk)
        u = jnp.maximum(_dot(hn, wu_ref[0, :, sl]), 0.0)
        acc = acc + _dot((u * u).astype(BF16), wd_ref[0, sl, :])
    if last:
        acc = _rms(acc, gf_ref[...], d)
    o_ref[...] = acc


def _mlp(h, g, wu, wd, gf, layer, last, tm):
    t, d = h.shape
    row = pl.BlockSpec((tm, d), lambda i: (i, 0))
    lay = functools.partial(_layer_slice, layer=layer)
    return pl.pallas_call(
        functools.partial(_mlp_kernel, last=last, ff_chunk=1024),
        grid=(t // tm,),
        in_specs=[row, lay(g), lay(wu), lay(wd), _full((1, d))],
        out_specs=row,
        out_shape=jax.ShapeDtypeStruct((t, d), F32),
        compiler_params=_params("parallel"),
        name="mlp",
    )(h, g, wu, wd, gf)


def _pad_cols(a, n):
    return jnp.pad(a, ((0, 0), (0, n - a.shape[1])))


def _swap_halves(a):
    half = a.shape[-1] // 2
    return jnp.concatenate([a[..., half:], a[..., :half]], axis=-1)


def _in_weights(w_in, mu):
    cq = w_in[:, :MLA_Q_RANK]
    ckv = w_in[:, MLA_Q_RANK:MLA_Q_RANK + MLA_KV_RANK]
    kr = w_in[:, MLA_Q_RANK + MLA_KV_RANK:MLA_COLS]
    mla = jnp.concatenate([_pad_cols(cq, MLA_Q_PAD), ckv, kr, _swap_halves(kr)], axis=1)
    w = jnp.concatenate([_pad_cols(mla, MLA_C_PAD), _pad_cols(w_in[:, MLA_COLS:], RW_C_PAD)], axis=1)
    return w.astype(BF16), _pad_cols(mu[None, :], RW_C_PAD)


def _head_pad(a, n_heads, per_head, lo, hi, at):
    rows = a.shape[0]
    a = a.reshape(rows, n_heads, per_head)[:, :, lo:hi]
    a = jnp.pad(a, ((0, 0), (0, 0), (at, HEAD_PAD - at - (hi - lo))))
    return a.reshape(rows, n_heads * HEAD_PAD)


def _mla_weights(w_uq, w_ukv):
    dq = MLA_NOPE_DIM + MLA_ROPE_DIM
    h = MLA_HEADS
    wq3 = w_uq.reshape(MLA_Q_RANK, h, dq)
    rope_sw = _swap_halves(wq3[:, :, MLA_NOPE_DIM:]).reshape(MLA_Q_RANK, h * MLA_ROPE_DIM)
    wqa = _head_pad(w_uq, h, dq, 0, dq, 0)
    wqb = _head_pad(rope_sw, h, MLA_ROPE_DIM, 0, MLA_ROPE_DIM, MLA_NOPE_DIM)
    pad_q = ((0, MLA_Q_PAD - MLA_Q_RANK), (0, 0))
    wqa = jnp.pad(wqa, pad_q)
    wqb = jnp.pad(wqb, pad_q)
    dkv = MLA_NOPE_DIM + MLA_V_DIM
    wk = _head_pad(w_ukv, h, dkv, 0, MLA_NOPE_DIM, 0)
    wvt = jnp.transpose(w_ukv.reshape(MLA_KV_RANK, h, dkv)[:, :, MLA_NOPE_DIM:], (1, 2, 0))
    wvt = jnp.pad(wvt, ((0, 0), (0, ATTN_ONES_ROWS), (0, 0))).reshape(-1, MLA_KV_RANK)
    eye = jnp.eye(MLA_ROPE_DIM, dtype=F32)
    place = _head_pad(jnp.tile(eye, (1, h)), h, MLA_ROPE_DIM, 0, MLA_ROPE_DIM, MLA_NOPE_DIM)
    zeros = jnp.zeros_like(place)
    pad_rows = jnp.zeros((LANE - 2 * MLA_ROPE_DIM, h * HEAD_PAD), F32)
    pa = jnp.concatenate([place, zeros, pad_rows], axis=0)
    pb = jnp.concatenate([zeros, place, pad_rows], axis=0)
    wka = jnp.concatenate([wk, pa], axis=0)
    return tuple(a.astype(BF16) for a in (wqa, wqb, wka, pb, wvt))


def _lora_weights(w2, a2, g2, v2):
    rw = RWKV_WIDTH
    z = lambda r: jnp.zeros((r, rw), F32)
    rows = [
        jnp.concatenate([w2, z(DECAY_RANK), z(DECAY_RANK), z(DECAY_RANK)], axis=1),
        jnp.concatenate([z(ICL_RANK), a2, z(ICL_RANK), z(ICL_RANK)], axis=1),
        jnp.concatenate([z(GATE_RANK), z(GATE_RANK), g2, z(GATE_RANK)], axis=1),
        jnp.concatenate([z(VRES_RANK)] * 3 + [v2 if v2 is not None else z(VRES_RANK)], axis=1),
    ]
    w = jnp.concatenate(rows, axis=0)
    return jnp.pad(w, ((0, LORA_PAD - w.shape[0]), (0, 0))).astype(BF16)


def _block_diag(n, blk, value):
    i = jnp.arange(n)
    return jnp.where((i[:, None] // blk) == (i[None, :] // blk), value, 0.0)


def kernel(x, mem, positions, mix_norm, w_in_first, w_in_rest, shift_mu_first, shift_mu_rest, mla_q_norm, mla_w_uq, mla_kv_norm, mla_w_ukv, mla_out_norm, rwkv_w0, rwkv_w2, rwkv_a0, rwkv_a2, rwkv_g2, rwkv_v0, rwkv_v2, rwkv_k_k, rwkv_k_a, rwkv_r_k, rwkv_ln_w, rwkv_ln_b, w_out, ca_norm, mem_norm, ca_wq, ca_wkv, ca_wo, mlp_norm, mlp_w_up, mlp_w_down, final_norm):
    b, s, d = x.shape
    depth = mix_norm.shape[0]
    mem_len = mem.shape[1]
    t = b * s
    tm = min(512, s)
    tm_wide = min(1024, s)
    attn_key_block = min(512, s // 2)
    wkv_rows = min(1024, s)
    rw = RWKV_WIDTH

    inv = ROPE_THETA ** (-jnp.arange(0, MLA_ROPE_DIM, 2, dtype=F32) / MLA_ROPE_DIM)
    inv_row = jnp.concatenate([jnp.zeros((MLA_NOPE_DIM,), F32), inv, inv,
                               jnp.zeros((HEAD_PAD - MLA_NOPE_DIM - MLA_ROPE_DIM,), F32)])[None, :]
    ct, st = _rope_tables(positions.astype(F32).reshape(t, 1), inv_row, tm)

    memkv = _mem_kv(mem.reshape(b * mem_len, d), mem_norm[None, :], ca_wkv.astype(BF16),
                    min(512, b * mem_len))

    woa = w_out[:, :MLA_HEADS * MLA_V_DIM].astype(BF16)
    wob = w_out[:, MLA_HEADS * MLA_V_DIM:].astype(BF16)
    wq_b = ca_wq.astype(BF16)
    wo_b = ca_wo.astype(BF16)
    wu_b = mlp_w_up.astype(BF16)
    wd_b = mlp_w_down.astype(BF16)
    ca_g = ca_norm[:, None, :]
    mlp_g = mlp_norm[:, None, :]
    bd512 = _block_diag(rw, RWKV_HEAD, 1.0).astype(BF16)

    h = x.reshape(t, d)
    v_first = None
    for l in range(depth):
        if l == 0:
            w_in, mu, v0, v2 = w_in_first, shift_mu_first, None, None
        else:
            w_in, mu, v0, v2 = w_in_rest[l - 1], shift_mu_rest[l - 1], rwkv_v0[l - 1], rwkv_v2[l - 1]
        w_all, mu_p = _in_weights(w_in, mu)
        wqa, wqb, wka, pb, wvt = _mla_weights(mla_w_uq[l], mla_w_ukv[l])
        gq = _pad_cols(mla_q_norm[l][None, :], MLA_Q_PAD)
        w2c = _lora_weights(rwkv_w2[l], rwkv_a2[l], rwkv_g2[l], v2)
        vec = lambda a: a.reshape(1, rw)
        q, k, vt, r_, k_, v_, kk_, kka_, lw_, g_ = _layer_in(
            h, mix_norm[l][None, :], w_all, mu_p, ct, st, gq, mla_kv_norm[l][None, :],
            wqa, wqb, wka, pb, wvt, v_first, w2c, bd512, vec(rwkv_w0[l]), vec(rwkv_a0[l]),
            vec(rwkv_k_k[l]), vec(rwkv_k_a[l]), None if v0 is None else vec(v0), s, tm_wide)
        if l == 0:
            v_first = v_

        att = _mla_attention(q, k, vt, mla_out_norm[l][None, :], s, attn_key_block,
                             ATTN_HEADS_PER_STEP)

        sh = lambda a: a.reshape(b, s, rw)
        y = _wkv(sh(r_), sh(k_), sh(v_), sh(kk_), sh(kka_), sh(lw_), sh(g_),
                 vec(rwkv_r_k[l]), vec(rwkv_ln_w[l]), vec(rwkv_ln_b[l]), wkv_rows, WKV_MODES)

        h = _mix_out(h, att.reshape(t, -1), y.reshape(t, rw), woa, wob, ca_g, wq_b, memkv, wo_b,
                     l, s, mem_len, tm_wide)
        h = _mlp(h, mlp_g, wu_b, wd_b, final_norm[None, :], l, l == depth - 1, tm_wide)
    return h.reshape(b, s, d)
```

```python
import functools
import math

import jax
import jax.numpy as jnp
from jax import lax
from jax.experimental import pallas as pl
from jax.experimental.pallas import tpu as pltpu

F32 = jnp.float32
BF16 = jnp.bfloat16

CHUNK = 64
NORM_EPS = 1e-6
MLA_HEADS = 8
MLA_V_DIM = 64
MLA_NOPE_DIM = 64
MLA_ROPE_DIM = 32
MLA_Q_RANK = 192
MLA_KV_RANK = 128
ROPE_THETA = 10000.0
RWKV_HEAD = 64
RWKV_HEADS = 8
RWKV_WIDTH = 512
DECAY_RANK = 32
ICL_RANK = 32
GATE_RANK = 96
VRES_RANK = 32
GN_EPS = 64e-5
CA_HEADS = 4
MLA_COLS = MLA_Q_RANK + MLA_KV_RANK + MLA_ROPE_DIM

LANE = 128
HEAD_PAD = 128
MLA_C_PAD = 512
MLA_Q_PAD = 256
MLA_KV_END = MLA_Q_PAD + MLA_KV_RANK
RW_C_PAD = 1792
LORA_PAD = 256
WKV_CHUNK = 64
WKV_GROUP = 4
WKV_AHEAD_STAGES = 2
LAYER_IN_PART_ROWS = 256
ATTN_HEADS_PER_STEP = 2
ATTN_ONES_ROWS = 16
LOG2E = 1.4426950408889634
NEG_BIG = -1e30
VMEM_LIMIT = 56 * 1024 * 1024

HI = lax.Precision.HIGHEST
WKV_MODES = ("bf16", "bf16", "bf16")


def _dot(a, b, precision=None):
    return jnp.dot(a, b, preferred_element_type=F32, precision=precision)


def _dot_nt(a, b, precision=None):
    return lax.dot_general(a, b, (((1,), (1,)), ((), ())),
                           preferred_element_type=F32, precision=precision)


def _rms(x, g, n, eps=NORM_EPS):
    ms = jnp.sum(x * x, axis=-1, keepdims=True) * (1.0 / n)
    return x * lax.rsqrt(ms + eps) * g


def _sigmoid(x):
    return 0.5 + 0.5 * jnp.tanh(0.5 * x)


def _params(*sem):
    return pltpu.CompilerParams(dimension_semantics=sem, vmem_limit_bytes=VMEM_LIMIT)


def _full(shape):
    nd = len(shape)
    return pl.BlockSpec(shape, lambda *_: (0,) * nd, pipeline_mode=pl.Buffered(1))


def _layer_slice(a, layer):
    nd = a.ndim
    return pl.BlockSpec((1,) + a.shape[1:], lambda i: (layer,) + (0,) * (nd - 1),
                        pipeline_mode=pl.Buffered(1))


def _rope_kernel(pos_ref, inv_ref, ct_ref, st_ref):
    ang = pos_ref[...] * inv_ref[...]
    c = jnp.cos(ang)
    s = jnp.sin(ang)
    lane = lax.broadcasted_iota(jnp.int32, ang.shape, 1)
    half = MLA_NOPE_DIM + MLA_ROPE_DIM // 2
    end = MLA_NOPE_DIM + MLA_ROPE_DIM
    ct_ref[...] = jnp.where(lane < MLA_NOPE_DIM, 1.0, jnp.where(lane < end, c, 0.0))
    st_ref[...] = jnp.where(lane < MLA_NOPE_DIM, 0.0,
                            jnp.where(lane < half, -s, jnp.where(lane < end, s, 0.0)))


def _rope_tables(pos_f, inv_row, tm):
    t = pos_f.shape[0]
    return pl.pallas_call(
        _rope_kernel,
        grid=(t // tm,),
        in_specs=[pl.BlockSpec((tm, 1), lambda i: (i, 0)), _full((1, LANE))],
        out_specs=[pl.BlockSpec((tm, LANE), lambda i: (i, 0))] * 2,
        out_shape=[jax.ShapeDtypeStruct((t, LANE), F32)] * 2,
        compiler_params=_params("parallel"),
        name="rope_tables",
    )(pos_f, inv_row)


def _mla_latents(c, ct_ref, st_ref, gq_ref, gkv_ref, wqa_ref, wqb_ref, wka_ref, pb_ref,
                 wv_ref, ones_ref, q_ref, k_ref, v_ref, q_scale):
    nq = _rms(c[:, :MLA_Q_PAD], gq_ref[...], MLA_Q_RANK).astype(BF16)
    nkv = _rms(c[:, MLA_Q_PAD:MLA_KV_END], gkv_ref[...], MLA_KV_RANK).astype(BF16)
    kr = c[:, MLA_KV_END:MLA_C_PAD].astype(BF16)
    ct = jnp.tile(ct_ref[...], (1, MLA_HEADS))
    st = jnp.tile(st_ref[...], (1, MLA_HEADS))
    q = _dot(nq, wqa_ref[...]) * ct + _dot(nq, wqb_ref[...]) * st
    q_ref[...] = (q * q_scale).astype(BF16)
    x = jnp.concatenate([nkv, kr], axis=1)
    k = _dot(x, wka_ref[...]) * ct + _dot(kr, pb_ref[...]) * st
    k_ref[...] = k.astype(BF16)
    v_ref[...] = (_dot_nt(wv_ref[...], nkv) + ones_ref[...]).astype(BF16)


def _rwkv_elementwise(x, vf_ref, w2_ref, bd_ref, w0_ref, a0_ref, kk_ref_p, ka_ref, v0_ref,
                      r_o, k_o, v_o, kk_o, kka_o, lw_o, g_o):
    rw = RWKV_WIDTH
    r_o[...] = x[:, 0:rw].astype(BF16)
    k = x[:, rw:2 * rw]
    v = x[:, 2 * rw:3 * rw]
    tail = x[:, 3 * rw:3 * rw + LORA_PAD]
    lane = lax.broadcasted_iota(jnp.int32, tail.shape, 1)
    g_lo = DECAY_RANK + ICL_RANK
    g_hi = g_lo + GATE_RANK
    is_gate = (lane >= g_lo) & (lane < g_hi)
    th = jnp.tanh(jnp.where(is_gate, 0.5 * tail, tail))
    act = jnp.where(lane < DECAY_RANK, th, jnp.where(is_gate, 0.5 + 0.5 * th, tail))
    lo = _dot(act.astype(BF16), w2_ref[...])
    lw_o[...] = -math.exp(-0.5) * _sigmoid(w0_ref[...] + lo[:, 0:rw])
    a = _sigmoid(a0_ref[...] + lo[:, rw:2 * rw])
    g_o[...] = lo[:, 2 * rw:3 * rw].astype(BF16)
    if vf_ref is not None:
        mix = _sigmoid(v0_ref[...] + lo[:, 3 * rw:4 * rw])
        v = v + (vf_ref[...].astype(F32) - v) * mix
    v_o[...] = v.astype(BF16)
    kk = k * kk_ref_p[...]
    sq = kk * kk
    sq_hi = sq.astype(BF16)
    sq_lo = (sq - sq_hi.astype(F32)).astype(BF16)
    ss = _dot(sq_hi, bd_ref[...]) + _dot(sq_lo, bd_ref[...])
    kk = kk * lax.rsqrt(jnp.maximum(ss, 1e-24))
    kk_o[...] = kk.astype(BF16)
    kka_o[...] = (kk * a).astype(BF16)
    k_o[...] = (k * (1.0 + (a - 1.0) * ka_ref[...])).astype(BF16)


def _layer_in_kernel(*refs, first, tiles_per_seq, q_scale):
    n_in = 20 if first else 22
    ins, outs, carry_ref = refs[:n_in], refs[n_in:n_in + 10], refs[n_in + 10]
    h_ref, g_ref, w_ref, mu_ref = ins[:4]
    mla_ins = ins[4:14]
    if first:
        w2_ref, bd_ref, w0_ref, a0_ref, kkp_ref, ka_ref = ins[14:20]
        vf_ref = v0_ref = None
    else:
        vf_ref, w2_ref, bd_ref, w0_ref, a0_ref, kkp_ref, ka_ref, v0_ref = ins[14:22]
    q_ref, k_ref, vt_ref = outs[:3]
    rw_outs = outs[3:]

    i = pl.program_id(0)
    tm, d = h_ref.shape
    hr = min(LAYER_IN_PART_ROWS, tm)

    @pl.when(i % tiles_per_seq == 0)
    def _():
        carry_ref[...] = jnp.zeros_like(carry_ref)

    parts = [pl.ds(p * hr, hr) for p in range(tm // hr)]
    hn = [_rms(h_ref[rows, :], g_ref[...], d).astype(BF16) for rows in parts]
    proj = [_dot(x, w_ref[...]) for x in hn]
    last = carry_ref[7:8, :]
    for rows, pr in zip(parts, proj):
        ct_v, st_v = mla_ins[0].at[rows], mla_ins[1].at[rows]
        _mla_latents(pr[:, :MLA_C_PAD], ct_v, st_v, *mla_ins[2:], q_ref.at[rows], k_ref.at[rows],
                     vt_ref.at[:, rows], q_scale)
        z = pr[:, MLA_C_PAD:]
        prev = pltpu.roll(z, 1, axis=0)
        row = lax.broadcasted_iota(jnp.int32, z.shape, 0)
        prev = jnp.where(row == 0, last, prev)
        x = z + (prev - z) * mu_ref[...]
        last = z[hr - 1:hr, :]
        _rwkv_elementwise(x, None if vf_ref is None else vf_ref.at[rows], w2_ref, bd_ref, w0_ref,
                          a0_ref, kkp_ref, ka_ref, v0_ref, *[o.at[rows] for o in rw_outs])
    carry_ref[...] = z[hr - 8:, :]


def _layer_in(h, g, w, mu, ct, st, gq, gkv, wqa, wqb, wka, pb, wvt, vf, w2, bd, w0, a0, k_k,
              k_a, v0, seq, tm):
    t, d = h.shape
    rw = RWKV_WIDTH
    hw = MLA_HEADS * HEAD_PAD
    vw = wvt.shape[0]
    first = vf is None
    ones_col = (jnp.arange(vw) % (MLA_V_DIM + ATTN_ONES_ROWS) == MLA_V_DIM).astype(F32)[:, None]
    q_scale = (MLA_NOPE_DIM + MLA_ROPE_DIM) ** -0.5 * LOG2E
    row = lambda width: pl.BlockSpec((tm, width), lambda i: (i, 0))
    vec = _full((1, rw))
    args = [h, g, w, mu, ct, st, gq, gkv, wqa, wqb, wka, pb, wvt, ones_col]
    in_specs = [row(d), _full((1, d)), _full(w.shape), _full((1, RW_C_PAD)), row(LANE), row(LANE),
                _full(gq.shape), _full(gkv.shape), _full(wqa.shape), _full(wqb.shape),
                _full(wka.shape), _full(pb.shape), _full(wvt.shape), _full(ones_col.shape)]
    if first:
        args += [w2, bd, w0, a0, k_k, k_a]
        in_specs += [_full(w2.shape), _full(bd.shape), vec, vec, vec, vec]
    else:
        args += [vf, w2, bd, w0, a0, k_k, k_a, v0]
        in_specs += [row(rw), _full(w2.shape), _full(bd.shape), vec, vec, vec, vec, vec]
    rw_dt = [BF16, BF16, BF16, BF16, BF16, F32, BF16]
    kern = functools.partial(_layer_in_kernel, first=first, tiles_per_seq=seq // tm,
                             q_scale=q_scale)
    return pl.pallas_call(
        kern,
        grid=(t // tm,),
        in_specs=in_specs,
        out_specs=[row(hw), row(hw), pl.BlockSpec((vw, tm), lambda i: (0, i))] + [row(rw)] * 7,
        out_shape=[jax.ShapeDtypeStruct((t, hw), BF16), jax.ShapeDtypeStruct((t, hw), BF16),
                   jax.ShapeDtypeStruct((vw, t), BF16)]
                  + [jax.ShapeDtypeStruct((t, rw), dt) for dt in rw_dt],
        scratch_shapes=[pltpu.VMEM((8, RW_C_PAD), F32)],
        compiler_params=_params("arbitrary"),
        name="layer_in",
    )(*args)


def _attn_kernel(q_ref, k_ref, vt_ref, g_ref, o_ref, m_ref, a_ref, p_ref, acc_ref, s_ref, s2_ref,
                 *, tq, tk, nh):
    qi = pl.program_id(2)
    dv = MLA_V_DIM
    heads = range(nh)

    ext = dv + ATTN_ONES_ROWS

    m_ref[...] = jnp.full_like(m_ref, NEG_BIG)
    a_ref[...] = jnp.ones_like(a_ref)
    p_ref[...] = jnp.zeros_like(p_ref)
    acc_ref[...] = jnp.zeros_like(acc_ref)

    def flush_values(j, q0=0):
        off = pl.multiple_of(j * tk, tk)
        pv = [_dot(vt_ref[h * ext:(h + 1) * ext, pl.ds(off, tk)], p_ref[h, :, q0:])
              for h in heads]
        for h in heads:
            acc_ref[h, :, q0:] = a_ref[h, :, q0:] * acc_ref[h, :, q0:] + pv[h]

    def scores(j, h, dst, q0=0):
        off = pl.multiple_of(j * tk, tk)
        dst[h, :, q0:] = _dot_nt(k_ref[0, pl.ds(off, tk), h * HEAD_PAD:(h + 1) * HEAD_PAD],
                                 q_ref[0, q0:, h * HEAD_PAD:(h + 1) * HEAD_PAD])

    def softmax(h, src, key0=None, q0=0):
        s = src[h, :, q0:]
        if key0 is not None:
            ck = (lax.broadcasted_iota(jnp.int32, s.shape, 0) + key0) // CHUNK
            cq = (lax.broadcasted_iota(jnp.int32, s.shape, 1) + q0) // CHUNK
            s = jnp.where(ck <= cq, s, NEG_BIG)
        m_old = m_ref[h, :, q0:]
        m_new = jnp.maximum(m_old, jnp.max(s, axis=0, keepdims=True))
        p_ref[h, :, q0:] = jnp.exp2((s - m_new).astype(BF16))
        a_ref[h, :, q0:] = jnp.exp2(m_old - m_new)
        m_ref[h, :, q0:] = m_new

    def stage(j, src, dst, key0=None, q0=0, next_q0=0):
        flush_values(jnp.maximum(j - 1, 0))
        for h in heads:
            if dst is not None:
                scores(j + 1, h, dst, next_q0)
            softmax(h, src, key0, q0)

    for h in heads:
        scores(0, h, s_ref)

    @pl.loop(0, qi)
    def _(t):
        stage(2 * t, s_ref, s2_ref)
        stage(2 * t + 1, s2_ref, s_ref)

    stage(2 * qi, s_ref, s2_ref, key0=0, next_q0=tk)
    stage(2 * qi + 1, s2_ref, None, key0=tk, q0=tk)
    flush_values(2 * qi + 1, q0=tk)
    outs = []
    for h in heads:
        acc = acc_ref[h]
        o = acc[:dv] / acc[dv:dv + 1]
        ms = jnp.sum(o * o, axis=0, keepdims=True) * (1.0 / dv)
        outs.append(o * lax.rsqrt(ms + NORM_EPS))
    o_t = jnp.concatenate(outs, axis=0)
    o_ref[0] = (o_t.T * g_ref[...]).astype(BF16)


def _mla_attention(q, k, vt, g, seq, tk, nh):
    tq = 2 * tk
    b = q.shape[0] // seq
    hw = q.shape[1]
    groups = hw // (nh * HEAD_PAD)
    ow = nh * MLA_V_DIM
    q3 = q.reshape(b, seq, hw)
    k3 = k.reshape(b, seq, hw)
    return pl.pallas_call(
        functools.partial(_attn_kernel, tq=tq, tk=tk, nh=nh),
        grid=(b, groups, seq // tq),
        in_specs=[pl.BlockSpec((1, tq, nh * HEAD_PAD), lambda bi, h, i: (bi, i, h)),
                  pl.BlockSpec((1, seq, nh * HEAD_PAD), lambda bi, h, i: (bi, 0, h)),
                  pl.BlockSpec((nh * (MLA_V_DIM + ATTN_ONES_ROWS), seq), lambda bi, h, i: (h, bi)),
                  pl.BlockSpec((1, ow), lambda bi, h, i: (0, h))],
        out_specs=pl.BlockSpec((1, tq, ow), lambda bi, h, i: (bi, i, h)),
        out_shape=jax.ShapeDtypeStruct((b, seq, groups * ow), BF16),
        scratch_shapes=[pltpu.VMEM((nh, 1, tq), F32), pltpu.VMEM((nh, 1, tq), F32),
                        pltpu.VMEM((nh, tk, tq), BF16),
                        pltpu.VMEM((nh, MLA_V_DIM + ATTN_ONES_ROWS, tq), F32),
                        pltpu.VMEM((nh, tk, tq), F32), pltpu.VMEM((nh, tk, tq), F32)],
        compiler_params=_params("parallel", "parallel", "arbitrary"),
        name="mla_attention",
    )(q3, k3, vt, g)


_DIMS = {"nn": (((1,), (0,)), ((), ())), "nt": (((1,), (1,)), ((), ())),
         "tn": (((0,), (0,)), ((), ()))}


def _split(x):
    hi = x.astype(BF16)
    return hi, (x - hi.astype(F32)).astype(BF16)


def _mm(a, b, mode, dims="nn"):
    dg = lambda x, y: lax.dot_general(x, y, _DIMS[dims], preferred_element_type=F32)
    if mode == "bf16":
        return dg(a.astype(BF16), b.astype(BF16))
    if mode == "x3":
        a_hi, a_lo = _split(a)
        b_hi, b_lo = _split(b)
        return dg(a_hi, b_hi) + (dg(a_lo, b_hi) + dg(a_hi, b_lo))
    if mode == "ax":
        n = b.shape[1]
        out = dg(a.astype(BF16), jnp.concatenate(_split(b), axis=1))
        return out[:, :n] + out[:, n:]
    return lax.dot_general(a, b, _DIMS[dims], preferred_element_type=F32, precision=HI)


def _unit_lower_solve_stages(a_list, rhs_list, row, col, eye, mode, out):
    d16 = (row // 16) == (col // 16)
    d32 = (row // 32) == (col // 32)
    n = range(len(a_list))
    d = [jnp.where(d16, a, 0.0) for a in a_list]
    x = [eye + d[i] for i in n]
    d2 = [_mm(d[i], d[i], mode) for i in n]
    yield
    t = [_mm(d2[i], jnp.concatenate([x[i], d2[i]], axis=1), mode) for i in n]
    x = [x[i] + t[i][:, :128] for i in n]
    d4 = [t[i][:, 128:] for i in n]
    yield
    t = [_mm(d4[i], jnp.concatenate([x[i], d4[i]], axis=1), mode) for i in n]
    x = [x[i] + t[i][:, :128] for i in n]
    d8 = [t[i][:, 128:] for i in n]
    yield
    t = [_mm(d8[i], x[i], mode) for i in n]
    x = [x[i] + t[i] for i in n]
    yield
    e = [jnp.where(d32, a_list[i], 0.0) - d[i] for i in n]
    t = [_mm(e[i], x[i], mode) for i in n]
    yield
    t = [_mm(x[i], t[i], mode) for i in n]
    x = [x[i] + t[i] for i in n]
    yield
    f = [jnp.where(d32, 0.0, a_list[i]) for i in n]
    z = [_mm(x[i], rhs_list[i], mode) for i in n]
    yield
    t = [_mm(f[i], z[i], mode) for i in n]
    yield
    t = [_mm(x[i], t[i], mode) for i in n]
    out.extend(z[i] + t[i] for i in n)
    yield


def _wkv_kernel(r_ref, k_ref, v_ref, kk_ref, kka_ref, lw_ref, g_ref,
                rk_ref, lnw_ref, lnb_ref, o_ref, s_ref, *, nch, modes):
    m_quad, m_inv, m_state = modes
    c_len = WKV_CHUNK

    npair = s_ref.shape[0]

    @pl.when(pl.program_id(1) == 0)
    def _():
        s_ref[...] = jnp.zeros_like(s_ref)

    row = lax.broadcasted_iota(jnp.int32, (128, 128), 0)
    col = lax.broadcasted_iota(jnp.int32, (128, 128), 1)
    same = (row // c_len) == (col // c_len)
    mask_s = same & (col < row)
    mask_i = same & (col <= row)
    eye = jnp.where(row == col, 1.0, 0.0)
    lane0 = lax.broadcasted_iota(jnp.int32, (c_len, 128), 1) < RWKV_HEAD
    tr = lax.broadcasted_iota(jnp.int32, (c_len, c_len), 0)
    tc = lax.broadcasted_iota(jnp.int32, (c_len, c_len), 1)
    ltri = jnp.where(tc <= tr, 1.0, 0.0)

    def stack(x):
        return jnp.concatenate([jnp.where(lane0, x, 0.0), jnp.where(lane0, 0.0, x)], axis=0)

    def head_sums(x):
        first = jnp.sum(jnp.where(lane0, x, 0.0), axis=-1, keepdims=True)
        both = jnp.sum(x, axis=-1, keepdims=True)
        return jnp.where(lane0, first, both - first)

    pairs = range(npair)
    par = {}
    wus = {}

    def independent_part(chunks):
        keys = [(c, hp) for c in chunks for hp in pairs]
        for c, hp in keys:
            sl = pl.ds(c * c_len, c_len)
            ln = slice(hp * LANE, (hp + 1) * LANE)
            r = r_ref[0, sl, ln].astype(F32)
            k = k_ref[0, sl, ln].astype(F32)
            v = v_ref[0, sl, ln].astype(F32)
            kk = kk_ref[0, sl, ln].astype(F32)
            kka = kka_ref[0, sl, ln].astype(F32)
            lw = lw_ref[0, sl, ln]
            cum = _mm(ltri, lw, "ax")
            tot = cum[c_len - 1:c_len, :]
            p_in = jnp.exp(cum)
            p_ex = jnp.exp(cum - lw)
            p_inv = jnp.exp(-cum)
            p_end = jnp.exp(tot - cum)
            at = stack(-kk * p_ex)
            rt = stack(r * p_in)
            bt = kka * p_inv
            kt = k * p_inv
            par[c, hp] = dict(
                r=r, k=k, v=v, tot=tot, at=at, rt=rt, vs=stack(v),
                bkh=jnp.concatenate([stack(kka * p_end), stack(k * p_end)], axis=0),
                lhs=jnp.concatenate([at, rt], axis=0),
                rhs=jnp.concatenate([bt, bt, kt, kt], axis=0))
        yield
        for key in keys:
            p = par[key]
            quad = _mm(p["lhs"], p["rhs"], m_quad, "nt")
            p["a_ab"] = jnp.where(mask_s, quad[:128, :128], 0.0)
            p["a_ak"] = jnp.where(mask_s, quad[:128, 128:], 0.0)
            p["a_r"] = jnp.concatenate([jnp.where(mask_i, quad[128:, :128], 0.0),
                                        jnp.where(mask_i, quad[128:, 128:], 0.0)], axis=1)
        yield
        for key in keys:
            par[key]["akv"] = _mm(par[key]["a_ak"], par[key]["vs"], m_quad)
        yield
        solved = []
        yield from _unit_lower_solve_stages(
            [par[key]["a_ab"] for key in keys],
            [jnp.concatenate([par[key]["at"], par[key]["akv"]], axis=1) for key in keys],
            row, col, eye, m_inv, solved)
        wus.update(zip(keys, solved))

    state = [s_ref[hp] for hp in pairs]

    def outputs(c, m1, uv):
        sl = pl.ds(c * c_len, c_len)
        pc = [par[c, hp] for hp in pairs]
        ys = [m1[hp][128:] + _mm(pc[hp]["a_r"], uv[hp], m_state) for hp in pairs]
        y = [ys[hp][:c_len] + ys[hp][c_len:] for hp in pairs]
        mean = [head_sums(y[hp]) * (1.0 / RWKV_HEAD) for hp in pairs]
        yc = [y[hp] - mean[hp] for hp in pairs]
        var = [head_sums(yc[hp] * yc[hp]) * (1.0 / RWKV_HEAD) for hp in pairs]
        for hp in pairs:
            ln = slice(hp * LANE, (hp + 1) * LANE)
            r, k, v = pc[hp]["r"], pc[hp]["k"], pc[hp]["v"]
            yn = yc[hp] * lax.rsqrt(var[hp] + GN_EPS) * lnw_ref[:, ln] + lnb_ref[:, ln]
            bonus = head_sums(r * k * rk_ref[:, ln])
            yn = yn + bonus * v
            o_ref[0, sl, ln] = (yn * g_ref[0, sl, ln].astype(F32)).astype(BF16)

    pending = []

    def state_part(chunks):
        for c in chunks:
            pc = [par[c, hp] for hp in pairs]
            wu = [wus[c, hp] for hp in pairs]
            m1 = [_mm(jnp.concatenate([wu[hp][:, :128], pc[hp]["rt"]], axis=0), state[hp],
                      m_state, "nt") for hp in pairs]
            uv = [jnp.concatenate([m1[hp][:128] + wu[hp][:, 128:], pc[hp]["vs"]], axis=0)
                  for hp in pairs]
            upd = [_mm(uv[hp], pc[hp]["bkh"], m_state, "tn") for hp in pairs]
            for hp in pairs:
                state[hp] = state[hp] * jnp.exp(pc[hp]["tot"]) + upd[hp]
            yield
            if pending:
                outputs(*pending.pop())
                yield
            pending.append((c, m1, uv))

    groups = [list(range(g, min(g + WKV_GROUP, nch))) for g in range(0, nch, WKV_GROUP)]
    for _ in independent_part(groups[0]):
        pass
    for g, chunks in enumerate(groups):
        chain = state_part(chunks)
        ahead = independent_part(groups[g + 1]) if g + 1 < len(groups) else iter(())
        for _ in chain:
            for _ in range(WKV_AHEAD_STAGES):
                next(ahead, None)
        for _ in ahead:
            pass
    outputs(*pending.pop())
    for hp in pairs:
        s_ref[hp] = state[hp]


def _wkv(r, k, v, kk, kka, lw, g, r_k, ln_w, ln_b, rows, modes):
    b, s, w = r.shape
    blk = pl.BlockSpec((1, rows, w), lambda bi, i: (bi, i, 0))
    kern = functools.partial(_wkv_kernel, nch=rows // WKV_CHUNK, modes=modes)
    return pl.pallas_call(
        kern,
        grid=(b, s // rows),
        in_specs=[blk] * 7 + [_full((1, w))] * 3,
        out_specs=blk,
        out_shape=jax.ShapeDtypeStruct((b, s, w), BF16),
        scratch_shapes=[pltpu.VMEM((w // LANE, LANE, LANE), F32)],
        compiler_params=_params("parallel", "arbitrary"),
        name="wkv7",
    )(r, k, v, kk, kka, lw, g, r_k, ln_w, ln_b)


def _mem_kv_kernel(m_ref, g_ref, w_ref, o_ref):
    d = m_ref.shape[-1]
    mn = _rms(m_ref[...], g_ref[...], d).astype(BF16)
    o_ref[...] = _dot(mn, w_ref[0]).astype(BF16)


def _mem_kv(mem2d, g, wkv, tm):
    t, d = mem2d.shape
    nl, _, n = wkv.shape
    return pl.pallas_call(
        _mem_kv_kernel,
        grid=(nl, t // tm),
        in_specs=[pl.BlockSpec((tm, d), lambda l, i: (i, 0)), _full((1, d)),
                  pl.BlockSpec((1, d, n), lambda l, i: (l, 0, 0))],
        out_specs=pl.BlockSpec((tm, n), lambda l, i: (i, l)),
        out_shape=jax.ShapeDtypeStruct((t, nl * n), BF16),
        compiler_params=_params("parallel", "parallel"),
        name="mem_kv",
    )(mem2d, g, wkv)


def _mix_out_kernel(h_ref, att_ref, rw_ref, woa_ref, wob_ref, g_ref, wq_ref,
                    mk_ref, mv_ref, wo_ref, o_ref, *, q_scale):
    d = h_ref.shape[-1]
    h1 = h_ref[...] + _dot(att_ref[...], woa_ref[0]) + _dot(rw_ref[...], wob_ref[0])
    hn = _rms(h1, g_ref[0], d).astype(BF16)
    q = (_dot(hn, wq_ref[0]) * q_scale).astype(BF16)
    dh = d // CA_HEADS
    outs = []
    for hd in range(CA_HEADS):
        sl = slice(hd * dh, (hd + 1) * dh)
        s = _dot_nt(q[:, sl], mk_ref[:, sl])
        m = jnp.max(s, axis=-1, keepdims=True)
        p = jnp.exp2(s - m)
        l = jnp.sum(p, axis=-1, keepdims=True)
        outs.append((_dot(p.astype(BF16), mv_ref[:, sl]) / l).astype(BF16))
    o = jnp.concatenate(outs, axis=1)
    o_ref[...] = h1 + _dot(o, wo_ref[0])


def _mix_out(h, att, rw, woa, wob, g, wq, memkv, wo, layer, seq, mem_len, tm):
    t, d = h.shape
    tiles = seq // tm
    q_scale = (d // CA_HEADS) ** -0.5 * LOG2E
    row = lambda w: pl.BlockSpec((tm, w), lambda i: (i, 0))
    lay = functools.partial(_layer_slice, layer=layer)
    return pl.pallas_call(
        functools.partial(_mix_out_kernel, q_scale=q_scale),
        grid=(t // tm,),
        in_specs=[row(d), row(att.shape[1]), row(rw.shape[1]), lay(woa), lay(wob), lay(g),
                  lay(wq),
                  pl.BlockSpec((mem_len, d), lambda i: (i // tiles, 2 * layer)),
                  pl.BlockSpec((mem_len, d), lambda i: (i // tiles, 2 * layer + 1)),
                  lay(wo)],
        out_specs=row(d),
        out_shape=jax.ShapeDtypeStruct((t, d), F32),
        compiler_params=_params("parallel"),
        name="mix_out_cross_attention",
    )(h, att, rw, woa, wob, g, wq, memkv, memkv, wo)


def _mlp_kernel(h_ref, g_ref, wu_ref, wd_ref, gf_ref, o_ref, *, last, ff_chunk):
    d = h_ref.shape[-1]
    h = h_ref[...]
    hn = _rms(h, g_ref[0], d).astype(BF16)
    acc = h
    for c in range(wu_ref.shape[-1] // ff_chunk):
        sl = slice(c * ff_chunk, (c + 1) * ff_chunk)
        u = jnp.maximum(_dot(hn, wu_ref[0, :, sl]), 0.0)
        acc = acc + _dot((u * u).astype(BF16), wd_ref[0, sl, :])
    if last:
        acc = _rms(acc, gf_ref[...], d)
    o_ref[...] = acc


def _mlp(h, g, wu, wd, gf, layer, last, tm):
    t, d = h.shape
    row = pl.BlockSpec((tm, d), lambda i: (i, 0))
    lay = functools.partial(_layer_slice, layer=layer)
    return pl.pallas_call(
        functools.partial(_mlp_kernel, last=last, ff_chunk=1024),
        grid=(t // tm,),
        in_specs=[row, lay(g), lay(wu), lay(wd), _full((1, d))],
        out_specs=row,
        out_shape=jax.ShapeDtypeStruct((t, d), F32),
        compiler_params=_params("parallel"),
        name="mlp",
    )(h, g, wu, wd, gf)


def _pad_cols(a, n):
    return jnp.pad(a, ((0, 0), (0, n - a.shape[1])))


def _swap_halves(a):
    half = a.shape[-1] // 2
    return jnp.concatenate([a[..., half:], a[..., :half]], axis=-1)


def _in_weights(w_in, mu):
    cq = w_in[:, :MLA_Q_RANK]
    ckv = w_in[:, MLA_Q_RANK:MLA_Q_RANK + MLA_KV_RANK]
    kr = w_in[:, MLA_Q_RANK + MLA_KV_RANK:MLA_COLS]
    mla = jnp.concatenate([_pad_cols(cq, MLA_Q_PAD), ckv, kr, _swap_halves(kr)], axis=1)
    w = jnp.concatenate([_pad_cols(mla, MLA_C_PAD), _pad_cols(w_in[:, MLA_COLS:], RW_C_PAD)], axis=1)
    return w.astype(BF16), _pad_cols(mu[None, :], RW_C_PAD)


def _head_pad(a, n_heads, per_head, lo, hi, at):
    rows = a.shape[0]
    a = a.reshape(rows, n_heads, per_head)[:, :, lo:hi]
    a = jnp.pad(a, ((0, 0), (0, 0), (at, HEAD_PAD - at - (hi - lo))))
    return a.reshape(rows, n_heads * HEAD_PAD)


def _mla_weights(w_uq, w_ukv):
    dq = MLA_NOPE_DIM + MLA_ROPE_DIM
    h = MLA_HEADS
    wq3 = w_uq.reshape(MLA_Q_RANK, h, dq)
    rope_sw = _swap_halves(wq3[:, :, MLA_NOPE_DIM:]).reshape(MLA_Q_RANK, h * MLA_ROPE_DIM)
    wqa = _head_pad(w_uq, h, dq, 0, dq, 0)
    wqb = _head_pad(rope_sw, h, MLA_ROPE_DIM, 0, MLA_ROPE_DIM, MLA_NOPE_DIM)
    pad_q = ((0, MLA_Q_PAD - MLA_Q_RANK), (0, 0))
    wqa = jnp.pad(wqa, pad_q)
    wqb = jnp.pad(wqb, pad_q)
    dkv = MLA_NOPE_DIM + MLA_V_DIM
    wk = _head_pad(w_ukv, h, dkv, 0, MLA_NOPE_DIM, 0)
    wvt = jnp.transpose(w_ukv.reshape(MLA_KV_RANK, h, dkv)[:, :, MLA_NOPE_DIM:], (1, 2, 0))
    wvt = jnp.pad(wvt, ((0, 0), (0, ATTN_ONES_ROWS), (0, 0))).reshape(-1, MLA_KV_RANK)
    eye = jnp.eye(MLA_ROPE_DIM, dtype=F32)
    place = _head_pad(jnp.tile(eye, (1, h)), h, MLA_ROPE_DIM, 0, MLA_ROPE_DIM, MLA_NOPE_DIM)
    zeros = jnp.zeros_like(place)
    pad_rows = jnp.zeros((LANE - 2 * MLA_ROPE_DIM, h * HEAD_PAD), F32)
    pa = jnp.concatenate([place, zeros, pad_rows], axis=0)
    pb = jnp.concatenate([zeros, place, pad_rows], axis=0)
    wka = jnp.concatenate([wk, pa], axis=0)
    return tuple(a.astype(BF16) for a in (wqa, wqb, wka, pb, wvt))


def _lora_weights(w2, a2, g2, v2):
    rw = RWKV_WIDTH
    z = lambda r: jnp.zeros((r, rw), F32)
    rows = [
        jnp.concatenate([w2, z(DECAY_RANK), z(DECAY_RANK), z(DECAY_RANK)], axis=1),
        jnp.concatenate([z(ICL_RANK), a2, z(ICL_RANK), z(ICL_RANK)], axis=1),
        jnp.concatenate([z(GATE_RANK), z(GATE_RANK), g2, z(GATE_RANK)], axis=1),
        jnp.concatenate([z(VRES_RANK)] * 3 + [v2 if v2 is not None else z(VRES_RANK)], axis=1),
    ]
    w = jnp.concatenate(rows, axis=0)
    return jnp.pad(w, ((0, LORA_PAD - w.shape[0]), (0, 0))).astype(BF16)


def _block_diag(n, blk, value):
    i = jnp.arange(n)
    return jnp.where((i[:, None] // blk) == (i[None, :] // blk), value, 0.0)


def kernel(x, mem, positions, mix_norm, w_in_first, w_in_rest, shift_mu_first, shift_mu_rest, mla_q_norm, mla_w_uq, mla_kv_norm, mla_w_ukv, mla_out_norm, rwkv_w0, rwkv_w2, rwkv_a0, rwkv_a2, rwkv_g2, rwkv_v0, rwkv_v2, rwkv_k_k, rwkv_k_a, rwkv_r_k, rwkv_ln_w, rwkv_ln_b, w_out, ca_norm, mem_norm, ca_wq, ca_wkv, ca_wo, mlp_norm, mlp_w_up, mlp_w_down, final_norm):
    b, s, d = x.shape
    depth = mix_norm.shape[0]
    mem_len = mem.shape[1]
    t = b * s
    tm = min(512, s)
    tm_wide = min(1024, s)
    attn_key_block = min(1024, s // 2)
    wkv_rows = min(1024, s)
    rw = RWKV_WIDTH

    inv = ROPE_THETA ** (-jnp.arange(0, MLA_ROPE_DIM, 2, dtype=F32) / MLA_ROPE_DIM)
    inv_row = jnp.concatenate([jnp.zeros((MLA_NOPE_DIM,), F32), inv, inv,
                               jnp.zeros((HEAD_PAD - MLA_NOPE_DIM - MLA_ROPE_DIM,), F32)])[None, :]
    ct, st = _rope_tables(positions.astype(F32).reshape(t, 1), inv_row, tm)

    memkv = _mem_kv(mem.reshape(b * mem_len, d), mem_norm[None, :], ca_wkv.astype(BF16),
                    min(512, b * mem_len))

    woa = w_out[:, :MLA_HEADS * MLA_V_DIM].astype(BF16)
    wob = w_out[:, MLA_HEADS * MLA_V_DIM:].astype(BF16)
    wq_b = ca_wq.astype(BF16)
    wo_b = ca_wo.astype(BF16)
    wu_b = mlp_w_up.astype(BF16)
    wd_b = mlp_w_down.astype(BF16)
    ca_g = ca_norm[:, None, :]
    mlp_g = mlp_norm[:, None, :]
    bd512 = _block_diag(rw, RWKV_HEAD, 1.0).astype(BF16)

    h = x.reshape(t, d)
    v_first = None
    for l in range(depth):
        if l == 0:
            w_in, mu, v0, v2 = w_in_first, shift_mu_first, None, None
        else:
            w_in, mu, v0, v2 = w_in_rest[l - 1], shift_mu_rest[l - 1], rwkv_v0[l - 1], rwkv_v2[l - 1]
        w_all, mu_p = _in_weights(w_in, mu)
        wqa, wqb, wka, pb, wvt = _mla_weights(mla_w_uq[l], mla_w_ukv[l])
        gq = _pad_cols(mla_q_norm[l][None, :], MLA_Q_PAD)
        w2c = _lora_weights(rwkv_w2[l], rwkv_a2[l], rwkv_g2[l], v2)
        vec = lambda a: a.reshape(1, rw)
        q, k, vt, r_, k_, v_, kk_, kka_, lw_, g_ = _layer_in(
            h, mix_norm[l][None, :], w_all, mu_p, ct, st, gq, mla_kv_norm[l][None, :],
            wqa, wqb, wka, pb, wvt, v_first, w2c, bd512, vec(rwkv_w0[l]), vec(rwkv_a0[l]),
            vec(rwkv_k_k[l]), vec(rwkv_k_a[l]), None if v0 is None else vec(v0), s, tm_wide)
        if l == 0:
            v_first = v_

        att = _mla_attention(q, k, vt, mla_out_norm[l][None, :], s, attn_key_block,
                             ATTN_HEADS_PER_STEP)

        sh = lambda a: a.reshape(b, s, rw)
        y = _wkv(sh(r_), sh(k_), sh(v_), sh(kk_), sh(kka_), sh(lw_), sh(g_),
                 vec(rwkv_r_k[l]), vec(rwkv_ln_w[l]), vec(rwkv_ln_b[l]), wkv_rows, WKV_MODES)

        h = _mix_out(h, att.reshape(t, -1), y.reshape(t, rw), woa, wob, ca_g, wq_b, memkv, wo_b,
                     l, s, mem_len, tm_wide)
        h = _mlp(h, mlp_g, wu_b, wd_b, final_norm[None, :], l, l == depth - 1, tm_wide)
    return h.reshape(b, s, d)
```
